```python
import math
import jax
import jax.numpy as jnp
from jax import lax
import numpy as np

D_MODEL = 1024
BATCH = 8
SEQ = 8192
DEPTH = 4

HEAD_DIM = 64
GROUP_HEADS = 4
GROUP_WIDTH = GROUP_HEADS * HEAD_DIM
D_MIX = 4 * GROUP_WIDTH
GLA_RANK = 16
GLA_TAU = 16.0
GLA_CHUNK = 64
GRID_W = 64
NA_ROWS_MAX = 8
NA_COLS = 16
LRU_CONV = 4
LRU_CONV_LEFT = 2
LRU_C = 8.0
DIL_PAIRS = ((128, 1), (512, 4), (2048, 16))
ROPE_THETA = 10000.0
D_FF = -(-8 * D_MODEL // (3 * 256)) * 256
EPS = 1e-6
SPLITS = (GROUP_WIDTH,) * 4 + (2 * GLA_RANK,) + (GROUP_WIDTH,) * 3 + (GROUP_WIDTH,) * 2 + (GROUP_WIDTH,) * 3
D_IN = sum(SPLITS)

kernel_name = "hybrid_parallel_head_group_encoder"


def rmsnorm(x, g):
    xf = x.astype(jnp.float32)
    y = xf * lax.rsqrt(jnp.mean(xf * xf, axis=-1, keepdims=True) + EPS)
    return (y * g.astype(jnp.float32)).astype(x.dtype)


def to_heads(t):
    B, L, _ = t.shape
    return t.reshape(B, L, -1, HEAD_DIM).transpose(0, 2, 1, 3)


def from_heads(t):
    B, H, L, dh = t.shape
    return t.transpose(0, 2, 1, 3).reshape(B, L, H * dh)


def rope(t, cos, sin):
    t1, t2 = jnp.split(t, 2, axis=-1)
    c = cos.astype(t.dtype)
    s = sin.astype(t.dtype)
    return jnp.concatenate([t1 * c - t2 * s, t2 * c + t1 * s], axis=-1)


def gla_chunked(q, k, v, log_a):
    B, H, L, dk = q.shape
    dv = v.shape[-1]
    C = GLA_CHUNK
    n = L // C
    q, k, v, log_a = [t.reshape(B, H, n, C, -1) for t in (q, k, v, log_a)]
    b = jnp.cumsum(log_a, axis=3)
    b_last = b[:, :, :, C - 1:C, :]
    b_mid = b[:, :, :, C // 2 - 1:C // 2, :]
    att = jnp.einsum('bhnck,bhnsk->bhncs', q * jnp.exp(b - b_mid), k * jnp.exp(b_mid - b))
    tri = np.tril(np.ones((C, C), dtype=bool))
    att = jnp.where(tri, att, 0.0)
    o_intra = jnp.einsum('bhncs,bhnsv->bhncv', att, v)
    chunk_kv = jnp.einsum('bhnck,bhncv->bhnkv', k * jnp.exp(b_last - b), v)
    decay = jnp.exp(b_last[:, :, :, 0, :])

    def step(S, inp):
        kv_c, d_c = inp
        return S * d_c[..., None] + kv_c, S

    S0 = jnp.zeros((B, H, dk, dv), jnp.float32)
    _, S_prev = lax.scan(step, S0, (jnp.moveaxis(chunk_kv, 2, 0), jnp.moveaxis(decay, 2, 0)))
    S_prev = jnp.moveaxis(S_prev, 0, 2)
    o_inter = jnp.einsum('bhnck,bhnkv->bhncv', q * jnp.exp(b), S_prev)
    return (o_intra + o_inter).reshape(B, H, L, dv)


def gla_mixer(q, k, v, g, z, w_gate, b_gate, norm_g):
    B, L, _ = q.shape
    f32 = jnp.float32
    zl = z.astype(f32).reshape(B, L, 2, GLA_RANK)
    logit = jnp.einsum('bler,erc->eblc', zl, w_gate.astype(f32)) + b_gate.astype(f32)[:, None, None, :]
    log_a = jax.nn.log_sigmoid(logit) / GLA_TAU
    qh = to_heads(q).astype(f32) * (HEAD_DIM ** -0.5)
    kh = to_heads(k).astype(f32)
    vh = to_heads(v).astype(f32)
    flip = lambda t: jnp.flip(t, axis=2)
    o_f = gla_chunked(qh, kh, vh, to_heads(log_a[0]))
    o_b = flip(gla_chunked(flip(qh), flip(kh), flip(vh), flip(to_heads(log_a[1]))))
    o = o_f + o_b
    o = o * lax.rsqrt(jnp.mean(o * o, axis=-1, keepdims=True) + EPS)
    o = o * norm_g.astype(f32).reshape(GROUP_HEADS, 1, HEAD_DIM)
    return (from_heads(o) * jax.nn.silu(g.astype(f32))).astype(q.dtype)


def neighbourhood_attention(q, k, v, rpb):
    B, H, L, dh = q.shape
    rows = L // GRID_W
    kr = min(NA_ROWS_MAX, rows)
    grid = lambda t: t.reshape(B, H, rows, GRID_W, dh)
    qg, kg, vg = grid(q), grid(k), grid(v)
    r = np.arange(rows)
    row_idx = np.clip(r - kr // 2, 0, rows - kr)[:, None] + np.arange(kr)[None, :]
    k_rows = kg[:, :, row_idx]
    v_rows = vg[:, :, row_idx]
    c = np.arange(GRID_W)
    col_start = np.clip(c - NA_COLS // 2, 0, GRID_W - NA_COLS)
    col_ok = (c[None, :] >= col_start[:, None]) & (c[None, :] < col_start[:, None] + NA_COLS)
    dr = row_idx - r[:, None]
    dc = np.clip(c[None, :] - c[:, None], -(NA_COLS - 1), NA_COLS - 1)
    bias = rpb[:, (dr + NA_ROWS_MAX - 1)[:, None, :, None], (dc + NA_COLS - 1)[None, :, None, :]]
    s = jnp.einsum('bhrqd,bhrikd->bhrqik', qg, k_rows).astype(jnp.float32) * (dh ** -0.5)
    s = s + bias.astype(jnp.float32)
    s = jnp.where(col_ok[:, None, :], s, -jnp.inf)
    p = jax.nn.softmax(s.reshape(B, H, rows, GRID_W, kr * GRID_W), axis=-1).reshape(s.shape)
    o = jnp.einsum('bhrqik,bhrikd->bhrqd', p.astype(v.dtype), v_rows)
    return o.reshape(B, H, L, dh)


def linear_scan(a, u):
    def combine(left, right):
        a_l, u_l = left
        a_r, u_r = right
        return a_l * a_r, a_r * u_l + u_r
    return lax.associative_scan(combine, (a, u), axis=1)[1]


def rglru_mixer(xb, gate, conv_w, conv_b, w_a, b_a, w_x, b_x, lam):
    B, L, C = xb.shape
    f32 = jnp.float32
    xp = jnp.pad(xb.astype(f32), ((0, 0), (LRU_CONV_LEFT, LRU_CONV - 1 - LRU_CONV_LEFT), (0, 0)))
    xc = conv_b.astype(f32)
    for j in range(LRU_CONV):
        xc = xc + xp[:, j:j + L, :] * conv_w[j].astype(f32)
    xh = xc.reshape(B, L, GROUP_HEADS, HEAD_DIM)
    r = jax.nn.sigmoid(jnp.einsum('blhi,ehij->eblhj', xh, w_a.astype(f32)).reshape(2, B, L, C)
                       + b_a.astype(f32)[:, None, None, :])
    i = jax.nn.sigmoid(jnp.einsum('blhi,ehij->eblhj', xh, w_x.astype(f32)).reshape(2, B, L, C)
                       + b_x.astype(f32)[:, None, None, :])
    log_a = -LRU_C * r * jax.nn.softplus(-lam.astype(f32))[:, None, None, :]
    a = jnp.exp(log_a)
    u = jnp.sqrt(-jnp.expm1(2.0 * log_a)) * (i * xc[None])
    flip = lambda t: jnp.flip(t, axis=1)
    h = linear_scan(a[0], u[0]) + flip(linear_scan(flip(a[1]), flip(u[1])))
    return (h * jax.nn.gelu(gate.astype(f32))).astype(xb.dtype)


def band_attention(q, k, v, radius):
    lead = q.shape[:-2]
    n, dh = q.shape[-2], q.shape[-1]
    Q = radius
    nb = -(-n // Q)
    n_pad = nb * Q
    nl = len(lead)
    qb = jnp.pad(q, ((0, 0),) * nl + ((0, n_pad - n), (0, 0))).reshape(lead + (nb, Q, dh))
    padkv = lambda t: jnp.pad(t, ((0, 0),) * nl + ((Q, n_pad - n + Q), (0, 0))).reshape(lead + (nb + 2, Q, dh))
    kp, vp = padkv(k), padkv(v)
    band = lambda t: jnp.concatenate([t[..., 0:nb, :, :], t[..., 1:nb + 1, :, :], t[..., 2:nb + 2, :, :]], axis=-2)
    kb, vb = band(kp), band(vp)
    blk = np.arange(nb)[:, None, None]
    qpos = blk * Q + np.arange(Q)[None, :, None]
    kpos = blk * Q + np.arange(3 * Q)[None, None, :] - Q
    valid = (np.abs(kpos - qpos) <= radius) & (kpos >= 0) & (kpos < n)
    s = jnp.einsum('...bqd,...bkd->...bqk', qb, kb).astype(jnp.float32) * (dh ** -0.5)
    s = jnp.where(valid, s, -jnp.inf)
    m = jnp.max(s, axis=-1, keepdims=True)
    e = jnp.exp(s - m)
    den = jnp.sum(e, axis=-1, keepdims=True)
    o = jnp.einsum('...bqk,...bkd->...bqd', (e / den).astype(v.dtype), vb)
    lse = (m + jnp.log(den))[..., 0]
    o = o.reshape(lead + (n_pad, dh))[..., :n, :]
    lse = lse.reshape(lead + (n_pad,))[..., :n]
    return o, lse


def dilated_attention(q, k, v):
    B, H, L, dh = q.shape
    outs, lses = [], []
    for window, dil in DIL_PAIRS:
        radius = window // (2 * dil)
        n = L // dil
        sub = lambda t: t.reshape(B, H, n, dil, dh).transpose(0, 1, 3, 2, 4)
        o, lse = band_attention(sub(q), sub(k), sub(v), radius)
        outs.append(o.transpose(0, 1, 3, 2, 4).reshape(B, H, L, dh).astype(jnp.float32))
        lses.append(lse.transpose(0, 1, 3, 2).reshape(B, H, L))
    wts = jax.nn.softmax(jnp.stack(lses, axis=0), axis=0)
    return jnp.einsum('gbhl,gbhld->bhld', wts, jnp.stack(outs, axis=0)).astype(q.dtype)


def _fwd_setup_inputs(seed: int = 0) -> dict:
    key = jax.random.key(seed)
    ks = jax.random.split(key, 24)
    f32 = jnp.float32
    nrm = lambda k, shape, scale: scale * jax.random.normal(k, shape, f32)
    gain = lambda k, d: 1.0 + 0.02 * jax.random.normal(k, (DEPTH, d), f32)
    u = jax.random.uniform(ks[14], (DEPTH, 2, GROUP_WIDTH), f32, 0.9, 0.999)
    return {
        "x": jax.random.normal(ks[0], (BATCH, SEQ, D_MODEL), f32),
        "mix_norm_pre": gain(ks[1], D_MODEL),
        "mix_norm_post": gain(ks[2], D_MODEL),
        "w_in": nrm(ks[3], (DEPTH, D_MODEL, D_IN), D_MODEL ** -0.5),
        "gla_w_gate": nrm(ks[4], (DEPTH, 2, GLA_RANK, GROUP_WIDTH), GLA_RANK ** -0.5),
        "gla_b_gate": nrm(ks[5], (DEPTH, 2, GROUP_WIDTH), 0.1),
        "gla_norm": gain(ks[6], GROUP_WIDTH),
        "na_rpb": nrm(ks[7], (DEPTH, GROUP_HEADS, 2 * NA_ROWS_MAX - 1, 2 * NA_COLS - 1), 0.1),
        "lru_conv_w": nrm(ks[8], (DEPTH, LRU_CONV, GROUP_WIDTH), LRU_CONV ** -0.5),
        "lru_conv_b": nrm(ks[9], (DEPTH, GROUP_WIDTH), 0.02),
        "lru_w_a": nrm(ks[10], (DEPTH, 2, GROUP_HEADS, HEAD_DIM, HEAD_DIM), HEAD_DIM ** -0.5),
        "lru_b_a": nrm(ks[11], (DEPTH, 2, GROUP_WIDTH), 0.1),
        "lru_w_x": nrm(ks[12], (DEPTH, 2, GROUP_HEADS, HEAD_DIM, HEAD_DIM), HEAD_DIM ** -0.5),
        "lru_b_x": nrm(ks[13], (DEPTH, 2, GROUP_WIDTH), 0.1),
        "lru_lambda": jnp.log(u) - jnp.log1p(-u),
        "w_out": nrm(ks[15], (DEPTH, D_MIX, D_MODEL), D_MIX ** -0.5),
        "ffn_norm_pre": gain(ks[16], D_MODEL),
        "ffn_norm_post": gain(ks[17], D_MODEL),
        "ffn_w_in": nrm(ks[18], (DEPTH, D_MODEL, 2 * D_FF), D_MODEL ** -0.5),
        "ffn_w_out": nrm(ks[19], (DEPTH, D_FF, D_MODEL), D_FF ** -0.5),
    }


def _fwd_reference(x, mix_norm_pre, mix_norm_post, w_in, gla_w_gate, gla_b_gate, gla_norm, na_rpb,
              lru_conv_w, lru_conv_b, lru_w_a, lru_b_a, lru_w_x, lru_b_x, lru_lambda, w_out,
              ffn_norm_pre, ffn_norm_post, ffn_w_in, ffn_w_out):
    B, L, _ = x.shape
    pos = jnp.arange(L, dtype=jnp.float32)
    inv_freq = ROPE_THETA ** (-jnp.arange(0, HEAD_DIM, 2, dtype=jnp.float32) / HEAD_DIM)
    ang = pos[:, None] * inv_freq[None, :]
    cos, sin = jnp.cos(ang), jnp.sin(ang)
    split_at = [int(s) for s in np.cumsum(SPLITS)[:-1]]
    for l in range(DEPTH):
        h = rmsnorm(x, mix_norm_pre[l])
        p = h @ w_in[l]
        qa, ka, va, ga, za, qb, kb, vb, xc, gc, qd, kd, vd = jnp.split(p, split_at, axis=-1)
        ya = gla_mixer(qa, ka, va, ga, za, gla_w_gate[l], gla_b_gate[l], gla_norm[l])
        yb = from_heads(neighbourhood_attention(to_heads(qb), to_heads(kb), to_heads(vb), na_rpb[l]))
        yc = rglru_mixer(xc, gc, lru_conv_w[l], lru_conv_b[l], lru_w_a[l], lru_b_a[l],
                         lru_w_x[l], lru_b_x[l], lru_lambda[l])
        yd = from_heads(dilated_attention(rope(to_heads(qd), cos, sin), rope(to_heads(kd), cos, sin),
                                          to_heads(vd)))
        y = jnp.concatenate([ya, yb.astype(x.dtype), yc, yd.astype(x.dtype)], axis=-1) @ w_out[l]
        x = x + rmsnorm(y, mix_norm_post[l])
        h = rmsnorm(x, ffn_norm_pre[l])
        gate, up = jnp.split(h @ ffn_w_in[l], 2, axis=-1)
        f = (jax.nn.silu(gate) * up) @ ffn_w_out[l]
        x = x + rmsnorm(f, ffn_norm_post[l])
    return x


import jax as _jax
import jax.numpy as _jnp

TWIN_FORMAT = 'train_step'
FWD_PARAMS = ['x', 'mix_norm_pre', 'mix_norm_post', 'w_in', 'gla_w_gate', 'gla_b_gate', 'gla_norm', 'na_rpb', 'lru_conv_w', 'lru_conv_b', 'lru_w_a', 'lru_b_a', 'lru_w_x', 'lru_b_x', 'lru_lambda', 'w_out', 'ffn_norm_pre', 'ffn_norm_post', 'ffn_w_in', 'ffn_w_out']
TWIN_WEIGHTS = ['mix_norm_pre', 'mix_norm_post', 'w_in', 'gla_w_gate', 'gla_b_gate', 'gla_norm', 'na_rpb', 'lru_conv_w', 'lru_conv_b', 'lru_w_a', 'lru_b_a', 'lru_w_x', 'lru_b_x', 'lru_lambda', 'w_out', 'ffn_norm_pre', 'ffn_norm_post', 'ffn_w_in', 'ffn_w_out']
TWIN_DIFF_INPUT = 'x'
TWIN_INPUTS = ['x', 'mix_norm_pre', 'mix_norm_post', 'w_in', 'gla_w_gate', 'gla_b_gate', 'gla_norm', 'na_rpb', 'lru_conv_w', 'lru_conv_b', 'lru_w_a', 'lru_b_a', 'lru_w_x', 'lru_b_x', 'lru_lambda', 'w_out', 'ffn_norm_pre', 'ffn_norm_post', 'ffn_w_in', 'ffn_w_out', 'loss_target', 'm_mix_norm_pre', 'm_mix_norm_post', 'm_w_in', 'm_gla_w_gate', 'm_gla_b_gate', 'm_gla_norm', 'm_na_rpb', 'm_lru_conv_w', 'm_lru_conv_b', 'm_lru_w_a', 'm_lru_b_a', 'm_lru_w_x', 'm_lru_b_x', 'm_lru_lambda', 'm_w_out', 'm_ffn_norm_pre', 'm_ffn_norm_post', 'm_ffn_w_in', 'm_ffn_w_out', 'v_mix_norm_pre', 'v_mix_norm_post', 'v_w_in', 'v_gla_w_gate', 'v_gla_b_gate', 'v_gla_norm', 'v_na_rpb', 'v_lru_conv_w', 'v_lru_conv_b', 'v_lru_w_a', 'v_lru_b_a', 'v_lru_w_x', 'v_lru_b_x', 'v_lru_lambda', 'v_w_out', 'v_ffn_norm_pre', 'v_ffn_norm_post', 'v_ffn_w_in', 'v_ffn_w_out']
TWIN_OUTPUTS = ['loss', 'grad_x', 'grad_mix_norm_pre', 'grad_mix_norm_post', 'grad_w_in', 'grad_gla_w_gate', 'grad_gla_b_gate', 'grad_gla_norm', 'grad_na_rpb', 'grad_lru_conv_w', 'grad_lru_conv_b', 'grad_lru_w_a', 'grad_lru_b_a', 'grad_lru_w_x', 'grad_lru_b_x', 'grad_lru_lambda', 'grad_w_out', 'grad_ffn_norm_pre', 'grad_ffn_norm_post', 'grad_ffn_w_in', 'grad_ffn_w_out', 'delta_mix_norm_pre', 'delta_mix_norm_post', 'delta_w_in', 'delta_gla_w_gate', 'delta_gla_b_gate', 'delta_gla_norm', 'delta_na_rpb', 'delta_lru_conv_w', 'delta_lru_conv_b', 'delta_lru_w_a', 'delta_lru_b_a', 'delta_lru_w_x', 'delta_lru_b_x', 'delta_lru_lambda', 'delta_w_out', 'delta_ffn_norm_pre', 'delta_ffn_norm_post', 'delta_ffn_w_in', 'delta_ffn_w_out', 'new_m_mix_norm_pre', 'new_m_mix_norm_post', 'new_m_w_in', 'new_m_gla_w_gate', 'new_m_gla_b_gate', 'new_m_gla_norm', 'new_m_na_rpb', 'new_m_lru_conv_w', 'new_m_lru_conv_b', 'new_m_lru_w_a', 'new_m_lru_b_a', 'new_m_lru_w_x', 'new_m_lru_b_x', 'new_m_lru_lambda', 'new_m_w_out', 'new_m_ffn_norm_pre', 'new_m_ffn_norm_post', 'new_m_ffn_w_in', 'new_m_ffn_w_out', 'new_v_mix_norm_pre', 'new_v_mix_norm_post', 'new_v_w_in', 'new_v_gla_w_gate', 'new_v_gla_b_gate', 'new_v_gla_norm', 'new_v_na_rpb', 'new_v_lru_conv_w', 'new_v_lru_conv_b', 'new_v_lru_w_a', 'new_v_lru_b_a', 'new_v_lru_w_x', 'new_v_lru_b_x', 'new_v_lru_lambda', 'new_v_w_out', 'new_v_ffn_norm_pre', 'new_v_ffn_norm_post', 'new_v_ffn_w_in', 'new_v_ffn_w_out']
TWIN_LEAF_KINDS = {'loss': 'loss', 'grad_x': 'grad_x', 'grad_mix_norm_pre': 'grad_w', 'grad_mix_norm_post': 'grad_w', 'grad_w_in': 'grad_w', 'grad_gla_w_gate': 'grad_w', 'grad_gla_b_gate': 'grad_w', 'grad_gla_norm': 'grad_w', 'grad_na_rpb': 'grad_w', 'grad_lru_conv_w': 'grad_w', 'grad_lru_conv_b': 'grad_w', 'grad_lru_w_a': 'grad_w', 'grad_lru_b_a': 'grad_w', 'grad_lru_w_x': 'grad_w', 'grad_lru_b_x': 'grad_w', 'grad_lru_lambda': 'grad_w', 'grad_w_out': 'grad_w', 'grad_ffn_norm_pre': 'grad_w', 'grad_ffn_norm_post': 'grad_w', 'grad_ffn_w_in': 'grad_w', 'grad_ffn_w_out': 'grad_w', 'delta_mix_norm_pre': 'delta_w', 'delta_mix_norm_post': 'delta_w', 'delta_w_in': 'delta_w', 'delta_gla_w_gate': 'delta_w', 'delta_gla_b_gate': 'delta_w', 'delta_gla_norm': 'delta_w', 'delta_na_rpb': 'delta_w', 'delta_lru_conv_w': 'delta_w', 'delta_lru_conv_b': 'delta_w', 'delta_lru_w_a': 'delta_w', 'delta_lru_b_a': 'delta_w', 'delta_lru_w_x': 'delta_w', 'delta_lru_b_x': 'delta_w', 'delta_lru_lambda': 'delta_w', 'delta_w_out': 'delta_w', 'delta_ffn_norm_pre': 'delta_w', 'delta_ffn_norm_post': 'delta_w', 'delta_ffn_w_in': 'delta_w', 'delta_ffn_w_out': 'delta_w', 'new_m_mix_norm_pre': 'new_m', 'new_m_mix_norm_post': 'new_m', 'new_m_w_in': 'new_m', 'new_m_gla_w_gate': 'new_m', 'new_m_gla_b_gate': 'new_m', 'new_m_gla_norm': 'new_m', 'new_m_na_rpb': 'new_m', 'new_m_lru_conv_w': 'new_m', 'new_m_lru_conv_b': 'new_m', 'new_m_lru_w_a': 'new_m', 'new_m_lru_b_a': 'new_m', 'new_m_lru_w_x': 'new_m', 'new_m_lru_b_x': 'new_m', 'new_m_lru_lambda': 'new_m', 'new_m_w_out': 'new_m', 'new_m_ffn_norm_pre': 'new_m', 'new_m_ffn_norm_post': 'new_m', 'new_m_ffn_w_in': 'new_m', 'new_m_ffn_w_out': 'new_m', 'new_v_mix_norm_pre': 'new_v', 'new_v_mix_norm_post': 'new_v', 'new_v_w_in': 'new_v', 'new_v_gla_w_gate': 'new_v', 'new_v_gla_b_gate': 'new_v', 'new_v_gla_norm': 'new_v', 'new_v_na_rpb': 'new_v', 'new_v_lru_conv_w': 'new_v', 'new_v_lru_conv_b': 'new_v', 'new_v_lru_w_a': 'new_v', 'new_v_lru_b_a': 'new_v', 'new_v_lru_w_x': 'new_v', 'new_v_lru_b_x': 'new_v', 'new_v_lru_lambda': 'new_v', 'new_v_w_out': 'new_v', 'new_v_ffn_norm_pre': 'new_v', 'new_v_ffn_norm_post': 'new_v', 'new_v_ffn_w_in': 'new_v', 'new_v_ffn_w_out': 'new_v'}


def _forward(args):
    return _fwd_reference(*[args[k] for k in FWD_PARAMS])


def _output_shape():
    def fwd():
        inp = _fwd_setup_inputs(0)
        return _fwd_reference(*[inp[k] for k in FWD_PARAMS])
    out = _jax.eval_shape(fwd)
    return out.shape, out.dtype

N_MICROBATCH = 1
ADAM_LR = 0.001
ADAM_B1 = 0.9
ADAM_B2 = 0.999
ADAM_EPS = 1e-08
ADAM_WD = 0.01
ADAM_STEP = 10
PER_EXAMPLE_BATCH_AXIS = {'x': 0, 'loss_target': 0}
SHARED_INPUTS = []
_WEIGHT_DTYPES = {'mix_norm_pre': _jnp.float32, 'mix_norm_post': _jnp.float32, 'w_in': _jnp.float32, 'gla_w_gate': _jnp.float32, 'gla_b_gate': _jnp.float32, 'gla_norm': _jnp.float32, 'na_rpb': _jnp.float32, 'lru_conv_w': _jnp.float32, 'lru_conv_b': _jnp.float32, 'lru_w_a': _jnp.float32, 'lru_b_a': _jnp.float32, 'lru_w_x': _jnp.float32, 'lru_b_x': _jnp.float32, 'lru_lambda': _jnp.float32, 'w_out': _jnp.float32, 'ffn_norm_pre': _jnp.float32, 'ffn_norm_post': _jnp.float32, 'ffn_w_in': _jnp.float32, 'ffn_w_out': _jnp.float32}
MOMENT_SCALE = {'mix_norm_pre': 4.265768e+00, 'mix_norm_post': 6.317299e+01, 'w_in': 2.442174e+00, 'gla_w_gate': 2.474912e-01, 'gla_b_gate': 9.078843e-01, 'gla_norm': 3.685873e+00, 'na_rpb': 3.310069e-01, 'lru_conv_w': 4.362431e+00, 'lru_conv_b': 8.901529e+01, 'lru_w_a': 9.700960e-01, 'lru_b_a': 8.678365e-01, 'lru_w_x': 1.924562e+00, 'lru_b_x': 1.082086e+00, 'lru_lambda': 1.587577e+00, 'w_out': 3.037116e+00, 'ffn_norm_pre': 2.806973e+00, 'ffn_norm_post': 6.337279e+01, 'ffn_w_in': 1.155153e+00, 'ffn_w_out': 2.082734e+00}


def _to_microbatches(a, axis):
    t = _jnp.moveaxis(a, axis, 0)
    t = t.reshape((N_MICROBATCH, t.shape[0] // N_MICROBATCH) + t.shape[1:])
    return _jnp.moveaxis(t, 1, axis + 1)


def setup_inputs(seed: int = 0) -> dict:
    inp = _fwd_setup_inputs(seed)
    key = _jax.random.fold_in(_jax.random.key(seed), 7919)
    shape, _ = _output_shape()
    out = dict(inp)
    out["loss_target"] = _jax.random.normal(_jax.random.fold_in(key, 0), shape, _jnp.float32)
    for i, name in enumerate(TWIN_WEIGHTS):
        w = inp[name].astype(_jnp.float32)
        if MOMENT_SCALE is None:
            s = _jnp.sqrt(_jnp.mean(_jnp.square(w)) + 1e-30)
        else:
            s = MOMENT_SCALE[name]
        km, kv = _jax.random.split(_jax.random.fold_in(key, i + 1))
        out[name] = w
        out["m_" + name] = s * _jax.random.normal(km, w.shape, _jnp.float32)
        out["v_" + name] = (s * s) * _jax.random.uniform(kv, w.shape, _jnp.float32, 0.5, 1.5)
    if N_MICROBATCH > 1:
        for name, axis in PER_EXAMPLE_BATCH_AXIS.items():
            out[name] = _to_microbatches(out[name], axis)
    return {'x': out['x'], 'mix_norm_pre': out['mix_norm_pre'], 'mix_norm_post': out['mix_norm_post'], 'w_in': out['w_in'], 'gla_w_gate': out['gla_w_gate'], 'gla_b_gate': out['gla_b_gate'], 'gla_norm': out['gla_norm'], 'na_rpb': out['na_rpb'], 'lru_conv_w': out['lru_conv_w'], 'lru_conv_b': out['lru_conv_b'], 'lru_w_a': out['lru_w_a'], 'lru_b_a': out['lru_b_a'], 'lru_w_x': out['lru_w_x'], 'lru_b_x': out['lru_b_x'], 'lru_lambda': out['lru_lambda'], 'w_out': out['w_out'], 'ffn_norm_pre': out['ffn_norm_pre'], 'ffn_norm_post': out['ffn_norm_post'], 'ffn_w_in': out['ffn_w_in'], 'ffn_w_out': out['ffn_w_out'], 'loss_target': out['loss_target'], 'm_mix_norm_pre': out['m_mix_norm_pre'], 'm_mix_norm_post': out['m_mix_norm_post'], 'm_w_in': out['m_w_in'], 'm_gla_w_gate': out['m_gla_w_gate'], 'm_gla_b_gate': out['m_gla_b_gate'], 'm_gla_norm': out['m_gla_norm'], 'm_na_rpb': out['m_na_rpb'], 'm_lru_conv_w': out['m_lru_conv_w'], 'm_lru_conv_b': out['m_lru_conv_b'], 'm_lru_w_a': out['m_lru_w_a'], 'm_lru_b_a': out['m_lru_b_a'], 'm_lru_w_x': out['m_lru_w_x'], 'm_lru_b_x': out['m_lru_b_x'], 'm_lru_lambda': out['m_lru_lambda'], 'm_w_out': out['m_w_out'], 'm_ffn_norm_pre': out['m_ffn_norm_pre'], 'm_ffn_norm_post': out['m_ffn_norm_post'], 'm_ffn_w_in': out['m_ffn_w_in'], 'm_ffn_w_out': out['m_ffn_w_out'], 'v_mix_norm_pre': out['v_mix_norm_pre'], 'v_mix_norm_post': out['v_mix_norm_post'], 'v_w_in': out['v_w_in'], 'v_gla_w_gate': out['v_gla_w_gate'], 'v_gla_b_gate': out['v_gla_b_gate'], 'v_gla_norm': out['v_gla_norm'], 'v_na_rpb': out['v_na_rpb'], 'v_lru_conv_w': out['v_lru_conv_w'], 'v_lru_conv_b': out['v_lru_conv_b'], 'v_lru_w_a': out['v_lru_w_a'], 'v_lru_b_a': out['v_lru_b_a'], 'v_lru_w_x': out['v_lru_w_x'], 'v_lru_b_x': out['v_lru_b_x'], 'v_lru_lambda': out['v_lru_lambda'], 'v_w_out': out['v_w_out'], 'v_ffn_norm_pre': out['v_ffn_norm_pre'], 'v_ffn_norm_post': out['v_ffn_norm_post'], 'v_ffn_w_in': out['v_ffn_w_in'], 'v_ffn_w_out': out['v_ffn_w_out']}


def _loss(weights, diff, rest, loss_target):
    with _jax.named_scope("forward"):
        args = {**rest, TWIN_DIFF_INPUT: diff, **{k: w.astype(_WEIGHT_DTYPES[k]) for k, w in weights.items()}}
        y = _forward(args)
    with _jax.named_scope("loss_head"):
        err = _jnp.square(y.astype(_jnp.float32) - loss_target)
        return 0.5 * _jnp.sum(_jnp.mean(err, axis=-1)) if err.ndim else 0.5 * err


def _adamw(w, g, m, v):
    m = ADAM_B1 * m + (1.0 - ADAM_B1) * g
    v = ADAM_B2 * v + (1.0 - ADAM_B2) * _jnp.square(g)
    m_hat = m / (1.0 - ADAM_B1 ** ADAM_STEP)
    v_hat = v / (1.0 - ADAM_B2 ** ADAM_STEP)
    delta = -ADAM_LR * (m_hat / (_jnp.sqrt(v_hat) + ADAM_EPS) + ADAM_WD * w)
    return delta, m, v


def reference(x, mix_norm_pre, mix_norm_post, w_in, gla_w_gate, gla_b_gate, gla_norm, na_rpb, lru_conv_w, lru_conv_b, lru_w_a, lru_b_a, lru_w_x, lru_b_x, lru_lambda, w_out, ffn_norm_pre, ffn_norm_post, ffn_w_in, ffn_w_out, loss_target, m_mix_norm_pre, m_mix_norm_post, m_w_in, m_gla_w_gate, m_gla_b_gate, m_gla_norm, m_na_rpb, m_lru_conv_w, m_lru_conv_b, m_lru_w_a, m_lru_b_a, m_lru_w_x, m_lru_b_x, m_lru_lambda, m_w_out, m_ffn_norm_pre, m_ffn_norm_post, m_ffn_w_in, m_ffn_w_out, v_mix_norm_pre, v_mix_norm_post, v_w_in, v_gla_w_gate, v_gla_b_gate, v_gla_norm, v_na_rpb, v_lru_conv_w, v_lru_conv_b, v_lru_w_a, v_lru_b_a, v_lru_w_x, v_lru_b_x, v_lru_lambda, v_w_out, v_ffn_norm_pre, v_ffn_norm_post, v_ffn_w_in, v_ffn_w_out):
    given = dict(x=x, mix_norm_pre=mix_norm_pre, mix_norm_post=mix_norm_post, w_in=w_in, gla_w_gate=gla_w_gate, gla_b_gate=gla_b_gate, gla_norm=gla_norm, na_rpb=na_rpb, lru_conv_w=lru_conv_w, lru_conv_b=lru_conv_b, lru_w_a=lru_w_a, lru_b_a=lru_b_a, lru_w_x=lru_w_x, lru_b_x=lru_b_x, lru_lambda=lru_lambda, w_out=w_out, ffn_norm_pre=ffn_norm_pre, ffn_norm_post=ffn_norm_post, ffn_w_in=ffn_w_in, ffn_w_out=ffn_w_out, loss_target=loss_target, m_mix_norm_pre=m_mix_norm_pre, m_mix_norm_post=m_mix_norm_post, m_w_in=m_w_in, m_gla_w_gate=m_gla_w_gate, m_gla_b_gate=m_gla_b_gate, m_gla_norm=m_gla_norm, m_na_rpb=m_na_rpb, m_lru_conv_w=m_lru_conv_w, m_lru_conv_b=m_lru_conv_b, m_lru_w_a=m_lru_w_a, m_lru_b_a=m_lru_b_a, m_lru_w_x=m_lru_w_x, m_lru_b_x=m_lru_b_x, m_lru_lambda=m_lru_lambda, m_w_out=m_w_out, m_ffn_norm_pre=m_ffn_norm_pre, m_ffn_norm_post=m_ffn_norm_post, m_ffn_w_in=m_ffn_w_in, m_ffn_w_out=m_ffn_w_out, v_mix_norm_pre=v_mix_norm_pre, v_mix_norm_post=v_mix_norm_post, v_w_in=v_w_in, v_gla_w_gate=v_gla_w_gate, v_gla_b_gate=v_gla_b_gate, v_gla_norm=v_gla_norm, v_na_rpb=v_na_rpb, v_lru_conv_w=v_lru_conv_w, v_lru_conv_b=v_lru_conv_b, v_lru_w_a=v_lru_w_a, v_lru_b_a=v_lru_b_a, v_lru_w_x=v_lru_w_x, v_lru_b_x=v_lru_b_x, v_lru_lambda=v_lru_lambda, v_w_out=v_w_out, v_ffn_norm_pre=v_ffn_norm_pre, v_ffn_norm_post=v_ffn_norm_post, v_ffn_w_in=v_ffn_w_in, v_ffn_w_out=v_ffn_w_out)
    weights = {n: given[n] for n in TWIN_WEIGHTS}
    shared = {n: given[n] for n in SHARED_INPUTS}
    per_example = {n: given[n] for n in ['x']}
    grad_fn = _jax.value_and_grad(_loss, argnums=(0, 1))

    def one_microbatch(ex, loss_target):
        ex = dict(ex)
        diff = ex.pop(TWIN_DIFF_INPUT)
        return grad_fn(weights, diff, {**shared, **ex}, loss_target)

    if N_MICROBATCH == 1:
        loss, (grad_w, grad_x) = one_microbatch(per_example, given["loss_target"])
    else:
        def body(carry, xs):
            loss_sum, grad_sum = carry
            l_k, (gw_k, gx_k) = one_microbatch(xs[0], xs[1])
            with _jax.named_scope("update"):
                return (loss_sum + l_k, _jax.tree.map(_jnp.add, grad_sum, gw_k)), gx_k

        init = (_jnp.zeros((), _jnp.float32), _jax.tree.map(_jnp.zeros_like, weights))
        (loss, grad_w), grad_x = _jax.lax.scan(body, init, (per_example, given["loss_target"]))
    with _jax.named_scope("update"):
        delta_w, new_m, new_v = {}, {}, {}
        for n in TWIN_WEIGHTS:
            delta_w[n], new_m[n], new_v[n] = _adamw(weights[n], grad_w[n], given["m_" + n], given["v_" + n])
    return (loss, grad_x, *[grad_w[n] for n in TWIN_WEIGHTS], *[delta_w[n] for n in TWIN_WEIGHTS],
            *[new_m[n] for n in TWIN_WEIGHTS], *[new_v[n] for n in TWIN_WEIGHTS])
```

```python
import functools

import numpy as np
import jax
import jax.numpy as jnp
from jax import lax
from jax.experimental import pallas as pl
from jax.experimental.pallas import tpu as pltpu

F32, BF16 = jnp.float32, jnp.bfloat16
HIGHEST = lax.Precision.HIGHEST
MESH = pl.DeviceIdType.MESH

HEAD_DIM = 64
GROUP_HEADS = 4
GROUP_WIDTH = GROUP_HEADS * HEAD_DIM
GLA_RANK = 16
GLA_TAU = 16.0
GLA_CHUNK = 64
GRID_W = 64
NA_ROWS = 8
NA_COLS = 16
LRU_CONV = 4
LRU_CONV_LEFT = 2
LRU_C = 8.0
DIL_PAIRS = ((128, 1), (512, 4), (2048, 16))
ROPE_THETA = 10000.0
EPS = 1e-6
ADAM_LR, ADAM_B1, ADAM_B2, ADAM_EPS, ADAM_WD, ADAM_STEP = 0.001, 0.9, 0.999, 1e-08, 0.01, 10
NEG = -1e30

LANE = 128
VMEM_LIMIT = 56 * 1024 * 1024
ROW_BUDGET = 10 * 1024 * 1024

P_QA, P_KA, P_VA, P_GA, P_QB, P_KB, P_VB, P_XC, P_GC, P_QD, P_KD, P_VD = range(12)
P_Z = 12 * GROUP_WIDTH
D_IN = 12 * GROUP_WIDTH + 2 * GLA_RANK
D_INP = 12 * GROUP_WIDTH + LANE

WEIGHTS = ['mix_norm_pre', 'mix_norm_post', 'w_in', 'gla_w_gate', 'gla_b_gate', 'gla_norm', 'na_rpb',
           'lru_conv_w', 'lru_conv_b', 'lru_w_a', 'lru_b_a', 'lru_w_x', 'lru_b_x', 'lru_lambda', 'w_out',
           'ffn_norm_pre', 'ffn_norm_post', 'ffn_w_in', 'ffn_w_out']
BIG = ('w_in', 'w_out', 'ffn_w_in', 'ffn_w_out')
SMALL_SHARDED = {'gla_w_gate': 3, 'gla_b_gate': 2, 'lru_conv_w': 2, 'lru_b_a': 2, 'lru_b_x': 2, 'lru_lambda': 2}


def _cparams(sem=None):
    return pltpu.CompilerParams(dimension_semantics=sem, vmem_limit_bytes=VMEM_LIMIT)


def _tile(dim, target, mult=LANE):
    best = None
    for t in range(mult, min(dim, target) + 1, mult):
        if dim % t == 0:
            best = t
    return best or dim


class Rows:
    def __init__(self, a, w=None, cb=0, lead=None):
        self.a, self.cb, self.lead = a, cb, lead
        self.w = a.shape[-1] if w is None else w
        self.nrows = a.shape[-2]

    def spec(self, tm):
        if self.lead is None:
            return pl.BlockSpec((tm, self.w), lambda i, cb=self.cb: (i, cb))
        return pl.BlockSpec((None, tm, self.w), lambda i, cb=self.cb, k=self.lead: (k, i, cb))

    def nbytes(self):
        return self.w * self.a.dtype.itemsize


def _as_rows(rs):
    return [r if isinstance(r, Rows) else Rows(r) for r in rs]


def _pick_tm(nrows, row_bytes, scale=1):
    tm = 512
    while tm > 8 and (tm * row_bytes * scale > ROW_BUDGET or nrows % tm):
        tm //= 2
    return tm


def _full_spec(a):
    nd = a.ndim
    return pl.BlockSpec(a.shape, lambda i, nd=nd: (0,) * nd)


def _rowwise(name, fn, rows, params, outs, acc_outs=()):
    rows = _as_rows(rows)
    nrows = rows[0].nrows
    tm = _pick_tm(nrows, sum(r.nbytes() for r in rows) + sum(w * jnp.dtype(d).itemsize for w, d in outs), 2)
    n_r, n_p, n_o = len(rows), len(params), len(outs)

    def body(*refs):
        vals = [r[...] for r in refs[:n_r + n_p]]
        res = fn(*vals)
        res = res if isinstance(res, (tuple, list)) else (res,)
        orefs = refs[n_r + n_p:]
        for o, v in zip(orefs[:n_o], res[:n_o]):
            o[...] = v.astype(o.dtype)
        for o, v in zip(orefs[n_o:], res[n_o:]):
            @pl.when(pl.program_id(0) == 0)
            def _(o=o):
                o[...] = jnp.zeros_like(o)
            o[...] += v

    out_shape = [jax.ShapeDtypeStruct((nrows, w), d) for w, d in outs] + [jax.ShapeDtypeStruct(s, F32) for s in acc_outs]
    out_specs = [pl.BlockSpec((tm, w), lambda i: (i, 0)) for w, _ in outs] + \
                [pl.BlockSpec(s, lambda i, nd=len(s): (0,) * nd) for s in acc_outs]
    res = pl.pallas_call(
        body, name=name, grid=(nrows // tm,),
        in_specs=[r.spec(tm) for r in rows] + [_full_spec(p) for p in params],
        out_specs=out_specs, out_shape=out_shape,
        compiler_params=_cparams(("arbitrary",) if acc_outs else ("parallel",)),
    )(*[r.a for r in rows], *params)
    return res


def _rowwise_bwd(name, fn, rows, params, ct_rows, ct_fn, row_grads, param_grads):
    rows, ct_rows = _as_rows(rows), _as_rows(ct_rows)
    nrows = rows[0].nrows
    gbytes = sum(r.w * jnp.dtype(d).itemsize for r, d in zip(rows, row_grads) if d is not None)
    tm = _pick_tm(nrows, sum(r.nbytes() for r in rows + ct_rows) + gbytes, 4)
    n_r, n_p, n_c = len(rows), len(params), len(ct_rows)
    diff = [k for k, d in enumerate(row_grads) if d is not None] + [n_r + k for k, g in enumerate(param_grads) if g]
    n_rg = sum(d is not None for d in row_grads)

    def body(*refs):
        vals = [r[...] for r in refs[:n_r + n_p]]
        cts_in = [r[...] for r in refs[n_r + n_p:n_r + n_p + n_c]]
        orefs = refs[n_r + n_p + n_c:]

        def f(*dv):
            full = list(vals)
            for k, v in zip(diff, dv):
                full[k] = v
            res = fn(*full)
            return tuple(res) if isinstance(res, (tuple, list)) else (res,)

        outs, vjp = jax.vjp(f, *[vals[k].astype(F32) for k in diff])
        cts = ct_fn(*cts_in)
        cts = cts if isinstance(cts, (tuple, list)) else (cts,)
        grads = vjp(tuple(c.astype(o.dtype) for c, o in zip(cts, outs)))
        for o, g in zip(orefs[:n_rg], grads[:n_rg]):
            o[...] = g.astype(o.dtype)
        for o, g in zip(orefs[n_rg:], grads[n_rg:]):
            @pl.when(pl.program_id(0) == 0)
            def _(o=o):
                o[...] = jnp.zeros_like(o)
            o[...] += g.astype(F32)

    out_shape = [jax.ShapeDtypeStruct((nrows, r.w), d) for r, d in zip(rows, row_grads) if d is not None] + \
                [jax.ShapeDtypeStruct(p.shape, F32) for p, g in zip(params, param_grads) if g]
    out_specs = [pl.BlockSpec((tm, r.w), lambda i: (i, 0)) for r, d in zip(rows, row_grads) if d is not None] + \
                [_full_spec(p) for p, g in zip(params, param_grads) if g]
    return pl.pallas_call(
        body, name=name, grid=(nrows // tm,),
        in_specs=[r.spec(tm) for r in rows] + [_full_spec(p) for p in params] + [r.spec(tm) for r in ct_rows],
        out_specs=out_specs, out_shape=out_shape,
        compiler_params=_cparams(("arbitrary",)),
    )(*[r.a for r in rows], *params, *[r.a for r in ct_rows])


def _assemble(name, groups, dtype):
    sizes = [len(g) for g in groups]
    flat = [a for g in groups for a in g]

    def fn(*tiles):
        out, k = [], 0
        for s in sizes:
            acc = tiles[k].astype(F32)
            for t in tiles[k + 1:k + s]:
                acc = acc + t.astype(F32)
            out.append(acc.astype(dtype))
            k += s
        return out[0] if len(out) == 1 else jnp.concatenate(out, axis=1)

    width = sum(g[0].shape[-1] if not isinstance(g[0], Rows) else g[0].w for g in groups)
    return _rowwise(name, fn, flat, [], [(width, dtype)])[0]


def _matmul(name, a, b, mode, out_dtype):
    if mode == 'nn':
        (M, K), N = a.shape, b.shape[1]
    elif mode == 'nt':
        (M, K), N = a.shape, b.shape[0]
    else:
        (K, M), N = a.shape, b.shape[1]
    tm, tn, tk = _tile(M, 512), _tile(N, 1024), _tile(K, 1024)
    nk = K // tk
    dn = {'nn': (((1,), (0,)), ((), ())), 'nt': (((1,), (1,)), ((), ())), 'tn': (((0,), (0,)), ((), ()))}[mode]

    def body(a_ref, b_ref, o_ref, acc):
        @pl.when(pl.program_id(2) == 0)
        def _():
            acc[...] = jnp.zeros_like(acc)
        acc[...] += lax.dot_general(a_ref[...].astype(BF16), b_ref[...].astype(BF16), dn, preferred_element_type=F32)

        @pl.when(pl.program_id(2) == nk - 1)
        def _():
            o_ref[...] = acc[...].astype(o_ref.dtype)

    a_spec = pl.BlockSpec((tk, tm), lambda i, j, k: (k, i)) if mode == 'tn' else pl.BlockSpec((tm, tk), lambda i, j, k: (i, k))
    b_spec = pl.BlockSpec((tn, tk), lambda i, j, k: (j, k)) if mode == 'nt' else pl.BlockSpec((tk, tn), lambda i, j, k: (k, j))
    return pl.pallas_call(
        body, name=name, grid=(M // tm, N // tn, nk),
        in_specs=[a_spec, b_spec], out_specs=pl.BlockSpec((tm, tn), lambda i, j, k: (i, j)),
        out_shape=jax.ShapeDtypeStruct((M, N), out_dtype),
        scratch_shapes=[pltpu.VMEM((tm, tn), F32)],
        compiler_params=_cparams(("parallel", "parallel", "arbitrary")),
    )(a, b)


def _small_dot(name, a, b, mode):
    dn = {'nn': (((1,), (0,)), ((), ())), 'nt': (((1,), (1,)), ((), ()))}[mode]
    M = a.shape[0]
    N = b.shape[1] if mode == 'nn' else b.shape[0]

    def body(a_ref, b_ref, o_ref):
        o_ref[...] = lax.dot_general(a_ref[...], b_ref[...], dn, precision=HIGHEST, preferred_element_type=F32)

    return pl.pallas_call(body, name=name, out_shape=jax.ShapeDtypeStruct((M, N), F32),
                          compiler_params=pltpu.CompilerParams(vmem_limit_bytes=VMEM_LIMIT))(a, b)


def _bdot(a, b, dn=(((1,), (0,)), ((), ()))):
    return lax.dot_general(a.astype(BF16), b.astype(BF16), dn, preferred_element_type=F32)


def _sigmoid(x):
    return 1.0 / (1.0 + jnp.exp(-x))


def _silu(x):
    return x * _sigmoid(x)


def _softplus(x):
    return jnp.maximum(x, 0.0) + jnp.log(1.0 + jnp.exp(-jnp.abs(x)))


def _gelu(x):
    return 0.5 * x * (1.0 + jnp.tanh(0.7978845608028654 * (x + 0.044715 * (x * x * x))))


def _rms(x, g):
    return x * lax.rsqrt(jnp.mean(x * x, axis=-1, keepdims=True) + EPS) * g


def _prenorm_fn(x, g):
    return _rms(x, g)


def _postnorm_fn(x, y, g):
    return x + _rms(y, g)


def _swiglu_fn(gate, up):
    return _silu(gate) * up


def _gla_pre_fn(z, wg, bg):
    logit = _bdot(z, wg) + bg
    return -_softplus(-logit) * (1.0 / GLA_TAU)


def _seg_mean(x, seg):
    return lax.dot_general(x, seg, (((1,), (0,)), ((), ())), precision=HIGHEST, preferred_element_type=F32)


def _gla_post_fn(of, ob, g, norm, seg):
    o = of + ob
    o = o * lax.rsqrt(_seg_mean(o * o, seg) + EPS) * norm
    return o * _silu(g)


@jax.custom_vjp
def _swap_halves(x):
    n = x.shape[-1]
    lane = lax.broadcasted_iota(jnp.int32, x.shape, x.ndim - 1)
    lo = (lane & (HEAD_DIM - 1)) < HEAD_DIM // 2
    return jnp.where(lo, pltpu.roll(x, n - HEAD_DIM // 2, x.ndim - 1), pltpu.roll(x, HEAD_DIM // 2, x.ndim - 1))


_swap_halves.defvjp(lambda x: (_swap_halves(x), None), lambda _, g: (_swap_halves(g),))


def _rope_fn(q, k, cs, sn):
    return q * cs + _swap_halves(q) * sn, k * cs + _swap_halves(k) * sn


def _dil_comb_fn(o1, o2, o3, l1, l2, l3):
    m = jnp.maximum(jnp.maximum(l1, l2), l3)
    e1, e2, e3 = jnp.exp(l1 - m), jnp.exp(l2 - m), jnp.exp(l3 - m)
    return (e1 * o1 + e2 * o2 + e3 * o3) / (e1 + e2 + e3)


LRU_PARAMS = ['lru_cw0', 'lru_cw1', 'lru_cw2', 'lru_cw3', 'lru_cb', 'lru_wa0', 'lru_wa1', 'lru_ba0', 'lru_ba1',
              'lru_wx0', 'lru_wx1', 'lru_bx0', 'lru_bx1', 'lru_lam0', 'lru_lam1']


def _lru_pre_fn(x0, x1, x2, x3, cw0, cw1, cw2, cw3, cb, wa0, wa1, ba0, ba1, wx0, wx1, bx0, bx1, lam0, lam1):
    xc = cb + x0 * cw0 + x1 * cw1 + x2 * cw2 + x3 * cw3
    outs = []
    for wa, ba, wx, bx, lam in ((wa0, ba0, wx0, bx0, lam0), (wa1, ba1, wx1, bx1, lam1)):
        r = _sigmoid(_bdot(xc, wa) + ba)
        i = _sigmoid(_bdot(xc, wx) + bx)
        log_a = -LRU_C * r * _softplus(-lam)
        a = jnp.exp(log_a)
        u = jnp.sqrt(-jnp.tanh(log_a) * (a * a + 1.0)) * (i * xc)
        outs += [a, u]
    return outs[0], outs[2], outs[1], outs[3]


def _lru_post_fn(hf, hb, gate):
    return (hf + hb) * _gelu(gate)


def _attn_tile(q, kw, vw, bias):
    s = _bdot(q, kw, (((1,), (1,)), ((), ()))) * (HEAD_DIM ** -0.5) + bias
    m = lax.stop_gradient(jnp.max(s, axis=-1, keepdims=True))
    e = jnp.exp(s - m)
    den = jnp.sum(e, axis=-1, keepdims=True)
    o = _bdot(e / den, vw)
    return o, m + jnp.log(den)


def _gla_chunk(q, k, v, la, st, rev):
    H, C, _ = q.shape
    ti = lax.broadcasted_iota(jnp.int32, (C, C), 0)
    si = lax.broadcasted_iota(jnp.int32, (C, C), 1)
    incl = (si >= ti) if rev else (si <= ti)
    tri = jnp.broadcast_to(incl.astype(F32)[None], (H, C, C))
    b = jnp.einsum('hts,hsk->htk', tri, la, precision=HIGHEST, preferred_element_type=F32)
    row = lax.broadcasted_iota(jnp.int32, (1, C, 1), 1)
    mid = (row >= C // 2) if rev else (row < C // 2)
    b_last = jnp.sum(la, axis=1, keepdims=True)
    b_mid = jnp.sum(jnp.where(mid, la, 0.0), axis=1, keepdims=True)
    qs = q * (HEAD_DIM ** -0.5)
    att = jnp.einsum('hck,hsk->hcs', (qs * jnp.exp(b - b_mid)).astype(BF16), (k * jnp.exp(b_mid - b)).astype(BF16),
                     preferred_element_type=F32)
    att = jnp.where(incl[None], att, 0.0)
    o = jnp.einsum('hcs,hsv->hcv', att.astype(BF16), v.astype(BF16), preferred_element_type=F32)
    o = o + jnp.einsum('hck,hvk->hcv', (qs * jnp.exp(b)).astype(BF16), st.astype(BF16), preferred_element_type=F32)
    vt = jnp.swapaxes(v, 1, 2)
    kv = jnp.einsum('hvc,hck->hvk', vt.astype(BF16), (k * jnp.exp(b_last - b)).astype(BF16), preferred_element_type=F32)
    return o, st * jnp.exp(b_last) + kv


def _gla_scan_fwd(name, q, k, v, la, rev):
    H, L, dh = q.shape
    C = GLA_CHUNK
    n = L // C
    cidx = (lambda i: n - 1 - i) if rev else (lambda i: i)

    def body(q_ref, k_ref, v_ref, la_ref, o_ref, sp_ref, st):
        @pl.when(pl.program_id(0) == 0)
        def _():
            st[...] = jnp.zeros_like(st)
        sp_ref[...] = st[...]
        o, st_new = _gla_chunk(q_ref[...], k_ref[...], v_ref[...], la_ref[...], st[...], rev)
        o_ref[...] = o
        st[...] = st_new

    blk = pl.BlockSpec((H, C, dh), lambda i: (0, cidx(i), 0))
    return pl.pallas_call(
        body, name=name, grid=(n,), in_specs=[blk] * 4,
        out_specs=[blk, pl.BlockSpec((None, H, dh, dh), lambda i: (cidx(i), 0, 0, 0))],
        out_shape=[jax.ShapeDtypeStruct((H, L, dh), F32), jax.ShapeDtypeStruct((n, H, dh, dh), F32)],
        scratch_shapes=[pltpu.VMEM((H, dh, dh), F32)],
        compiler_params=_cparams(("arbitrary",)),
    )(q, k, v, la)


def _gla_scan_bwd(name, q, k, v, la, sp, do, rev):
    H, L, dh = q.shape
    C = GLA_CHUNK
    n = L // C
    cidx = (lambda i: i) if rev else (lambda i: n - 1 - i)

    def body(q_ref, k_ref, v_ref, la_ref, sp_ref, do_ref, dq_ref, dk_ref, dv_ref, dla_ref, dst):
        @pl.when(pl.program_id(0) == 0)
        def _():
            dst[...] = jnp.zeros_like(dst)
        _, vjp = jax.vjp(functools.partial(_gla_chunk, rev=rev), q_ref[...], k_ref[...], v_ref[...], la_ref[...], sp_ref[...])
        dq, dk, dv, dla, dsp = vjp((do_ref[...], dst[...]))
        dq_ref[...], dk_ref[...], dv_ref[...], dla_ref[...] = dq, dk, dv, dla
        dst[...] = dsp

    blk = pl.BlockSpec((H, C, dh), lambda i: (0, cidx(i), 0))
    return pl.pallas_call(
        body, name=name, grid=(n,),
        in_specs=[blk] * 4 + [pl.BlockSpec((None, H, dh, dh), lambda i: (cidx(i), 0, 0, 0)), blk],
        out_specs=[blk] * 4, out_shape=[jax.ShapeDtypeStruct((H, L, dh), F32)] * 4,
        scratch_shapes=[pltpu.VMEM((H, dh, dh), F32)],
        compiler_params=_cparams(("arbitrary",)),
    )(q, k, v, la, sp, do)


def _band_valid(i, tq, w, halo, n):
    a = lax.broadcasted_iota(jnp.int32, (tq, w), 0)
    b = lax.broadcasted_iota(jnp.int32, (tq, w), 1)
    kpos = i * tq - halo + b
    d = b - halo - a
    return (d <= halo) & (d >= -halo) & (kpos >= 0) & (kpos < n)


def _na_valid(i, tq, w, halo, n):
    rows = n // GRID_W
    qc = lax.broadcasted_iota(jnp.int32, (tq, w), 0)
    b = lax.broadcasted_iota(jnp.int32, (tq, w), 1)
    krow = i + (b >> 6) - (NA_ROWS - 1)
    kc = b & (GRID_W - 1)
    c0 = jnp.clip(qc - NA_COLS // 2, 0, GRID_W - NA_COLS)
    r0 = jnp.clip(i - NA_ROWS // 2, 0, rows - NA_ROWS)
    return (kc >= c0) & (kc < c0 + NA_COLS) & (krow >= r0) & (krow < r0 + NA_ROWS)


def _local_attn_fwd(name, q, kp, vp, bias, tq, halo, valid_fn, want_lse):
    G, n, dh = q.shape
    w = tq + 2 * halo
    nb = 0 if bias is None else 1

    def body(*refs):
        q_ref, k_ref, v_ref = refs[:3]
        i = pl.program_id(1)
        start = pl.multiple_of(i * tq, tq)
        valid = valid_fn(i, tq, w, halo, n)
        bt = jnp.where(valid, refs[3][...] if nb else 0.0, NEG)
        o, lse = _attn_tile(q_ref[...], k_ref[pl.ds(start, w), :], v_ref[pl.ds(start, w), :], bt)
        refs[3 + nb][...] = o
        if want_lse:
            refs[4 + nb][...] = jnp.broadcast_to(lse, (tq, dh))

    qblk = pl.BlockSpec((None, tq, dh), lambda g, i: (g, i, 0))
    kblk = pl.BlockSpec((None, n + 2 * halo, dh), lambda g, i: (g, 0, 0))
    in_specs = [qblk, kblk, kblk] + ([pl.BlockSpec((None, tq, w), lambda g, i: (g, 0, 0))] if nb else [])
    n_out = 2 if want_lse else 1
    return pl.pallas_call(
        body, name=name, grid=(G, n // tq), in_specs=in_specs, out_specs=[qblk] * n_out,
        out_shape=[jax.ShapeDtypeStruct((G, n, dh), F32)] * n_out,
        compiler_params=_cparams(("parallel", "arbitrary")),
    )(q, kp, vp, *([bias] if nb else []))


def _local_attn_bwd(name, q, kp, vp, bias, do, dlse, tq, halo, valid_fn):
    G, n, dh = q.shape
    w = tq + 2 * halo
    nb = 0 if bias is None else 1
    nl = 0 if dlse is None else 1

    def body(*refs):
        q_ref, k_ref, v_ref = refs[:3]
        b_ref = refs[3] if nb else None
        do_ref = refs[3 + nb]
        dl_ref = refs[4 + nb] if nl else None
        dq_ref, dk_ref, dv_ref = refs[4 + nb + nl:7 + nb + nl]
        db_ref = refs[7 + nb + nl] if nb else None
        i = pl.program_id(1)
        start = pl.multiple_of(i * tq, tq)
        valid = valid_fn(i, tq, w, halo, n)

        @pl.when(i == 0)
        def _():
            dk_ref[...] = jnp.zeros_like(dk_ref)
            dv_ref[...] = jnp.zeros_like(dv_ref)
            if nb:
                db_ref[...] = jnp.zeros_like(db_ref)

        qv = q_ref[...].astype(F32)
        kw = k_ref[pl.ds(start, w), :].astype(F32)
        vw = v_ref[pl.ds(start, w), :].astype(F32)
        dl = jnp.sum(dl_ref[...], axis=1, keepdims=True) if nl else jnp.zeros((tq, 1), F32)
        if nb:
            _, vjp = jax.vjp(lambda a, b, c, t: _attn_tile(a, b, c, jnp.where(valid, t, NEG)), qv, kw, vw, b_ref[...])
            dq, dkw, dvw, db = vjp((do_ref[...], dl))
            db_ref[...] += db
        else:
            _, vjp = jax.vjp(lambda a, b, c: _attn_tile(a, b, c, jnp.where(valid, 0.0, NEG)), qv, kw, vw)
            dq, dkw, dvw = vjp((do_ref[...], dl))
        dq_ref[...] = dq
        dk_ref[pl.ds(start, w), :] += dkw
        dv_ref[pl.ds(start, w), :] += dvw

    qblk = pl.BlockSpec((None, tq, dh), lambda g, i: (g, i, 0))
    kblk = pl.BlockSpec((None, n + 2 * halo, dh), lambda g, i: (g, 0, 0))
    bblk = pl.BlockSpec((None, tq, w), lambda g, i: (g, 0, 0))
    in_specs = [qblk, kblk, kblk] + ([bblk] if nb else []) + [qblk] + ([qblk] if nl else [])
    out_shape = [jax.ShapeDtypeStruct((G, n, dh), F32)] + [jax.ShapeDtypeStruct((G, n + 2 * halo, dh), F32)] * 2
    out_specs = [qblk, kblk, kblk]
    if nb:
        out_shape.append(jax.ShapeDtypeStruct(bias.shape, F32))
        out_specs.append(bblk)
    return pl.pallas_call(
        body, name=name, grid=(G, n // tq), in_specs=in_specs, out_specs=out_specs, out_shape=out_shape,
        compiler_params=_cparams(("parallel", "arbitrary")),
    )(q, kp, vp, *([bias] if nb else []), do, *([dlse] if nl else []))


def _lin_scan(name, coef, inp, rev):
    L, C = coef.shape
    tt = 256 if L % 256 == 0 else L
    nt = L // tt
    tidx = (lambda i: (nt - 1 - i, 0)) if rev else (lambda i: (i, 0))

    def body(a_ref, u_ref, o_ref, carry):
        @pl.when(pl.program_id(0) == 0)
        def _():
            carry[...] = jnp.zeros_like(carry)
        a, u = a_ref[...], u_ref[...]
        row = lax.broadcasted_iota(jnp.int32, (tt, C), 0)
        s = 1
        while s < tt:
            ok = (row < tt - s) if rev else (row >= s)
            sh = tt - s if rev else s
            u = u + a * jnp.where(ok, pltpu.roll(u, sh, 0), 0.0)
            a = a * jnp.where(ok, pltpu.roll(a, sh, 0), 1.0)
            s *= 2
        out = u + a * carry[...]
        o_ref[...] = out
        carry[...] = out[0:1] if rev else out[tt - 1:tt]

    blk = pl.BlockSpec((tt, C), tidx)
    return pl.pallas_call(
        body, name=name, grid=(nt,), in_specs=[blk, blk], out_specs=blk,
        out_shape=jax.ShapeDtypeStruct((L, C), F32), scratch_shapes=[pltpu.VMEM((1, C), F32)],
        compiler_params=_cparams(("arbitrary",)),
    )(coef, inp)


def _exchange(name, src, axes, gather):
    flips = {'xy': [(1, 0, 0), (0, 1, 0), (1, 1, 0)], 'c': [(0, 0, 1)],
             'xyc': [(fx, fy, fc) for fx in (0, 1) for fy in (0, 1) for fc in (0, 1)][1:]}[axes]
    n = len(flips) + 1
    blk_shape = src.shape if gather else src.shape[1:]

    def number(px, py, pc):
        return {'xy': 2 * px + py, 'c': pc, 'xyc': 4 * px + 2 * py + pc}[axes]

    def body(src_ref, out_ref, send_sems, recv_sems, loc_sem):
        x, y, c = lax.axis_index("x"), lax.axis_index("y"), lax.axis_index("c")
        me = number(x, y, c)
        piece = (lambda k: src_ref) if gather else (lambda k: src_ref.at[k])
        loc = pltpu.make_async_copy(piece(me), out_ref.at[me], loc_sem)
        loc.start()
        copies = []
        for s, (fx, fy, fc) in enumerate(flips):
            px, py, pc = (x + fx) % 2, (y + fy) % 2, (c + fc) % 2
            cp = pltpu.make_async_remote_copy(
                src_ref=piece(number(px, py, pc)), dst_ref=out_ref.at[me],
                send_sem=send_sems.at[s], recv_sem=recv_sems.at[s],
                device_id=(px, py, pc), device_id_type=MESH)
            cp.start()
            copies.append(cp)
        for cp in copies:
            cp.wait()
        loc.wait()

    return pl.pallas_call(
        body, name=name, out_shape=jax.ShapeDtypeStruct((n,) + tuple(blk_shape), src.dtype),
        in_specs=[pl.BlockSpec(memory_space=pl.ANY)], out_specs=pl.BlockSpec(memory_space=pl.ANY),
        scratch_shapes=[pltpu.SemaphoreType.DMA((n - 1,)), pltpu.SemaphoreType.DMA((n - 1,)), pltpu.SemaphoreType.DMA],
    )(src)


def _ordered_sum(name, buf):
    n = buf.shape[0]

    def fn(*t):
        acc = t[0]
        for v in t[1:]:
            acc = acc + v
        return acc

    return _rowwise(name, fn, [Rows(buf, lead=k) for k in range(n)], [], [(buf.shape[-1], F32)])[0]


def _reduce_big(name, g):
    mine = _ordered_sum(name + "_sum_c", _exchange(name + "_swap_c", g, 'c', False).reshape(2, -1, g.shape[-1]))
    mine = mine.reshape(g.shape[1:])
    tot = _ordered_sum(name + "_sum_xy", _exchange(name + "_a2a_xy", mine, 'xy', False))
    return _exchange(name + "_share_c", tot, 'c', True)


def _to_heads(t):
    return t.reshape(t.shape[0], GROUP_HEADS, HEAD_DIM).transpose(1, 0, 2)


def _from_heads(t):
    return t.transpose(1, 0, 2).reshape(t.shape[1], GROUP_WIDTH)


def _dilate(t, dil):
    H, L, dh = t.shape
    return t.reshape(H, L // dil, dil, dh).transpose(0, 2, 1, 3).reshape(H * dil, L // dil, dh)


def _undilate(t, dil):
    G, n, dh = t.shape
    return t.reshape(G // dil, dil, n, dh).transpose(0, 2, 1, 3).reshape(G // dil, n * dil, dh)


def _pad_rows(t, halo):
    return jnp.pad(t, ((0, 0), (halo, halo), (0, 0)))


def _pcol(p, blk):
    return Rows(p, GROUP_WIDTH, blk)


def _pslice(p, blk):
    return p[:, blk * GROUP_WIDTH:(blk + 1) * GROUP_WIDTH]


def _rpb_tables():
    c = np.arange(GRID_W)
    dc = np.clip(c[None, :] - c[:, None], -(NA_COLS - 1), NA_COLS - 1) + NA_COLS - 1
    t = np.zeros((2 * NA_COLS - 1, GRID_W, GRID_W), np.float32)
    t[dc, c[:, None], c[None, :]] = 1.0
    return jnp.asarray(t.reshape(2 * NA_COLS - 1, GRID_W * GRID_W))


def _seg_matrix():
    h = np.arange(GROUP_WIDTH) // HEAD_DIM
    return jnp.asarray((h[:, None] == h[None, :]).astype(np.float32) / HEAD_DIM)


def _rope_tables(L):
    pos = jnp.arange(L, dtype=F32)
    inv_freq = ROPE_THETA ** (-jnp.arange(0, HEAD_DIM, 2, dtype=F32) / HEAD_DIM)
    ang = pos[:, None] * inv_freq[None, :]
    cos, sin = jnp.cos(ang), jnp.sin(ang)
    cs = jnp.tile(jnp.concatenate([cos, cos], axis=1), (1, GROUP_HEADS))
    sn = jnp.tile(jnp.concatenate([-sin, sin], axis=1), (1, GROUP_HEADS))
    return cs, sn


NA_HALO = (NA_ROWS - 1) * GRID_W
NA_W = GRID_W + 2 * NA_HALO


def _rpb_expand(rpb, tab):
    H = rpb.shape[0]
    xt = _small_dot("na_bias", rpb.reshape(H * (2 * NA_ROWS - 1), 2 * NA_COLS - 1), tab, 'nn')
    return xt.reshape(H, 2 * NA_ROWS - 1, GRID_W, GRID_W).transpose(0, 2, 1, 3).reshape(H, GRID_W, NA_W)


def _rpb_contract(db, tab):
    H = db.shape[0]
    dx = db.reshape(H, GRID_W, 2 * NA_ROWS - 1, GRID_W).transpose(0, 2, 1, 3).reshape(H * (2 * NA_ROWS - 1), GRID_W * GRID_W)
    return _small_dot("na_bias_b", dx, tab, 'nt').reshape(H, 2 * NA_ROWS - 1, 2 * NA_COLS - 1)


def _shift_rows(t, k):
    if k == 0:
        return t
    z = jnp.zeros((abs(k), t.shape[1]), t.dtype)
    return jnp.concatenate([z, t[:-k]], axis=0) if k > 0 else jnp.concatenate([t[-k:], z], axis=0)


def _dil_branches(L):
    out = []
    for window, dil in DIL_PAIRS:
        radius = window // (2 * dil)
        n = L // dil
        out.append((dil, radius, min(256, n)))
    return out


def _layer_fwd(x, w, c):
    L = x.shape[0]
    sv = {'x_in': x}
    h = _rowwise("mix_prenorm", _prenorm_fn, [x], [w['mix_norm_pre']], [(x.shape[1], BF16)])[0]
    p = _matmul("mix_proj", h, w['w_in'], 'nn', F32)
    sv['p'] = p

    la = _rowwise("gla_pre", _gla_pre_fn, [Rows(p, LANE, P_Z // LANE)], [w['gla_wg'], w['gla_bg']], [(2 * GROUP_WIDTH, F32)])[0]
    qa, ka, va = (_to_heads(_pslice(p, b)) for b in (P_QA, P_KA, P_VA))
    la_f, la_b = _to_heads(la[:, :GROUP_WIDTH]), _to_heads(la[:, GROUP_WIDTH:])
    of, sf = _gla_scan_fwd("gla_scan_f", qa, ka, va, la_f, False)
    ob, sb = _gla_scan_fwd("gla_scan_b", qa, ka, va, la_b, True)
    of, ob = _from_heads(of), _from_heads(ob)
    ya = _rowwise("gla_post", _gla_post_fn, [of, ob, _pcol(p, P_GA)], [w['gla_norm'], c['seg']], [(GROUP_WIDTH, BF16)])[0]
    sv.update(la=la, sf=sf, sb=sb, of=of, ob=ob)

    qb, kb, vb = (_to_heads(_pslice(p, b)).astype(BF16) for b in (P_QB, P_KB, P_VB))
    btab = _rpb_expand(w['na_rpb'], c['rpb_tab'])
    yb = _local_attn_fwd("na_attn", qb, _pad_rows(kb, NA_HALO), _pad_rows(vb, NA_HALO), btab, GRID_W, NA_HALO, _na_valid, False)[0]
    yb = _from_heads(yb)

    xb = _pslice(p, P_XC)
    xs = [_shift_rows(xb, LRU_CONV_LEFT - j) for j in range(LRU_CONV)]
    lru_params = [w[k] for k in LRU_PARAMS]
    a0, a1, u0, u1 = _rowwise("lru_pre", _lru_pre_fn, xs, lru_params, [(GROUP_WIDTH, F32)] * 4)
    hf = _lin_scan("lru_scan_f", a0, u0, False)
    hb = _lin_scan("lru_scan_b", a1, u1, True)
    yc = _rowwise("lru_post", _lru_post_fn, [hf, hb, _pcol(p, P_GC)], [], [(GROUP_WIDTH, BF16)])[0]
    sv.update(a0=a0, a1=a1, hf=hf, hb=hb)

    qr, kr = _rowwise("rope", _rope_fn, [_pcol(p, P_QD), _pcol(p, P_KD), c['cos'], c['sin']], [], [(GROUP_WIDTH, F32)] * 2)
    qh, kh, vh = _to_heads(qr).astype(BF16), _to_heads(kr).astype(BF16), _to_heads(_pslice(p, P_VD)).astype(BF16)
    os_, ls_ = [], []
    for dil, radius, tq in _dil_branches(L):
        o, lse = _local_attn_fwd(f"dil_attn{dil}", _dilate(qh, dil), _pad_rows(_dilate(kh, dil), radius),
                                 _pad_rows(_dilate(vh, dil), radius), None, tq, radius, _band_valid, True)
        os_.append(_from_heads(_undilate(o, dil)))
        ls_.append(_from_heads(_undilate(lse, dil)))
    yd = _rowwise("dil_comb", _dil_comb_fn, os_ + ls_, [], [(GROUP_WIDTH, BF16)])[0]
    sv.update(qr=qr, kr=kr, dil_o=os_, dil_l=ls_)

    ycat = _assemble("mix_cat", [[ya], [yb], [yc], [yd]], BF16)
    y = _matmul("mix_out", ycat, w['w_out'], 'nn', F32)
    xm = _rowwise("mix_postnorm", _postnorm_fn, [x, y], [w['mix_norm_post']], [(x.shape[1], F32)])[0]
    sv.update(ycat=ycat, y=y, x_mid=xm)

    h2 = _rowwise("ffn_prenorm", _prenorm_fn, [xm], [w['ffn_norm_pre']], [(x.shape[1], BF16)])[0]
    gu = _matmul("ffn_in", h2, w['ffn_w_in'], 'nn', F32)
    dff = gu.shape[1] // 2
    act = _rowwise("ffn_act", _swiglu_fn, [Rows(gu, dff, 0), Rows(gu, dff, 1)], [], [(dff, BF16)])[0]
    f = _matmul("ffn_out", act, w['ffn_w_out'], 'nn', F32)
    xo = _rowwise("ffn_postnorm", _postnorm_fn, [xm, f], [w['ffn_norm_post']], [(x.shape[1], F32)])[0]
    sv.update(gu=gu, act=act, f=f)
    return xo, sv


def _layer_bwd(dx, w, c, sv):
    L, D = dx.shape
    g = {}
    as_f32 = lambda t: (t.astype(F32),)
    df, g['ffn_norm_post'] = _rowwise_bwd("ffn_postnorm_b", lambda y, gn: _rms(y, gn), [sv['f']], [w['ffn_norm_post']],
                                          [dx], as_f32, [BF16], [True])
    dact = _matmul("ffn_out_bx", df, w['ffn_w_out'], 'nt', F32)
    g['ffn_w_out'] = _matmul("ffn_out_bw", sv['act'], df, 'tn', F32)
    gu = sv['gu']
    dff = gu.shape[1] // 2
    dgate, dup = _rowwise_bwd("ffn_act_b", _swiglu_fn, [Rows(gu, dff, 0), Rows(gu, dff, 1)], [], [dact], as_f32, [BF16, BF16], [])
    dgu = _assemble("ffn_dgu", [[dgate], [dup]], BF16)
    dh2 = _matmul("ffn_in_bx", dgu, w['ffn_w_in'], 'nt', F32)
    xm = sv['x_mid']
    h2 = _rowwise("ffn_prenorm_r", _prenorm_fn, [xm], [w['ffn_norm_pre']], [(D, BF16)])[0]
    g['ffn_w_in'] = _matmul("ffn_in_bw", h2, dgu, 'tn', F32)
    dxm_part, g['ffn_norm_pre'] = _rowwise_bwd("ffn_prenorm_b", _prenorm_fn, [xm], [w['ffn_norm_pre']], [dh2], as_f32, [F32], [True])
    dxm = _assemble("ffn_dx", [[dx, dxm_part]], F32)

    dy, g['mix_norm_post'] = _rowwise_bwd("mix_postnorm_b", lambda y, gn: _rms(y, gn), [sv['y']], [w['mix_norm_post']],
                                          [dxm], as_f32, [BF16], [True])
    dycat = _matmul("mix_out_bx", dy, w['w_out'], 'nt', F32)
    g['w_out'] = _matmul("mix_out_bw", sv['ycat'], dy, 'tn', F32)
    p = sv['p']
    dya, dyb, dyc, dyd = (Rows(dycat, GROUP_WIDTH, k) for k in range(4))

    dof, dga, g['gla_norm'] = _rowwise_bwd("gla_post_b", _gla_post_fn, [sv['of'], sv['ob'], _pcol(p, P_GA)],
                                           [w['gla_norm'], c['seg']], [dya], as_f32, [F32, None, F32], [True, False])
    qa, ka, va = (_to_heads(_pslice(p, b)) for b in (P_QA, P_KA, P_VA))
    la = sv['la']
    la_f, la_b = _to_heads(la[:, :GROUP_WIDTH]), _to_heads(la[:, GROUP_WIDTH:])
    doh = _to_heads(dof)
    gf = _gla_scan_bwd("gla_scan_f_b", qa, ka, va, la_f, sv['sf'], doh, False)
    gb = _gla_scan_bwd("gla_scan_b_b", qa, ka, va, la_b, sv['sb'], doh, True)
    dqa, dka, dva = ([_from_heads(gf[k]), _from_heads(gb[k])] for k in range(3))
    dla = _assemble("gla_dla", [[_from_heads(gf[3])], [_from_heads(gb[3])]], F32)
    dz, g['gla_wg'], g['gla_bg'] = _rowwise_bwd("gla_pre_b", _gla_pre_fn, [Rows(p, LANE, P_Z // LANE)], [w['gla_wg'], w['gla_bg']],
                                                [dla], as_f32, [BF16], [True, True])

    qb, kb, vb = (_to_heads(_pslice(p, b)).astype(BF16) for b in (P_QB, P_KB, P_VB))
    btab = _rpb_expand(w['na_rpb'], c['rpb_tab'])
    dqb, dkb, dvb, dbt = _local_attn_bwd("na_attn_b", qb, _pad_rows(kb, NA_HALO), _pad_rows(vb, NA_HALO), btab,
                                         _to_heads(dycat[:, GROUP_WIDTH:2 * GROUP_WIDTH]), None, GRID_W, NA_HALO, _na_valid)
    g['na_rpb'] = _rpb_contract(dbt, c['rpb_tab'])
    dqb, dkb, dvb = _from_heads(dqb), _from_heads(dkb[:, NA_HALO:NA_HALO + L]), _from_heads(dvb[:, NA_HALO:NA_HALO + L])

    dh, dgc = _rowwise_bwd("lru_post_b", _lru_post_fn, [sv['hf'], sv['hb'], _pcol(p, P_GC)], [], [dyc], as_f32, [F32, None, F32], [])
    lam0 = _lin_scan("lru_scan_f_b", _shift_rows(sv['a0'], -1), dh, True)
    lam1 = _lin_scan("lru_scan_b_b", _shift_rows(sv['a1'], 1), dh, False)
    xb = _pslice(p, P_XC)
    xs = [_shift_rows(xb, LRU_CONV_LEFT - j) for j in range(LRU_CONV)]
    lru_params = [w[k] for k in LRU_PARAMS]
    res = _rowwise_bwd("lru_pre_b", _lru_pre_fn, xs, lru_params,
                       [lam0, lam1, _shift_rows(sv['hf'], 1), _shift_rows(sv['hb'], -1)],
                       lambda l0, l1, hfp, hbn: (l0 * hfp, l1 * hbn, l0, l1), [F32] * 4, [True] * len(LRU_PARAMS))
    dxs = res[:4]
    for k, nm in enumerate(LRU_PARAMS):
        g[nm] = res[4 + k]
    dxc = [_shift_rows(dxs[j], j - LRU_CONV_LEFT) for j in range(LRU_CONV)]

    comb = _rowwise_bwd("dil_comb_b", _dil_comb_fn, sv['dil_o'] + sv['dil_l'], [], [dyd], as_f32, [F32] * 6, [])
    qr, kr = sv['qr'], sv['kr']
    qh, kh, vh = _to_heads(qr).astype(BF16), _to_heads(kr).astype(BF16), _to_heads(_pslice(p, P_VD)).astype(BF16)
    dqs, dks, dvs = [], [], []
    for k, (dil, radius, tq) in enumerate(_dil_branches(L)):
        n = L // dil
        dq_, dk_, dv_ = _local_attn_bwd(f"dil_attn{dil}_b", _dilate(qh, dil), _pad_rows(_dilate(kh, dil), radius),
                                        _pad_rows(_dilate(vh, dil), radius), None, _dilate(_to_heads(comb[k]), dil),
                                        _dilate(_to_heads(comb[3 + k]), dil), tq, radius, _band_valid)
        dqs.append(_from_heads(_undilate(dq_, dil)))
        dks.append(_from_heads(_undilate(dk_[:, radius:radius + n], dil)))
        dvs.append(_from_heads(_undilate(dv_[:, radius:radius + n], dil)))
    dqd, dkd = _rowwise_bwd("rope_b", _rope_fn, [_pcol(p, P_QD), _pcol(p, P_KD), c['cos'], c['sin']], [], dqs + dks,
                            lambda a1, a2, a3, b1, b2, b3: (a1 + a2 + a3, b1 + b2 + b3), [F32, F32, None, None], [])

    dp = _assemble("mix_dp", [dqa, dka, dva, [dga], [dqb], [dkb], [dvb], dxc, [dgc], [dqd], [dkd], dvs, [dz]], BF16)
    dh1 = _matmul("mix_proj_bx", dp, w['w_in'], 'nt', F32)
    x_in = sv['x_in']
    h = _rowwise("mix_prenorm_r", _prenorm_fn, [x_in], [w['mix_norm_pre']], [(D, BF16)])[0]
    g['w_in'] = _matmul("mix_proj_bw", h, dp, 'tn', F32)
    dxi_part, g['mix_norm_pre'] = _rowwise_bwd("mix_prenorm_b", _prenorm_fn, [x_in], [w['mix_norm_pre']], [dh1], as_f32, [F32], [True])
    dxi = _assemble("mix_dx", [[dxm, dxi_part]], F32)
    return dxi, g


def _loss_fn(y, t):
    e = y - t
    return e * (1.0 / y.shape[1]), jnp.sum(e * e, axis=0, keepdims=True)


def _gather_cols(name, shard, axis):
    full = _exchange(name, shard, 'xy', True)
    full = jnp.moveaxis(full, 0, axis)
    shp = list(shard.shape)
    shp[axis] *= 4
    return full.reshape(shp)


def _pack(arrs, mult=8 * LANE):
    flat = jnp.concatenate([a.reshape(-1) for a in arrs])
    pad = (-flat.shape[0]) % mult
    return jnp.pad(flat, (0, pad)).reshape(-1, LANE)


def _unpack(buf, shapes):
    flat, out, k = buf.reshape(-1), [], 0
    for s in shapes:
        sz = int(np.prod(s))
        out.append(flat[k:k + sz].reshape(s))
        k += sz
    return out


def _perm_in(w_in):
    pad = jnp.zeros(w_in.shape[:-1] + (D_INP - D_IN,), w_in.dtype)
    return jnp.concatenate([w_in[..., :P_QB * GROUP_WIDTH], w_in[..., P_QB * GROUP_WIDTH + 2 * GLA_RANK:],
                            w_in[..., P_QB * GROUP_WIDTH:P_QB * GROUP_WIDTH + 2 * GLA_RANK], pad], axis=-1)


def _unperm_in(g):
    return jnp.concatenate([g[..., :P_QB * GROUP_WIDTH], g[..., P_Z:P_Z + 2 * GLA_RANK], g[..., P_QB * GROUP_WIDTH:P_Z]], axis=-1)


def _block_diag(wb):
    l = wb.shape[0]
    eye = jnp.eye(GROUP_HEADS, dtype=wb.dtype)
    return jnp.einsum('lehij,hg->lehigj', wb, eye).reshape(l, 2, GROUP_WIDTH, GROUP_WIDTH)


def _block_diag_grad(gw):
    l = gw.shape[0]
    g6 = gw.reshape(l, 2, GROUP_HEADS, HEAD_DIM, GROUP_HEADS, HEAD_DIM)
    return jnp.stack([g6[:, :, h, :, h, :] for h in range(GROUP_HEADS)], axis=2)


def _gate_matrix(wg):
    l = wg.shape[0]
    m = jnp.zeros((l, LANE, 2 * GROUP_WIDTH), wg.dtype)
    for e in range(2):
        m = m.at[:, e * GLA_RANK:(e + 1) * GLA_RANK, e * GROUP_WIDTH:(e + 1) * GROUP_WIDTH].set(wg[:, e])
    return m


def _gate_matrix_grad(gm):
    return jnp.stack([gm[:, e * GLA_RANK:(e + 1) * GLA_RANK, e * GROUP_WIDTH:(e + 1) * GROUP_WIDTH] for e in range(2)], axis=1)


def _adam_fn(w, g, m, v):
    m = ADAM_B1 * m + (1.0 - ADAM_B1) * g
    v = ADAM_B2 * v + (1.0 - ADAM_B2) * (g * g)
    m_hat = m / (1.0 - ADAM_B1 ** ADAM_STEP)
    v_hat = v / (1.0 - ADAM_B2 ** ADAM_STEP)
    return -ADAM_LR * (m_hat / (jnp.sqrt(v_hat) + ADAM_EPS) + ADAM_WD * w), m, v


def _adam(name, w, g, m, v):
    shp = w.shape
    two = lambda t: t.reshape(-1, shp[-1])
    res = _rowwise(name, _adam_fn, [two(w), two(g), two(m), two(v)], [], [(shp[-1], F32)] * 3)
    return [r.reshape(shp) for r in res]


def _local_step(x, target, fw):
    L, D = x.shape
    cs, sn = _rope_tables(L)
    consts = {'seg': _seg_matrix(), 'rpb_tab': _rpb_tables(), 'cos': cs, 'sin': sn}

    def fwd(xc, wl):
        xo, sv = _layer_fwd(xc, wl, consts)
        return xo, sv

    xo, saved = lax.scan(fwd, x, fw)
    dy, sq = _rowwise("loss", _loss_fn, [xo, target], [], [(D, F32)], acc_outs=[(1, D)])

    def bwd(dxc, wl_sv):
        wl, sv = wl_sv
        dxi, g = _layer_bwd(dxc, wl, consts, sv)
        return dxi, g

    dx0, grads = lax.scan(bwd, dy, (fw, saved), reverse=True)
    return sq, dx0, grads


def kernel(x, mix_norm_pre, mix_norm_post, w_in, gla_w_gate, gla_b_gate, gla_norm, na_rpb, lru_conv_w, lru_conv_b, lru_w_a, lru_b_a, lru_w_x, lru_b_x, lru_lambda, w_out, ffn_norm_pre, ffn_norm_post, ffn_w_in, ffn_w_out, loss_target, m_mix_norm_pre, m_mix_norm_post, m_w_in, m_gla_w_gate, m_gla_b_gate, m_gla_norm, m_na_rpb, m_lru_conv_w, m_lru_conv_b, m_lru_w_a, m_lru_b_a, m_lru_w_x, m_lru_b_x, m_lru_lambda, m_w_out, m_ffn_norm_pre, m_ffn_norm_post, m_ffn_w_in, m_ffn_w_out, v_mix_norm_pre, v_mix_norm_post, v_w_in, v_gla_w_gate, v_gla_b_gate, v_gla_norm, v_na_rpb, v_lru_conv_w, v_lru_conv_b, v_lru_w_a, v_lru_b_a, v_lru_w_x, v_lru_b_x, v_lru_lambda, v_w_out, v_ffn_norm_pre, v_ffn_norm_post, v_ffn_w_in, v_ffn_w_out):
    args = (mix_norm_pre, mix_norm_post, w_in, gla_w_gate, gla_b_gate, gla_norm, na_rpb, lru_conv_w, lru_conv_b, lru_w_a, lru_b_a, lru_w_x, lru_b_x, lru_lambda, w_out, ffn_norm_pre, ffn_norm_post, ffn_w_in, ffn_w_out,
            m_mix_norm_pre, m_mix_norm_post, m_w_in, m_gla_w_gate, m_gla_b_gate, m_gla_norm, m_na_rpb, m_lru_conv_w, m_lru_conv_b, m_lru_w_a, m_lru_b_a, m_lru_w_x, m_lru_b_x, m_lru_lambda, m_w_out, m_ffn_norm_pre, m_ffn_norm_post, m_ffn_w_in, m_ffn_w_out,
            v_mix_norm_pre, v_mix_norm_post, v_w_in, v_gla_w_gate, v_gla_b_gate, v_gla_norm, v_na_rpb, v_lru_conv_w, v_lru_conv_b, v_lru_w_a, v_lru_b_a, v_lru_w_x, v_lru_b_x, v_lru_lambda, v_w_out, v_ffn_norm_pre, v_ffn_norm_post, v_ffn_w_in, v_ffn_w_out)
    nw = len(WEIGHTS)
    W = dict(zip(WEIGHTS, args[:nw]))
    M = dict(zip(WEIGHTS, args[nw:2 * nw]))
    V = dict(zip(WEIGHTS, args[2 * nw:]))
    depth = w_in.shape[0]
    chip = 2 * lax.axis_index("x") + lax.axis_index("y")

    full = dict(W)
    full['w_in'] = _gather_cols("ag_w_in", w_in.astype(BF16), 2)
    full['ffn_w_in'] = _gather_cols("ag_ffn_w_in", ffn_w_in.astype(BF16), 2)
    full['w_out'] = _gather_cols("ag_w_out", w_out.astype(BF16), 1)
    full['ffn_w_out'] = _gather_cols("ag_ffn_w_out", ffn_w_out.astype(BF16), 1)
    small = list(SMALL_SHARDED)
    got = _exchange("ag_small", _pack([W[k] for k in small]), 'xy', True)
    for k, parts in zip(small, zip(*[_unpack(got[j], [W[k].shape for k in small]) for j in range(4)])):
        ax = SMALL_SHARDED[k]
        stacked = jnp.moveaxis(jnp.stack(parts), 0, ax)
        shp = list(W[k].shape)
        shp[ax] *= 4
        full[k] = stacked.reshape(shp)

    sq, dx0, g = _local_step(x[0], loss_target[0], _layer_weights(full))
    loss = lax.psum(0.5 * jnp.sum(sq) / x.shape[-1], ("x", "y", "c"))
    gfull = _stored_grads(g)

    grad = {}
    for k in BIG:
        gk = gfull[k]
        if k in ('w_in', 'ffn_w_in'):
            l, K, N = gk.shape
            cut = gk.reshape(2, l * K // 2, 4, N // 4).transpose(0, 2, 1, 3)
            red = _reduce_big("rs_" + k, cut)
            grad[k] = red.reshape(l, K, N // 4)
        else:
            l, K, N = gk.shape
            cut = gk.reshape(2, l // 2, 4, K // 4, N).transpose(0, 2, 1, 3, 4).reshape(2, 4, (l // 2) * (K // 4), N)
            red = _reduce_big("rs_" + k, cut)
            grad[k] = red.reshape(l, K // 4, N)
    rest = [k for k in WEIGHTS if k not in BIG]
    allg = _exchange("ar_small", _pack([gfull[k] for k in rest]), 'xyc', True)
    summed = _unpack(_ordered_sum("ar_small_sum", allg), [gfull[k].shape for k in rest])
    for k, s in zip(rest, summed):
        if k in SMALL_SHARDED:
            ax = SMALL_SHARDED[k]
            n = W[k].shape[ax]
            s = lax.dynamic_slice_in_dim(s, chip * n, n, axis=ax)
        grad[k] = s

    delta, new_m, new_v = {}, {}, {}
    for k in BIG:
        delta[k], new_m[k], new_v[k] = _adam("adam_" + k, W[k], grad[k], M[k], V[k])
    shapes = [W[k].shape for k in rest]
    res = _rowwise("adam_small", _adam_fn, [_pack([d[k] for k in rest]) for d in (W, grad, M, V)], [], [(LANE, F32)] * 3)
    for d, r in zip((delta, new_m, new_v), res):
        for k, t in zip(rest, _unpack(r, shapes)):
            d[k] = t

    return (loss, dx0[None], *[grad[k] for k in WEIGHTS], *[delta[k] for k in WEIGHTS],
            *[new_m[k] for k in WEIGHTS], *[new_v[k] for k in WEIGHTS])


def _layer_weights(full):
    depth = full['w_in'].shape[0]
    row = lambda t: t[:, None, :]
    fw = {
        'mix_norm_pre': row(full['mix_norm_pre']), 'mix_norm_post': row(full['mix_norm_post']),
        'ffn_norm_pre': row(full['ffn_norm_pre']), 'ffn_norm_post': row(full['ffn_norm_post']),
        'w_in': _perm_in(full['w_in']), 'w_out': full['w_out'], 'ffn_w_in': full['ffn_w_in'], 'ffn_w_out': full['ffn_w_out'],
        'gla_wg': _gate_matrix(full['gla_w_gate']), 'gla_bg': full['gla_b_gate'].reshape(depth, 1, 2 * GROUP_WIDTH),
        'gla_norm': row(full['gla_norm']), 'na_rpb': full['na_rpb'],
        'lru_cb': row(full['lru_conv_b']),
    }
    wa_bd, wx_bd = _block_diag(full['lru_w_a']), _block_diag(full['lru_w_x'])
    for j in range(LRU_CONV):
        fw[f'lru_cw{j}'] = row(full['lru_conv_w'][:, j])
    for e in range(2):
        fw[f'lru_wa{e}'], fw[f'lru_wx{e}'] = wa_bd[:, e], wx_bd[:, e]
        fw[f'lru_ba{e}'], fw[f'lru_bx{e}'] = row(full['lru_b_a'][:, e]), row(full['lru_b_x'][:, e])
        fw[f'lru_lam{e}'] = row(full['lru_lambda'][:, e])
    return fw


def _stored_grads(g):
    depth = g['w_in'].shape[0]
    return {
        'mix_norm_pre': g['mix_norm_pre'][:, 0], 'mix_norm_post': g['mix_norm_post'][:, 0],
        'ffn_norm_pre': g['ffn_norm_pre'][:, 0], 'ffn_norm_post': g['ffn_norm_post'][:, 0],
        'w_in': _unperm_in(g['w_in']), 'w_out': g['w_out'], 'ffn_w_in': g['ffn_w_in'], 'ffn_w_out': g['ffn_w_out'],
        'gla_w_gate': _gate_matrix_grad(g['gla_wg']), 'gla_b_gate': g['gla_bg'].reshape(depth, 2, GROUP_WIDTH),
        'gla_norm': g['gla_norm'][:, 0], 'na_rpb': g['na_rpb'],
        'lru_conv_w': jnp.stack([g[f'lru_cw{j}'][:, 0] for j in range(LRU_CONV)], axis=1), 'lru_conv_b': g['lru_cb'][:, 0],
        'lru_w_a': _block_diag_grad(jnp.stack([g['lru_wa0'], g['lru_wa1']], axis=1)),
        'lru_w_x': _block_diag_grad(jnp.stack([g['lru_wx0'], g['lru_wx1']], axis=1)),
        'lru_b_a': jnp.stack([g['lru_ba0'][:, 0], g['lru_ba1'][:, 0]], axis=1),
        'lru_b_x': jnp.stack([g['lru_bx0'][:, 0], g['lru_bx1'][:, 0]], axis=1),
        'lru_lambda': jnp.stack([g['lru_lam0'][:, 0], g['lru_lam1'][:, 0]], axis=1),
    }
```

```python
import functools

import numpy as np
import jax
import jax.numpy as jnp
from jax import lax
from jax.experimental import pallas as pl
from jax.experimental.pallas import tpu as pltpu

F32, BF16 = jnp.float32, jnp.bfloat16
HIGHEST = lax.Precision.HIGHEST
MESH = pl.DeviceIdType.MESH

HEAD_DIM = 64
GROUP_HEADS = 4
GROUP_WIDTH = GROUP_HEADS * HEAD_DIM
GLA_RANK = 16
GLA_TAU = 16.0
GLA_CHUNK = 64
GRID_W = 64
NA_ROWS = 8
NA_COLS = 16
LRU_CONV = 4
LRU_CONV_LEFT = 2
LRU_C = 8.0
DIL_PAIRS = ((128, 1), (512, 4), (2048, 16))
ROPE_THETA = 10000.0
EPS = 1e-6
ADAM_LR, ADAM_B1, ADAM_B2, ADAM_EPS, ADAM_WD, ADAM_STEP = 0.001, 0.9, 0.999, 1e-08, 0.01, 10
NEG = -1e30

LANE = 128
VMEM_LIMIT = 56 * 1024 * 1024
ROW_BUDGET = 16 * 1024 * 1024
DMA_PIECES = 8

P_QA, P_KA, P_VA, P_GA, P_QB, P_KB, P_VB, P_XC, P_GC, P_QD, P_KD, P_VD = range(12)
P_Z = 12 * GROUP_WIDTH
D_IN = 12 * GROUP_WIDTH + 2 * GLA_RANK
D_INP = 12 * GROUP_WIDTH + LANE

WEIGHTS = ['mix_norm_pre', 'mix_norm_post', 'w_in', 'gla_w_gate', 'gla_b_gate', 'gla_norm', 'na_rpb',
           'lru_conv_w', 'lru_conv_b', 'lru_w_a', 'lru_b_a', 'lru_w_x', 'lru_b_x', 'lru_lambda', 'w_out',
           'ffn_norm_pre', 'ffn_norm_post', 'ffn_w_in', 'ffn_w_out']
BIG = ('w_in', 'w_out', 'ffn_w_in', 'ffn_w_out')
SMALL_SHARDED = {'gla_w_gate': 3, 'gla_b_gate': 2, 'lru_conv_w': 2, 'lru_b_a': 2, 'lru_b_x': 2, 'lru_lambda': 2}
HEADS = [slice(h * HEAD_DIM, (h + 1) * HEAD_DIM) for h in range(GROUP_HEADS)]


def _cparams(sem=None):
    return pltpu.CompilerParams(dimension_semantics=sem, vmem_limit_bytes=VMEM_LIMIT)


def _tile(dim, target, mult=LANE):
    best = None
    for t in range(mult, min(dim, target) + 1, mult):
        if dim % t == 0:
            best = t
    return best or dim


class Rows:
    def __init__(self, a, w=None, cb=0, lead=None):
        self.a, self.cb, self.lead = a, cb, lead
        self.w = a.shape[-1] if w is None else w
        self.nrows = a.shape[-2]

    def spec(self, tm, ncol=1):
        w = self.w // ncol
        if self.lead is None:
            return pl.BlockSpec((tm, w), lambda i, j, cb=self.cb: (i, cb * ncol + j))
        return pl.BlockSpec((None, tm, w), lambda i, j, cb=self.cb, k=self.lead: (k, i, cb * ncol + j))

    def nbytes(self):
        return self.w * self.a.dtype.itemsize


def _as_rows(rs):
    return [r if isinstance(r, Rows) else Rows(r) for r in rs]


def _pick_tm(nrows, row_bytes, scale):
    tm = 512
    while tm > 16 and (tm * row_bytes * scale > ROW_BUDGET or nrows % tm):
        tm //= 2
    assert nrows % tm == 0, (nrows, tm)
    return tm


def _full_spec(a):
    nd = a.ndim
    return pl.BlockSpec(a.shape, lambda i, j, nd=nd: (0,) * nd)


def _rowwise(name, fn, rows, params, outs, acc_outs=(), ncol=1):
    rows = _as_rows(rows)
    nrows = rows[0].nrows
    assert ncol == 1 or not (acc_outs or params)
    tm = _pick_tm(nrows, (sum(r.nbytes() for r in rows) + sum(w * jnp.dtype(d).itemsize for w, d in outs)) // ncol, 2)
    n_r, n_p, n_o = len(rows), len(params), len(outs)

    def body(*refs):
        vals = [r[...] for r in refs[:n_r + n_p]]
        res = fn(*vals)
        res = res if isinstance(res, (tuple, list)) else (res,)
        orefs = refs[n_r + n_p:]
        for o, v in zip(orefs[:n_o], res[:n_o]):
            o[...] = v.astype(o.dtype)
        for o, v in zip(orefs[n_o:], res[n_o:]):
            @pl.when(pl.program_id(0) == 0)
            def _(o=o):
                o[...] = jnp.zeros_like(o)
            o[...] += v

    out_shape = [jax.ShapeDtypeStruct((nrows, w), d) for w, d in outs] + [jax.ShapeDtypeStruct(s, F32) for s in acc_outs]
    out_specs = [pl.BlockSpec((tm, w // ncol), lambda i, j: (i, j)) for w, _ in outs] + \
                [pl.BlockSpec(s, lambda i, j, nd=len(s): (0,) * nd) for s in acc_outs]
    return pl.pallas_call(
        body, name=name, grid=(nrows // tm, ncol),
        in_specs=[r.spec(tm, ncol) for r in rows] + [_full_spec(p) for p in params],
        out_specs=out_specs, out_shape=out_shape,
        compiler_params=_cparams(("arbitrary", "arbitrary") if acc_outs else ("parallel", "parallel")),
    )(*[r.a for r in rows], *params)


def _rowwise_bwd(name, fn, rows, params, ct_rows, ct_fn, row_grads, param_grads, row_grad_add=None, ncol=1):
    rows, ct_rows = _as_rows(rows), _as_rows(ct_rows)
    nrows = rows[0].nrows
    n_rg = sum(d is not None for d in row_grads)
    adds = [a for a in (row_grad_add or []) if a is not None]
    add_at = [k for k, a in enumerate(row_grad_add or []) if a is not None]
    assert ncol == 1 or not (params or adds)
    gbytes = sum(r.w * jnp.dtype(d).itemsize for r, d in zip(rows, row_grads) if d is not None)
    tm = _pick_tm(nrows, (sum(r.nbytes() for r in rows + ct_rows + _as_rows(adds)) + gbytes) // ncol, 4)
    n_r, n_p, n_c, n_a = len(rows), len(params), len(ct_rows), len(adds)
    diff = [k for k, d in enumerate(row_grads) if d is not None] + [n_r + k for k, g in enumerate(param_grads) if g]

    def body(*refs):
        vals = [r[...] for r in refs[:n_r + n_p]]
        cts_in = [r[...] for r in refs[n_r + n_p:n_r + n_p + n_c]]
        add_in = [r[...] for r in refs[n_r + n_p + n_c:n_r + n_p + n_c + n_a]]
        orefs = refs[n_r + n_p + n_c + n_a:]

        def f(*dv):
            full = list(vals)
            for k, v in zip(diff, dv):
                full[k] = v
            res = fn(*full)
            return tuple(res) if isinstance(res, (tuple, list)) else (res,)

        outs, vjp = jax.vjp(f, *[vals[k].astype(F32) for k in diff])
        cts = ct_fn(*cts_in)
        cts = cts if isinstance(cts, (tuple, list)) else (cts,)
        grads = list(vjp(tuple(c.astype(o.dtype) for c, o in zip(cts, outs))))
        for k, a in zip(add_at, add_in):
            grads[k] = grads[k] + a.astype(F32)
        for o, g in zip(orefs[:n_rg], grads[:n_rg]):
            o[...] = g.astype(o.dtype)
        for o, g in zip(orefs[n_rg:], grads[n_rg:]):
            @pl.when(pl.program_id(0) == 0)
            def _(o=o):
                o[...] = jnp.zeros_like(o)
            o[...] += g.astype(F32)

    out_shape = [jax.ShapeDtypeStruct((nrows, r.w), d) for r, d in zip(rows, row_grads) if d is not None] + \
                [jax.ShapeDtypeStruct(p.shape, F32) for p, g in zip(params, param_grads) if g]
    out_specs = [pl.BlockSpec((tm, r.w // ncol), lambda i, j: (i, j)) for r, d in zip(rows, row_grads) if d is not None] + \
                [_full_spec(p) for p, g in zip(params, param_grads) if g]
    return pl.pallas_call(
        body, name=name, grid=(nrows // tm, ncol),
        in_specs=[r.spec(tm, ncol) for r in rows] + [_full_spec(p) for p in params] + [r.spec(tm, ncol) for r in ct_rows]
        + [r.spec(tm, ncol) for r in _as_rows(adds)],
        out_specs=out_specs, out_shape=out_shape,
        compiler_params=_cparams(("arbitrary", "arbitrary")),
    )(*[r.a for r in rows], *params, *[r.a for r in ct_rows], *adds)


def _assemble(name, groups, dtype):
    sizes = [len(g) for g in groups]
    flat = [a for g in groups for a in g]

    def fn(*tiles):
        out, k = [], 0
        for s in sizes:
            acc = tiles[k].astype(F32)
            for t in tiles[k + 1:k + s]:
                acc = acc + t.astype(F32)
            out.append(acc.astype(dtype))
            k += s
        return out[0] if len(out) == 1 else jnp.concatenate(out, axis=1)

    width = sum(g[0].shape[-1] if not isinstance(g[0], Rows) else g[0].w for g in groups)
    return _rowwise(name, fn, flat, [], [(width, dtype)])[0]


def _matmul(name, a, b, mode, out_dtype, acc_in=None):
    if mode == 'nn':
        (M, K), N = a.shape, b.shape[1]
    elif mode == 'nt':
        (M, K), N = a.shape, b.shape[0]
    else:
        (K, M), N = a.shape, b.shape[1]
    tm, tn, tk = _tile(M, 1536), _tile(N, 1536), _tile(K, 1536)
    nk = K // tk
    dn = {'nn': (((1,), (0,)), ((), ())), 'nt': (((1,), (1,)), ((), ())), 'tn': (((0,), (0,)), ((), ()))}[mode]
    has_acc = acc_in is not None

    def body(*refs):
        a_ref, b_ref = refs[:2]
        o_ref, acc = refs[-2:]

        @pl.when(pl.program_id(2) == 0)
        def _():
            acc[...] = refs[2][...] if has_acc else jnp.zeros_like(acc)
        acc[...] += lax.dot_general(a_ref[...].astype(BF16), b_ref[...].astype(BF16), dn, preferred_element_type=F32)

        @pl.when(pl.program_id(2) == nk - 1)
        def _():
            o_ref[...] = acc[...].astype(o_ref.dtype)

    a_spec = pl.BlockSpec((tk, tm), lambda i, j, k: (k, i)) if mode == 'tn' else pl.BlockSpec((tm, tk), lambda i, j, k: (i, k))
    b_spec = pl.BlockSpec((tn, tk), lambda i, j, k: (j, k)) if mode == 'nt' else pl.BlockSpec((tk, tn), lambda i, j, k: (k, j))
    o_spec = pl.BlockSpec((tm, tn), lambda i, j, k: (i, j))
    return pl.pallas_call(
        body, name=name, grid=(M // tm, N // tn, nk),
        in_specs=[a_spec, b_spec] + ([o_spec] if has_acc else []), out_specs=o_spec,
        out_shape=jax.ShapeDtypeStruct((M, N), out_dtype),
        scratch_shapes=[pltpu.VMEM((tm, tn), F32)],
        compiler_params=_cparams(("parallel", "parallel", "arbitrary")),
    )(a, b, *([acc_in] if has_acc else []))


def _small_dot(name, a, b, mode):
    dn = {'nn': (((1,), (0,)), ((), ())), 'nt': (((1,), (1,)), ((), ()))}[mode]
    M = a.shape[0]
    N = b.shape[1] if mode == 'nn' else b.shape[0]

    def body(a_ref, b_ref, o_ref):
        o_ref[...] = lax.dot_general(a_ref[...], b_ref[...], dn, precision=HIGHEST, preferred_element_type=F32)

    return pl.pallas_call(body, name=name, out_shape=jax.ShapeDtypeStruct((M, N), F32),
                          compiler_params=pltpu.CompilerParams(vmem_limit_bytes=VMEM_LIMIT))(a, b)


NN, NT, TN = (((1,), (0,)), ((), ())), (((1,), (1,)), ((), ())), (((0,), (0,)), ((), ()))


def _bdot(a, b, dn=NN):
    return lax.dot_general(a.astype(BF16), b.astype(BF16), dn, preferred_element_type=F32)


def _sigmoid(x):
    return 0.5 * jnp.tanh(0.5 * x) + 0.5


def _silu(x):
    return x * _sigmoid(x)


def _softplus(x):
    return jnp.maximum(x, 0.0) + jnp.log(1.0 + jnp.exp(-jnp.abs(x)))


def _gelu(x):
    return 0.5 * x * (1.0 + jnp.tanh(0.7978845608028654 * (x + 0.044715 * (x * x * x))))


def _rms(x, g):
    return x * lax.rsqrt(jnp.mean(x * x, axis=-1, keepdims=True) + EPS) * g


def _prenorm_fn(x, g):
    return _rms(x, g)


def _postnorm_fn(x, y, g):
    return x + _rms(y, g)


def _swiglu_fn(gate, up):
    return _silu(gate.astype(F32)) * up.astype(F32)


def _gla_pre_fn(z, wg, bg):
    logit = _bdot(z, wg) + bg
    return -_softplus(-logit) * (1.0 / GLA_TAU)


def _seg_mean(x, seg):
    return lax.dot_general(x, seg, NN, precision=HIGHEST, preferred_element_type=F32)


def _gla_post_fn(of, ob, g, norm, seg):
    o = of + ob
    o = o * lax.rsqrt(_seg_mean(o * o, seg) + EPS) * norm
    return o * _silu(g)


@jax.custom_vjp
def _swap_halves(x):
    n = x.shape[-1]
    lane = lax.broadcasted_iota(jnp.int32, x.shape, x.ndim - 1)
    lo = (lane & (HEAD_DIM - 1)) < HEAD_DIM // 2
    return jnp.where(lo, pltpu.roll(x, n - HEAD_DIM // 2, x.ndim - 1), pltpu.roll(x, HEAD_DIM // 2, x.ndim - 1))


_swap_halves.defvjp(lambda x: (_swap_halves(x), None), lambda _, g: (_swap_halves(g),))


def _rope_fn(q, k, cs, sn):
    return q * cs + _swap_halves(q) * sn, k * cs + _swap_halves(k) * sn


def _dil_comb_fn(o1, o2, o3, l1, l2, l3):
    m = jnp.maximum(jnp.maximum(l1, l2), l3)
    e1, e2, e3 = jnp.exp(l1 - m), jnp.exp(l2 - m), jnp.exp(l3 - m)
    return (e1 * o1 + e2 * o2 + e3 * o3) / (e1 + e2 + e3)


LRU_PARAMS = ['lru_cw0', 'lru_cw1', 'lru_cw2', 'lru_cw3', 'lru_cb', 'lru_wa0', 'lru_wa1', 'lru_ba0', 'lru_ba1',
              'lru_wx0', 'lru_wx1', 'lru_bx0', 'lru_bx1', 'lru_lam0', 'lru_lam1']


def _lru_pre_fn(x0, x1, x2, x3, cw0, cw1, cw2, cw3, cb, wa0, wa1, ba0, ba1, wx0, wx1, bx0, bx1, lam0, lam1):
    xc = cb + x0 * cw0 + x1 * cw1 + x2 * cw2 + x3 * cw3
    outs = []
    for wa, ba, wx, bx, lam in ((wa0, ba0, wx0, bx0, lam0), (wa1, ba1, wx1, bx1, lam1)):
        r = _sigmoid(_bdot(xc, wa) + ba)
        i = _sigmoid(_bdot(xc, wx) + bx)
        log_a = -LRU_C * r * _softplus(-lam)
        a = jnp.exp(log_a)
        u = jnp.sqrt(-jnp.tanh(log_a) * (a * a + 1.0)) * (i * xc)
        outs += [a, u]
    return outs[0], outs[2], outs[1], outs[3]


def _lru_post_fn(hf, hb, gate):
    return (hf + hb) * _gelu(gate)


def _attn_tile(q, kw, vw, bias):
    s = _bdot(q, kw, NT) * (HEAD_DIM ** -0.5) + bias
    m = lax.stop_gradient(jnp.max(s, axis=-1, keepdims=True))
    e = jnp.exp(s - m)
    den = jnp.sum(e, axis=-1, keepdims=True)
    o = _bdot(e * (1.0 / den), vw)
    return o, m + jnp.log(den)


def _gla_chunk(q, k, v, la, st, rev):
    C = q.shape[0]
    ti = lax.broadcasted_iota(jnp.int32, (C, C), 0)
    si = lax.broadcasted_iota(jnp.int32, (C, C), 1)
    incl = (si >= ti) if rev else (si <= ti)
    b = lax.dot_general(incl.astype(F32), la, NN, precision=HIGHEST, preferred_element_type=F32)
    row = lax.broadcasted_iota(jnp.int32, (C, 1), 0)
    mid = (row >= C // 2) if rev else (row < C // 2)
    b_last = jnp.sum(la, axis=0, keepdims=True)
    b_mid = jnp.sum(jnp.where(mid, la, 0.0), axis=0, keepdims=True)
    qs = q * (HEAD_DIM ** -0.5)
    att = _bdot(qs * jnp.exp(b - b_mid), k * jnp.exp(b_mid - b), NT)
    att = jnp.where(incl, att, 0.0)
    o = _bdot(att, v) + _bdot(qs * jnp.exp(b), st, NT)
    kv = _bdot(v, k * jnp.exp(b_last - b), TN)
    return o, st * jnp.exp(b_last) + kv


def _gla_specs(n, blocks, first):
    C = GLA_CHUNK
    at = (lambda i: i) if first else (lambda i: n - 1 - i)
    return [pl.BlockSpec((C, GROUP_WIDTH), lambda i, b=b: (at(i), b)) for b in blocks], at


def _gla_scan_fwd(p, la):
    L = p.shape[0]
    C, H, dh = GLA_CHUNK, GROUP_HEADS, HEAD_DIM
    n = L // C
    f_specs, f_at = _gla_specs(n, (P_QA, P_KA, P_VA), True)
    b_specs, b_at = _gla_specs(n, (P_QA, P_KA, P_VA), False)
    tile = lambda at, blk=0: pl.BlockSpec((C, GROUP_WIDTH), lambda i: (at(i), blk))
    st_spec = lambda at: pl.BlockSpec((None, H, dh, dh), lambda i: (at(i), 0, 0, 0))

    def body(qf, kf, vf, lf, qb, kb, vb, lb, of_ref, ob_ref, sf_ref, sb_ref, stf, stb):
        @pl.when(pl.program_id(0) == 0)
        def _():
            stf[...] = jnp.zeros_like(stf)
            stb[...] = jnp.zeros_like(stb)
        sf_ref[...] = stf[...]
        sb_ref[...] = stb[...]
        for q, k, v, l, o_ref, st, rev in ((qf, kf, vf, lf, of_ref, stf, False), (qb, kb, vb, lb, ob_ref, stb, True)):
            for h, sl in enumerate(HEADS):
                o, st_new = _gla_chunk(q[:, sl], k[:, sl], v[:, sl], l[:, sl], st[h], rev)
                o_ref[:, sl] = o
                st[h] = st_new

    return pl.pallas_call(
        body, name="gla_scan", grid=(n,),
        in_specs=f_specs + [tile(f_at, 0)] + b_specs + [tile(b_at, 1)],
        out_specs=[tile(f_at), tile(b_at), st_spec(f_at), st_spec(b_at)],
        out_shape=[jax.ShapeDtypeStruct((L, GROUP_WIDTH), F32)] * 2 + [jax.ShapeDtypeStruct((n, H, dh, dh), F32)] * 2,
        scratch_shapes=[pltpu.VMEM((H, dh, dh), F32)] * 2,
        compiler_params=_cparams(("arbitrary",)),
    )(p, p, p, la, p, p, p, la)


def _gla_scan_bwd(p, la, sf, sb, do):
    L = p.shape[0]
    C, H, dh = GLA_CHUNK, GROUP_HEADS, HEAD_DIM
    n = L // C
    f_specs, f_at = _gla_specs(n, (P_QA, P_KA, P_VA), False)
    b_specs, b_at = _gla_specs(n, (P_QA, P_KA, P_VA), True)
    tile = lambda at, blk=0: pl.BlockSpec((C, GROUP_WIDTH), lambda i: (at(i), blk))
    st_spec = lambda at: pl.BlockSpec((None, H, dh, dh), lambda i: (at(i), 0, 0, 0))

    def body(qf, kf, vf, lf, spf, dof, qb, kb, vb, lb, spb, dob, *rest):
        outs_f, outs_b, (dstf, dstb) = rest[0:4], rest[4:8], rest[8:10]

        @pl.when(pl.program_id(0) == 0)
        def _():
            dstf[...] = jnp.zeros_like(dstf)
            dstb[...] = jnp.zeros_like(dstb)
        for q, k, v, l, sp, d, outs, dst, rev in ((qf, kf, vf, lf, spf, dof, outs_f, dstf, False),
                                                  (qb, kb, vb, lb, spb, dob, outs_b, dstb, True)):
            for h, sl in enumerate(HEADS):
                _, vjp = jax.vjp(functools.partial(_gla_chunk, rev=rev), q[:, sl], k[:, sl], v[:, sl], l[:, sl], sp[h])
                dq, dk, dv, dl, dsp = vjp((d[:, sl], dst[h]))
                for o_ref, g in zip(outs, (dq, dk, dv, dl)):
                    o_ref[:, sl] = g
                dst[h] = dsp

    return pl.pallas_call(
        body, name="gla_scan_b", grid=(n,),
        in_specs=f_specs + [tile(f_at, 0), st_spec(f_at), tile(f_at)] + b_specs + [tile(b_at, 1), st_spec(b_at), tile(b_at)],
        out_specs=[tile(f_at)] * 4 + [tile(b_at)] * 4,
        out_shape=[jax.ShapeDtypeStruct((L, GROUP_WIDTH), F32)] * 8,
        scratch_shapes=[pltpu.VMEM((H, dh, dh), F32)] * 2,
        compiler_params=_cparams(("arbitrary",)),
    )(p, p, p, la, sf, do, p, p, p, la, sb, do)


NA_W = NA_ROWS * GRID_W
NA_BW = (2 * NA_ROWS - 1) * GRID_W


def _na_start(i, rows):
    return jnp.clip(i - NA_ROWS // 2, 0, rows - NA_ROWS)


def _na_fwd(p, kb, vb, btab):
    L = p.shape[0]
    rows = L // GRID_W

    def body(q_ref, k_ref, v_ref, b_ref, o_ref):
        r = pl.program_id(0)
        s = _na_start(r, rows)
        start = pl.multiple_of(s * GRID_W, GRID_W)
        for h, sl in enumerate(HEADS):
            o, _ = _attn_tile(q_ref[:, sl], k_ref[pl.ds(start, NA_W), sl], v_ref[pl.ds(start, NA_W), sl], b_ref[s - r + NA_ROWS - 1, h])
            o_ref[:, sl] = o.astype(o_ref.dtype)

    whole = lambda a: pl.BlockSpec(a.shape, lambda i, nd=a.ndim: (0,) * nd)
    return pl.pallas_call(
        body, name="na_attn", grid=(rows,),
        in_specs=[pl.BlockSpec((GRID_W, GROUP_WIDTH), lambda i: (i, P_QB)), whole(kb), whole(vb), whole(btab)],
        out_specs=pl.BlockSpec((GRID_W, GROUP_WIDTH), lambda i: (i, 0)),
        out_shape=jax.ShapeDtypeStruct((L, GROUP_WIDTH), BF16),
        compiler_params=_cparams(("arbitrary",)),
    )(p, kb, vb, btab)


def _na_bwd(p, kb, vb, btab, dycat):
    L = p.shape[0]
    rows = L // GRID_W
    flush = NA_ROWS - 1
    emit = lambda i: jnp.where(i < rows, _na_start(i, rows), i - flush)

    def body(q_ref, k_ref, v_ref, b_ref, do_ref, dq_ref, dk_ref, dv_ref, db_ref, acc_k, acc_v):
        i = pl.program_id(0)

        @pl.when(i == 0)
        def _():
            acc_k[...] = jnp.zeros_like(acc_k)
            acc_v[...] = jnp.zeros_like(acc_v)
            db_ref[...] = jnp.zeros_like(db_ref)

        @pl.when((i > 0) & (emit(i) != emit(i - 1)))
        def _():
            for acc in (acc_k, acc_v):
                moved = acc[GRID_W:NA_W, :]
                acc[0:NA_W - GRID_W, :] = moved
                acc[NA_W - GRID_W:NA_W, :] = jnp.zeros((GRID_W, GROUP_WIDTH), F32)

        @pl.when(i < rows)
        def _():
            s = _na_start(i, rows)
            sv = s - i + NA_ROWS - 1
            start = pl.multiple_of(s * GRID_W, GRID_W)
            for h, sl in enumerate(HEADS):
                _, vjp = jax.vjp(_attn_tile, q_ref[:, sl], k_ref[pl.ds(start, NA_W), sl].astype(F32),
                                 v_ref[pl.ds(start, NA_W), sl].astype(F32), b_ref[sv, h])
                dq, dkw, dvw, db = vjp((do_ref[:, sl], jnp.zeros((GRID_W, 1), F32)))
                dq_ref[:, sl] = dq
                acc_k[:, sl] += dkw
                acc_v[:, sl] += dvw
                db_ref[sv, h] += db

        dk_ref[...] = acc_k[0:GRID_W, :]
        dv_ref[...] = acc_v[0:GRID_W, :]

    whole = lambda a: pl.BlockSpec(a.shape, lambda i, nd=a.ndim: (0,) * nd)
    qrow = lambda blk: pl.BlockSpec((GRID_W, GROUP_WIDTH), lambda i: (jnp.minimum(i, rows - 1), blk))
    erow = pl.BlockSpec((GRID_W, GROUP_WIDTH), lambda i: (emit(i), 0))
    return pl.pallas_call(
        body, name="na_attn_b", grid=(rows + flush,),
        in_specs=[qrow(P_QB), whole(kb), whole(vb), whole(btab), qrow(1)],
        out_specs=[qrow(0), erow, erow, whole(btab)],
        out_shape=[jax.ShapeDtypeStruct((L, GROUP_WIDTH), F32)] * 3 + [jax.ShapeDtypeStruct(btab.shape, F32)],
        scratch_shapes=[pltpu.VMEM((NA_W, GROUP_WIDTH), F32)] * 2,
        compiler_params=_cparams(("arbitrary",)),
    )(p, kb, vb, btab, dycat)


def _na_col_ok():
    qc = np.arange(GRID_W)[:, None]
    kc = (np.arange(NA_W) % GRID_W)[None, :]
    c0 = np.clip(qc - NA_COLS // 2, 0, GRID_W - NA_COLS)
    return (kc >= c0) & (kc < c0 + NA_COLS)


def _rpb_tables():
    c = np.arange(GRID_W)
    dc = np.clip(c[None, :] - c[:, None], -(NA_COLS - 1), NA_COLS - 1) + NA_COLS - 1
    t = np.zeros((2 * NA_COLS - 1, GRID_W, GRID_W), np.float32)
    t[dc, c[:, None], c[None, :]] = 1.0
    return jnp.asarray(t.reshape(2 * NA_COLS - 1, GRID_W * GRID_W))


def _rpb_expand(rpb, tab):
    H = rpb.shape[0]
    xt = _small_dot("na_bias", rpb.reshape(H * (2 * NA_ROWS - 1), 2 * NA_COLS - 1), tab, 'nn')
    b15 = xt.reshape(H, 2 * NA_ROWS - 1, GRID_W, GRID_W).transpose(0, 2, 1, 3).reshape(H, GRID_W, NA_BW)
    ok = jnp.asarray(_na_col_ok())
    return jnp.stack([jnp.where(ok, b15[:, :, sv * GRID_W:sv * GRID_W + NA_W], NEG) for sv in range(NA_ROWS)])


def _rpb_contract(dbv, tab):
    H = dbv.shape[1]
    db = sum(jnp.pad(dbv[sv], ((0, 0), (0, 0), (sv * GRID_W, NA_BW - NA_W - sv * GRID_W))) for sv in range(NA_ROWS))
    dx = db.reshape(H, GRID_W, 2 * NA_ROWS - 1, GRID_W).transpose(0, 2, 1, 3).reshape(H * (2 * NA_ROWS - 1), GRID_W * GRID_W)
    return _small_dot("na_bias_b", dx, tab, 'nt').reshape(H, 2 * NA_ROWS - 1, 2 * NA_COLS - 1)


def _band_bias(i, tq, w, halo, n):
    a = lax.broadcasted_iota(jnp.int32, (tq, w), 0)
    b = lax.broadcasted_iota(jnp.int32, (tq, w), 1)
    kpos = i * tq - halo + b
    d = b - halo - a
    return jnp.where((d <= halo) & (d >= -halo) & (kpos >= 0) & (kpos < n), 0.0, NEG)


def _band_fwd(name, q, kp, vp, tq, halo):
    G, n, _ = q.shape
    w = tq + 2 * halo

    def body(q_ref, k_ref, v_ref, o_ref, l_ref):
        i = pl.program_id(1)
        start = pl.multiple_of(i * tq, tq)
        bias = _band_bias(i, tq, w, halo, n)
        for sl in HEADS:
            o, lse = _attn_tile(q_ref[:, sl], k_ref[pl.ds(start, w), sl], v_ref[pl.ds(start, w), sl], bias)
            o_ref[:, sl] = o
            l_ref[:, sl] = jnp.broadcast_to(lse, (tq, HEAD_DIM))

    qblk = pl.BlockSpec((None, tq, GROUP_WIDTH), lambda g, i: (g, i, 0))
    kblk = pl.BlockSpec((None, n + 2 * halo, GROUP_WIDTH), lambda g, i: (g, 0, 0))
    return pl.pallas_call(
        body, name=name, grid=(G, n // tq), in_specs=[qblk, kblk, kblk], out_specs=[qblk, qblk],
        out_shape=[jax.ShapeDtypeStruct((G, n, GROUP_WIDTH), F32)] * 2,
        compiler_params=_cparams(("parallel", "arbitrary")),
    )(q, kp, vp)


def _band_bwd(name, q, kp, vp, do, dl, tq, halo):
    G, n, _ = q.shape
    w = tq + 2 * halo
    nq = n // tq

    def body(q_ref, k_ref, v_ref, do_ref, dl_ref, dq_ref, dk_ref, dv_ref, acc_k, acc_v):
        i = pl.program_id(1)

        @pl.when(i == 0)
        def _():
            acc_k[...] = jnp.zeros_like(acc_k)
            acc_v[...] = jnp.zeros_like(acc_v)

        @pl.when(i > 0)
        def _():
            for acc in (acc_k, acc_v):
                moved = acc[tq:w, :]
                acc[0:2 * halo, :] = moved
                acc[2 * halo:w, :] = jnp.zeros((tq, GROUP_WIDTH), F32)

        @pl.when(i < nq)
        def _():
            start = pl.multiple_of(i * tq, tq)
            bias = _band_bias(i, tq, w, halo, n)
            for sl in HEADS:
                _, vjp = jax.vjp(lambda a, b, c: _attn_tile(a, b, c, bias), q_ref[:, sl].astype(F32),
                                 k_ref[pl.ds(start, w), sl].astype(F32), v_ref[pl.ds(start, w), sl].astype(F32))
                dq, dkw, dvw = vjp((do_ref[:, sl], jnp.sum(dl_ref[:, sl], axis=1, keepdims=True)))
                dq_ref[:, sl] = dq
                acc_k[:, sl] += dkw
                acc_v[:, sl] += dvw

        dk_ref[...] = acc_k[0:tq, :]
        dv_ref[...] = acc_v[0:tq, :]

    qblk = pl.BlockSpec((None, tq, GROUP_WIDTH), lambda g, i: (g, jnp.minimum(i, nq - 1), 0))
    kblk = pl.BlockSpec((None, n + 2 * halo, GROUP_WIDTH), lambda g, i: (g, 0, 0))
    eblk = pl.BlockSpec((None, tq, GROUP_WIDTH), lambda g, i: (g, i, 0))
    return pl.pallas_call(
        body, name=name, grid=(G, nq + 1), in_specs=[qblk, kblk, kblk, qblk, qblk], out_specs=[qblk, eblk, eblk],
        out_shape=[jax.ShapeDtypeStruct((G, n, GROUP_WIDTH), F32)] + [jax.ShapeDtypeStruct((G, (nq + 1) * tq, GROUP_WIDTH), F32)] * 2,
        scratch_shapes=[pltpu.VMEM((w, GROUP_WIDTH), F32)] * 2,
        compiler_params=_cparams(("parallel", "arbitrary")),
    )(q, kp, vp, do, dl)


def _lin_scan(name, coef, inp, rev):
    L, C = coef.shape
    tt = 256 if L % 256 == 0 else L
    nt = L // tt
    tidx = (lambda i: (nt - 1 - i, 0)) if rev else (lambda i: (i, 0))

    def body(a_ref, u_ref, o_ref, carry):
        @pl.when(pl.program_id(0) == 0)
        def _():
            carry[...] = jnp.zeros_like(carry)
        a, u = a_ref[...], u_ref[...]
        row = lax.broadcasted_iota(jnp.int32, (tt, C), 0)
        s = 1
        while s < tt:
            ok = (row < tt - s) if rev else (row >= s)
            sh = tt - s if rev else s
            u = u + a * jnp.where(ok, pltpu.roll(u, sh, 0), 0.0)
            a = a * jnp.where(ok, pltpu.roll(a, sh, 0), 1.0)
            s *= 2
        out = u + a * carry[...]
        o_ref[...] = out
        carry[...] = out[0:1] if rev else out[tt - 1:tt]

    blk = pl.BlockSpec((tt, C), tidx)
    return pl.pallas_call(
        body, name=name, grid=(nt,), in_specs=[blk, blk], out_specs=blk,
        out_shape=jax.ShapeDtypeStruct((L, C), F32), scratch_shapes=[pltpu.VMEM((1, C), F32)],
        compiler_params=_cparams(("arbitrary",)),
    )(coef, inp)


def _pieces(shape):
    n0 = max(d for d in range(1, DMA_PIECES + 1) if shape[0] % d == 0)
    n1 = 1
    if len(shape) >= 3:
        n1 = max(d for d in range(1, DMA_PIECES // n0 + 1) if shape[1] % d == 0)
    s0, s1 = shape[0] // n0, (shape[1] // n1 if len(shape) >= 3 else 0)
    out = []
    for i in range(n0):
        for j in range(n1):
            out.append((pl.ds(i * s0, s0),) + ((pl.ds(j * s1, s1),) if len(shape) >= 3 else ()))
    return out


def _exchange(name, src, axes, gather):
    flips = {'xy': [(1, 0, 0), (0, 1, 0), (1, 1, 0)], 'c': [(0, 0, 1)],
             'xyc': [(fx, fy, fc) for fx in (0, 1) for fy in (0, 1) for fc in (0, 1)][1:]}[axes]
    n = len(flips) + 1
    blk_shape = tuple(src.shape if gather else src.shape[1:])
    pieces = _pieces(blk_shape)

    def number(px, py, pc):
        return {'xy': 2 * px + py, 'c': pc, 'xyc': 4 * px + 2 * py + pc}[axes]

    def body(src_ref, out_ref, send_sems, recv_sems, loc_sem):
        x, y, c = lax.axis_index("x"), lax.axis_index("y"), lax.axis_index("c")
        me = number(x, y, c)
        piece = (lambda k: src_ref) if gather else (lambda k: src_ref.at[k])
        for ix in pieces:
            pltpu.make_async_copy(piece(me).at[ix], out_ref.at[me].at[ix], loc_sem).start()
        peers = []
        for s, (fx, fy, fc) in enumerate(flips):
            px, py, pc = (x + fx) % 2, (y + fy) % 2, (c + fc) % 2

            def copy(ix, s=s, px=px, py=py, pc=pc):
                part = (lambda r: r) if ix is None else (lambda r: r.at[ix])
                return pltpu.make_async_remote_copy(
                    src_ref=part(piece(number(px, py, pc))), dst_ref=part(out_ref.at[me]),
                    send_sem=send_sems.at[s], recv_sem=recv_sems.at[s],
                    device_id=(px, py, pc), device_id_type=MESH)

            for ix in pieces:
                copy(ix).start()
            peers.append(copy)
        for copy in peers:
            copy(None).wait()
        pltpu.make_async_copy(piece(me), out_ref.at[me], loc_sem).wait()

    return pl.pallas_call(
        body, name=name, out_shape=jax.ShapeDtypeStruct((n,) + blk_shape, src.dtype),
        in_specs=[pl.BlockSpec(memory_space=pl.ANY)], out_specs=pl.BlockSpec(memory_space=pl.ANY),
        scratch_shapes=[pltpu.SemaphoreType.DMA((n - 1,)), pltpu.SemaphoreType.DMA((n - 1,)), pltpu.SemaphoreType.DMA],
    )(src)


def _ordered_sum(name, buf):
    n = buf.shape[0]

    def fn(*t):
        acc = t[0]
        for v in t[1:]:
            acc = acc + v
        return acc

    return _rowwise(name, fn, [Rows(buf, lead=k) for k in range(n)], [], [(buf.shape[-1], F32)])[0]


def _reduce_big(name, g):
    mine = _ordered_sum(name + "_sum_c", _exchange(name + "_swap_c", g, 'c', False).reshape(2, -1, g.shape[-1]))
    mine = mine.reshape(g.shape[1:])
    tot = _ordered_sum(name + "_sum_xy", _exchange(name + "_a2a_xy", mine, 'xy', False))
    return _exchange(name + "_share_c", tot, 'c', True)


def _dilate(t, dil):
    L, C = t.shape
    return t.reshape(L // dil, dil, C).transpose(1, 0, 2)


def _undilate(t):
    dil, n, C = t.shape
    return t.transpose(1, 0, 2).reshape(dil * n, C)


def _pad_rows(t, halo):
    return jnp.pad(t, ((0, 0), (halo, halo), (0, 0)))


def _pcol(p, blk):
    return Rows(p, GROUP_WIDTH, blk)


def _pslice(p, blk):
    return p[:, blk * GROUP_WIDTH:(blk + 1) * GROUP_WIDTH]


def _seg_matrix():
    h = np.arange(GROUP_WIDTH) // HEAD_DIM
    return jnp.asarray((h[:, None] == h[None, :]).astype(np.float32) / HEAD_DIM)


def _rope_tables(L):
    pos = jnp.arange(L, dtype=F32)
    inv_freq = ROPE_THETA ** (-jnp.arange(0, HEAD_DIM, 2, dtype=F32) / HEAD_DIM)
    ang = pos[:, None] * inv_freq[None, :]
    cos, sin = jnp.cos(ang), jnp.sin(ang)
    cs = jnp.tile(jnp.concatenate([cos, cos], axis=1), (1, GROUP_HEADS))
    sn = jnp.tile(jnp.concatenate([-sin, sin], axis=1), (1, GROUP_HEADS))
    return cs, sn


def _shift_rows(t, k):
    if k == 0:
        return t
    z = jnp.zeros((abs(k), t.shape[1]), t.dtype)
    return jnp.concatenate([z, t[:-k]], axis=0) if k > 0 else jnp.concatenate([t[-k:], z], axis=0)


def _dil_branches(L):
    out = []
    for window, dil in DIL_PAIRS:
        radius = window // (2 * dil)
        n = L // dil
        out.append((dil, radius, min(256, n)))
    return out


def _dil_operands(qr, kr, p, dil, radius):
    q = _dilate(qr.astype(BF16), dil)
    k = _pad_rows(_dilate(kr.astype(BF16), dil), radius)
    v = _pad_rows(_dilate(_pslice(p, P_VD).astype(BF16), dil), radius)
    return q, k, v


def _layer_fwd(x, w, c):
    L, D = x.shape
    sv = {'x_in': x}
    h = _rowwise("mix_prenorm", _prenorm_fn, [x], [w['mix_norm_pre']], [(D, BF16)])[0]
    p = _matmul("mix_proj", h, w['w_in'], 'nn', F32)
    sv['p'] = p

    la = _rowwise("gla_pre", _gla_pre_fn, [Rows(p, LANE, P_Z // LANE)], [w['gla_wg'], w['gla_bg']], [(2 * GROUP_WIDTH, F32)])[0]
    of, ob, sf, sb = _gla_scan_fwd(p, la)
    ya = _rowwise("gla_post", _gla_post_fn, [of, ob, _pcol(p, P_GA)], [w['gla_norm'], c['seg']], [(GROUP_WIDTH, BF16)])[0]
    sv.update(la=la, sf=sf, sb=sb, of=of, ob=ob)

    yb = _na_fwd(p, _pslice(p, P_KB).astype(BF16), _pslice(p, P_VB).astype(BF16), _rpb_expand(w['na_rpb'], c['rpb_tab']))

    xb = _pslice(p, P_XC)
    xs = [_shift_rows(xb, LRU_CONV_LEFT - j) for j in range(LRU_CONV)]
    a0, a1, u0, u1 = _rowwise("lru_pre", _lru_pre_fn, xs, [w[k] for k in LRU_PARAMS], [(GROUP_WIDTH, F32)] * 4)
    hf = _lin_scan("lru_scan_f", a0, u0, False)
    hb = _lin_scan("lru_scan_b", a1, u1, True)
    yc = _rowwise("lru_post", _lru_post_fn, [hf, hb, _pcol(p, P_GC)], [], [(GROUP_WIDTH, BF16)])[0]
    sv.update(a0=a0, a1=a1, hf=hf, hb=hb)

    qr, kr = _rowwise("rope", _rope_fn, [_pcol(p, P_QD), _pcol(p, P_KD), c['cos'], c['sin']], [], [(GROUP_WIDTH, F32)] * 2)
    os_, ls_ = [], []
    for dil, radius, tq in _dil_branches(L):
        o, lse = _band_fwd(f"dil_attn{dil}", *_dil_operands(qr, kr, p, dil, radius), tq, radius)
        os_.append(_undilate(o))
        ls_.append(_undilate(lse))
    yd = _rowwise("dil_comb", _dil_comb_fn, os_ + ls_, [], [(GROUP_WIDTH, BF16)])[0]
    sv.update(qr=qr, kr=kr, dil_o=os_, dil_l=ls_)

    ycat = _assemble("mix_cat", [[ya], [yb], [yc], [yd]], BF16)
    y = _matmul("mix_out", ycat, w['w_out'], 'nn', F32)
    xm = _rowwise("mix_postnorm", _postnorm_fn, [x, y], [w['mix_norm_post']], [(D, F32)])[0]
    sv.update(ycat=ycat, y=y, x_mid=xm)

    h2 = _rowwise("ffn_prenorm", _prenorm_fn, [xm], [w['ffn_norm_pre']], [(D, BF16)])[0]
    gate = _matmul("ffn_gate", h2, w['ffn_wg'], 'nn', BF16)
    up = _matmul("ffn_up", h2, w['ffn_wu'], 'nn', BF16)
    dff = gate.shape[1]
    act = _rowwise("ffn_act", _swiglu_fn, [gate, up], [], [(dff, BF16)], ncol=dff // _tile(dff, 512))[0]
    f = _matmul("ffn_out", act, w['ffn_w_out'], 'nn', F32)
    xo = _rowwise("ffn_postnorm", _postnorm_fn, [xm, f], [w['ffn_norm_post']], [(D, F32)])[0]
    sv.update(gate=gate, up=up, act=act, f=f)
    return xo, sv


def _layer_bwd(dx, w, c, sv):
    L, D = dx.shape
    g = {}
    as_f32 = lambda t: (t.astype(F32),)
    df, g['ffn_norm_post'] = _rowwise_bwd("ffn_postnorm_b", lambda y, gn: _rms(y, gn), [sv['f']], [w['ffn_norm_post']],
                                          [dx], as_f32, [BF16], [True])
    dact = _matmul("ffn_out_bx", df, w['ffn_w_out'], 'nt', BF16)
    g['ffn_w_out'] = _matmul("ffn_out_bw", sv['act'], df, 'tn', F32)
    gate, up = sv['gate'], sv['up']
    dff = gate.shape[1]
    dgate, dup = _rowwise_bwd("ffn_act_b", _swiglu_fn, [gate, up], [], [dact], as_f32, [BF16, BF16], [], ncol=dff // _tile(dff, 512))
    dh2 = _matmul("ffn_up_bx", dup, w['ffn_wu'], 'nt', F32, acc_in=_matmul("ffn_gate_bx", dgate, w['ffn_wg'], 'nt', F32))
    xm = sv['x_mid']
    h2 = _rowwise("ffn_prenorm_r", _prenorm_fn, [xm], [w['ffn_norm_pre']], [(D, BF16)])[0]
    g['ffn_wg'] = _matmul("ffn_gate_bw", h2, dgate, 'tn', F32)
    g['ffn_wu'] = _matmul("ffn_up_bw", h2, dup, 'tn', F32)
    dxm, g['ffn_norm_pre'] = _rowwise_bwd("ffn_prenorm_b", _prenorm_fn, [xm], [w['ffn_norm_pre']], [dh2], as_f32, [F32], [True],
                                          row_grad_add=[dx])

    dy, g['mix_norm_post'] = _rowwise_bwd("mix_postnorm_b", lambda y, gn: _rms(y, gn), [sv['y']], [w['mix_norm_post']],
                                          [dxm], as_f32, [BF16], [True])
    dycat = _matmul("mix_out_bx", dy, w['w_out'], 'nt', F32)
    g['w_out'] = _matmul("mix_out_bw", sv['ycat'], dy, 'tn', F32)
    p = sv['p']
    dya, dyb, dyc, dyd = (Rows(dycat, GROUP_WIDTH, k) for k in range(4))

    dof, dga, g['gla_norm'] = _rowwise_bwd("gla_post_b", _gla_post_fn, [sv['of'], sv['ob'], _pcol(p, P_GA)],
                                           [w['gla_norm'], c['seg']], [dya], as_f32, [F32, None, F32], [True, False])
    la = sv['la']
    dqf, dkf, dvf, dlf, dqb_, dkb_, dvb_, dlb = _gla_scan_bwd(p, la, sv['sf'], sv['sb'], dof)
    dz, g['gla_wg'], g['gla_bg'] = _rowwise_bwd("gla_pre_b", _gla_pre_fn, [Rows(p, LANE, P_Z // LANE)], [w['gla_wg'], w['gla_bg']],
                                                [dlf, dlb], lambda a, b: (jnp.concatenate([a, b], axis=1),), [BF16], [True, True])

    btab = _rpb_expand(w['na_rpb'], c['rpb_tab'])
    dqn, dkn, dvn, dbt = _na_bwd(p, _pslice(p, P_KB).astype(BF16), _pslice(p, P_VB).astype(BF16), btab, dycat)
    g['na_rpb'] = _rpb_contract(dbt, c['rpb_tab'])

    dh, dgc = _rowwise_bwd("lru_post_b", _lru_post_fn, [sv['hf'], sv['hb'], _pcol(p, P_GC)], [], [dyc], as_f32, [F32, None, F32], [])
    lam0 = _lin_scan("lru_scan_f_b", _shift_rows(sv['a0'], -1), dh, True)
    lam1 = _lin_scan("lru_scan_b_b", _shift_rows(sv['a1'], 1), dh, False)
    xb = _pslice(p, P_XC)
    xs = [_shift_rows(xb, LRU_CONV_LEFT - j) for j in range(LRU_CONV)]
    res = _rowwise_bwd("lru_pre_b", _lru_pre_fn, xs, [w[k] for k in LRU_PARAMS],
                       [lam0, lam1, _shift_rows(sv['hf'], 1), _shift_rows(sv['hb'], -1)],
                       lambda l0, l1, hfp, hbn: (l0 * hfp, l1 * hbn, l0, l1), [F32] * 4, [True] * len(LRU_PARAMS))
    dxs = res[:4]
    for k, nm in enumerate(LRU_PARAMS):
        g[nm] = res[4 + k]
    dxc = [_shift_rows(dxs[j], j - LRU_CONV_LEFT) for j in range(LRU_CONV)]

    comb = _rowwise_bwd("dil_comb_b", _dil_comb_fn, sv['dil_o'] + sv['dil_l'], [], [dyd], as_f32, [F32] * 6, [])
    qr, kr = sv['qr'], sv['kr']
    dqs, dks, dvs = [], [], []
    for k, (dil, radius, tq) in enumerate(_dil_branches(L)):
        n = L // dil
        dq_, dk_, dv_ = _band_bwd(f"dil_attn{dil}_b", *_dil_operands(qr, kr, p, dil, radius),
                                  _dilate(comb[k], dil), _dilate(comb[3 + k], dil), tq, radius)
        dqs.append(_undilate(dq_))
        dks.append(_undilate(dk_[:, radius:radius + n]))
        dvs.append(_undilate(dv_[:, radius:radius + n]))
    dqd, dkd = _rowwise_bwd("rope_b", _rope_fn, [_pcol(p, P_QD), _pcol(p, P_KD), c['cos'], c['sin']], [], dqs + dks,
                            lambda a1, a2, a3, b1, b2, b3: (a1 + a2 + a3, b1 + b2 + b3), [F32, F32, None, None], [])

    dp = _assemble("mix_dp", [[dqf, dqb_], [dkf, dkb_], [dvf, dvb_], [dga], [dqn], [dkn], [dvn], dxc, [dgc], [dqd], [dkd], dvs, [dz]], BF16)
    dh1 = _matmul("mix_proj_bx", dp, w['w_in'], 'nt', F32)
    x_in = sv['x_in']
    h = _rowwise("mix_prenorm_r", _prenorm_fn, [x_in], [w['mix_norm_pre']], [(D, BF16)])[0]
    g['w_in'] = _matmul("mix_proj_bw", h, dp, 'tn', F32)
    dxi, g['mix_norm_pre'] = _rowwise_bwd("mix_prenorm_b", _prenorm_fn, [x_in], [w['mix_norm_pre']], [dh1], as_f32, [F32], [True],
                                          row_grad_add=[dxm])
    return dxi, g


def _loss_fn(y, t):
    e = y - t
    return e * (1.0 / y.shape[1]), jnp.sum(e * e, axis=0, keepdims=True)


def _gather_cols(name, shard, axis):
    full = _exchange(name, shard, 'xy', True)
    full = jnp.moveaxis(full, 0, axis)
    shp = list(shard.shape)
    shp[axis] *= 4
    return full.reshape(shp)


def _pack(arrs, mult=64 * LANE):
    flat = jnp.concatenate([a.reshape(-1) for a in arrs])
    pad = (-flat.shape[0]) % mult
    return jnp.pad(flat, (0, pad)).reshape(-1, LANE)


def _unpack(buf, shapes):
    flat, out, k = buf.reshape(-1), [], 0
    for s in shapes:
        sz = int(np.prod(s))
        out.append(flat[k:k + sz].reshape(s))
        k += sz
    return out


def _perm_in(w_in):
    pad = jnp.zeros(w_in.shape[:-1] + (D_INP - D_IN,), w_in.dtype)
    return jnp.concatenate([w_in[..., :P_QB * GROUP_WIDTH], w_in[..., P_QB * GROUP_WIDTH + 2 * GLA_RANK:],
                            w_in[..., P_QB * GROUP_WIDTH:P_QB * GROUP_WIDTH + 2 * GLA_RANK], pad], axis=-1)


def _unperm_in(g):
    return jnp.concatenate([g[..., :P_QB * GROUP_WIDTH], g[..., P_Z:P_Z + 2 * GLA_RANK], g[..., P_QB * GROUP_WIDTH:P_Z]], axis=-1)


def _block_diag(wb):
    l = wb.shape[0]
    eye = jnp.eye(GROUP_HEADS, dtype=wb.dtype)
    return jnp.einsum('lehij,hg->lehigj', wb, eye).reshape(l, 2, GROUP_WIDTH, GROUP_WIDTH)


def _block_diag_grad(gw):
    l = gw.shape[0]
    g6 = gw.reshape(l, 2, GROUP_HEADS, HEAD_DIM, GROUP_HEADS, HEAD_DIM)
    return jnp.stack([g6[:, :, h, :, h, :] for h in range(GROUP_HEADS)], axis=2)


def _gate_matrix(wg):
    l = wg.shape[0]
    m = jnp.zeros((l, LANE, 2 * GROUP_WIDTH), wg.dtype)
    for e in range(2):
        m = m.at[:, e * GLA_RANK:(e + 1) * GLA_RANK, e * GROUP_WIDTH:(e + 1) * GROUP_WIDTH].set(wg[:, e])
    return m


def _gate_matrix_grad(gm):
    return jnp.stack([gm[:, e * GLA_RANK:(e + 1) * GLA_RANK, e * GROUP_WIDTH:(e + 1) * GROUP_WIDTH] for e in range(2)], axis=1)


def _adam_fn(w, g, m, v):
    m = ADAM_B1 * m + (1.0 - ADAM_B1) * g
    v = ADAM_B2 * v + (1.0 - ADAM_B2) * (g * g)
    m_hat = m / (1.0 - ADAM_B1 ** ADAM_STEP)
    v_hat = v / (1.0 - ADAM_B2 ** ADAM_STEP)
    return -ADAM_LR * (m_hat / (jnp.sqrt(v_hat) + ADAM_EPS) + ADAM_WD * w), m, v


def _adam(name, w, g, m, v):
    shp = w.shape
    two = lambda t: t.reshape(-1, shp[-1])
    res = _rowwise(name, _adam_fn, [two(w), two(g), two(m), two(v)], [], [(shp[-1], F32)] * 3)
    return [r.reshape(shp) for r in res]


def _local_step(x, target, fw):
    L, D = x.shape
    depth = fw['w_in'].shape[0]
    cs, sn = _rope_tables(L)
    consts = {'seg': _seg_matrix(), 'rpb_tab': _rpb_tables(), 'cos': cs, 'sin': sn}
    layer = lambda l: {k: v[l] for k, v in fw.items()}
    saved = []
    for l in range(depth):
        x, sv = _layer_fwd(x, layer(l), consts)
        saved.append(sv)
    dx, sq = _rowwise("loss", _loss_fn, [x, target], [], [(D, F32)], acc_outs=[(1, D)])
    grads = [None] * depth
    for l in reversed(range(depth)):
        dx, grads[l] = _layer_bwd(dx, layer(l), consts, saved[l])
    return sq, dx, {k: jnp.stack([g[k] for g in grads]) for k in grads[0]}


def kernel(x, mix_norm_pre, mix_norm_post, w_in, gla_w_gate, gla_b_gate, gla_norm, na_rpb, lru_conv_w, lru_conv_b, lru_w_a, lru_b_a, lru_w_x, lru_b_x, lru_lambda, w_out, ffn_norm_pre, ffn_norm_post, ffn_w_in, ffn_w_out, loss_target, m_mix_norm_pre, m_mix_norm_post, m_w_in, m_gla_w_gate, m_gla_b_gate, m_gla_norm, m_na_rpb, m_lru_conv_w, m_lru_conv_b, m_lru_w_a, m_lru_b_a, m_lru_w_x, m_lru_b_x, m_lru_lambda, m_w_out, m_ffn_norm_pre, m_ffn_norm_post, m_ffn_w_in, m_ffn_w_out, v_mix_norm_pre, v_mix_norm_post, v_w_in, v_gla_w_gate, v_gla_b_gate, v_gla_norm, v_na_rpb, v_lru_conv_w, v_lru_conv_b, v_lru_w_a, v_lru_b_a, v_lru_w_x, v_lru_b_x, v_lru_lambda, v_w_out, v_ffn_norm_pre, v_ffn_norm_post, v_ffn_w_in, v_ffn_w_out):
    args = (mix_norm_pre, mix_norm_post, w_in, gla_w_gate, gla_b_gate, gla_norm, na_rpb, lru_conv_w, lru_conv_b, lru_w_a, lru_b_a, lru_w_x, lru_b_x, lru_lambda, w_out, ffn_norm_pre, ffn_norm_post, ffn_w_in, ffn_w_out,
            m_mix_norm_pre, m_mix_norm_post, m_w_in, m_gla_w_gate, m_gla_b_gate, m_gla_norm, m_na_rpb, m_lru_conv_w, m_lru_conv_b, m_lru_w_a, m_lru_b_a, m_lru_w_x, m_lru_b_x, m_lru_lambda, m_w_out, m_ffn_norm_pre, m_ffn_norm_post, m_ffn_w_in, m_ffn_w_out,
            v_mix_norm_pre, v_mix_norm_post, v_w_in, v_gla_w_gate, v_gla_b_gate, v_gla_norm, v_na_rpb, v_lru_conv_w, v_lru_conv_b, v_lru_w_a, v_lru_b_a, v_lru_w_x, v_lru_b_x, v_lru_lambda, v_w_out, v_ffn_norm_pre, v_ffn_norm_post, v_ffn_w_in, v_ffn_w_out)
    nw = len(WEIGHTS)
    W = dict(zip(WEIGHTS, args[:nw]))
    M = dict(zip(WEIGHTS, args[nw:2 * nw]))
    V = dict(zip(WEIGHTS, args[2 * nw:]))
    chip = 2 * lax.axis_index("x") + lax.axis_index("y")

    full = dict(W)
    full['w_in'] = _gather_cols("ag_w_in", w_in.astype(BF16), 2)
    full['ffn_w_in'] = _gather_cols("ag_ffn_w_in", ffn_w_in.astype(BF16), 2)
    full['w_out'] = _gather_cols("ag_w_out", w_out.astype(BF16), 1)
    full['ffn_w_out'] = _gather_cols("ag_ffn_w_out", ffn_w_out.astype(BF16), 1)
    small = list(SMALL_SHARDED)
    got = _exchange("ag_small", _pack([W[k] for k in small]), 'xy', True)
    for k, parts in zip(small, zip(*[_unpack(got[j], [W[k].shape for k in small]) for j in range(4)])):
        ax = SMALL_SHARDED[k]
        stacked = jnp.moveaxis(jnp.stack(parts), 0, ax)
        shp = list(W[k].shape)
        shp[ax] *= 4
        full[k] = stacked.reshape(shp)

    sq, dx0, g = _local_step(x[0], loss_target[0], _layer_weights(full))
    loss = lax.psum(0.5 * jnp.sum(sq) / x.shape[-1], ("x", "y", "c"))
    gfull = _stored_grads(g)

    grad = {}
    for k in BIG:
        gk = gfull[k]
        if k in ('w_in', 'ffn_w_in'):
            l, K, N = gk.shape
            cut = gk.reshape(2, l * K // 2, 4, N // 4).transpose(0, 2, 1, 3)
            red = _reduce_big("rs_" + k, cut)
            grad[k] = red.reshape(l, K, N // 4)
        else:
            l, K, N = gk.shape
            cut = gk.reshape(2, l // 2, 4, K // 4, N).transpose(0, 2, 1, 3, 4).reshape(2, 4, (l // 2) * (K // 4), N)
            red = _reduce_big("rs_" + k, cut)
            grad[k] = red.reshape(l, K // 4, N)
    rest = [k for k in WEIGHTS if k not in BIG]
    allg = _exchange("ar_small", _pack([gfull[k] for k in rest]), 'xyc', True)
    summed = _unpack(_ordered_sum("ar_small_sum", allg), [gfull[k].shape for k in rest])
    for k, s in zip(rest, summed):
        if k in SMALL_SHARDED:
            ax = SMALL_SHARDED[k]
            n = W[k].shape[ax]
            s = lax.dynamic_slice_in_dim(s, chip * n, n, axis=ax)
        grad[k] = s

    delta, new_m, new_v = {}, {}, {}
    for k in BIG:
        delta[k], new_m[k], new_v[k] = _adam("adam_" + k, W[k], grad[k], M[k], V[k])
    shapes = [W[k].shape for k in rest]
    res = _rowwise("adam_small", _adam_fn, [_pack([d[k] for k in rest]) for d in (W, grad, M, V)], [], [(LANE, F32)] * 3)
    for d, r in zip((delta, new_m, new_v), res):
        for k, t in zip(rest, _unpack(r, shapes)):
            d[k] = t

    return (loss, dx0[None], *[grad[k] for k in WEIGHTS], *[delta[k] for k in WEIGHTS],
            *[new_m[k] for k in WEIGHTS], *[new_v[k] for k in WEIGHTS])


def _layer_weights(full):
    depth = full['w_in'].shape[0]
    dff = full['ffn_w_in'].shape[-1] // 2
    row = lambda t: t[:, None, :]
    fw = {
        'mix_norm_pre': row(full['mix_norm_pre']), 'mix_norm_post': row(full['mix_norm_post']),
        'ffn_norm_pre': row(full['ffn_norm_pre']), 'ffn_norm_post': row(full['ffn_norm_post']),
        'w_in': _perm_in(full['w_in']), 'w_out': full['w_out'],
        'ffn_wg': full['ffn_w_in'][..., :dff], 'ffn_wu': full['ffn_w_in'][..., dff:], 'ffn_w_out': full['ffn_w_out'],
        'gla_wg': _gate_matrix(full['gla_w_gate']), 'gla_bg': full['gla_b_gate'].reshape(depth, 1, 2 * GROUP_WIDTH),
        'gla_norm': row(full['gla_norm']), 'na_rpb': full['na_rpb'],
        'lru_cb': row(full['lru_conv_b']),
    }
    wa_bd, wx_bd = _block_diag(full['lru_w_a']), _block_diag(full['lru_w_x'])
    for j in range(LRU_CONV):
        fw[f'lru_cw{j}'] = row(full['lru_conv_w'][:, j])
    for e in range(2):
        fw[f'lru_wa{e}'], fw[f'lru_wx{e}'] = wa_bd[:, e], wx_bd[:, e]
        fw[f'lru_ba{e}'], fw[f'lru_bx{e}'] = row(full['lru_b_a'][:, e]), row(full['lru_b_x'][:, e])
        fw[f'lru_lam{e}'] = row(full['lru_lambda'][:, e])
    return fw


def _stored_grads(g):
    depth = g['w_in'].shape[0]
    return {
        'mix_norm_pre': g['mix_norm_pre'][:, 0], 'mix_norm_post': g['mix_norm_post'][:, 0],
        'ffn_norm_pre': g['ffn_norm_pre'][:, 0], 'ffn_norm_post': g['ffn_norm_post'][:, 0],
        'w_in': _unperm_in(g['w_in']), 'w_out': g['w_out'],
        'ffn_w_in': jnp.concatenate([g['ffn_wg'], g['ffn_wu']], axis=-1), 'ffn_w_out': g['ffn_w_out'],
        'gla_w_gate': _gate_matrix_grad(g['gla_wg']), 'gla_b_gate': g['gla_bg'].reshape(depth, 2, GROUP_WIDTH),
        'gla_norm': g['gla_norm'][:, 0], 'na_rpb': g['na_rpb'],
        'lru_conv_w': jnp.stack([g[f'lru_cw{j}'][:, 0] for j in range(LRU_CONV)], axis=1), 'lru_conv_b': g['lru_cb'][:, 0],
        'lru_w_a': _block_diag_grad(jnp.stack([g['lru_wa0'], g['lru_wa1']], axis=1)),
        'lru_w_x': _block_diag_grad(jnp.stack([g['lru_wx0'], g['lru_wx1']], axis=1)),
        'lru_b_a': jnp.stack([g['lru_ba0'][:, 0], g['lru_ba1'][:, 0]], axis=1),
        'lru_b_x': jnp.stack([g['lru_bx0'][:, 0], g['lru_bx1'][:, 0]], axis=1),
        'lru_lambda': jnp.stack([g['lru_lam0'][:, 0], g['lru_lam1'][:, 0]], axis=1),
    }
```

```python
import functools

import numpy as np
import jax
import jax.numpy as jnp
from jax import lax
from jax.experimental import pallas as pl
from jax.experimental.pallas import tpu as pltpu

F32, BF16 = jnp.float32, jnp.bfloat16
HIGHEST = lax.Precision.HIGHEST
MESH = pl.DeviceIdType.MESH

HEAD_DIM = 64
GROUP_HEADS = 4
GROUP_WIDTH = GROUP_HEADS * HEAD_DIM
GLA_RANK = 16
GLA_TAU = 16.0
GLA_CHUNK = 64
GRID_W = 64
NA_ROWS = 8
NA_COLS = 16
LRU_CONV = 4
LRU_CONV_LEFT = 2
LRU_C = 8.0
DIL_PAIRS = ((128, 1), (512, 4), (2048, 16))
ROPE_THETA = 10000.0
EPS = 1e-6
ADAM_LR, ADAM_B1, ADAM_B2, ADAM_EPS, ADAM_WD, ADAM_STEP = 0.001, 0.9, 0.999, 1e-08, 0.01, 10
NEG = -1e30

LANE = 128
VMEM_LIMIT = 56 * 1024 * 1024
ROW_BUDGET = 16 * 1024 * 1024
DMA_PIECES = 8

P_QA, P_KA, P_VA, P_GA, P_QB, P_KB, P_VB, P_XC, P_GC, P_QD, P_KD, P_VD = range(12)
P_Z = 12 * GROUP_WIDTH
D_IN = 12 * GROUP_WIDTH + 2 * GLA_RANK
D_INP = 12 * GROUP_WIDTH + LANE

WEIGHTS = ['mix_norm_pre', 'mix_norm_post', 'w_in', 'gla_w_gate', 'gla_b_gate', 'gla_norm', 'na_rpb',
           'lru_conv_w', 'lru_conv_b', 'lru_w_a', 'lru_b_a', 'lru_w_x', 'lru_b_x', 'lru_lambda', 'w_out',
           'ffn_norm_pre', 'ffn_norm_post', 'ffn_w_in', 'ffn_w_out']
BIG = ('w_in', 'w_out', 'ffn_w_in', 'ffn_w_out')
SMALL_SHARDED = {'gla_w_gate': 3, 'gla_b_gate': 2, 'lru_conv_w': 2, 'lru_b_a': 2, 'lru_b_x': 2, 'lru_lambda': 2}
HEADS = [slice(h * HEAD_DIM, (h + 1) * HEAD_DIM) for h in range(GROUP_HEADS)]


def _cparams(sem=None):
    return pltpu.CompilerParams(dimension_semantics=sem, vmem_limit_bytes=VMEM_LIMIT)


def _tile(dim, target, mult=LANE):
    best = None
    for t in range(mult, min(dim, target) + 1, mult):
        if dim % t == 0:
            best = t
    return best or dim


class Rows:
    def __init__(self, a, w=None, cb=0, lead=None):
        self.a, self.cb, self.lead = a, cb, lead
        self.w = a.shape[-1] if w is None else w
        self.nrows = a.shape[-2]

    def spec(self, tm, ncol=1):
        w = self.w // ncol
        if self.lead is None:
            return pl.BlockSpec((tm, w), lambda i, j, cb=self.cb: (i, cb * ncol + j))
        return pl.BlockSpec((None, tm, w), lambda i, j, cb=self.cb, k=self.lead: (k, i, cb * ncol + j))

    def nbytes(self):
        return self.w * self.a.dtype.itemsize


def _as_rows(rs):
    return [r if isinstance(r, Rows) else Rows(r) for r in rs]


def _pick_tm(nrows, row_bytes, scale):
    tm = 512
    while tm > 16 and (tm * row_bytes * scale > ROW_BUDGET or nrows % tm):
        tm //= 2
    assert nrows % tm == 0, (nrows, tm)
    return tm


def _full_spec(a):
    nd = a.ndim
    return pl.BlockSpec(a.shape, lambda i, j, nd=nd: (0,) * nd)


def _rowwise(name, fn, rows, params, outs, acc_outs=(), ncol=1):
    rows = _as_rows(rows)
    nrows = rows[0].nrows
    assert ncol == 1 or not (acc_outs or params)
    tm = _pick_tm(nrows, (sum(r.nbytes() for r in rows) + sum(w * jnp.dtype(d).itemsize for w, d in outs)) // ncol, 2)
    n_r, n_p, n_o = len(rows), len(params), len(outs)

    def body(*refs):
        vals = [r[...] for r in refs[:n_r + n_p]]
        res = fn(*vals)
        res = res if isinstance(res, (tuple, list)) else (res,)
        orefs = refs[n_r + n_p:]
        for o, v in zip(orefs[:n_o], res[:n_o]):
            o[...] = v.astype(o.dtype)
        for o, v in zip(orefs[n_o:], res[n_o:]):
            @pl.when(pl.program_id(0) == 0)
            def _(o=o):
                o[...] = jnp.zeros_like(o)
            o[...] += v

    out_shape = [jax.ShapeDtypeStruct((nrows, w), d) for w, d in outs] + [jax.ShapeDtypeStruct(s, F32) for s in acc_outs]
    out_specs = [pl.BlockSpec((tm, w // ncol), lambda i, j: (i, j)) for w, _ in outs] + \
                [pl.BlockSpec(s, lambda i, j, nd=len(s): (0,) * nd) for s in acc_outs]
    return pl.pallas_call(
        body, name=name, grid=(nrows // tm, ncol),
        in_specs=[r.spec(tm, ncol) for r in rows] + [_full_spec(p) for p in params],
        out_specs=out_specs, out_shape=out_shape,
        compiler_params=_cparams(("arbitrary", "arbitrary") if acc_outs else ("parallel", "parallel")),
    )(*[r.a for r in rows], *params)


def _rowwise_bwd(name, fn, rows, params, ct_rows, ct_fn, row_grads, param_grads, row_grad_add=None, ncol=1):
    rows, ct_rows = _as_rows(rows), _as_rows(ct_rows)
    nrows = rows[0].nrows
    n_rg = sum(d is not None for d in row_grads)
    adds = [a for a in (row_grad_add or []) if a is not None]
    add_at = [k for k, a in enumerate(row_grad_add or []) if a is not None]
    assert ncol == 1 or not (params or adds)
    gbytes = sum(r.w * jnp.dtype(d).itemsize for r, d in zip(rows, row_grads) if d is not None)
    tm = _pick_tm(nrows, (sum(r.nbytes() for r in rows + ct_rows + _as_rows(adds)) + gbytes) // ncol, 4)
    n_r, n_p, n_c, n_a = len(rows), len(params), len(ct_rows), len(adds)
    diff = [k for k, d in enumerate(row_grads) if d is not None] + [n_r + k for k, g in enumerate(param_grads) if g]

    def body(*refs):
        vals = [r[...] for r in refs[:n_r + n_p]]
        cts_in = [r[...] for r in refs[n_r + n_p:n_r + n_p + n_c]]
        add_in = [r[...] for r in refs[n_r + n_p + n_c:n_r + n_p + n_c + n_a]]
        orefs = refs[n_r + n_p + n_c + n_a:]

        def f(*dv):
            full = list(vals)
            for k, v in zip(diff, dv):
                full[k] = v
            res = fn(*full)
            return tuple(res) if isinstance(res, (tuple, list)) else (res,)

        outs, vjp = jax.vjp(f, *[vals[k].astype(F32) for k in diff])
        cts = ct_fn(*cts_in)
        cts = cts if isinstance(cts, (tuple, list)) else (cts,)
        grads = list(vjp(tuple(c.astype(o.dtype) for c, o in zip(cts, outs))))
        for k, a in zip(add_at, add_in):
            grads[k] = grads[k] + a.astype(F32)
        for o, g in zip(orefs[:n_rg], grads[:n_rg]):
            o[...] = g.astype(o.dtype)
        for o, g in zip(orefs[n_rg:], grads[n_rg:]):
            @pl.when(pl.program_id(0) == 0)
            def _(o=o):
                o[...] = jnp.zeros_like(o)
            o[...] += g.astype(F32)

    out_shape = [jax.ShapeDtypeStruct((nrows, r.w), d) for r, d in zip(rows, row_grads) if d is not None] + \
                [jax.ShapeDtypeStruct(p.shape, F32) for p, g in zip(params, param_grads) if g]
    out_specs = [pl.BlockSpec((tm, r.w // ncol), lambda i, j: (i, j)) for r, d in zip(rows, row_grads) if d is not None] + \
                [_full_spec(p) for p, g in zip(params, param_grads) if g]
    return pl.pallas_call(
        body, name=name, grid=(nrows // tm, ncol),
        in_specs=[r.spec(tm, ncol) for r in rows] + [_full_spec(p) for p in params] + [r.spec(tm, ncol) for r in ct_rows]
        + [r.spec(tm, ncol) for r in _as_rows(adds)],
        out_specs=out_specs, out_shape=out_shape,
        compiler_params=_cparams(("arbitrary", "arbitrary")),
    )(*[r.a for r in rows], *params, *[r.a for r in ct_rows], *adds)


def _assemble(name, groups, dtype):
    sizes = [len(g) for g in groups]
    flat = [a for g in groups for a in g]

    def fn(*tiles):
        out, k = [], 0
        for s in sizes:
            acc = tiles[k].astype(F32)
            for t in tiles[k + 1:k + s]:
                acc = acc + t.astype(F32)
            out.append(acc.astype(dtype))
            k += s
        return out[0] if len(out) == 1 else jnp.concatenate(out, axis=1)

    width = sum(g[0].shape[-1] if not isinstance(g[0], Rows) else g[0].w for g in groups)
    return _rowwise(name, fn, flat, [], [(width, dtype)])[0]


def _matmul(name, a, b, mode, out_dtype, acc_in=None):
    if mode == 'nn':
        (M, K), N = a.shape, b.shape[1]
    elif mode == 'nt':
        (M, K), N = a.shape, b.shape[0]
    else:
        (K, M), N = a.shape, b.shape[1]
    tm, tn, tk = _tile(M, 1536), _tile(N, 1536), _tile(K, 1536)
    nk = K // tk
    dn = {'nn': (((1,), (0,)), ((), ())), 'nt': (((1,), (1,)), ((), ())), 'tn': (((0,), (0,)), ((), ()))}[mode]
    has_acc = acc_in is not None

    def body(*refs):
        a_ref, b_ref = refs[:2]
        o_ref, acc = refs[-2:]

        @pl.when(pl.program_id(2) == 0)
        def _():
            acc[...] = refs[2][...] if has_acc else jnp.zeros_like(acc)
        acc[...] += lax.dot_general(a_ref[...].astype(BF16), b_ref[...].astype(BF16), dn, preferred_element_type=F32)

        @pl.when(pl.program_id(2) == nk - 1)
        def _():
            o_ref[...] = acc[...].astype(o_ref.dtype)

    a_spec = pl.BlockSpec((tk, tm), lambda i, j, k: (k, i)) if mode == 'tn' else pl.BlockSpec((tm, tk), lambda i, j, k: (i, k))
    b_spec = pl.BlockSpec((tn, tk), lambda i, j, k: (j, k)) if mode == 'nt' else pl.BlockSpec((tk, tn), lambda i, j, k: (k, j))
    o_spec = pl.BlockSpec((tm, tn), lambda i, j, k: (i, j))
    return pl.pallas_call(
        body, name=name, grid=(M // tm, N // tn, nk),
        in_specs=[a_spec, b_spec] + ([o_spec] if has_acc else []), out_specs=o_spec,
        out_shape=jax.ShapeDtypeStruct((M, N), out_dtype),
        scratch_shapes=[pltpu.VMEM((tm, tn), F32)],
        compiler_params=_cparams(("parallel", "parallel", "arbitrary")),
    )(a, b, *([acc_in] if has_acc else []))


def _small_dot(name, a, b, mode):
    dn = {'nn': (((1,), (0,)), ((), ())), 'nt': (((1,), (1,)), ((), ()))}[mode]
    M = a.shape[0]
    N = b.shape[1] if mode == 'nn' else b.shape[0]

    def body(a_ref, b_ref, o_ref):
        o_ref[...] = lax.dot_general(a_ref[...], b_ref[...], dn, precision=HIGHEST, preferred_element_type=F32)

    return pl.pallas_call(body, name=name, out_shape=jax.ShapeDtypeStruct((M, N), F32),
                          compiler_params=pltpu.CompilerParams(vmem_limit_bytes=VMEM_LIMIT))(a, b)


NN, NT, TN = (((1,), (0,)), ((), ())), (((1,), (1,)), ((), ())), (((0,), (0,)), ((), ()))


def _bdot(a, b, dn=NN):
    return lax.dot_general(a.astype(BF16), b.astype(BF16), dn, preferred_element_type=F32)


def _sigmoid(x):
    return 0.5 * jnp.tanh(0.5 * x) + 0.5


def _silu(x):
    return x * _sigmoid(x)


def _softplus(x):
    return jnp.maximum(x, 0.0) + jnp.log(1.0 + jnp.exp(-jnp.abs(x)))


def _gelu(x):
    return 0.5 * x * (1.0 + jnp.tanh(0.7978845608028654 * (x + 0.044715 * (x * x * x))))


def _rms(x, g):
    return x * lax.rsqrt(jnp.mean(x * x, axis=-1, keepdims=True) + EPS) * g


def _prenorm_fn(x, g):
    return _rms(x, g)


def _postnorm_fn(x, y, g):
    return x + _rms(y, g)


def _swiglu_fn(gate, up):
    return _silu(gate.astype(F32)) * up.astype(F32)


def _gla_pre_fn(z, wg, bg):
    logit = _bdot(z, wg) + bg
    return -_softplus(-logit) * (1.0 / GLA_TAU)


def _seg_mean(x, seg):
    return lax.dot_general(x, seg, NN, precision=HIGHEST, preferred_element_type=F32)


def _gla_post_fn(of, ob, g, norm, seg):
    o = of + ob
    o = o * lax.rsqrt(_seg_mean(o * o, seg) + EPS) * norm
    return o * _silu(g)


@jax.custom_vjp
def _swap_halves(x):
    n = x.shape[-1]
    lane = lax.broadcasted_iota(jnp.int32, x.shape, x.ndim - 1)
    lo = (lane & (HEAD_DIM - 1)) < HEAD_DIM // 2
    return jnp.where(lo, pltpu.roll(x, n - HEAD_DIM // 2, x.ndim - 1), pltpu.roll(x, HEAD_DIM // 2, x.ndim - 1))


_swap_halves.defvjp(lambda x: (_swap_halves(x), None), lambda _, g: (_swap_halves(g),))


def _rope_fn(q, k, cs, sn):
    return q * cs + _swap_halves(q) * sn, k * cs + _swap_halves(k) * sn


def _dil_comb_fn(o1, o2, o3, l1, l2, l3):
    m = jnp.maximum(jnp.maximum(l1, l2), l3)
    e1, e2, e3 = jnp.exp(l1 - m), jnp.exp(l2 - m), jnp.exp(l3 - m)
    return (e1 * o1 + e2 * o2 + e3 * o3) / (e1 + e2 + e3)


LRU_PARAMS = ['lru_cw0', 'lru_cw1', 'lru_cw2', 'lru_cw3', 'lru_cb', 'lru_wa0', 'lru_wa1', 'lru_ba0', 'lru_ba1',
              'lru_wx0', 'lru_wx1', 'lru_bx0', 'lru_bx1', 'lru_lam0', 'lru_lam1']


def _lru_pre_fn(x0, x1, x2, x3, cw0, cw1, cw2, cw3, cb, wa0, wa1, ba0, ba1, wx0, wx1, bx0, bx1, lam0, lam1):
    xc = cb + x0 * cw0 + x1 * cw1 + x2 * cw2 + x3 * cw3
    outs = []
    for wa, ba, wx, bx, lam in ((wa0, ba0, wx0, bx0, lam0), (wa1, ba1, wx1, bx1, lam1)):
        r = _sigmoid(_bdot(xc, wa) + ba)
        i = _sigmoid(_bdot(xc, wx) + bx)
        log_a = -LRU_C * r * _softplus(-lam)
        a = jnp.exp(log_a)
        u = jnp.sqrt(-jnp.tanh(log_a) * (a * a + 1.0)) * (i * xc)
        outs += [a, u]
    return outs[0], outs[2], outs[1], outs[3]


def _lru_post_fn(hf, hb, gate):
    return (hf + hb) * _gelu(gate)


def _attn_heads(qs, kws, vws, biases):
    ss = [_bdot(q, kw, NT) * (HEAD_DIM ** -0.5) + b for q, kw, b in zip(qs, kws, biases)]
    ms = [lax.stop_gradient(jnp.max(s, axis=-1, keepdims=True)) for s in ss]
    es = [jnp.exp(s - m) for s, m in zip(ss, ms)]
    dens = [jnp.sum(e, axis=-1, keepdims=True) for e in es]
    ps = [e * (1.0 / d) for e, d in zip(es, dens)]
    os_ = [_bdot(p_, vw) for p_, vw in zip(ps, vws)]
    return os_, [m + jnp.log(d) for m, d in zip(ms, dens)]


def _gla_chunks(qs, ks, vs, las, sts, revs):
    C = qs[0].shape[0]
    ti = lax.broadcasted_iota(jnp.int32, (C, C), 0)
    si = lax.broadcasted_iota(jnp.int32, (C, C), 1)
    row = lax.broadcasted_iota(jnp.int32, (C, 1), 0)
    incl = {False: si <= ti, True: si >= ti}
    mid = {False: row < C // 2, True: row >= C // 2}
    bs = [lax.dot_general(incl[r].astype(F32), la, NN, precision=HIGHEST, preferred_element_type=F32) for la, r in zip(las, revs)]
    bls = [jnp.sum(la, axis=0, keepdims=True) for la in las]
    bms = [jnp.sum(jnp.where(mid[r], la, 0.0), axis=0, keepdims=True) for la, r in zip(las, revs)]
    qss = [q * (HEAD_DIM ** -0.5) for q in qs]
    qi = [q * jnp.exp(b - bm) for q, b, bm in zip(qss, bs, bms)]
    ki = [k * jnp.exp(bm - b) for k, b, bm in zip(ks, bs, bms)]
    atts = [jnp.where(incl[r], _bdot(a, b, NT), 0.0) for a, b, r in zip(qi, ki, revs)]
    qe = [q * jnp.exp(b) for q, b in zip(qss, bs)]
    kl = [k * jnp.exp(bl - b) for k, b, bl in zip(ks, bs, bls)]
    o1 = [_bdot(a, v) for a, v in zip(atts, vs)]
    o2 = [_bdot(q, st, NT) for q, st in zip(qe, sts)]
    kvs = [_bdot(v, k, TN) for v, k in zip(vs, kl)]
    return [a + b for a, b in zip(o1, o2)], [st * jnp.exp(bl) + kv for st, bl, kv in zip(sts, bls, kvs)]


def _gla_specs(n, blocks, first):
    C = GLA_CHUNK
    at = (lambda i: i) if first else (lambda i: n - 1 - i)
    return [pl.BlockSpec((C, GROUP_WIDTH), lambda i, b=b: (at(i), b)) for b in blocks], at


def _gla_scan_fwd(p, la):
    L = p.shape[0]
    C, H, dh = GLA_CHUNK, GROUP_HEADS, HEAD_DIM
    n = L // C
    f_specs, f_at = _gla_specs(n, (P_QA, P_KA, P_VA), True)
    b_specs, b_at = _gla_specs(n, (P_QA, P_KA, P_VA), False)
    tile = lambda at, blk=0: pl.BlockSpec((C, GROUP_WIDTH), lambda i: (at(i), blk))
    st_spec = lambda at: pl.BlockSpec((None, H, dh, dh), lambda i: (at(i), 0, 0, 0))

    def body(qf, kf, vf, lf, qb, kb, vb, lb, of_ref, ob_ref, sf_ref, sb_ref, stf, stb):
        @pl.when(pl.program_id(0) == 0)
        def _():
            stf[...] = jnp.zeros_like(stf)
            stb[...] = jnp.zeros_like(stb)
        sf_ref[...] = stf[...]
        sb_ref[...] = stb[...]
        chains = [(t, h, sl) for t in ((qf, kf, vf, lf, of_ref, stf, False), (qb, kb, vb, lb, ob_ref, stb, True))
                  for h, sl in enumerate(HEADS)]
        os_, sts = _gla_chunks(*[[t[j][:, sl] for t, h, sl in chains] for j in range(4)],
                               [t[5][h] for t, h, sl in chains], [t[6] for t, h, sl in chains])
        for (t, h, sl), o, st_new in zip(chains, os_, sts):
            t[4][:, sl] = o
            t[5][h] = st_new

    return pl.pallas_call(
        body, name="gla_scan", grid=(n,),
        in_specs=f_specs + [tile(f_at, 0)] + b_specs + [tile(b_at, 1)],
        out_specs=[tile(f_at), tile(b_at), st_spec(f_at), st_spec(b_at)],
        out_shape=[jax.ShapeDtypeStruct((L, GROUP_WIDTH), F32)] * 2 + [jax.ShapeDtypeStruct((n, H, dh, dh), F32)] * 2,
        scratch_shapes=[pltpu.VMEM((H, dh, dh), F32)] * 2,
        compiler_params=_cparams(("arbitrary",)),
    )(p, p, p, la, p, p, p, la)


def _gla_scan_bwd(p, la, sf, sb, do):
    L = p.shape[0]
    C, H, dh = GLA_CHUNK, GROUP_HEADS, HEAD_DIM
    n = L // C
    f_specs, f_at = _gla_specs(n, (P_QA, P_KA, P_VA), False)
    b_specs, b_at = _gla_specs(n, (P_QA, P_KA, P_VA), True)
    tile = lambda at, blk=0: pl.BlockSpec((C, GROUP_WIDTH), lambda i: (at(i), blk))
    st_spec = lambda at: pl.BlockSpec((None, H, dh, dh), lambda i: (at(i), 0, 0, 0))

    def body(qf, kf, vf, lf, spf, dof, qb, kb, vb, lb, spb, dob, *rest):
        outs_f, outs_b, (dstf, dstb) = rest[0:4], rest[4:8], rest[8:10]

        @pl.when(pl.program_id(0) == 0)
        def _():
            dstf[...] = jnp.zeros_like(dstf)
            dstb[...] = jnp.zeros_like(dstb)
        chains = [(t, h, sl) for t in ((qf, kf, vf, lf, spf, dof, outs_f, dstf, False), (qb, kb, vb, lb, spb, dob, outs_b, dstb, True))
                  for h, sl in enumerate(HEADS)]
        nc = len(chains)
        revs = [t[8] for t, h, sl in chains]
        flat = [t[j][:, sl] for j in range(4) for t, h, sl in chains] + [t[4][h] for t, h, sl in chains]

        def f(*a):
            os_, sts = _gla_chunks(*[list(a[j * nc:(j + 1) * nc]) for j in range(5)], revs)
            return tuple(os_) + tuple(sts)

        _, vjp = jax.vjp(f, *flat)
        grads = vjp(tuple(t[5][:, sl] for t, h, sl in chains) + tuple(t[7][h] for t, h, sl in chains))
        for c_, (t, h, sl) in enumerate(chains):
            for j in range(4):
                t[6][j][:, sl] = grads[j * nc + c_]
            t[7][h] = grads[4 * nc + c_]

    return pl.pallas_call(
        body, name="gla_scan_b", grid=(n,),
        in_specs=f_specs + [tile(f_at, 0), st_spec(f_at), tile(f_at)] + b_specs + [tile(b_at, 1), st_spec(b_at), tile(b_at)],
        out_specs=[tile(f_at)] * 4 + [tile(b_at)] * 4,
        out_shape=[jax.ShapeDtypeStruct((L, GROUP_WIDTH), F32)] * 8,
        scratch_shapes=[pltpu.VMEM((H, dh, dh), F32)] * 2,
        compiler_params=_cparams(("arbitrary",)),
    )(p, p, p, la, sf, do, p, p, p, la, sb, do)


NA_W = NA_ROWS * GRID_W
NA_BW = (2 * NA_ROWS - 1) * GRID_W


def _na_start(i, rows):
    return jnp.clip(i - NA_ROWS // 2, 0, rows - NA_ROWS)


def _na_fwd(p, kb, vb, btab):
    L = p.shape[0]
    rows = L // GRID_W

    def body(q_ref, k_ref, v_ref, b_ref, o_ref):
        r = pl.program_id(0)
        s = _na_start(r, rows)
        start = pl.multiple_of(s * GRID_W, GRID_W)
        os_, _ = _attn_heads([q_ref[:, sl] for sl in HEADS], [k_ref[pl.ds(start, NA_W), sl] for sl in HEADS],
                             [v_ref[pl.ds(start, NA_W), sl] for sl in HEADS],
                             [b_ref[s - r + NA_ROWS - 1, h] for h in range(GROUP_HEADS)])
        for sl, o in zip(HEADS, os_):
            o_ref[:, sl] = o.astype(o_ref.dtype)

    whole = lambda a: pl.BlockSpec(a.shape, lambda i, nd=a.ndim: (0,) * nd)
    return pl.pallas_call(
        body, name="na_attn", grid=(rows,),
        in_specs=[pl.BlockSpec((GRID_W, GROUP_WIDTH), lambda i: (i, P_QB)), whole(kb), whole(vb), whole(btab)],
        out_specs=pl.BlockSpec((GRID_W, GROUP_WIDTH), lambda i: (i, 0)),
        out_shape=jax.ShapeDtypeStruct((L, GROUP_WIDTH), BF16),
        compiler_params=_cparams(("arbitrary",)),
    )(p, kb, vb, btab)


def _na_bwd(p, kb, vb, btab, dycat):
    L = p.shape[0]
    rows = L // GRID_W
    flush = NA_ROWS - 1
    emit = lambda i: jnp.where(i < rows, _na_start(i, rows), i - flush)

    def body(q_ref, k_ref, v_ref, b_ref, do_ref, dq_ref, dk_ref, dv_ref, db_ref, acc_k, acc_v):
        i = pl.program_id(0)

        @pl.when(i == 0)
        def _():
            acc_k[...] = jnp.zeros_like(acc_k)
            acc_v[...] = jnp.zeros_like(acc_v)
            db_ref[...] = jnp.zeros_like(db_ref)

        @pl.when((i > 0) & (emit(i) != emit(i - 1)))
        def _():
            for acc in (acc_k, acc_v):
                moved = acc[GRID_W:NA_W, :]
                acc[0:NA_W - GRID_W, :] = moved
                acc[NA_W - GRID_W:NA_W, :] = jnp.zeros((GRID_W, GROUP_WIDTH), F32)

        @pl.when(i < rows)
        def _():
            s = _na_start(i, rows)
            sv = s - i + NA_ROWS - 1
            start = pl.multiple_of(s * GRID_W, GRID_W)
            H = GROUP_HEADS
            flat = [q_ref[:, sl] for sl in HEADS] + [k_ref[pl.ds(start, NA_W), sl].astype(F32) for sl in HEADS] + \
                   [v_ref[pl.ds(start, NA_W), sl].astype(F32) for sl in HEADS] + [b_ref[sv, h] for h in range(H)]

            def f(*a):
                os_, lses = _attn_heads(a[0:H], a[H:2 * H], a[2 * H:3 * H], a[3 * H:4 * H])
                return tuple(os_) + tuple(lses)

            _, vjp = jax.vjp(f, *flat)
            grads = vjp(tuple(do_ref[:, sl] for sl in HEADS) + (jnp.zeros((GRID_W, 1), F32),) * H)
            for h, sl in enumerate(HEADS):
                dq_ref[:, sl] = grads[h]
                acc_k[:, sl] += grads[H + h]
                acc_v[:, sl] += grads[2 * H + h]
                db_ref[sv, h] += grads[3 * H + h]

        dk_ref[...] = acc_k[0:GRID_W, :]
        dv_ref[...] = acc_v[0:GRID_W, :]

    whole = lambda a: pl.BlockSpec(a.shape, lambda i, nd=a.ndim: (0,) * nd)
    qrow = lambda blk: pl.BlockSpec((GRID_W, GROUP_WIDTH), lambda i: (jnp.minimum(i, rows - 1), blk))
    erow = pl.BlockSpec((GRID_W, GROUP_WIDTH), lambda i: (emit(i), 0))
    return pl.pallas_call(
        body, name="na_attn_b", grid=(rows + flush,),
        in_specs=[qrow(P_QB), whole(kb), whole(vb), whole(btab), qrow(1)],
        out_specs=[qrow(0), erow, erow, whole(btab)],
        out_shape=[jax.ShapeDtypeStruct((L, GROUP_WIDTH), F32)] * 3 + [jax.ShapeDtypeStruct(btab.shape, F32)],
        scratch_shapes=[pltpu.VMEM((NA_W, GROUP_WIDTH), F32)] * 2,
        compiler_params=_cparams(("arbitrary",)),
    )(p, kb, vb, btab, dycat)


def _na_col_ok():
    qc = np.arange(GRID_W)[:, None]
    kc = (np.arange(NA_W) % GRID_W)[None, :]
    c0 = np.clip(qc - NA_COLS // 2, 0, GRID_W - NA_COLS)
    return (kc >= c0) & (kc < c0 + NA_COLS)


def _rpb_tables():
    c = np.arange(GRID_W)
    dc = np.clip(c[None, :] - c[:, None], -(NA_COLS - 1), NA_COLS - 1) + NA_COLS - 1
    t = np.zeros((2 * NA_COLS - 1, GRID_W, GRID_W), np.float32)
    t[dc, c[:, None], c[None, :]] = 1.0
    return jnp.asarray(t.reshape(2 * NA_COLS - 1, GRID_W * GRID_W))


def _rpb_expand(rpb, tab):
    H = rpb.shape[0]
    xt = _small_dot("na_bias", rpb.reshape(H * (2 * NA_ROWS - 1), 2 * NA_COLS - 1), tab, 'nn')
    b15 = xt.reshape(H, 2 * NA_ROWS - 1, GRID_W, GRID_W).transpose(0, 2, 1, 3).reshape(H, GRID_W, NA_BW)
    ok = jnp.asarray(_na_col_ok())
    return jnp.stack([jnp.where(ok, b15[:, :, sv * GRID_W:sv * GRID_W + NA_W], NEG) for sv in range(NA_ROWS)])


def _rpb_contract(dbv, tab):
    H = dbv.shape[1]
    db = sum(jnp.pad(dbv[sv], ((0, 0), (0, 0), (sv * GRID_W, NA_BW - NA_W - sv * GRID_W))) for sv in range(NA_ROWS))
    dx = db.reshape(H, GRID_W, 2 * NA_ROWS - 1, GRID_W).transpose(0, 2, 1, 3).reshape(H * (2 * NA_ROWS - 1), GRID_W * GRID_W)
    return _small_dot("na_bias_b", dx, tab, 'nt').reshape(H, 2 * NA_ROWS - 1, 2 * NA_COLS - 1)


def _band_bias(i, tq, w, halo, n):
    a = lax.broadcasted_iota(jnp.int32, (tq, w), 0)
    b = lax.broadcasted_iota(jnp.int32, (tq, w), 1)
    kpos = i * tq - halo + b
    d = b - halo - a
    return jnp.where((d <= halo) & (d >= -halo) & (kpos >= 0) & (kpos < n), 0.0, NEG)


def _band_fwd(name, q, kp, vp, tq, halo):
    G, n, _ = q.shape
    w = tq + 2 * halo

    def body(q_ref, k_ref, v_ref, o_ref, l_ref):
        i = pl.program_id(1)
        start = pl.multiple_of(i * tq, tq)
        bias = _band_bias(i, tq, w, halo, n)
        os_, lses = _attn_heads([q_ref[:, sl] for sl in HEADS], [k_ref[pl.ds(start, w), sl] for sl in HEADS],
                                [v_ref[pl.ds(start, w), sl] for sl in HEADS], [bias] * GROUP_HEADS)
        for sl, o, lse in zip(HEADS, os_, lses):
            o_ref[:, sl] = o
            l_ref[:, sl] = jnp.broadcast_to(lse, (tq, HEAD_DIM))

    qblk = pl.BlockSpec((None, tq, GROUP_WIDTH), lambda g, i: (g, i, 0))
    kblk = pl.BlockSpec((None, n + 2 * halo, GROUP_WIDTH), lambda g, i: (g, 0, 0))
    return pl.pallas_call(
        body, name=name, grid=(G, n // tq), in_specs=[qblk, kblk, kblk], out_specs=[qblk, qblk],
        out_shape=[jax.ShapeDtypeStruct((G, n, GROUP_WIDTH), F32)] * 2,
        compiler_params=_cparams(("parallel", "arbitrary")),
    )(q, kp, vp)


def _band_bwd(name, q, kp, vp, do, dl, tq, halo):
    G, n, _ = q.shape
    w = tq + 2 * halo
    nq = n // tq

    def body(q_ref, k_ref, v_ref, do_ref, dl_ref, dq_ref, dk_ref, dv_ref, acc_k, acc_v):
        i = pl.program_id(1)

        @pl.when(i == 0)
        def _():
            acc_k[...] = jnp.zeros_like(acc_k)
            acc_v[...] = jnp.zeros_like(acc_v)

        @pl.when(i > 0)
        def _():
            for acc in (acc_k, acc_v):
                moved = acc[tq:w, :]
                acc[0:2 * halo, :] = moved
                acc[2 * halo:w, :] = jnp.zeros((tq, GROUP_WIDTH), F32)

        @pl.when(i < nq)
        def _():
            start = pl.multiple_of(i * tq, tq)
            bias = _band_bias(i, tq, w, halo, n)
            H = GROUP_HEADS
            flat = [q_ref[:, sl].astype(F32) for sl in HEADS] + [k_ref[pl.ds(start, w), sl].astype(F32) for sl in HEADS] + \
                   [v_ref[pl.ds(start, w), sl].astype(F32) for sl in HEADS]

            def f(*a):
                os_, lses = _attn_heads(a[0:H], a[H:2 * H], a[2 * H:3 * H], [bias] * H)
                return tuple(os_) + tuple(lses)

            _, vjp = jax.vjp(f, *flat)
            grads = vjp(tuple(do_ref[:, sl] for sl in HEADS) + tuple(jnp.sum(dl_ref[:, sl], axis=1, keepdims=True) for sl in HEADS))
            for h, sl in enumerate(HEADS):
                dq_ref[:, sl] = grads[h]
                acc_k[:, sl] += grads[H + h]
                acc_v[:, sl] += grads[2 * H + h]

        dk_ref[...] = acc_k[0:tq, :]
        dv_ref[...] = acc_v[0:tq, :]

    qblk = pl.BlockSpec((None, tq, GROUP_WIDTH), lambda g, i: (g, jnp.minimum(i, nq - 1), 0))
    kblk = pl.BlockSpec((None, n + 2 * halo, GROUP_WIDTH), lambda g, i: (g, 0, 0))
    eblk = pl.BlockSpec((None, tq, GROUP_WIDTH), lambda g, i: (g, i, 0))
    return pl.pallas_call(
        body, name=name, grid=(G, nq + 1), in_specs=[qblk, kblk, kblk, qblk, qblk], out_specs=[qblk, eblk, eblk],
        out_shape=[jax.ShapeDtypeStruct((G, n, GROUP_WIDTH), F32)] + [jax.ShapeDtypeStruct((G, (nq + 1) * tq, GROUP_WIDTH), F32)] * 2,
        scratch_shapes=[pltpu.VMEM((w, GROUP_WIDTH), F32)] * 2,
        compiler_params=_cparams(("parallel", "arbitrary")),
    )(q, kp, vp, do, dl)


def _lin_scan(name, coef, inp, rev):
    L, C = coef.shape
    tt = 256 if L % 256 == 0 else L
    nt = L // tt
    tidx = (lambda i: (nt - 1 - i, 0)) if rev else (lambda i: (i, 0))

    def body(a_ref, u_ref, o_ref, carry):
        @pl.when(pl.program_id(0) == 0)
        def _():
            carry[...] = jnp.zeros_like(carry)
        a, u = a_ref[...], u_ref[...]
        row = lax.broadcasted_iota(jnp.int32, (tt, C), 0)
        s = 1
        while s < tt:
            ok = (row < tt - s) if rev else (row >= s)
            sh = tt - s if rev else s
            u = u + a * jnp.where(ok, pltpu.roll(u, sh, 0), 0.0)
            a = a * jnp.where(ok, pltpu.roll(a, sh, 0), 1.0)
            s *= 2
        out = u + a * carry[...]
        o_ref[...] = out
        carry[...] = out[0:1] if rev else out[tt - 1:tt]

    blk = pl.BlockSpec((tt, C), tidx)
    return pl.pallas_call(
        body, name=name, grid=(nt,), in_specs=[blk, blk], out_specs=blk,
        out_shape=jax.ShapeDtypeStruct((L, C), F32), scratch_shapes=[pltpu.VMEM((1, C), F32)],
        compiler_params=_cparams(("arbitrary",)),
    )(coef, inp)


def _pieces(shape):
    n0 = max(d for d in range(1, DMA_PIECES + 1) if shape[0] % d == 0)
    n1 = 1
    if len(shape) >= 3:
        n1 = max(d for d in range(1, DMA_PIECES // n0 + 1) if shape[1] % d == 0)
    s0, s1 = shape[0] // n0, (shape[1] // n1 if len(shape) >= 3 else 0)
    out = []
    for i in range(n0):
        for j in range(n1):
            out.append((pl.ds(i * s0, s0),) + ((pl.ds(j * s1, s1),) if len(shape) >= 3 else ()))
    return out


def _exchange(name, src, axes, gather):
    flips = {'xy': [(1, 0, 0), (0, 1, 0), (1, 1, 0)], 'c': [(0, 0, 1)],
             'xyc': [(fx, fy, fc) for fx in (0, 1) for fy in (0, 1) for fc in (0, 1)][1:]}[axes]
    n = len(flips) + 1
    blk_shape = tuple(src.shape if gather else src.shape[1:])
    pieces = _pieces(blk_shape)

    def number(px, py, pc):
        return {'xy': 2 * px + py, 'c': pc, 'xyc': 4 * px + 2 * py + pc}[axes]

    def body(src_ref, out_ref, send_sems, recv_sems):
        x, y, c = lax.axis_index("x"), lax.axis_index("y"), lax.axis_index("c")
        me = number(x, y, c)
        piece = (lambda k: src_ref) if gather else (lambda k: src_ref.at[k])
        peers = []
        for s, (fx, fy, fc) in enumerate(flips):
            px, py, pc = (x + fx) % 2, (y + fy) % 2, (c + fc) % 2

            def copy(ix, s=s, px=px, py=py, pc=pc):
                part = (lambda r: r) if ix is None else (lambda r: r.at[ix])
                return pltpu.make_async_remote_copy(
                    src_ref=part(piece(number(px, py, pc))), dst_ref=part(out_ref.at[me]),
                    send_sem=send_sems.at[s], recv_sem=recv_sems.at[s],
                    device_id=(px, py, pc), device_id_type=MESH)

            for ix in pieces:
                copy(ix).start()
            peers.append(copy)
        for copy in peers:
            copy(None).wait()

    out = pl.pallas_call(
        body, name=name, out_shape=jax.ShapeDtypeStruct((n,) + blk_shape, src.dtype),
        in_specs=[pl.BlockSpec(memory_space=pl.ANY)], out_specs=pl.BlockSpec(memory_space=pl.ANY),
        scratch_shapes=[pltpu.SemaphoreType.DMA((n - 1,)), pltpu.SemaphoreType.DMA((n - 1,))],
    )(src)
    me = number(lax.axis_index("x"), lax.axis_index("y"), lax.axis_index("c"))
    own = src if gather else lax.dynamic_index_in_dim(src, me, 0, keepdims=False)
    return lax.dynamic_update_index_in_dim(out, own, me, 0)


def _ordered_sum(name, buf, dtype=F32):
    n = buf.shape[0]

    def fn(*t):
        acc = t[0].astype(F32)
        for v in t[1:]:
            acc = acc + v.astype(F32)
        return acc

    return _rowwise(name, fn, [Rows(buf, lead=k) for k in range(n)], [], [(buf.shape[-1], dtype)])[0]


def _reduce_big(name, g):
    mine = _ordered_sum(name + "_sum_c", _exchange(name + "_swap_c", g, 'c', False).reshape(2, -1, g.shape[-1]), BF16)
    mine = mine.reshape(g.shape[1:])
    tot = _ordered_sum(name + "_sum_xy", _exchange(name + "_a2a_xy", mine, 'xy', False))
    return _exchange(name + "_share_c", tot, 'c', True)


def _dilate(t, dil):
    L, C = t.shape
    return t.reshape(L // dil, dil, C).transpose(1, 0, 2)


def _undilate(t):
    dil, n, C = t.shape
    return t.transpose(1, 0, 2).reshape(dil * n, C)


def _pad_rows(t, halo):
    return jnp.pad(t, ((0, 0), (halo, halo), (0, 0)))


def _pcol(p, blk):
    return Rows(p, GROUP_WIDTH, blk)


def _pslice(p, blk):
    return p[:, blk * GROUP_WIDTH:(blk + 1) * GROUP_WIDTH]


def _seg_matrix():
    h = np.arange(GROUP_WIDTH) // HEAD_DIM
    return jnp.asarray((h[:, None] == h[None, :]).astype(np.float32) / HEAD_DIM)


def _rope_tables(L):
    pos = jnp.arange(L, dtype=F32)
    inv_freq = ROPE_THETA ** (-jnp.arange(0, HEAD_DIM, 2, dtype=F32) / HEAD_DIM)
    ang = pos[:, None] * inv_freq[None, :]
    cos, sin = jnp.cos(ang), jnp.sin(ang)
    cs = jnp.tile(jnp.concatenate([cos, cos], axis=1), (1, GROUP_HEADS))
    sn = jnp.tile(jnp.concatenate([-sin, sin], axis=1), (1, GROUP_HEADS))
    return cs, sn


def _shift_rows(t, k):
    if k == 0:
        return t
    z = jnp.zeros((abs(k), t.shape[1]), t.dtype)
    return jnp.concatenate([z, t[:-k]], axis=0) if k > 0 else jnp.concatenate([t[-k:], z], axis=0)


def _dil_branches(L):
    out = []
    for window, dil in DIL_PAIRS:
        radius = window // (2 * dil)
        n = L // dil
        out.append((dil, radius, min(256, n)))
    return out


def _dil_operands(qr, kr, p, dil, radius):
    q = _dilate(qr.astype(BF16), dil)
    k = _pad_rows(_dilate(kr.astype(BF16), dil), radius)
    v = _pad_rows(_dilate(_pslice(p, P_VD).astype(BF16), dil), radius)
    return q, k, v


def _layer_fwd(x, w, c):
    L, D = x.shape
    sv = {'x_in': x}
    h = _rowwise("mix_prenorm", _prenorm_fn, [x], [w['mix_norm_pre']], [(D, BF16)])[0]
    p = _matmul("mix_proj", h, w['w_in'], 'nn', F32)
    sv['p'] = p

    la = _rowwise("gla_pre", _gla_pre_fn, [Rows(p, LANE, P_Z // LANE)], [w['gla_wg'], w['gla_bg']], [(2 * GROUP_WIDTH, F32)])[0]
    of, ob, sf, sb = _gla_scan_fwd(p, la)
    ya = _rowwise("gla_post", _gla_post_fn, [of, ob, _pcol(p, P_GA)], [w['gla_norm'], c['seg']], [(GROUP_WIDTH, BF16)])[0]
    sv.update(la=la, sf=sf, sb=sb, of=of, ob=ob)

    yb = _na_fwd(p, _pslice(p, P_KB).astype(BF16), _pslice(p, P_VB).astype(BF16), _rpb_expand(w['na_rpb'], c['rpb_tab']))

    xb = _pslice(p, P_XC)
    xs = [_shift_rows(xb, LRU_CONV_LEFT - j) for j in range(LRU_CONV)]
    a0, a1, u0, u1 = _rowwise("lru_pre", _lru_pre_fn, xs, [w[k] for k in LRU_PARAMS], [(GROUP_WIDTH, F32)] * 4)
    hf = _lin_scan("lru_scan_f", a0, u0, False)
    hb = _lin_scan("lru_scan_b", a1, u1, True)
    yc = _rowwise("lru_post", _lru_post_fn, [hf, hb, _pcol(p, P_GC)], [], [(GROUP_WIDTH, BF16)])[0]
    sv.update(a0=a0, a1=a1, hf=hf, hb=hb)

    qr, kr = _rowwise("rope", _rope_fn, [_pcol(p, P_QD), _pcol(p, P_KD), c['cos'], c['sin']], [], [(GROUP_WIDTH, F32)] * 2)
    os_, ls_ = [], []
    for dil, radius, tq in _dil_branches(L):
        o, lse = _band_fwd(f"dil_attn{dil}", *_dil_operands(qr, kr, p, dil, radius), tq, radius)
        os_.append(_undilate(o))
        ls_.append(_undilate(lse))
    yd = _rowwise("dil_comb", _dil_comb_fn, os_ + ls_, [], [(GROUP_WIDTH, BF16)])[0]
    sv.update(qr=qr, kr=kr, dil_o=os_, dil_l=ls_)

    ycat = _assemble("mix_cat", [[ya], [yb], [yc], [yd]], BF16)
    y = _matmul("mix_out", ycat, w['w_out'], 'nn', F32)
    xm = _rowwise("mix_postnorm", _postnorm_fn, [x, y], [w['mix_norm_post']], [(D, F32)])[0]
    sv.update(ycat=ycat, y=y, x_mid=xm)

    h2 = _rowwise("ffn_prenorm", _prenorm_fn, [xm], [w['ffn_norm_pre']], [(D, BF16)])[0]
    gate = _matmul("ffn_gate", h2, w['ffn_wg'], 'nn', BF16)
    up = _matmul("ffn_up", h2, w['ffn_wu'], 'nn', BF16)
    dff = gate.shape[1]
    act = _rowwise("ffn_act", _swiglu_fn, [gate, up], [], [(dff, BF16)], ncol=dff // _tile(dff, 1536))[0]
    f = _matmul("ffn_out", act, w['ffn_w_out'], 'nn', F32)
    xo = _rowwise("ffn_postnorm", _postnorm_fn, [xm, f], [w['ffn_norm_post']], [(D, F32)])[0]
    sv.update(gate=gate, up=up, act=act, f=f)
    return xo, sv


def _layer_bwd(dx, w, c, sv):
    L, D = dx.shape
    g = {}
    as_f32 = lambda t: (t.astype(F32),)
    df, g['ffn_norm_post'] = _rowwise_bwd("ffn_postnorm_b", lambda y, gn: _rms(y, gn), [sv['f']], [w['ffn_norm_post']],
                                          [dx], as_f32, [BF16], [True])
    dact = _matmul("ffn_out_bx", df, w['ffn_w_out'], 'nt', BF16)
    g['ffn_w_out'] = _matmul("ffn_out_bw", sv['act'], df, 'tn', F32)
    gate, up = sv['gate'], sv['up']
    dff = gate.shape[1]
    dgate, dup = _rowwise_bwd("ffn_act_b", _swiglu_fn, [gate, up], [], [dact], as_f32, [BF16, BF16], [], ncol=dff // _tile(dff, 1536))
    dh2 = _matmul("ffn_up_bx", dup, w['ffn_wu'], 'nt', F32, acc_in=_matmul("ffn_gate_bx", dgate, w['ffn_wg'], 'nt', F32))
    xm = sv['x_mid']
    h2 = _rowwise("ffn_prenorm_r", _prenorm_fn, [xm], [w['ffn_norm_pre']], [(D, BF16)])[0]
    g['ffn_wg'] = _matmul("ffn_gate_bw", h2, dgate, 'tn', F32)
    g['ffn_wu'] = _matmul("ffn_up_bw", h2, dup, 'tn', F32)
    dxm, g['ffn_norm_pre'] = _rowwise_bwd("ffn_prenorm_b", _prenorm_fn, [xm], [w['ffn_norm_pre']], [dh2], as_f32, [F32], [True],
                                          row_grad_add=[dx])

    dy, g['mix_norm_post'] = _rowwise_bwd("mix_postnorm_b", lambda y, gn: _rms(y, gn), [sv['y']], [w['mix_norm_post']],
                                          [dxm], as_f32, [BF16], [True])
    dycat = _matmul("mix_out_bx", dy, w['w_out'], 'nt', F32)
    g['w_out'] = _matmul("mix_out_bw", sv['ycat'], dy, 'tn', F32)
    p = sv['p']
    dya, dyb, dyc, dyd = (Rows(dycat, GROUP_WIDTH, k) for k in range(4))

    dof, dga, g['gla_norm'] = _rowwise_bwd("gla_post_b", _gla_post_fn, [sv['of'], sv['ob'], _pcol(p, P_GA)],
                                           [w['gla_norm'], c['seg']], [dya], as_f32, [F32, None, F32], [True, False])
    la = sv['la']
    dqf, dkf, dvf, dlf, dqb_, dkb_, dvb_, dlb = _gla_scan_bwd(p, la, sv['sf'], sv['sb'], dof)
    dz, g['gla_wg'], g['gla_bg'] = _rowwise_bwd("gla_pre_b", _gla_pre_fn, [Rows(p, LANE, P_Z // LANE)], [w['gla_wg'], w['gla_bg']],
                                                [dlf, dlb], lambda a, b: (jnp.concatenate([a, b], axis=1),), [BF16], [True, True])

    btab = _rpb_expand(w['na_rpb'], c['rpb_tab'])
    dqn, dkn, dvn, dbt = _na_bwd(p, _pslice(p, P_KB).astype(BF16), _pslice(p, P_VB).astype(BF16), btab, dycat)
    g['na_rpb'] = _rpb_contract(dbt, c['rpb_tab'])

    dh, dgc = _rowwise_bwd("lru_post_b", _lru_post_fn, [sv['hf'], sv['hb'], _pcol(p, P_GC)], [], [dyc], as_f32, [F32, None, F32], [])
    lam0 = _lin_scan("lru_scan_f_b", _shift_rows(sv['a0'], -1), dh, True)
    lam1 = _lin_scan("lru_scan_b_b", _shift_rows(sv['a1'], 1), dh, False)
    xb = _pslice(p, P_XC)
    xs = [_shift_rows(xb, LRU_CONV_LEFT - j) for j in range(LRU_CONV)]
    res = _rowwise_bwd("lru_pre_b", _lru_pre_fn, xs, [w[k] for k in LRU_PARAMS],
                       [lam0, lam1, _shift_rows(sv['hf'], 1), _shift_rows(sv['hb'], -1)],
                       lambda l0, l1, hfp, hbn: (l0 * hfp, l1 * hbn, l0, l1), [F32] * 4, [True] * len(LRU_PARAMS))
    dxs = res[:4]
    for k, nm in enumerate(LRU_PARAMS):
        g[nm] = res[4 + k]
    dxc = [_shift_rows(dxs[j], j - LRU_CONV_LEFT) for j in range(LRU_CONV)]

    comb = _rowwise_bwd("dil_comb_b", _dil_comb_fn, sv['dil_o'] + sv['dil_l'], [], [dyd], as_f32, [F32] * 6, [])
    qr, kr = sv['qr'], sv['kr']
    dqs, dks, dvs = [], [], []
    for k, (dil, radius, tq) in enumerate(_dil_branches(L)):
        n = L // dil
        dq_, dk_, dv_ = _band_bwd(f"dil_attn{dil}_b", *_dil_operands(qr, kr, p, dil, radius),
                                  _dilate(comb[k], dil), _dilate(comb[3 + k], dil), tq, radius)
        dqs.append(_undilate(dq_))
        dks.append(_undilate(dk_[:, radius:radius + n]))
        dvs.append(_undilate(dv_[:, radius:radius + n]))
    dqd, dkd = _rowwise_bwd("rope_b", _rope_fn, [_pcol(p, P_QD), _pcol(p, P_KD), c['cos'], c['sin']], [], dqs + dks,
                            lambda a1, a2, a3, b1, b2, b3: (a1 + a2 + a3, b1 + b2 + b3), [F32, F32, None, None], [])

    dp = _assemble("mix_dp", [[dqf, dqb_], [dkf, dkb_], [dvf, dvb_], [dga], [dqn], [dkn], [dvn], dxc, [dgc], [dqd], [dkd], dvs, [dz]], BF16)
    dh1 = _matmul("mix_proj_bx", dp, w['w_in'], 'nt', F32)
    x_in = sv['x_in']
    h = _rowwise("mix_prenorm_r", _prenorm_fn, [x_in], [w['mix_norm_pre']], [(D, BF16)])[0]
    g['w_in'] = _matmul("mix_proj_bw", h, dp, 'tn', F32)
    dxi, g['mix_norm_pre'] = _rowwise_bwd("mix_prenorm_b", _prenorm_fn, [x_in], [w['mix_norm_pre']], [dh1], as_f32, [F32], [True],
                                          row_grad_add=[dxm])
    return dxi, g


def _loss_fn(y, t):
    e = y - t
    return e * (1.0 / y.shape[1]), jnp.sum(e * e, axis=0, keepdims=True)


def _gather_cols(name, shard, axis):
    half = shard.shape[0] // 2
    mine = lax.dynamic_slice_in_dim(shard, lax.axis_index("c") * half, half, axis=0)
    both = _exchange(name + "_c", _exchange(name + "_xy", mine, 'xy', True), 'c', True)
    shp = list(shard.shape)
    shp[axis] *= 4
    return jnp.moveaxis(both, 1, axis + 1).reshape(shp)


def _pack(arrs, mult=64 * LANE):
    flat = jnp.concatenate([a.reshape(-1) for a in arrs])
    pad = (-flat.shape[0]) % mult
    return jnp.pad(flat, (0, pad)).reshape(-1, LANE)


def _unpack(buf, shapes):
    flat, out, k = buf.reshape(-1), [], 0
    for s in shapes:
        sz = int(np.prod(s))
        out.append(flat[k:k + sz].reshape(s))
        k += sz
    return out


def _perm_in(w_in):
    pad = jnp.zeros(w_in.shape[:-1] + (D_INP - D_IN,), w_in.dtype)
    return jnp.concatenate([w_in[..., :P_QB * GROUP_WIDTH], w_in[..., P_QB * GROUP_WIDTH + 2 * GLA_RANK:],
                            w_in[..., P_QB * GROUP_WIDTH:P_QB * GROUP_WIDTH + 2 * GLA_RANK], pad], axis=-1)


def _unperm_in(g):
    return jnp.concatenate([g[..., :P_QB * GROUP_WIDTH], g[..., P_Z:P_Z + 2 * GLA_RANK], g[..., P_QB * GROUP_WIDTH:P_Z]], axis=-1)


def _block_diag(wb):
    l = wb.shape[0]
    eye = jnp.eye(GROUP_HEADS, dtype=wb.dtype)
    return jnp.einsum('lehij,hg->lehigj', wb, eye).reshape(l, 2, GROUP_WIDTH, GROUP_WIDTH)


def _block_diag_grad(gw):
    l = gw.shape[0]
    g6 = gw.reshape(l, 2, GROUP_HEADS, HEAD_DIM, GROUP_HEADS, HEAD_DIM)
    return jnp.stack([g6[:, :, h, :, h, :] for h in range(GROUP_HEADS)], axis=2)


def _gate_matrix(wg):
    l = wg.shape[0]
    m = jnp.zeros((l, LANE, 2 * GROUP_WIDTH), wg.dtype)
    for e in range(2):
        m = m.at[:, e * GLA_RANK:(e + 1) * GLA_RANK, e * GROUP_WIDTH:(e + 1) * GROUP_WIDTH].set(wg[:, e])
    return m


def _gate_matrix_grad(gm):
    return jnp.stack([gm[:, e * GLA_RANK:(e + 1) * GLA_RANK, e * GROUP_WIDTH:(e + 1) * GROUP_WIDTH] for e in range(2)], axis=1)


def _adam_fn(w, g, m, v):
    m = ADAM_B1 * m + (1.0 - ADAM_B1) * g
    v = ADAM_B2 * v + (1.0 - ADAM_B2) * (g * g)
    m_hat = m / (1.0 - ADAM_B1 ** ADAM_STEP)
    v_hat = v / (1.0 - ADAM_B2 ** ADAM_STEP)
    return -ADAM_LR * (m_hat / (jnp.sqrt(v_hat) + ADAM_EPS) + ADAM_WD * w), m, v


def _adam(name, w, g, m, v):
    shp = w.shape
    two = lambda t: t.reshape(-1, shp[-1])
    res = _rowwise(name, _adam_fn, [two(w), two(g), two(m), two(v)], [], [(shp[-1], F32)] * 3)
    return [r.reshape(shp) for r in res]


def _local_step(x, target, fw):
    L, D = x.shape
    depth = fw['w_in'].shape[0]
    cs, sn = _rope_tables(L)
    consts = {'seg': _seg_matrix(), 'rpb_tab': _rpb_tables(), 'cos': cs, 'sin': sn}
    layer = lambda l: {k: v[l] for k, v in fw.items()}
    saved = []
    for l in range(depth):
        x, sv = _layer_fwd(x, layer(l), consts)
        saved.append(sv)
    dx, sq = _rowwise("loss", _loss_fn, [x, target], [], [(D, F32)], acc_outs=[(1, D)])
    grads = [None] * depth
    for l in reversed(range(depth)):
        dx, grads[l] = _layer_bwd(dx, layer(l), consts, saved[l])
    return sq, dx, {k: jnp.stack([g[k] for g in grads]) for k in grads[0]}


def kernel(x, mix_norm_pre, mix_norm_post, w_in, gla_w_gate, gla_b_gate, gla_norm, na_rpb, lru_conv_w, lru_conv_b, lru_w_a, lru_b_a, lru_w_x, lru_b_x, lru_lambda, w_out, ffn_norm_pre, ffn_norm_post, ffn_w_in, ffn_w_out, loss_target, m_mix_norm_pre, m_mix_norm_post, m_w_in, m_gla_w_gate, m_gla_b_gate, m_gla_norm, m_na_rpb, m_lru_conv_w, m_lru_conv_b, m_lru_w_a, m_lru_b_a, m_lru_w_x, m_lru_b_x, m_lru_lambda, m_w_out, m_ffn_norm_pre, m_ffn_norm_post, m_ffn_w_in, m_ffn_w_out, v_mix_norm_pre, v_mix_norm_post, v_w_in, v_gla_w_gate, v_gla_b_gate, v_gla_norm, v_na_rpb, v_lru_conv_w, v_lru_conv_b, v_lru_w_a, v_lru_b_a, v_lru_w_x, v_lru_b_x, v_lru_lambda, v_w_out, v_ffn_norm_pre, v_ffn_norm_post, v_ffn_w_in, v_ffn_w_out):
    args = (mix_norm_pre, mix_norm_post, w_in, gla_w_gate, gla_b_gate, gla_norm, na_rpb, lru_conv_w, lru_conv_b, lru_w_a, lru_b_a, lru_w_x, lru_b_x, lru_lambda, w_out, ffn_norm_pre, ffn_norm_post, ffn_w_in, ffn_w_out,
            m_mix_norm_pre, m_mix_norm_post, m_w_in, m_gla_w_gate, m_gla_b_gate, m_gla_norm, m_na_rpb, m_lru_conv_w, m_lru_conv_b, m_lru_w_a, m_lru_b_a, m_lru_w_x, m_lru_b_x, m_lru_lambda, m_w_out, m_ffn_norm_pre, m_ffn_norm_post, m_ffn_w_in, m_ffn_w_out,
            v_mix_norm_pre, v_mix_norm_post, v_w_in, v_gla_w_gate, v_gla_b_gate, v_gla_norm, v_na_rpb, v_lru_conv_w, v_lru_conv_b, v_lru_w_a, v_lru_b_a, v_lru_w_x, v_lru_b_x, v_lru_lambda, v_w_out, v_ffn_norm_pre, v_ffn_norm_post, v_ffn_w_in, v_ffn_w_out)
    nw = len(WEIGHTS)
    W = dict(zip(WEIGHTS, args[:nw]))
    M = dict(zip(WEIGHTS, args[nw:2 * nw]))
    V = dict(zip(WEIGHTS, args[2 * nw:]))
    chip = 2 * lax.axis_index("x") + lax.axis_index("y")

    full = dict(W)
    full['w_in'] = _gather_cols("ag_w_in", w_in.astype(BF16), 2)
    full['ffn_w_in'] = _gather_cols("ag_ffn_w_in", ffn_w_in.astype(BF16), 2)
    full['w_out'] = _gather_cols("ag_w_out", w_out.astype(BF16), 1)
    full['ffn_w_out'] = _gather_cols("ag_ffn_w_out", ffn_w_out.astype(BF16), 1)
    small = list(SMALL_SHARDED)
    got = _exchange("ag_small", _pack([W[k] for k in small]), 'xy', True)
    for k, parts in zip(small, zip(*[_unpack(got[j], [W[k].shape for k in small]) for j in range(4)])):
        ax = SMALL_SHARDED[k]
        stacked = jnp.moveaxis(jnp.stack(parts), 0, ax)
        shp = list(W[k].shape)
        shp[ax] *= 4
        full[k] = stacked.reshape(shp)

    sq, dx0, g = _local_step(x[0], loss_target[0], _layer_weights(full))
    loss = lax.psum(0.5 * jnp.sum(sq) / x.shape[-1], ("x", "y", "c"))
    gfull = _stored_grads(g)

    grad = {}
    for k in BIG:
        gk = gfull[k]
        if k in ('w_in', 'ffn_w_in'):
            l, K, N = gk.shape
            cut = gk.reshape(2, l * K // 2, 4, N // 4).transpose(0, 2, 1, 3)
            red = _reduce_big("rs_" + k, cut)
            grad[k] = red.reshape(l, K, N // 4)
        else:
            l, K, N = gk.shape
            cut = gk.reshape(2, l // 2, 4, K // 4, N).transpose(0, 2, 1, 3, 4).reshape(2, 4, (l // 2) * (K // 4), N)
            red = _reduce_big("rs_" + k, cut)
            grad[k] = red.reshape(l, K // 4, N)
    rest = [k for k in WEIGHTS if k not in BIG]
    allg = _exchange("ar_small", _pack([gfull[k] for k in rest]), 'xyc', True)
    summed = _unpack(_ordered_sum("ar_small_sum", allg), [gfull[k].shape for k in rest])
    for k, s in zip(rest, summed):
        if k in SMALL_SHARDED:
            ax = SMALL_SHARDED[k]
            n = W[k].shape[ax]
            s = lax.dynamic_slice_in_dim(s, chip * n, n, axis=ax)
        grad[k] = s

    delta, new_m, new_v = {}, {}, {}
    for k in BIG:
        delta[k], new_m[k], new_v[k] = _adam("adam_" + k, W[k], grad[k], M[k], V[k])
    shapes = [W[k].shape for k in rest]
    res = _rowwise("adam_small", _adam_fn, [_pack([d[k] for k in rest]) for d in (W, grad, M, V)], [], [(LANE, F32)] * 3)
    for d, r in zip((delta, new_m, new_v), res):
        for k, t in zip(rest, _unpack(r, shapes)):
            d[k] = t

    return (loss, dx0[None], *[grad[k] for k in WEIGHTS], *[delta[k] for k in WEIGHTS],
            *[new_m[k] for k in WEIGHTS], *[new_v[k] for k in WEIGHTS])


def _layer_weights(full):
    depth = full['w_in'].shape[0]
    dff = full['ffn_w_in'].shape[-1] // 2
    row = lambda t: t[:, None, :]
    fw = {
        'mix_norm_pre': row(full['mix_norm_pre']), 'mix_norm_post': row(full['mix_norm_post']),
        'ffn_norm_pre': row(full['ffn_norm_pre']), 'ffn_norm_post': row(full['ffn_norm_post']),
        'w_in': _perm_in(full['w_in']), 'w_out': full['w_out'],
        'ffn_wg': full['ffn_w_in'][..., :dff], 'ffn_wu': full['ffn_w_in'][..., dff:], 'ffn_w_out': full['ffn_w_out'],
        'gla_wg': _gate_matrix(full['gla_w_gate']), 'gla_bg': full['gla_b_gate'].reshape(depth, 1, 2 * GROUP_WIDTH),
        'gla_norm': row(full['gla_norm']), 'na_rpb': full['na_rpb'],
        'lru_cb': row(full['lru_conv_b']),
    }
    wa_bd, wx_bd = _block_diag(full['lru_w_a']), _block_diag(full['lru_w_x'])
    for j in range(LRU_CONV):
        fw[f'lru_cw{j}'] = row(full['lru_conv_w'][:, j])
    for e in range(2):
        fw[f'lru_wa{e}'], fw[f'lru_wx{e}'] = wa_bd[:, e], wx_bd[:, e]
        fw[f'lru_ba{e}'], fw[f'lru_bx{e}'] = row(full['lru_b_a'][:, e]), row(full['lru_b_x'][:, e])
        fw[f'lru_lam{e}'] = row(full['lru_lambda'][:, e])
    return fw


def _stored_grads(g):
    depth = g['w_in'].shape[0]
    return {
        'mix_norm_pre': g['mix_norm_pre'][:, 0], 'mix_norm_post': g['mix_norm_post'][:, 0],
        'ffn_norm_pre': g['ffn_norm_pre'][:, 0], 'ffn_norm_post': g['ffn_norm_post'][:, 0],
        'w_in': _unperm_in(g['w_in']), 'w_out': g['w_out'],
        'ffn_w_in': jnp.concatenate([g['ffn_wg'], g['ffn_wu']], axis=-1), 'ffn_w_out': g['ffn_w_out'],
        'gla_w_gate': _gate_matrix_grad(g['gla_wg']), 'gla_b_gate': g['gla_bg'].reshape(depth, 2, GROUP_WIDTH),
        'gla_norm': g['gla_norm'][:, 0], 'na_rpb': g['na_rpb'],
        'lru_conv_w': jnp.stack([g[f'lru_cw{j}'][:, 0] for j in range(LRU_CONV)], axis=1), 'lru_conv_b': g['lru_cb'][:, 0],
        'lru_w_a': _block_diag_grad(jnp.stack([g['lru_wa0'], g['lru_wa1']], axis=1)),
        'lru_w_x': _block_diag_grad(jnp.stack([g['lru_wx0'], g['lru_wx1']], axis=1)),
        'lru_b_a': jnp.stack([g['lru_ba0'][:, 0], g['lru_ba1'][:, 0]], axis=1),
        'lru_b_x': jnp.stack([g['lru_bx0'][:, 0], g['lru_bx1'][:, 0]], axis=1),
        'lru_lambda': jnp.stack([g['lru_lam0'][:, 0], g['lru_lam1'][:, 0]], axis=1),
    }
```

```python
import numpy as np
import jax
import jax.numpy as jnp
from jax import lax
from jax.experimental import pallas as pl
from jax.experimental.pallas import tpu as pltpu

F32, BF16 = jnp.float32, jnp.bfloat16
HIGHEST = lax.Precision.HIGHEST
MESH = pl.DeviceIdType.MESH

HEAD_DIM = 64
GROUP_HEADS = 4
GROUP_WIDTH = GROUP_HEADS * HEAD_DIM
GLA_RANK = 16
GLA_TAU = 16.0
GLA_CHUNK = 64
GRID_W = 64
NA_ROWS = 8
NA_COLS = 16
LRU_CONV = 4
LRU_CONV_LEFT = 2
LRU_C = 8.0
DIL_PAIRS = ((128, 1), (512, 4), (2048, 16))
ROPE_THETA = 10000.0
EPS = 1e-6
ADAM_LR, ADAM_B1, ADAM_B2, ADAM_EPS, ADAM_WD, ADAM_STEP = 0.001, 0.9, 0.999, 1e-08, 0.01, 10
NEG = -1e30

LANE = 128
VMEM_LIMIT = 56 * 1024 * 1024
ROW_BUDGET = 16 * 1024 * 1024
DMA_PIECES = 8

P_QA, P_KA, P_VA, P_GA, P_QB, P_KB, P_VB, P_XC, P_GC, P_QD, P_KD, P_VD = range(12)
P_Z = 12 * GROUP_WIDTH
D_IN = 12 * GROUP_WIDTH + 2 * GLA_RANK
D_INP = 12 * GROUP_WIDTH + LANE

WEIGHTS = ['mix_norm_pre', 'mix_norm_post', 'w_in', 'gla_w_gate', 'gla_b_gate', 'gla_norm', 'na_rpb',
           'lru_conv_w', 'lru_conv_b', 'lru_w_a', 'lru_b_a', 'lru_w_x', 'lru_b_x', 'lru_lambda', 'w_out',
           'ffn_norm_pre', 'ffn_norm_post', 'ffn_w_in', 'ffn_w_out']
BIG = ('w_in', 'w_out', 'ffn_w_in', 'ffn_w_out')
SMALL_SHARDED = {'gla_w_gate': 3, 'gla_b_gate': 2, 'lru_conv_w': 2, 'lru_b_a': 2, 'lru_b_x': 2, 'lru_lambda': 2}
HEADS = [slice(h * HEAD_DIM, (h + 1) * HEAD_DIM) for h in range(GROUP_HEADS)]


def _cparams(sem=None):
    return pltpu.CompilerParams(dimension_semantics=sem, vmem_limit_bytes=VMEM_LIMIT)


def _tile(dim, target, mult=LANE):
    best = None
    for t in range(mult, min(dim, target) + 1, mult):
        if dim % t == 0:
            best = t
    return best or dim


class Rows:
    def __init__(self, a, w=None, cb=0, lead=None, shifts=None):
        self.a, self.cb, self.lead, self.shifts = a, cb, lead, shifts
        self.w = a.shape[-1] if w is None else w
        self.nrows = a.shape[-2]

    def spec(self, tm, ncol=1, off=0):
        w = self.w // ncol
        last = self.nrows // tm - 1
        row = (lambda i: i) if off == 0 else (lambda i: jnp.clip(i + off, 0, last))
        if self.lead is None:
            return pl.BlockSpec((tm, w), lambda i, j, cb=self.cb: (row(i), cb * ncol + j))
        return pl.BlockSpec((None, tm, w), lambda i, j, cb=self.cb, k=self.lead: (k, row(i), cb * ncol + j))

    def nbytes(self):
        return self.w * self.a.dtype.itemsize * (1 if self.shifts is None else 3)


def _as_rows(rs):
    return [r if isinstance(r, Rows) else Rows(r) for r in rs]


def _shift_tile(prev, cur, nxt, k, t, nt):
    if k == 0:
        return cur
    tm = cur.shape[0]
    row = lax.broadcasted_iota(jnp.int32, cur.shape, 0)
    if k > 0:
        edge = jnp.where(t > 0, pltpu.roll(prev, k, 0), 0.0)
        return jnp.where(row < k, edge, pltpu.roll(cur, k, 0))
    edge = jnp.where(t < nt - 1, pltpu.roll(nxt, tm + k, 0), 0.0)
    return jnp.where(row >= tm + k, edge, pltpu.roll(cur, tm + k, 0))


def _operands(rows, tm, ncol):
    specs, arrs = [], []
    for r in rows:
        if r.shifts is None:
            specs.append(r.spec(tm, ncol))
            arrs.append(r.a)
        else:
            assert ncol == 1
            specs += [r.spec(tm, 1, off) for off in (-1, 0, 1)]
            arrs += [r.a] * 3

    def load(refs):
        vals, k = [], 0
        t, nt = pl.program_id(0), rows[0].nrows // tm
        for r in rows:
            if r.shifts is None:
                vals.append(refs[k][...])
                k += 1
            else:
                prev, cur, nxt = refs[k][...], refs[k + 1][...], refs[k + 2][...]
                vals += [_shift_tile(prev, cur, nxt, s, t, nt) for s in r.shifts]
                k += 3
        return vals

    return specs, arrs, load


def _expand(rows):
    return [r for r in rows for _ in (r.shifts or [0])]


def _pick_tm(nrows, row_bytes, scale):
    tm = 512
    while tm > 16 and (tm * row_bytes * scale > ROW_BUDGET or nrows % tm):
        tm //= 2
    assert nrows % tm == 0, (nrows, tm)
    return tm


def _full_spec(a):
    nd = a.ndim
    return pl.BlockSpec(a.shape, lambda i, j, nd=nd: (0,) * nd)


def _rowwise(name, fn, rows, params, outs, acc_outs=(), ncol=1):
    rows = _as_rows(rows)
    nrows = rows[0].nrows
    assert ncol == 1 or not (acc_outs or params)
    tm = _pick_tm(nrows, (sum(r.nbytes() for r in rows) + sum(w * jnp.dtype(d).itemsize for w, d in outs)) // ncol, 2)
    specs, arrs, load = _operands(rows, tm, ncol)
    n_r, n_p, n_o = len(specs), len(params), len(outs)

    def body(*refs):
        vals = load(refs[:n_r]) + [r[...] for r in refs[n_r:n_r + n_p]]
        res = fn(*vals)
        res = res if isinstance(res, (tuple, list)) else (res,)
        orefs = refs[n_r + n_p:]
        for o, v in zip(orefs[:n_o], res[:n_o]):
            o[...] = v.astype(o.dtype)
        for o, v in zip(orefs[n_o:], res[n_o:]):
            @pl.when(pl.program_id(0) == 0)
            def _(o=o):
                o[...] = jnp.zeros_like(o)
            o[...] += v

    out_shape = [jax.ShapeDtypeStruct((nrows, w), d) for w, d in outs] + [jax.ShapeDtypeStruct(s, F32) for s in acc_outs]
    out_specs = [pl.BlockSpec((tm, w // ncol), lambda i, j: (i, j)) for w, _ in outs] + \
                [pl.BlockSpec(s, lambda i, j, nd=len(s): (0,) * nd) for s in acc_outs]
    return pl.pallas_call(
        body, name=name, grid=(nrows // tm, ncol),
        in_specs=specs + [_full_spec(p) for p in params],
        out_specs=out_specs, out_shape=out_shape,
        compiler_params=_cparams(("arbitrary", "arbitrary") if acc_outs else ("parallel", "parallel")),
    )(*arrs, *params)


def _rowwise_bwd(name, fn, rows, params, ct_rows, ct_fn, row_grads, param_grads, row_grad_add=None, ncol=1):
    rows, ct_rows = _as_rows(rows), _as_rows(ct_rows)
    nrows = rows[0].nrows
    n_rg = sum(d is not None for d in row_grads)
    adds = _as_rows([a for a in (row_grad_add or []) if a is not None])
    add_at = [k for k, a in enumerate(row_grad_add or []) if a is not None]
    assert ncol == 1 or not (params or adds)
    seen = _expand(rows)
    gbytes = sum(r.w * jnp.dtype(d).itemsize for r, d in zip(seen, row_grads) if d is not None)
    tm = _pick_tm(nrows, (sum(r.nbytes() for r in rows + ct_rows + adds) + gbytes) // ncol, 4)
    r_specs, r_arrs, r_load = _operands(rows, tm, ncol)
    c_specs, c_arrs, c_load = _operands(ct_rows, tm, ncol)
    a_specs, a_arrs, a_load = _operands(adds, tm, ncol)
    n_r, n_p, n_c, n_a = len(r_specs), len(params), len(c_specs), len(a_specs)
    diff = [k for k, d in enumerate(row_grads) if d is not None] + [len(seen) + k for k, g in enumerate(param_grads) if g]

    def body(*refs):
        vals = r_load(refs[:n_r]) + [r[...] for r in refs[n_r:n_r + n_p]]
        cts_in = c_load(refs[n_r + n_p:n_r + n_p + n_c])
        add_in = a_load(refs[n_r + n_p + n_c:n_r + n_p + n_c + n_a]) if n_a else []
        orefs = refs[n_r + n_p + n_c + n_a:]

        def f(*dv):
            full = list(vals)
            for k, v in zip(diff, dv):
                full[k] = v
            res = fn(*full)
            return tuple(res) if isinstance(res, (tuple, list)) else (res,)

        outs, vjp = jax.vjp(f, *[vals[k].astype(F32) for k in diff])
        cts = ct_fn(*cts_in)
        cts = cts if isinstance(cts, (tuple, list)) else (cts,)
        grads = list(vjp(tuple(c.astype(o.dtype) for c, o in zip(cts, outs))))
        for k, a in zip(add_at, add_in):
            grads[k] = grads[k] + a.astype(F32)
        for o, g in zip(orefs[:n_rg], grads[:n_rg]):
            o[...] = g.astype(o.dtype)
        for o, g in zip(orefs[n_rg:], grads[n_rg:]):
            @pl.when(pl.program_id(0) == 0)
            def _(o=o):
                o[...] = jnp.zeros_like(o)
            o[...] += g.astype(F32)

    out_shape = [jax.ShapeDtypeStruct((nrows, r.w), d) for r, d in zip(seen, row_grads) if d is not None] + \
                [jax.ShapeDtypeStruct(p.shape, F32) for p, g in zip(params, param_grads) if g]
    out_specs = [pl.BlockSpec((tm, r.w // ncol), lambda i, j: (i, j)) for r, d in zip(seen, row_grads) if d is not None] + \
                [_full_spec(p) for p, g in zip(params, param_grads) if g]
    return pl.pallas_call(
        body, name=name, grid=(nrows // tm, ncol),
        in_specs=r_specs + [_full_spec(p) for p in params] + c_specs + a_specs,
        out_specs=out_specs, out_shape=out_shape,
        compiler_params=_cparams(("arbitrary", "arbitrary")),
    )(*r_arrs, *params, *c_arrs, *a_arrs)


def _assemble(name, groups, dtype):
    sizes = [len(g) for g in groups]
    flat = [a for g in groups for a in g]

    def fn(*tiles):
        out, k = [], 0
        for s in sizes:
            acc = tiles[k].astype(F32)
            for t in tiles[k + 1:k + s]:
                acc = acc + t.astype(F32)
            out.append(acc.astype(dtype))
            k += s
        return out[0] if len(out) == 1 else jnp.concatenate(out, axis=1)

    width = sum(g[0].shape[-1] if not isinstance(g[0], Rows) else g[0].w for g in groups)
    return _rowwise(name, fn, flat, [], [(width, dtype)])[0]


def _matmul(name, a, b, mode, out_dtype, acc_in=None):
    if mode == 'nn':
        (M, K), N = a.shape, b.shape[1]
    elif mode == 'nt':
        (M, K), N = a.shape, b.shape[0]
    else:
        (K, M), N = a.shape, b.shape[1]
    tm, tn, tk = _tile(M, 1536), _tile(N, 1536), _tile(K, 1536)
    nk = K // tk
    dn = {'nn': (((1,), (0,)), ((), ())), 'nt': (((1,), (1,)), ((), ())), 'tn': (((0,), (0,)), ((), ()))}[mode]
    has_acc = acc_in is not None

    def body(*refs):
        a_ref, b_ref = refs[:2]
        o_ref, acc = refs[-2:]

        @pl.when(pl.program_id(2) == 0)
        def _():
            acc[...] = refs[2][...] if has_acc else jnp.zeros_like(acc)
        acc[...] += lax.dot_general(a_ref[...].astype(BF16), b_ref[...].astype(BF16), dn, preferred_element_type=F32)

        @pl.when(pl.program_id(2) == nk - 1)
        def _():
            o_ref[...] = acc[...].astype(o_ref.dtype)

    a_spec = pl.BlockSpec((tk, tm), lambda i, j, k: (k, i)) if mode == 'tn' else pl.BlockSpec((tm, tk), lambda i, j, k: (i, k))
    b_spec = pl.BlockSpec((tn, tk), lambda i, j, k: (j, k)) if mode == 'nt' else pl.BlockSpec((tk, tn), lambda i, j, k: (k, j))
    o_spec = pl.BlockSpec((tm, tn), lambda i, j, k: (i, j))
    return pl.pallas_call(
        body, name=name, grid=(M // tm, N // tn, nk),
        in_specs=[a_spec, b_spec] + ([o_spec] if has_acc else []), out_specs=o_spec,
        out_shape=jax.ShapeDtypeStruct((M, N), out_dtype),
        scratch_shapes=[pltpu.VMEM((tm, tn), F32)],
        compiler_params=_cparams(("parallel", "parallel", "arbitrary")),
    )(a, b, *([acc_in] if has_acc else []))


def _small_dot(name, a, b, mode):
    dn = {'nn': (((1,), (0,)), ((), ())), 'nt': (((1,), (1,)), ((), ()))}[mode]
    M = a.shape[0]
    N = b.shape[1] if mode == 'nn' else b.shape[0]

    def body(a_ref, b_ref, o_ref):
        o_ref[...] = lax.dot_general(a_ref[...], b_ref[...], dn, precision=HIGHEST, preferred_element_type=F32)

    return pl.pallas_call(body, name=name, out_shape=jax.ShapeDtypeStruct((M, N), F32),
                          compiler_params=pltpu.CompilerParams(vmem_limit_bytes=VMEM_LIMIT))(a, b)


NN, NT, TN = (((1,), (0,)), ((), ())), (((1,), (1,)), ((), ())), (((0,), (0,)), ((), ()))


def _bdot(a, b, dn=NN):
    return lax.dot_general(a.astype(BF16), b.astype(BF16), dn, preferred_element_type=F32)


def _sigmoid(x):
    return 0.5 * jnp.tanh(0.5 * x) + 0.5


def _silu(x):
    return x * _sigmoid(x)


def _softplus(x):
    return jnp.maximum(x, 0.0) + jnp.log(1.0 + jnp.exp(-jnp.abs(x)))


def _gelu(x):
    return 0.5 * x * (1.0 + jnp.tanh(0.7978845608028654 * (x + 0.044715 * (x * x * x))))


def _rms(x, g):
    return x * lax.rsqrt(jnp.mean(x * x, axis=-1, keepdims=True) + EPS) * g


def _prenorm_fn(x, g):
    return _rms(x, g)


def _postnorm_fn(x, y, g):
    return x + _rms(y, g)


@jax.custom_vjp
def _swiglu_fn(gate, up):
    return _silu(gate.astype(F32)) * up.astype(F32)


def _swiglu_bwd(res, ct):
    g, u = res[0].astype(F32), res[1].astype(F32)
    s = _sigmoid(g)
    gs = g * s
    return (ct * u * (s + gs - gs * s)).astype(res[0].dtype), (ct * gs).astype(res[1].dtype)


_swiglu_fn.defvjp(lambda gate, up: (_swiglu_fn(gate, up), (gate, up)), _swiglu_bwd)


def _gla_pre_fn(z, wg, bg):
    logit = _bdot(z, wg) + bg
    return -_softplus(-logit) * (1.0 / GLA_TAU)


def _seg_mean(x, seg):
    return lax.dot_general(x, seg, NN, precision=HIGHEST, preferred_element_type=F32)


def _gla_post_fn(of, ob, g, norm, seg):
    o = of + ob
    o = o * lax.rsqrt(_seg_mean(o * o, seg) + EPS) * norm
    return o * _silu(g)


@jax.custom_vjp
def _swap_halves(x):
    n = x.shape[-1]
    lane = lax.broadcasted_iota(jnp.int32, x.shape, x.ndim - 1)
    lo = (lane & (HEAD_DIM - 1)) < HEAD_DIM // 2
    return jnp.where(lo, pltpu.roll(x, n - HEAD_DIM // 2, x.ndim - 1), pltpu.roll(x, HEAD_DIM // 2, x.ndim - 1))


_swap_halves.defvjp(lambda x: (_swap_halves(x), None), lambda _, g: (_swap_halves(g),))


def _rope_fn(q, k, cs, sn):
    return q * cs + _swap_halves(q) * sn, k * cs + _swap_halves(k) * sn


def _dil_comb_fn(o1, o2, o3, l1, l2, l3):
    m = jnp.maximum(jnp.maximum(l1, l2), l3)
    e1, e2, e3 = jnp.exp(l1 - m), jnp.exp(l2 - m), jnp.exp(l3 - m)
    return (e1 * o1 + e2 * o2 + e3 * o3) / (e1 + e2 + e3)


LRU_PARAMS = ['lru_cw0', 'lru_cw1', 'lru_cw2', 'lru_cw3', 'lru_cb', 'lru_wa0', 'lru_wa1', 'lru_ba0', 'lru_ba1',
              'lru_wx0', 'lru_wx1', 'lru_bx0', 'lru_bx1', 'lru_lam0', 'lru_lam1']


def _lru_pre_fn(x0, x1, x2, x3, cw0, cw1, cw2, cw3, cb, wa0, wa1, ba0, ba1, wx0, wx1, bx0, bx1, lam0, lam1):
    xc = cb + x0 * cw0 + x1 * cw1 + x2 * cw2 + x3 * cw3
    outs = []
    for wa, ba, wx, bx, lam in ((wa0, ba0, wx0, bx0, lam0), (wa1, ba1, wx1, bx1, lam1)):
        r = _sigmoid(_bdot(xc, wa) + ba)
        i = _sigmoid(_bdot(xc, wx) + bx)
        log_a = -LRU_C * r * _softplus(-lam)
        a = jnp.exp(log_a)
        u = jnp.sqrt(-jnp.tanh(log_a) * (a * a + 1.0)) * (i * xc)
        outs += [a, u]
    return outs[0], outs[2], outs[1], outs[3]


def _lru_post_fn(hf, hb, gate):
    return (hf + hb) * _gelu(gate)


def _mix_post_fn(of, ob, ga, yb, hf, hb, gc, o1, o2, o3, l1, l2, l3, norm, seg):
    ya = _gla_post_fn(of, ob, ga, norm, seg)
    yc = _lru_post_fn(hf, hb, gc)
    yd = _dil_comb_fn(o1, o2, o3, l1, l2, l3)
    return jnp.concatenate([ya.astype(BF16), yb.astype(BF16), yc.astype(BF16), yd.astype(BF16)], axis=1)


def _attn_heads(qs, kws, vws, biases):
    ss = [_bdot(q, kw, NT) * (HEAD_DIM ** -0.5) + b for q, kw, b in zip(qs, kws, biases)]
    ms = [lax.stop_gradient(jnp.max(s, axis=-1, keepdims=True)) for s in ss]
    es = [jnp.exp(s - m) for s, m in zip(ss, ms)]
    dens = [jnp.sum(e, axis=-1, keepdims=True) for e in es]
    ps = [e * (1.0 / d) for e, d in zip(es, dens)]
    os_ = [_bdot(p_, vw) for p_, vw in zip(ps, vws)]
    return os_, [m + jnp.log(d) for m, d in zip(ms, dens)]


def _cumsum_rows(x, rev):
    n = x.shape[0]
    row = lax.broadcasted_iota(jnp.int32, x.shape, 0)
    s = 1
    while s < n:
        if rev:
            x = x + jnp.where(row < n - s, pltpu.roll(x, n - s, 0), 0.0)
        else:
            x = x + jnp.where(row >= s, pltpu.roll(x, s, 0), 0.0)
        s *= 2
    return x


def _gla_chunks(qs, ks, vs, bs, sts, revs):
    C = qs[0].shape[0]
    ti = lax.broadcasted_iota(jnp.int32, (C, C), 0)
    si = lax.broadcasted_iota(jnp.int32, (C, C), 1)
    row = lax.broadcasted_iota(jnp.int32, (C, 1), 0)
    incl = {False: si <= ti, True: si >= ti}
    last = {False: row == C - 1, True: row == 0}
    mid = {False: row == C // 2 - 1, True: row == C // 2}
    bls = [jnp.sum(jnp.where(last[r], b, 0.0), axis=0, keepdims=True) for b, r in zip(bs, revs)]
    bms = [jnp.sum(jnp.where(mid[r], b, 0.0), axis=0, keepdims=True) for b, r in zip(bs, revs)]
    qss = [q * (HEAD_DIM ** -0.5) for q in qs]
    qi = [q * jnp.exp(b - bm) for q, b, bm in zip(qss, bs, bms)]
    ki = [k * jnp.exp(bm - b) for k, b, bm in zip(ks, bs, bms)]
    atts = [jnp.where(incl[r], _bdot(a, b, NT), 0.0) for a, b, r in zip(qi, ki, revs)]
    qe = [q * jnp.exp(b) for q, b in zip(qss, bs)]
    kl = [k * jnp.exp(bl - b) for k, b, bl in zip(ks, bs, bls)]
    o1 = [_bdot(a, v) for a, v in zip(atts, vs)]
    o2 = [_bdot(q, st, NT) for q, st in zip(qe, sts)]
    kvs = [_bdot(v, k, TN) for v, k in zip(vs, kl)]
    return [a + b for a, b in zip(o1, o2)], [st * jnp.exp(bl) + kv for st, bl, kv in zip(sts, bls, kvs)]


def _gla_specs(n, blocks, first):
    C = GLA_CHUNK
    at = (lambda i: i) if first else (lambda i: n - 1 - i)
    return [pl.BlockSpec((C, GROUP_WIDTH), lambda i, b=b: (at(i), b)) for b in blocks], at


def _gla_scan_fwd(p, la):
    L = p.shape[0]
    C, H, dh = GLA_CHUNK, GROUP_HEADS, HEAD_DIM
    n = L // C
    f_specs, f_at = _gla_specs(n, (P_QA, P_KA, P_VA), True)
    b_specs, b_at = _gla_specs(n, (P_QA, P_KA, P_VA), False)
    tile = lambda at, blk=0: pl.BlockSpec((C, GROUP_WIDTH), lambda i: (at(i), blk))
    st_spec = lambda at: pl.BlockSpec((None, H, dh, dh), lambda i: (at(i), 0, 0, 0))

    def body(qf, kf, vf, lf, qb, kb, vb, lb, of_ref, ob_ref, sf_ref, sb_ref, stf, stb):
        @pl.when(pl.program_id(0) == 0)
        def _():
            stf[...] = jnp.zeros_like(stf)
            stb[...] = jnp.zeros_like(stb)
        sf_ref[...] = stf[...]
        sb_ref[...] = stb[...]
        chains = [(t, h, sl) for t in ((qf, kf, vf, _cumsum_rows(lf[...], False), of_ref, stf, False),
                                       (qb, kb, vb, _cumsum_rows(lb[...], True), ob_ref, stb, True))
                  for h, sl in enumerate(HEADS)]
        os_, sts = _gla_chunks(*[[t[j][:, sl] for t, h, sl in chains] for j in range(4)],
                               [t[5][h] for t, h, sl in chains], [t[6] for t, h, sl in chains])
        for (t, h, sl), o, st_new in zip(chains, os_, sts):
            t[4][:, sl] = o
            t[5][h] = st_new

    return pl.pallas_call(
        body, name="gla_scan", grid=(n,),
        in_specs=f_specs + [tile(f_at, 0)] + b_specs + [tile(b_at, 1)],
        out_specs=[tile(f_at), tile(b_at), st_spec(f_at), st_spec(b_at)],
        out_shape=[jax.ShapeDtypeStruct((L, GROUP_WIDTH), F32)] * 2 + [jax.ShapeDtypeStruct((n, H, dh, dh), F32)] * 2,
        scratch_shapes=[pltpu.VMEM((H, dh, dh), F32)] * 2,
        compiler_params=_cparams(("arbitrary",)),
    )(p, p, p, la, p, p, p, la)


def _gla_scan_bwd(p, la, sf, sb, do):
    L = p.shape[0]
    C, H, dh = GLA_CHUNK, GROUP_HEADS, HEAD_DIM
    n = L // C
    f_specs, f_at = _gla_specs(n, (P_QA, P_KA, P_VA), False)
    b_specs, b_at = _gla_specs(n, (P_QA, P_KA, P_VA), True)
    tile = lambda at, blk=0: pl.BlockSpec((C, GROUP_WIDTH), lambda i: (at(i), blk))
    st_spec = lambda at: pl.BlockSpec((None, H, dh, dh), lambda i: (at(i), 0, 0, 0))

    def body(qf, kf, vf, lf, spf, dof, qb, kb, vb, lb, spb, dob, *rest):
        outs_f, outs_b, (dstf, dstb) = rest[0:4], rest[4:8], rest[8:10]

        @pl.when(pl.program_id(0) == 0)
        def _():
            dstf[...] = jnp.zeros_like(dstf)
            dstb[...] = jnp.zeros_like(dstb)
        chains = [(t, h, sl) for t in ((qf, kf, vf, _cumsum_rows(lf[...], False), spf, dof, outs_f, dstf, False),
                                       (qb, kb, vb, _cumsum_rows(lb[...], True), spb, dob, outs_b, dstb, True))
                  for h, sl in enumerate(HEADS)]
        nc = len(chains)
        revs = [t[8] for t, h, sl in chains]
        flat = [t[j][:, sl] for j in range(4) for t, h, sl in chains] + [t[4][h] for t, h, sl in chains]

        def f(*a):
            os_, sts = _gla_chunks(*[list(a[j * nc:(j + 1) * nc]) for j in range(5)], revs)
            return tuple(os_) + tuple(sts)

        _, vjp = jax.vjp(f, *flat)
        grads = vjp(tuple(t[5][:, sl] for t, h, sl in chains) + tuple(t[7][h] for t, h, sl in chains))
        for c_, (t, h, sl) in enumerate(chains):
            for j in range(4):
                t[6][j][:, sl] = grads[j * nc + c_].astype(t[6][j].dtype)
            t[7][h] = grads[4 * nc + c_]
        for outs, rev in ((outs_f, False), (outs_b, True)):
            outs[3][...] = _cumsum_rows(outs[3][...], not rev)

    return pl.pallas_call(
        body, name="gla_scan_b", grid=(n,),
        in_specs=f_specs + [tile(f_at, 0), st_spec(f_at), tile(f_at)] + b_specs + [tile(b_at, 1), st_spec(b_at), tile(b_at)],
        out_specs=[tile(f_at)] * 4 + [tile(b_at)] * 4,
        out_shape=[jax.ShapeDtypeStruct((L, GROUP_WIDTH), d) for d in (BF16, BF16, BF16, F32)] * 2,
        scratch_shapes=[pltpu.VMEM((H, dh, dh), F32)] * 2,
        compiler_params=_cparams(("arbitrary",)),
    )(p, p, p, la, sf, do, p, p, p, la, sb, do)


NA_W = NA_ROWS * GRID_W
NA_BW = (2 * NA_ROWS - 1) * GRID_W


def _na_start(i, rows):
    return jnp.clip(i - NA_ROWS // 2, 0, rows - NA_ROWS)


def _na_fwd(p, kb, vb, btab):
    L = p.shape[0]
    rows = L // GRID_W

    def body(q_ref, k_ref, v_ref, b_ref, o_ref):
        r = pl.program_id(0)
        s = _na_start(r, rows)
        start = pl.multiple_of(s * GRID_W, GRID_W)
        os_, _ = _attn_heads([q_ref[:, sl] for sl in HEADS], [k_ref[pl.ds(start, NA_W), sl] for sl in HEADS],
                             [v_ref[pl.ds(start, NA_W), sl] for sl in HEADS],
                             [b_ref[s - r + NA_ROWS - 1, h] for h in range(GROUP_HEADS)])
        for sl, o in zip(HEADS, os_):
            o_ref[:, sl] = o.astype(o_ref.dtype)

    whole = lambda a: pl.BlockSpec(a.shape, lambda i, nd=a.ndim: (0,) * nd)
    return pl.pallas_call(
        body, name="na_attn", grid=(rows,),
        in_specs=[pl.BlockSpec((GRID_W, GROUP_WIDTH), lambda i: (i, P_QB)), whole(kb), whole(vb), whole(btab)],
        out_specs=pl.BlockSpec((GRID_W, GROUP_WIDTH), lambda i: (i, 0)),
        out_shape=jax.ShapeDtypeStruct((L, GROUP_WIDTH), BF16),
        compiler_params=_cparams(("arbitrary",)),
    )(p, kb, vb, btab)


def _na_bwd(p, kb, vb, btab, dycat):
    L = p.shape[0]
    rows = L // GRID_W
    flush = NA_ROWS - 1
    emit = lambda i: jnp.where(i < rows, _na_start(i, rows), i - flush)

    def body(q_ref, k_ref, v_ref, b_ref, do_ref, dq_ref, dk_ref, dv_ref, db_ref, acc_k, acc_v):
        i = pl.program_id(0)

        @pl.when(i == 0)
        def _():
            acc_k[...] = jnp.zeros_like(acc_k)
            acc_v[...] = jnp.zeros_like(acc_v)
            db_ref[...] = jnp.zeros_like(db_ref)

        @pl.when((i > 0) & (emit(i) != emit(i - 1)))
        def _():
            for acc in (acc_k, acc_v):
                moved = acc[GRID_W:NA_W, :]
                acc[0:NA_W - GRID_W, :] = moved
                acc[NA_W - GRID_W:NA_W, :] = jnp.zeros((GRID_W, GROUP_WIDTH), F32)

        @pl.when(i < rows)
        def _():
            s = _na_start(i, rows)
            sv = s - i + NA_ROWS - 1
            start = pl.multiple_of(s * GRID_W, GRID_W)
            H = GROUP_HEADS
            flat = [q_ref[:, sl] for sl in HEADS] + [k_ref[pl.ds(start, NA_W), sl].astype(F32) for sl in HEADS] + \
                   [v_ref[pl.ds(start, NA_W), sl].astype(F32) for sl in HEADS] + [b_ref[sv, h] for h in range(H)]

            def f(*a):
                os_, lses = _attn_heads(a[0:H], a[H:2 * H], a[2 * H:3 * H], a[3 * H:4 * H])
                return tuple(os_) + tuple(lses)

            _, vjp = jax.vjp(f, *flat)
            grads = vjp(tuple(do_ref[:, sl] for sl in HEADS) + (jnp.zeros((GRID_W, 1), F32),) * H)
            for h, sl in enumerate(HEADS):
                dq_ref[:, sl] = grads[h].astype(dq_ref.dtype)
                acc_k[:, sl] += grads[H + h]
                acc_v[:, sl] += grads[2 * H + h]
                db_ref[sv, h] += grads[3 * H + h]

        dk_ref[...] = acc_k[0:GRID_W, :].astype(dk_ref.dtype)
        dv_ref[...] = acc_v[0:GRID_W, :].astype(dv_ref.dtype)

    whole = lambda a: pl.BlockSpec(a.shape, lambda i, nd=a.ndim: (0,) * nd)
    qrow = lambda blk: pl.BlockSpec((GRID_W, GROUP_WIDTH), lambda i: (jnp.minimum(i, rows - 1), blk))
    erow = pl.BlockSpec((GRID_W, GROUP_WIDTH), lambda i: (emit(i), 0))
    return pl.pallas_call(
        body, name="na_attn_b", grid=(rows + flush,),
        in_specs=[qrow(P_QB), whole(kb), whole(vb), whole(btab), qrow(1)],
        out_specs=[qrow(0), erow, erow, whole(btab)],
        out_shape=[jax.ShapeDtypeStruct((L, GROUP_WIDTH), BF16)] * 3 + [jax.ShapeDtypeStruct(btab.shape, F32)],
        scratch_shapes=[pltpu.VMEM((NA_W, GROUP_WIDTH), F32)] * 2,
        compiler_params=_cparams(("arbitrary",)),
    )(p, kb, vb, btab, dycat)


def _na_col_ok():
    qc = np.arange(GRID_W)[:, None]
    kc = (np.arange(NA_W) % GRID_W)[None, :]
    c0 = np.clip(qc - NA_COLS // 2, 0, GRID_W - NA_COLS)
    return (kc >= c0) & (kc < c0 + NA_COLS)


def _rpb_tables():
    c = np.arange(GRID_W)
    dc = np.clip(c[None, :] - c[:, None], -(NA_COLS - 1), NA_COLS - 1) + NA_COLS - 1
    t = np.zeros((2 * NA_COLS - 1, GRID_W, GRID_W), np.float32)
    t[dc, c[:, None], c[None, :]] = 1.0
    return jnp.asarray(t.reshape(2 * NA_COLS - 1, GRID_W * GRID_W))


def _rpb_expand(rpb, tab):
    H = rpb.shape[0]
    xt = _small_dot("na_bias", rpb.reshape(H * (2 * NA_ROWS - 1), 2 * NA_COLS - 1), tab, 'nn')
    b15 = xt.reshape(H, 2 * NA_ROWS - 1, GRID_W, GRID_W).transpose(0, 2, 1, 3).reshape(H, GRID_W, NA_BW)
    ok = jnp.asarray(_na_col_ok())
    return jnp.stack([jnp.where(ok, b15[:, :, sv * GRID_W:sv * GRID_W + NA_W], NEG) for sv in range(NA_ROWS)])


def _rpb_contract(dbv, tab):
    H = dbv.shape[1]
    db = sum(jnp.pad(dbv[sv], ((0, 0), (0, 0), (sv * GRID_W, NA_BW - NA_W - sv * GRID_W))) for sv in range(NA_ROWS))
    dx = db.reshape(H, GRID_W, 2 * NA_ROWS - 1, GRID_W).transpose(0, 2, 1, 3).reshape(H * (2 * NA_ROWS - 1), GRID_W * GRID_W)
    return _small_dot("na_bias_b", dx, tab, 'nt').reshape(H, 2 * NA_ROWS - 1, 2 * NA_COLS - 1)


def _band_bias(i, tq, w, halo, n):
    a = lax.broadcasted_iota(jnp.int32, (tq, w), 0)
    b = lax.broadcasted_iota(jnp.int32, (tq, w), 1)
    kpos = i * tq - halo + b
    d = b - halo - a
    return jnp.where((d <= halo) & (d >= -halo) & (kpos >= 0) & (kpos < n), 0.0, NEG)


def _band_fwd(name, q, kp, vp, tq, halo):
    G, n, _ = q.shape
    w = tq + 2 * halo

    def body(q_ref, k_ref, v_ref, o_ref, l_ref):
        i = pl.program_id(1)
        start = pl.multiple_of(i * tq, tq)
        bias = _band_bias(i, tq, w, halo, n)
        os_, lses = _attn_heads([q_ref[:, sl] for sl in HEADS], [k_ref[pl.ds(start, w), sl] for sl in HEADS],
                                [v_ref[pl.ds(start, w), sl] for sl in HEADS], [bias] * GROUP_HEADS)
        for sl, o, lse in zip(HEADS, os_, lses):
            o_ref[:, sl] = o.astype(o_ref.dtype)
            l_ref[:, sl] = jnp.broadcast_to(lse, (tq, HEAD_DIM))

    qblk = pl.BlockSpec((None, tq, GROUP_WIDTH), lambda g, i: (g, i, 0))
    kblk = pl.BlockSpec((None, n + 2 * halo, GROUP_WIDTH), lambda g, i: (g, 0, 0))
    return pl.pallas_call(
        body, name=name, grid=(G, n // tq), in_specs=[qblk, kblk, kblk], out_specs=[qblk, qblk],
        out_shape=[jax.ShapeDtypeStruct((G, n, GROUP_WIDTH), d) for d in (BF16, F32)],
        compiler_params=_cparams(("parallel", "arbitrary")),
    )(q, kp, vp)


def _band_bwd(name, q, kp, vp, do, dl, tq, halo):
    G, n, _ = q.shape
    w = tq + 2 * halo
    nq = n // tq

    def body(q_ref, k_ref, v_ref, do_ref, dl_ref, dq_ref, dk_ref, dv_ref, acc_k, acc_v):
        i = pl.program_id(1)

        @pl.when(i == 0)
        def _():
            acc_k[...] = jnp.zeros_like(acc_k)
            acc_v[...] = jnp.zeros_like(acc_v)

        @pl.when(i > 0)
        def _():
            for acc in (acc_k, acc_v):
                moved = acc[tq:w, :]
                acc[0:2 * halo, :] = moved
                acc[2 * halo:w, :] = jnp.zeros((tq, GROUP_WIDTH), F32)

        @pl.when(i < nq)
        def _():
            start = pl.multiple_of(i * tq, tq)
            bias = _band_bias(i, tq, w, halo, n)
            H = GROUP_HEADS
            flat = [q_ref[:, sl].astype(F32) for sl in HEADS] + [k_ref[pl.ds(start, w), sl].astype(F32) for sl in HEADS] + \
                   [v_ref[pl.ds(start, w), sl].astype(F32) for sl in HEADS]

            def f(*a):
                os_, lses = _attn_heads(a[0:H], a[H:2 * H], a[2 * H:3 * H], [bias] * H)
                return tuple(os_) + tuple(lses)

            _, vjp = jax.vjp(f, *flat)
            grads = vjp(tuple(do_ref[:, sl].astype(F32) for sl in HEADS) +
                        tuple(jnp.sum(dl_ref[:, sl], axis=1, keepdims=True) for sl in HEADS))
            for h, sl in enumerate(HEADS):
                dq_ref[:, sl] = grads[h].astype(dq_ref.dtype)
                acc_k[:, sl] += grads[H + h]
                acc_v[:, sl] += grads[2 * H + h]

        dk_ref[...] = acc_k[0:tq, :].astype(dk_ref.dtype)
        dv_ref[...] = acc_v[0:tq, :].astype(dv_ref.dtype)

    qblk = pl.BlockSpec((None, tq, GROUP_WIDTH), lambda g, i: (g, jnp.minimum(i, nq - 1), 0))
    kblk = pl.BlockSpec((None, n + 2 * halo, GROUP_WIDTH), lambda g, i: (g, 0, 0))
    eblk = pl.BlockSpec((None, tq, GROUP_WIDTH), lambda g, i: (g, i, 0))
    return pl.pallas_call(
        body, name=name, grid=(G, nq + 1), in_specs=[qblk, kblk, kblk, qblk, qblk], out_specs=[qblk, eblk, eblk],
        out_shape=[jax.ShapeDtypeStruct((G, n, GROUP_WIDTH), BF16)] + [jax.ShapeDtypeStruct((G, (nq + 1) * tq, GROUP_WIDTH), BF16)] * 2,
        scratch_shapes=[pltpu.VMEM((w, GROUP_WIDTH), F32)] * 2,
        compiler_params=_cparams(("parallel", "arbitrary")),
    )(q, kp, vp, do, dl)


def _lin_scan(name, coef, inp, rev, coef_shift=0):
    L, C = coef.shape
    tt = 256 if L % 256 == 0 else L
    nt = L // tt
    tile = (lambda i: nt - 1 - i) if rev else (lambda i: i)
    tidx = lambda i: (tile(i), 0)

    def body(*refs):
        u_ref, o_ref, carry = refs[-3:]

        @pl.when(pl.program_id(0) == 0)
        def _():
            carry[...] = jnp.zeros_like(carry)
        if coef_shift:
            a = _shift_tile(refs[0][...], refs[1][...], refs[2][...], coef_shift, tile(pl.program_id(0)), nt)
        else:
            a = refs[0][...]
        u = u_ref[...]
        row = lax.broadcasted_iota(jnp.int32, (tt, C), 0)
        s = 1
        while s < tt:
            ok = (row < tt - s) if rev else (row >= s)
            sh = tt - s if rev else s
            u = u + a * jnp.where(ok, pltpu.roll(u, sh, 0), 0.0)
            a = a * jnp.where(ok, pltpu.roll(a, sh, 0), 1.0)
            s *= 2
        out = u + a * carry[...]
        o_ref[...] = out
        carry[...] = out[0:1] if rev else out[tt - 1:tt]

    blk = pl.BlockSpec((tt, C), tidx)
    near = [pl.BlockSpec((tt, C), lambda i, off=off: (jnp.clip(tile(i) + off, 0, nt - 1), 0)) for off in (-1, 0, 1)]
    coef_specs = near if coef_shift else [blk]
    return pl.pallas_call(
        body, name=name, grid=(nt,), in_specs=coef_specs + [blk], out_specs=blk,
        out_shape=jax.ShapeDtypeStruct((L, C), F32), scratch_shapes=[pltpu.VMEM((1, C), F32)],
        compiler_params=_cparams(("arbitrary",)),
    )(*([coef] * len(coef_specs)), inp)


def _pieces(shape):
    n0 = max(d for d in range(1, DMA_PIECES + 1) if shape[0] % d == 0)
    n1 = 1
    if len(shape) >= 3:
        n1 = max(d for d in range(1, DMA_PIECES // n0 + 1) if shape[1] % d == 0)
    s0, s1 = shape[0] // n0, (shape[1] // n1 if len(shape) >= 3 else 0)
    out = []
    for i in range(n0):
        for j in range(n1):
            out.append((pl.ds(i * s0, s0),) + ((pl.ds(j * s1, s1),) if len(shape) >= 3 else ()))
    return out


def _exchange(name, src, axes, gather):
    flips = {'xy': [(1, 0, 0), (0, 1, 0), (1, 1, 0)], 'c': [(0, 0, 1)],
             'xyc': [(fx, fy, fc) for fx in (0, 1) for fy in (0, 1) for fc in (0, 1)][1:]}[axes]
    n = len(flips) + 1
    blk_shape = tuple(src.shape if gather else src.shape[1:])
    pieces = _pieces(blk_shape)

    def number(px, py, pc):
        return {'xy': 2 * px + py, 'c': pc, 'xyc': 4 * px + 2 * py + pc}[axes]

    def body(src_ref, out_ref, send_sems, recv_sems):
        x, y, c = lax.axis_index("x"), lax.axis_index("y"), lax.axis_index("c")
        me = number(x, y, c)
        piece = (lambda k: src_ref) if gather else (lambda k: src_ref.at[k])
        peers = []
        for s, (fx, fy, fc) in enumerate(flips):
            px, py, pc = (x + fx) % 2, (y + fy) % 2, (c + fc) % 2

            def copy(ix, s=s, px=px, py=py, pc=pc):
                part = (lambda r: r) if ix is None else (lambda r: r.at[ix])
                return pltpu.make_async_remote_copy(
                    src_ref=part(piece(number(px, py, pc))), dst_ref=part(out_ref.at[me]),
                    send_sem=send_sems.at[s], recv_sem=recv_sems.at[s],
                    device_id=(px, py, pc), device_id_type=MESH)

            for ix in pieces:
                copy(ix).start()
            peers.append(copy)
        for copy in peers:
            copy(None).wait()

    out = pl.pallas_call(
        body, name=name, out_shape=jax.ShapeDtypeStruct((n,) + blk_shape, src.dtype),
        in_specs=[pl.BlockSpec(memory_space=pl.ANY)], out_specs=pl.BlockSpec(memory_space=pl.ANY),
        scratch_shapes=[pltpu.SemaphoreType.DMA((n - 1,)), pltpu.SemaphoreType.DMA((n - 1,))],
    )(src)
    me = number(lax.axis_index("x"), lax.axis_index("y"), lax.axis_index("c"))
    own = src if gather else lax.dynamic_index_in_dim(src, me, 0, keepdims=False)
    return lax.dynamic_update_index_in_dim(out, own, me, 0)


def _ordered_sum(name, buf, dtype=F32):
    n = buf.shape[0]

    def fn(*t):
        acc = t[0].astype(F32)
        for v in t[1:]:
            acc = acc + v.astype(F32)
        return acc

    return _rowwise(name, fn, [Rows(buf, lead=k) for k in range(n)], [], [(buf.shape[-1], dtype)])[0]


def _reduce_big(name, g):
    mine = _ordered_sum(name + "_sum_c", _exchange(name + "_swap_c", g, 'c', False).reshape(2, -1, g.shape[-1]), BF16)
    mine = mine.reshape(g.shape[1:])
    tot = _ordered_sum(name + "_sum_xy", _exchange(name + "_a2a_xy", mine, 'xy', False))
    return _exchange(name + "_share_c", tot, 'c', True)


def _dilate(t, dil):
    L, C = t.shape
    return t.reshape(L // dil, dil, C).transpose(1, 0, 2)


def _undilate(t):
    dil, n, C = t.shape
    return t.transpose(1, 0, 2).reshape(dil * n, C)


def _pad_rows(t, halo):
    return jnp.pad(t, ((0, 0), (halo, halo), (0, 0)))


def _pcol(p, blk):
    return Rows(p, GROUP_WIDTH, blk)


def _pslice(p, blk):
    return p[:, blk * GROUP_WIDTH:(blk + 1) * GROUP_WIDTH]


def _conv_taps(p):
    return Rows(p, GROUP_WIDTH, P_XC, shifts=[LRU_CONV_LEFT - j for j in range(LRU_CONV)])


def _seg_matrix():
    h = np.arange(GROUP_WIDTH) // HEAD_DIM
    return jnp.asarray((h[:, None] == h[None, :]).astype(np.float32) / HEAD_DIM)


def _rope_tables(L):
    pos = jnp.arange(L, dtype=F32)
    inv_freq = ROPE_THETA ** (-jnp.arange(0, HEAD_DIM, 2, dtype=F32) / HEAD_DIM)
    ang = pos[:, None] * inv_freq[None, :]
    cos, sin = jnp.cos(ang), jnp.sin(ang)
    cs = jnp.tile(jnp.concatenate([cos, cos], axis=1), (1, GROUP_HEADS))
    sn = jnp.tile(jnp.concatenate([-sin, sin], axis=1), (1, GROUP_HEADS))
    return cs, sn


def _dil_branches(L):
    out = []
    for window, dil in DIL_PAIRS:
        radius = window // (2 * dil)
        n = L // dil
        out.append((dil, radius, min(256, n)))
    return out


def _dil_operands(qr, kr, p, dil, radius):
    q = _dilate(qr.astype(BF16), dil)
    k = _pad_rows(_dilate(kr.astype(BF16), dil), radius)
    v = _pad_rows(_dilate(_pslice(p, P_VD).astype(BF16), dil), radius)
    return q, k, v


def _layer_fwd(x, w, c):
    L, D = x.shape
    sv = {'x_in': x}
    h = _rowwise("mix_prenorm", _prenorm_fn, [x], [w['mix_norm_pre']], [(D, BF16)])[0]
    p = _matmul("mix_proj", h, w['w_in'], 'nn', F32)
    sv['p'] = p

    la = _rowwise("gla_pre", _gla_pre_fn, [Rows(p, LANE, P_Z // LANE)], [w['gla_wg'], w['gla_bg']], [(2 * GROUP_WIDTH, F32)])[0]
    of, ob, sf, sb = _gla_scan_fwd(p, la)
    sv.update(la=la, sf=sf, sb=sb, of=of, ob=ob)

    yb = _na_fwd(p, _pslice(p, P_KB).astype(BF16), _pslice(p, P_VB).astype(BF16), _rpb_expand(w['na_rpb'], c['rpb_tab']))

    a0, a1, u0, u1 = _rowwise("lru_pre", _lru_pre_fn, [_conv_taps(p)], [w[k] for k in LRU_PARAMS], [(GROUP_WIDTH, F32)] * 4)
    hf = _lin_scan("lru_scan_f", a0, u0, False)
    hb = _lin_scan("lru_scan_b", a1, u1, True)
    sv.update(a0=a0, a1=a1, hf=hf, hb=hb)

    qr, kr = _rowwise("rope", _rope_fn, [_pcol(p, P_QD), _pcol(p, P_KD), c['cos'], c['sin']], [], [(GROUP_WIDTH, F32)] * 2)
    os_, ls_ = [], []
    for dil, radius, tq in _dil_branches(L):
        o, lse = _band_fwd(f"dil_attn{dil}", *_dil_operands(qr, kr, p, dil, radius), tq, radius)
        os_.append(_undilate(o))
        ls_.append(_undilate(lse))
    sv.update(qr=qr, kr=kr, dil_o=os_, dil_l=ls_)

    ycat = _rowwise("mix_post", _mix_post_fn, [of, ob, _pcol(p, P_GA), yb, hf, hb, _pcol(p, P_GC)] + os_ + ls_,
                    [w['gla_norm'], c['seg']], [(4 * GROUP_WIDTH, BF16)])[0]
    y = _matmul("mix_out", ycat, w['w_out'], 'nn', F32)
    xm = _rowwise("mix_postnorm", _postnorm_fn, [x, y], [w['mix_norm_post']], [(D, F32)])[0]
    sv.update(ycat=ycat, y=y, x_mid=xm)

    h2 = _rowwise("ffn_prenorm", _prenorm_fn, [xm], [w['ffn_norm_pre']], [(D, BF16)])[0]
    gate = _matmul("ffn_gate", h2, w['ffn_wg'], 'nn', BF16)
    up = _matmul("ffn_up", h2, w['ffn_wu'], 'nn', BF16)
    dff = gate.shape[1]
    act = _rowwise("ffn_act", _swiglu_fn, [gate, up], [], [(dff, BF16)], ncol=dff // _tile(dff, 1536))[0]
    f = _matmul("ffn_out", act, w['ffn_w_out'], 'nn', F32)
    xo = _rowwise("ffn_postnorm", _postnorm_fn, [xm, f], [w['ffn_norm_post']], [(D, F32)])[0]
    sv.update(gate=gate, up=up, act=act, f=f)
    return xo, sv


def _layer_bwd(dx, w, c, sv):
    L, D = dx.shape
    g = {}
    as_f32 = lambda t: (t.astype(F32),)
    df, g['ffn_norm_post'] = _rowwise_bwd("ffn_postnorm_b", lambda y, gn: _rms(y, gn), [sv['f']], [w['ffn_norm_post']],
                                          [dx], as_f32, [BF16], [True])
    dact = _matmul("ffn_out_bx", df, w['ffn_w_out'], 'nt', BF16)
    g['ffn_w_out'] = _matmul("ffn_out_bw", sv['act'], df, 'tn', F32)
    gate, up = sv['gate'], sv['up']
    dff = gate.shape[1]
    dgate, dup = _rowwise_bwd("ffn_act_b", _swiglu_fn, [gate, up], [], [dact], as_f32, [BF16, BF16], [], ncol=dff // _tile(dff, 1536))
    dh2 = _matmul("ffn_up_bx", dup, w['ffn_wu'], 'nt', F32, acc_in=_matmul("ffn_gate_bx", dgate, w['ffn_wg'], 'nt', F32))
    xm = sv['x_mid']
    h2 = _rowwise("ffn_prenorm_r", _prenorm_fn, [xm], [w['ffn_norm_pre']], [(D, BF16)])[0]
    g['ffn_wg'] = _matmul("ffn_gate_bw", h2, dgate, 'tn', F32)
    g['ffn_wu'] = _matmul("ffn_up_bw", h2, dup, 'tn', F32)
    dxm, g['ffn_norm_pre'] = _rowwise_bwd("ffn_prenorm_b", _prenorm_fn, [xm], [w['ffn_norm_pre']], [dh2], as_f32, [F32], [True],
                                          row_grad_add=[dx])

    dy, g['mix_norm_post'] = _rowwise_bwd("mix_postnorm_b", lambda y, gn: _rms(y, gn), [sv['y']], [w['mix_norm_post']],
                                          [dxm], as_f32, [BF16], [True])
    dycat = _matmul("mix_out_bx", dy, w['w_out'], 'nt', F32)
    g['w_out'] = _matmul("mix_out_bw", sv['ycat'], dy, 'tn', F32)
    p = sv['p']
    dya, dyb, dyc, dyd = (Rows(dycat, GROUP_WIDTH, k) for k in range(4))

    dof, dga, g['gla_norm'] = _rowwise_bwd("gla_post_b", _gla_post_fn, [sv['of'], sv['ob'], _pcol(p, P_GA)],
                                           [w['gla_norm'], c['seg']], [dya], as_f32, [F32, None, BF16], [True, False])
    la = sv['la']
    dqf, dkf, dvf, dlf, dqb_, dkb_, dvb_, dlb = _gla_scan_bwd(p, la, sv['sf'], sv['sb'], dof)
    dz, g['gla_wg'], g['gla_bg'] = _rowwise_bwd("gla_pre_b", _gla_pre_fn, [Rows(p, LANE, P_Z // LANE)], [w['gla_wg'], w['gla_bg']],
                                                [dlf, dlb], lambda a, b: (jnp.concatenate([a, b], axis=1),), [BF16], [True, True])

    btab = _rpb_expand(w['na_rpb'], c['rpb_tab'])
    dqn, dkn, dvn, dbt = _na_bwd(p, _pslice(p, P_KB).astype(BF16), _pslice(p, P_VB).astype(BF16), btab, dycat)
    g['na_rpb'] = _rpb_contract(dbt, c['rpb_tab'])

    dh, dgc = _rowwise_bwd("lru_post_b", _lru_post_fn, [sv['hf'], sv['hb'], _pcol(p, P_GC)], [], [dyc], as_f32, [F32, None, BF16], [])
    lam0 = _lin_scan("lru_scan_f_b", sv['a0'], dh, True, coef_shift=-1)
    lam1 = _lin_scan("lru_scan_b_b", sv['a1'], dh, False, coef_shift=1)
    res = _rowwise_bwd("lru_pre_b", _lru_pre_fn, [_conv_taps(p)], [w[k] for k in LRU_PARAMS],
                       [lam0, lam1, Rows(sv['hf'], shifts=[1]), Rows(sv['hb'], shifts=[-1])],
                       lambda l0, l1, hfp, hbn: (l0 * hfp, l1 * hbn, l0, l1), [F32] * 4, [True] * len(LRU_PARAMS))
    dxs = res[:4]
    for k, nm in enumerate(LRU_PARAMS):
        g[nm] = res[4 + k]
    dxc = [Rows(dxs[j], shifts=[j - LRU_CONV_LEFT]) for j in range(LRU_CONV)]

    comb = _rowwise_bwd("dil_comb_b", _dil_comb_fn, sv['dil_o'] + sv['dil_l'], [], [dyd], as_f32, [BF16] * 3 + [F32] * 3, [])
    qr, kr = sv['qr'], sv['kr']
    dqs, dks, dvs = [], [], []
    for k, (dil, radius, tq) in enumerate(_dil_branches(L)):
        n = L // dil
        dq_, dk_, dv_ = _band_bwd(f"dil_attn{dil}_b", *_dil_operands(qr, kr, p, dil, radius),
                                  _dilate(comb[k], dil), _dilate(comb[3 + k], dil), tq, radius)
        dqs.append(_undilate(dq_))
        dks.append(_undilate(dk_[:, radius:radius + n]))
        dvs.append(_undilate(dv_[:, radius:radius + n]))
    dqd, dkd = _rowwise_bwd("rope_b", _rope_fn, [_pcol(p, P_QD), _pcol(p, P_KD), c['cos'], c['sin']], [], dqs + dks,
                            lambda *t: (sum(v.astype(F32) for v in t[:3]), sum(v.astype(F32) for v in t[3:])), [BF16, BF16, None, None], [])

    dp = _assemble("mix_dp", [[dqf, dqb_], [dkf, dkb_], [dvf, dvb_], [dga], [dqn], [dkn], [dvn], dxc, [dgc], [dqd], [dkd], dvs, [dz]], BF16)
    dh1 = _matmul("mix_proj_bx", dp, w['w_in'], 'nt', F32)
    x_in = sv['x_in']
    h = _rowwise("mix_prenorm_r", _prenorm_fn, [x_in], [w['mix_norm_pre']], [(D, BF16)])[0]
    g['w_in'] = _matmul("mix_proj_bw", h, dp, 'tn', F32)
    dxi, g['mix_norm_pre'] = _rowwise_bwd("mix_prenorm_b", _prenorm_fn, [x_in], [w['mix_norm_pre']], [dh1], as_f32, [F32], [True],
                                          row_grad_add=[dxm])
    return dxi, g


def _loss_fn(y, t):
    e = y - t
    return e * (1.0 / y.shape[1]), jnp.sum(e * e, axis=0, keepdims=True)


def _gather_cols(name, shard, axis):
    half = shard.shape[0] // 2
    mine = lax.dynamic_slice_in_dim(shard, lax.axis_index("c") * half, half, axis=0).astype(BF16)
    both = _exchange(name + "_c", _exchange(name + "_xy", mine, 'xy', True), 'c', True)
    shp = list(shard.shape)
    shp[axis] *= 4
    return jnp.moveaxis(both, 1, axis + 1).reshape(shp)


def _pack(arrs, mult=64 * LANE):
    flat = jnp.concatenate([a.reshape(-1) for a in arrs])
    pad = (-flat.shape[0]) % mult
    return jnp.pad(flat, (0, pad)).reshape(-1, LANE)


def _unpack(buf, shapes):
    flat, out, k = buf.reshape(-1), [], 0
    for s in shapes:
        sz = int(np.prod(s))
        out.append(flat[k:k + sz].reshape(s))
        k += sz
    return out


def _perm_in(w_in):
    pad = jnp.zeros(w_in.shape[:-1] + (D_INP - D_IN,), w_in.dtype)
    return jnp.concatenate([w_in[..., :P_QB * GROUP_WIDTH], w_in[..., P_QB * GROUP_WIDTH + 2 * GLA_RANK:],
                            w_in[..., P_QB * GROUP_WIDTH:P_QB * GROUP_WIDTH + 2 * GLA_RANK], pad], axis=-1)


def _unperm_in(g):
    return jnp.concatenate([g[..., :P_QB * GROUP_WIDTH], g[..., P_Z:P_Z + 2 * GLA_RANK], g[..., P_QB * GROUP_WIDTH:P_Z]], axis=-1)


def _block_diag(wb):
    l = wb.shape[0]
    eye = jnp.eye(GROUP_HEADS, dtype=wb.dtype)
    return jnp.einsum('lehij,hg->lehigj', wb, eye).reshape(l, 2, GROUP_WIDTH, GROUP_WIDTH)


def _block_diag_grad(gw):
    l = gw.shape[0]
    g6 = gw.reshape(l, 2, GROUP_HEADS, HEAD_DIM, GROUP_HEADS, HEAD_DIM)
    return jnp.stack([g6[:, :, h, :, h, :] for h in range(GROUP_HEADS)], axis=2)


def _gate_matrix(wg):
    l = wg.shape[0]
    m = jnp.zeros((l, LANE, 2 * GROUP_WIDTH), wg.dtype)
    for e in range(2):
        m = m.at[:, e * GLA_RANK:(e + 1) * GLA_RANK, e * GROUP_WIDTH:(e + 1) * GROUP_WIDTH].set(wg[:, e])
    return m


def _gate_matrix_grad(gm):
    return jnp.stack([gm[:, e * GLA_RANK:(e + 1) * GLA_RANK, e * GROUP_WIDTH:(e + 1) * GROUP_WIDTH] for e in range(2)], axis=1)


def _adam_fn(w, g, m, v):
    m = ADAM_B1 * m + (1.0 - ADAM_B1) * g
    v = ADAM_B2 * v + (1.0 - ADAM_B2) * (g * g)
    m_hat = m / (1.0 - ADAM_B1 ** ADAM_STEP)
    v_hat = v / (1.0 - ADAM_B2 ** ADAM_STEP)
    return -ADAM_LR * (m_hat / (jnp.sqrt(v_hat) + ADAM_EPS) + ADAM_WD * w), m, v


def _adam(name, w, g, m, v):
    shp = w.shape
    two = lambda t: t.reshape(-1, shp[-1])
    res = _rowwise(name, _adam_fn, [two(w), two(g), two(m), two(v)], [], [(shp[-1], F32)] * 3)
    return [r.reshape(shp) for r in res]


def _local_step(x, target, fw):
    L, D = x.shape
    depth = fw['w_in'].shape[0]
    cs, sn = _rope_tables(L)
    consts = {'seg': _seg_matrix(), 'rpb_tab': _rpb_tables(), 'cos': cs, 'sin': sn}
    layer = lambda l: {k: v[l] for k, v in fw.items()}
    saved = []
    for l in range(depth):
        x, sv = _layer_fwd(x, layer(l), consts)
        saved.append(sv)
    dx, sq = _rowwise("loss", _loss_fn, [x, target], [], [(D, F32)], acc_outs=[(1, D)])
    grads = [None] * depth
    for l in reversed(range(depth)):
        dx, grads[l] = _layer_bwd(dx, layer(l), consts, saved[l])
    return sq, dx, {k: jnp.stack([g[k] for g in grads]) for k in grads[0]}


def kernel(x, mix_norm_pre, mix_norm_post, w_in, gla_w_gate, gla_b_gate, gla_norm, na_rpb, lru_conv_w, lru_conv_b, lru_w_a, lru_b_a, lru_w_x, lru_b_x, lru_lambda, w_out, ffn_norm_pre, ffn_norm_post, ffn_w_in, ffn_w_out, loss_target, m_mix_norm_pre, m_mix_norm_post, m_w_in, m_gla_w_gate, m_gla_b_gate, m_gla_norm, m_na_rpb, m_lru_conv_w, m_lru_conv_b, m_lru_w_a, m_lru_b_a, m_lru_w_x, m_lru_b_x, m_lru_lambda, m_w_out, m_ffn_norm_pre, m_ffn_norm_post, m_ffn_w_in, m_ffn_w_out, v_mix_norm_pre, v_mix_norm_post, v_w_in, v_gla_w_gate, v_gla_b_gate, v_gla_norm, v_na_rpb, v_lru_conv_w, v_lru_conv_b, v_lru_w_a, v_lru_b_a, v_lru_w_x, v_lru_b_x, v_lru_lambda, v_w_out, v_ffn_norm_pre, v_ffn_norm_post, v_ffn_w_in, v_ffn_w_out):
    args = (mix_norm_pre, mix_norm_post, w_in, gla_w_gate, gla_b_gate, gla_norm, na_rpb, lru_conv_w, lru_conv_b, lru_w_a, lru_b_a, lru_w_x, lru_b_x, lru_lambda, w_out, ffn_norm_pre, ffn_norm_post, ffn_w_in, ffn_w_out,
            m_mix_norm_pre, m_mix_norm_post, m_w_in, m_gla_w_gate, m_gla_b_gate, m_gla_norm, m_na_rpb, m_lru_conv_w, m_lru_conv_b, m_lru_w_a, m_lru_b_a, m_lru_w_x, m_lru_b_x, m_lru_lambda, m_w_out, m_ffn_norm_pre, m_ffn_norm_post, m_ffn_w_in, m_ffn_w_out,
            v_mix_norm_pre, v_mix_norm_post, v_w_in, v_gla_w_gate, v_gla_b_gate, v_gla_norm, v_na_rpb, v_lru_conv_w, v_lru_conv_b, v_lru_w_a, v_lru_b_a, v_lru_w_x, v_lru_b_x, v_lru_lambda, v_w_out, v_ffn_norm_pre, v_ffn_norm_post, v_ffn_w_in, v_ffn_w_out)
    nw = len(WEIGHTS)
    W = dict(zip(WEIGHTS, args[:nw]))
    M = dict(zip(WEIGHTS, args[nw:2 * nw]))
    V = dict(zip(WEIGHTS, args[2 * nw:]))
    chip = 2 * lax.axis_index("x") + lax.axis_index("y")

    full = dict(W)
    full['w_in'] = _gather_cols("ag_w_in", w_in, 2)
    full['ffn_w_in'] = _gather_cols("ag_ffn_w_in", ffn_w_in, 2)
    full['w_out'] = _gather_cols("ag_w_out", w_out, 1)
    full['ffn_w_out'] = _gather_cols("ag_ffn_w_out", ffn_w_out, 1)
    small = list(SMALL_SHARDED)
    got = _exchange("ag_small", _pack([W[k] for k in small]), 'xy', True)
    for k, parts in zip(small, zip(*[_unpack(got[j], [W[k].shape for k in small]) for j in range(4)])):
        ax = SMALL_SHARDED[k]
        stacked = jnp.moveaxis(jnp.stack(parts), 0, ax)
        shp = list(W[k].shape)
        shp[ax] *= 4
        full[k] = stacked.reshape(shp)

    sq, dx0, g = _local_step(x[0], loss_target[0], _layer_weights(full))
    loss = lax.psum(0.5 * jnp.sum(sq) / x.shape[-1], ("x", "y", "c"))
    gfull = _stored_grads(g)

    grad = {}
    for k in BIG:
        gk = gfull[k]
        if k in ('w_in', 'ffn_w_in'):
            l, K, N = gk.shape
            cut = gk.reshape(2, l * K // 2, 4, N // 4).transpose(0, 2, 1, 3)
            red = _reduce_big("rs_" + k, cut)
            grad[k] = red.reshape(l, K, N // 4)
        else:
            l, K, N = gk.shape
            cut = gk.reshape(2, l // 2, 4, K // 4, N).transpose(0, 2, 1, 3, 4).reshape(2, 4, (l // 2) * (K // 4), N)
            red = _reduce_big("rs_" + k, cut)
            grad[k] = red.reshape(l, K // 4, N)
    rest = [k for k in WEIGHTS if k not in BIG]
    allg = _exchange("ar_small", _pack([gfull[k] for k in rest]), 'xyc', True)
    summed = _unpack(_ordered_sum("ar_small_sum", allg), [gfull[k].shape for k in rest])
    for k, s in zip(rest, summed):
        if k in SMALL_SHARDED:
            ax = SMALL_SHARDED[k]
            n = W[k].shape[ax]
            s = lax.dynamic_slice_in_dim(s, chip * n, n, axis=ax)
        grad[k] = s

    delta, new_m, new_v = {}, {}, {}
    for k in BIG:
        delta[k], new_m[k], new_v[k] = _adam("adam_" + k, W[k], grad[k], M[k], V[k])
    shapes = [W[k].shape for k in rest]
    res = _rowwise("adam_small", _adam_fn, [_pack([d[k] for k in rest]) for d in (W, grad, M, V)], [], [(LANE, F32)] * 3)
    for d, r in zip((delta, new_m, new_v), res):
        for k, t in zip(rest, _unpack(r, shapes)):
            d[k] = t

    return (loss, dx0[None], *[grad[k] for k in WEIGHTS], *[delta[k] for k in WEIGHTS],
            *[new_m[k] for k in WEIGHTS], *[new_v[k] for k in WEIGHTS])


def _layer_weights(full):
    depth = full['w_in'].shape[0]
    dff = full['ffn_w_in'].shape[-1] // 2
    row = lambda t: t[:, None, :]
    fw = {
        'mix_norm_pre': row(full['mix_norm_pre']), 'mix_norm_post': row(full['mix_norm_post']),
        'ffn_norm_pre': row(full['ffn_norm_pre']), 'ffn_norm_post': row(full['ffn_norm_post']),
        'w_in': _perm_in(full['w_in']), 'w_out': full['w_out'],
        'ffn_wg': full['ffn_w_in'][..., :dff], 'ffn_wu': full['ffn_w_in'][..., dff:], 'ffn_w_out': full['ffn_w_out'],
        'gla_wg': _gate_matrix(full['gla_w_gate']), 'gla_bg': full['gla_b_gate'].reshape(depth, 1, 2 * GROUP_WIDTH),
        'gla_norm': row(full['gla_norm']), 'na_rpb': full['na_rpb'],
        'lru_cb': row(full['lru_conv_b']),
    }
    wa_bd, wx_bd = _block_diag(full['lru_w_a']), _block_diag(full['lru_w_x'])
    for j in range(LRU_CONV):
        fw[f'lru_cw{j}'] = row(full['lru_conv_w'][:, j])
    for e in range(2):
        fw[f'lru_wa{e}'], fw[f'lru_wx{e}'] = wa_bd[:, e], wx_bd[:, e]
        fw[f'lru_ba{e}'], fw[f'lru_bx{e}'] = row(full['lru_b_a'][:, e]), row(full['lru_b_x'][:, e])
        fw[f'lru_lam{e}'] = row(full['lru_lambda'][:, e])
    return fw


def _stored_grads(g):
    depth = g['w_in'].shape[0]
    return {
        'mix_norm_pre': g['mix_norm_pre'][:, 0], 'mix_norm_post': g['mix_norm_post'][:, 0],
        'ffn_norm_pre': g['ffn_norm_pre'][:, 0], 'ffn_norm_post': g['ffn_norm_post'][:, 0],
        'w_in': _unperm_in(g['w_in']), 'w_out': g['w_out'],
        'ffn_w_in': jnp.concatenate([g['ffn_wg'], g['ffn_wu']], axis=-1), 'ffn_w_out': g['ffn_w_out'],
        'gla_w_gate': _gate_matrix_grad(g['gla_wg']), 'gla_b_gate': g['gla_bg'].reshape(depth, 2, GROUP_WIDTH),
        'gla_norm': g['gla_norm'][:, 0], 'na_rpb': g['na_rpb'],
        'lru_conv_w': jnp.stack([g[f'lru_cw{j}'][:, 0] for j in range(LRU_CONV)], axis=1), 'lru_conv_b': g['lru_cb'][:, 0],
        'lru_w_a': _block_diag_grad(jnp.stack([g['lru_wa0'], g['lru_wa1']], axis=1)),
        'lru_w_x': _block_diag_grad(jnp.stack([g['lru_wx0'], g['lru_wx1']], axis=1)),
        'lru_b_a': jnp.stack([g['lru_ba0'][:, 0], g['lru_ba1'][:, 0]], axis=1),
        'lru_b_x': jnp.stack([g['lru_bx0'][:, 0], g['lru_bx1'][:, 0]], axis=1),
        'lru_lambda': jnp.stack([g['lru_lam0'][:, 0], g['lru_lam1'][:, 0]], axis=1),
    }
```

```python
import numpy as np
import jax
import jax.numpy as jnp
from jax import lax
from jax.experimental import pallas as pl
from jax.experimental.pallas import tpu as pltpu

F32, BF16 = jnp.float32, jnp.bfloat16
HIGHEST = lax.Precision.HIGHEST
MESH = pl.DeviceIdType.MESH

HEAD_DIM = 64
GROUP_HEADS = 4
GROUP_WIDTH = GROUP_HEADS * HEAD_DIM
GLA_RANK = 16
GLA_TAU = 16.0
GLA_CHUNK = 64
GRID_W = 64
NA_ROWS = 8
NA_COLS = 16
LRU_CONV = 4
LRU_CONV_LEFT = 2
LRU_C = 8.0
DIL_PAIRS = ((128, 1), (512, 4), (2048, 16))
ROPE_THETA = 10000.0
EPS = 1e-6
ADAM_LR, ADAM_B1, ADAM_B2, ADAM_EPS, ADAM_WD, ADAM_STEP = 0.001, 0.9, 0.999, 1e-08, 0.01, 10
NEG = -1e30

LANE = 128
VMEM_LIMIT = 56 * 1024 * 1024
ROW_BUDGET = 16 * 1024 * 1024
DMA_PIECES = 8

P_QA, P_KA, P_VA, P_GA, P_QB, P_KB, P_VB, P_XC, P_GC, P_QD, P_KD, P_VD = range(12)
P_Z = 12 * GROUP_WIDTH
D_IN = 12 * GROUP_WIDTH + 2 * GLA_RANK
D_INP = 12 * GROUP_WIDTH + LANE

WEIGHTS = ['mix_norm_pre', 'mix_norm_post', 'w_in', 'gla_w_gate', 'gla_b_gate', 'gla_norm', 'na_rpb',
           'lru_conv_w', 'lru_conv_b', 'lru_w_a', 'lru_b_a', 'lru_w_x', 'lru_b_x', 'lru_lambda', 'w_out',
           'ffn_norm_pre', 'ffn_norm_post', 'ffn_w_in', 'ffn_w_out']
BIG = ('w_in', 'w_out', 'ffn_w_in', 'ffn_w_out')
SMALL_SHARDED = {'gla_w_gate': 3, 'gla_b_gate': 2, 'lru_conv_w': 2, 'lru_b_a': 2, 'lru_b_x': 2, 'lru_lambda': 2}
HEADS = [slice(h * HEAD_DIM, (h + 1) * HEAD_DIM) for h in range(GROUP_HEADS)]


def _cparams(sem=None):
    return pltpu.CompilerParams(dimension_semantics=sem, vmem_limit_bytes=VMEM_LIMIT)


def _tile(dim, target, mult=LANE):
    best = None
    for t in range(mult, min(dim, target) + 1, mult):
        if dim % t == 0:
            best = t
    return best or dim


class Rows:
    def __init__(self, a, w=None, cb=0, lead=None, shifts=None):
        self.a, self.cb, self.lead, self.shifts = a, cb, lead, shifts
        self.w = a.shape[-1] if w is None else w
        self.nrows = a.shape[-2]

    def spec(self, tm, ncol=1, off=0):
        w = self.w // ncol
        last = self.nrows // tm - 1
        row = (lambda i: i) if off == 0 else (lambda i: jnp.clip(i + off, 0, last))
        if self.lead is None:
            return pl.BlockSpec((tm, w), lambda i, j, cb=self.cb: (row(i), cb * ncol + j))
        return pl.BlockSpec((None, tm, w), lambda i, j, cb=self.cb, k=self.lead: (k, row(i), cb * ncol + j))

    def nbytes(self):
        return self.w * self.a.dtype.itemsize * (1 if self.shifts is None else 3)


def _as_rows(rs):
    return [r if isinstance(r, Rows) else Rows(r) for r in rs]


def _shift_tile(prev, cur, nxt, k, t, nt):
    if k == 0:
        return cur
    tm = cur.shape[0]
    row = lax.broadcasted_iota(jnp.int32, cur.shape, 0)
    if k > 0:
        edge = jnp.where(t > 0, pltpu.roll(prev, k, 0), 0.0)
        return jnp.where(row < k, edge, pltpu.roll(cur, k, 0))
    edge = jnp.where(t < nt - 1, pltpu.roll(nxt, tm + k, 0), 0.0)
    return jnp.where(row >= tm + k, edge, pltpu.roll(cur, tm + k, 0))


def _operands(rows, tm, ncol):
    specs, arrs = [], []
    for r in rows:
        if r.shifts is None:
            specs.append(r.spec(tm, ncol))
            arrs.append(r.a)
        else:
            assert ncol == 1
            specs += [r.spec(tm, 1, off) for off in (-1, 0, 1)]
            arrs += [r.a] * 3

    def load(refs):
        vals, k = [], 0
        t, nt = pl.program_id(0), rows[0].nrows // tm
        for r in rows:
            if r.shifts is None:
                vals.append(refs[k][...])
                k += 1
            else:
                prev, cur, nxt = refs[k][...], refs[k + 1][...], refs[k + 2][...]
                vals += [_shift_tile(prev, cur, nxt, s, t, nt) for s in r.shifts]
                k += 3
        return vals

    return specs, arrs, load


def _expand(rows):
    return [r for r in rows for _ in (r.shifts or [0])]


def _pick_tm(nrows, row_bytes, scale, cap=512):
    tm = cap
    while tm > 16 and (tm * row_bytes * scale > ROW_BUDGET or nrows % tm):
        tm //= 2
    assert nrows % tm == 0, (nrows, tm)
    return tm


def _full_spec(a):
    nd = a.ndim
    return pl.BlockSpec(a.shape, lambda i, j, nd=nd: (0,) * nd)


def _rowwise(name, fn, rows, params, outs, acc_outs=(), ncol=1, tm_cap=512):
    rows = _as_rows(rows)
    nrows = rows[0].nrows
    assert ncol == 1 or not (acc_outs or params)
    tm = _pick_tm(nrows, (sum(r.nbytes() for r in rows) + sum(w * jnp.dtype(d).itemsize for w, d in outs)) // ncol, 2, tm_cap)
    specs, arrs, load = _operands(rows, tm, ncol)
    n_r, n_p, n_o = len(specs), len(params), len(outs)

    def body(*refs):
        vals = load(refs[:n_r]) + [r[...] for r in refs[n_r:n_r + n_p]]
        res = fn(*vals)
        res = res if isinstance(res, (tuple, list)) else (res,)
        orefs = refs[n_r + n_p:]
        for o, v in zip(orefs[:n_o], res[:n_o]):
            o[...] = v.astype(o.dtype)
        for o, v in zip(orefs[n_o:], res[n_o:]):
            @pl.when(pl.program_id(0) == 0)
            def _(o=o):
                o[...] = jnp.zeros_like(o)
            o[...] += v

    out_shape = [jax.ShapeDtypeStruct((nrows, w), d) for w, d in outs] + [jax.ShapeDtypeStruct(s, F32) for s in acc_outs]
    out_specs = [pl.BlockSpec((tm, w // ncol), lambda i, j: (i, j)) for w, _ in outs] + \
                [pl.BlockSpec(s, lambda i, j, nd=len(s): (0,) * nd) for s in acc_outs]
    return pl.pallas_call(
        body, name=name, grid=(nrows // tm, ncol),
        in_specs=specs + [_full_spec(p) for p in params],
        out_specs=out_specs, out_shape=out_shape,
        compiler_params=_cparams(("arbitrary", "arbitrary") if acc_outs else ("parallel", "parallel")),
    )(*arrs, *params)


def _rowwise_bwd(name, fn, rows, params, ct_rows, ct_fn, row_grads, param_grads, row_grad_add=None, ncol=1):
    rows, ct_rows = _as_rows(rows), _as_rows(ct_rows)
    nrows = rows[0].nrows
    n_rg = sum(d is not None for d in row_grads)
    adds = _as_rows([a for a in (row_grad_add or []) if a is not None])
    add_at = [k for k, a in enumerate(row_grad_add or []) if a is not None]
    assert ncol == 1 or not (params or adds)
    seen = _expand(rows)
    gbytes = sum(r.w * jnp.dtype(d).itemsize for r, d in zip(seen, row_grads) if d is not None)
    tm = _pick_tm(nrows, (sum(r.nbytes() for r in rows + ct_rows + adds) + gbytes) // ncol, 4)
    r_specs, r_arrs, r_load = _operands(rows, tm, ncol)
    c_specs, c_arrs, c_load = _operands(ct_rows, tm, ncol)
    a_specs, a_arrs, a_load = _operands(adds, tm, ncol)
    n_r, n_p, n_c, n_a = len(r_specs), len(params), len(c_specs), len(a_specs)
    diff = [k for k, d in enumerate(row_grads) if d is not None] + [len(seen) + k for k, g in enumerate(param_grads) if g]

    def body(*refs):
        vals = r_load(refs[:n_r]) + [r[...] for r in refs[n_r:n_r + n_p]]
        cts_in = c_load(refs[n_r + n_p:n_r + n_p + n_c])
        add_in = a_load(refs[n_r + n_p + n_c:n_r + n_p + n_c + n_a]) if n_a else []
        orefs = refs[n_r + n_p + n_c + n_a:]

        def f(*dv):
            full = list(vals)
            for k, v in zip(diff, dv):
                full[k] = v
            res = fn(*full)
            return tuple(res) if isinstance(res, (tuple, list)) else (res,)

        outs, vjp = jax.vjp(f, *[vals[k].astype(F32) for k in diff])
        cts = ct_fn(*cts_in)
        cts = cts if isinstance(cts, (tuple, list)) else (cts,)
        grads = list(vjp(tuple(c.astype(o.dtype) for c, o in zip(cts, outs))))
        for k, a in zip(add_at, add_in):
            grads[k] = grads[k] + a.astype(F32)
        for o, g in zip(orefs[:n_rg], grads[:n_rg]):
            o[...] = g.astype(o.dtype)
        for o, g in zip(orefs[n_rg:], grads[n_rg:]):
            @pl.when(pl.program_id(0) == 0)
            def _(o=o):
                o[...] = jnp.zeros_like(o)
            o[...] += g.astype(F32)

    out_shape = [jax.ShapeDtypeStruct((nrows, r.w), d) for r, d in zip(seen, row_grads) if d is not None] + \
                [jax.ShapeDtypeStruct(p.shape, F32) for p, g in zip(params, param_grads) if g]
    out_specs = [pl.BlockSpec((tm, r.w // ncol), lambda i, j: (i, j)) for r, d in zip(seen, row_grads) if d is not None] + \
                [_full_spec(p) for p, g in zip(params, param_grads) if g]
    return pl.pallas_call(
        body, name=name, grid=(nrows // tm, ncol),
        in_specs=r_specs + [_full_spec(p) for p in params] + c_specs + a_specs,
        out_specs=out_specs, out_shape=out_shape,
        compiler_params=_cparams(("arbitrary", "arbitrary")),
    )(*r_arrs, *params, *c_arrs, *a_arrs)


def _assemble(name, groups, dtype):
    sizes = [len(g) for g in groups]
    flat = [a for g in groups for a in g]

    def fn(*tiles):
        out, k = [], 0
        for s in sizes:
            acc = tiles[k].astype(F32)
            for t in tiles[k + 1:k + s]:
                acc = acc + t.astype(F32)
            out.append(acc.astype(dtype))
            k += s
        return out[0] if len(out) == 1 else jnp.concatenate(out, axis=1)

    width = sum(g[0].shape[-1] if not isinstance(g[0], Rows) else g[0].w for g in groups)
    return _rowwise(name, fn, flat, [], [(width, dtype)])[0]


def _matmul(name, a, b, mode, out_dtype, acc_in=None):
    if mode == 'nn':
        (M, K), N = a.shape, b.shape[1]
    elif mode == 'nt':
        (M, K), N = a.shape, b.shape[0]
    else:
        (K, M), N = a.shape, b.shape[1]
    tm, tn, tk = _tile(M, 1536), _tile(N, 1536), _tile(K, 2048 if mode == 'tn' else 3328)
    nk = K // tk
    dn = {'nn': NN, 'nt': NT, 'tn': TN}[mode]
    has_acc = acc_in is not None

    def body(*refs):
        a_ref, b_ref = refs[:2]
        part = lax.dot_general(a_ref[...].astype(BF16), b_ref[...].astype(BF16), dn, preferred_element_type=F32)
        if nk == 1:
            o_ref = refs[-1]
            o_ref[...] = ((refs[2][...] + part) if has_acc else part).astype(o_ref.dtype)
            return
        o_ref, acc = refs[-2:]

        @pl.when(pl.program_id(2) == 0)
        def _():
            acc[...] = refs[2][...] if has_acc else jnp.zeros_like(acc)
        acc[...] += part

        @pl.when(pl.program_id(2) == nk - 1)
        def _():
            o_ref[...] = acc[...].astype(o_ref.dtype)

    a_spec = pl.BlockSpec((tk, tm), lambda i, j, k: (k, i)) if mode == 'tn' else pl.BlockSpec((tm, tk), lambda i, j, k: (i, k))
    b_spec = pl.BlockSpec((tn, tk), lambda i, j, k: (j, k)) if mode == 'nt' else pl.BlockSpec((tk, tn), lambda i, j, k: (k, j))
    o_spec = pl.BlockSpec((tm, tn), lambda i, j, k: (i, j))
    return pl.pallas_call(
        body, name=name, grid=(M // tm, N // tn, nk),
        in_specs=[a_spec, b_spec] + ([o_spec] if has_acc else []), out_specs=o_spec,
        out_shape=jax.ShapeDtypeStruct((M, N), out_dtype),
        scratch_shapes=[] if nk == 1 else [pltpu.VMEM((tm, tn), F32)],
        compiler_params=_cparams(("parallel", "parallel", "arbitrary")),
    )(a, b, *([acc_in] if has_acc else []))


def _ffn_up(name, h, wg, wu):
    (M, K), N = h.shape, wg.shape[1]
    tm, tn = _tile(M, 512), _tile(N, 1536)

    def body(h_ref, g_ref, u_ref, gate_ref, up_ref, act_ref):
        a = h_ref[...].astype(BF16)
        gate = _bdot(a, g_ref[...]).astype(BF16)
        up = _bdot(a, u_ref[...]).astype(BF16)
        gate_ref[...], up_ref[...] = gate, up
        act_ref[...] = _swiglu_fn(gate, up).astype(BF16)

    w_spec = pl.BlockSpec((K, tn), lambda i, j: (0, j))
    o_spec = pl.BlockSpec((tm, tn), lambda i, j: (i, j))
    return pl.pallas_call(
        body, name=name, grid=(M // tm, N // tn),
        in_specs=[pl.BlockSpec((tm, K), lambda i, j: (i, 0)), w_spec, w_spec], out_specs=[o_spec] * 3,
        out_shape=[jax.ShapeDtypeStruct((M, N), BF16)] * 3,
        compiler_params=_cparams(("parallel", "parallel")),
    )(h, wg, wu)


def _ffn_down_bwd(name, df, w_out, gate, up):
    (M, K), N = df.shape, w_out.shape[0]
    tm, tn = _tile(M, 512), _tile(N, 1536)

    def body(d_ref, w_ref, gate_ref, up_ref, dg_ref, du_ref):
        dact = _bdot(d_ref[...], w_ref[...], NT)
        dg, du = _swiglu_bwd((gate_ref[...], up_ref[...]), dact)
        dg_ref[...], du_ref[...] = dg, du

    o_spec = pl.BlockSpec((tm, tn), lambda i, j: (i, j))
    return pl.pallas_call(
        body, name=name, grid=(M // tm, N // tn),
        in_specs=[pl.BlockSpec((tm, K), lambda i, j: (i, 0)), pl.BlockSpec((tn, K), lambda i, j: (j, 0)), o_spec, o_spec],
        out_specs=[o_spec] * 2, out_shape=[jax.ShapeDtypeStruct((M, N), BF16)] * 2,
        compiler_params=_cparams(("parallel", "parallel")),
    )(df, w_out, gate, up)


def _small_dot(name, a, b, mode):
    dn = {'nn': (((1,), (0,)), ((), ())), 'nt': (((1,), (1,)), ((), ()))}[mode]
    M = a.shape[0]
    N = b.shape[1] if mode == 'nn' else b.shape[0]

    def body(a_ref, b_ref, o_ref):
        o_ref[...] = lax.dot_general(a_ref[...], b_ref[...], dn, precision=HIGHEST, preferred_element_type=F32)

    return pl.pallas_call(body, name=name, out_shape=jax.ShapeDtypeStruct((M, N), F32),
                          compiler_params=pltpu.CompilerParams(vmem_limit_bytes=VMEM_LIMIT))(a, b)


NN, NT, TN = (((1,), (0,)), ((), ())), (((1,), (1,)), ((), ())), (((0,), (0,)), ((), ()))


def _bdot(a, b, dn=NN):
    return lax.dot_general(a.astype(BF16), b.astype(BF16), dn, preferred_element_type=F32)


def _sigmoid(x):
    return 0.5 * jnp.tanh(0.5 * x) + 0.5


def _silu(x):
    return x * _sigmoid(x)


def _softplus(x):
    return jnp.maximum(x, 0.0) + jnp.log(1.0 + jnp.exp(-jnp.abs(x)))


def _gelu(x):
    return 0.5 * x * (1.0 + jnp.tanh(0.7978845608028654 * (x + 0.044715 * (x * x * x))))


def _rms(x, g):
    return x * lax.rsqrt(jnp.mean(x * x, axis=-1, keepdims=True) + EPS) * g


def _prenorm_fn(x, g):
    return _rms(x, g)


def _postnorm_fn(x, y, g):
    return x + _rms(y, g)


@jax.custom_vjp
def _swiglu_fn(gate, up):
    return _silu(gate.astype(F32)) * up.astype(F32)


def _swiglu_bwd(res, ct):
    g, u = res[0].astype(F32), res[1].astype(F32)
    s = _sigmoid(g)
    gs = g * s
    return (ct * u * (s + gs - gs * s)).astype(res[0].dtype), (ct * gs).astype(res[1].dtype)


_swiglu_fn.defvjp(lambda gate, up: (_swiglu_fn(gate, up), (gate, up)), _swiglu_bwd)


def _gla_pre_fn(z, wg, bg):
    logit = _bdot(z, wg) + bg
    return -_softplus(-logit) * (1.0 / GLA_TAU)


def _seg_mean(x, seg):
    return lax.dot_general(x, seg, NN, precision=HIGHEST, preferred_element_type=F32)


def _gla_post_fn(of, ob, g, norm, seg):
    o = of + ob
    o = o * lax.rsqrt(_seg_mean(o * o, seg) + EPS) * norm
    return o * _silu(g)


@jax.custom_vjp
def _swap_halves(x):
    n = x.shape[-1]
    lane = lax.broadcasted_iota(jnp.int32, x.shape, x.ndim - 1)
    lo = (lane & (HEAD_DIM - 1)) < HEAD_DIM // 2
    return jnp.where(lo, pltpu.roll(x, n - HEAD_DIM // 2, x.ndim - 1), pltpu.roll(x, HEAD_DIM // 2, x.ndim - 1))


_swap_halves.defvjp(lambda x: (_swap_halves(x), None), lambda _, g: (_swap_halves(g),))


def _rope_fn(q, k, cs, sn):
    return q * cs + _swap_halves(q) * sn, k * cs + _swap_halves(k) * sn


def _dil_comb_fn(o1, o2, o3, l1, l2, l3):
    m = jnp.maximum(jnp.maximum(l1, l2), l3)
    e1, e2, e3 = jnp.exp(l1 - m), jnp.exp(l2 - m), jnp.exp(l3 - m)
    return (e1 * o1 + e2 * o2 + e3 * o3) / (e1 + e2 + e3)


LRU_PARAMS = ['lru_cw0', 'lru_cw1', 'lru_cw2', 'lru_cw3', 'lru_cb', 'lru_wa0', 'lru_wa1', 'lru_ba0', 'lru_ba1',
              'lru_wx0', 'lru_wx1', 'lru_bx0', 'lru_bx1', 'lru_lam0', 'lru_lam1']


def _lru_pre_fn(x0, x1, x2, x3, cw0, cw1, cw2, cw3, cb, wa0, wa1, ba0, ba1, wx0, wx1, bx0, bx1, lam0, lam1):
    xc = cb + x0 * cw0 + x1 * cw1 + x2 * cw2 + x3 * cw3
    outs = []
    for wa, ba, wx, bx, lam in ((wa0, ba0, wx0, bx0, lam0), (wa1, ba1, wx1, bx1, lam1)):
        r = _sigmoid(_bdot(xc, wa) + ba)
        i = _sigmoid(_bdot(xc, wx) + bx)
        log_a = -LRU_C * r * _softplus(-lam)
        a = jnp.exp(log_a)
        u = jnp.sqrt(-jnp.tanh(log_a) * (a * a + 1.0)) * (i * xc)
        outs += [a, u]
    return outs[0], outs[2], outs[1], outs[3]


def _lru_post_fn(hf, hb, gate):
    return (hf + hb) * _gelu(gate)


def _mix_post_fn(of, ob, ga, yb, hf, hb, gc, o1, o2, o3, l1, l2, l3, norm, seg):
    ya = _gla_post_fn(of, ob, ga, norm, seg)
    yc = _lru_post_fn(hf, hb, gc)
    yd = _dil_comb_fn(o1, o2, o3, l1, l2, l3)
    return jnp.concatenate([ya.astype(BF16), yb.astype(BF16), yc.astype(BF16), yd.astype(BF16)], axis=1)


def _attn_heads(qs, kws, vws, biases):
    ss = [_bdot(q, kw, NT) * (HEAD_DIM ** -0.5) + b for q, kw, b in zip(qs, kws, biases)]
    ms = [lax.stop_gradient(jnp.max(s, axis=-1, keepdims=True)) for s in ss]
    es = [jnp.exp(s - m) for s, m in zip(ss, ms)]
    dens = [jnp.sum(e, axis=-1, keepdims=True) for e in es]
    ps = [e * (1.0 / d) for e, d in zip(es, dens)]
    os_ = [_bdot(p_, vw) for p_, vw in zip(ps, vws)]
    return os_, [m + jnp.log(d) for m, d in zip(ms, dens)]


def _cumsum_rows(x, rev):
    n = x.shape[0]
    row = lax.broadcasted_iota(jnp.int32, x.shape, 0)
    s = 1
    while s < n:
        if rev:
            x = x + jnp.where(row < n - s, pltpu.roll(x, n - s, 0), 0.0)
        else:
            x = x + jnp.where(row >= s, pltpu.roll(x, s, 0), 0.0)
        s *= 2
    return x


def _gla_chunks(qs, ks, vs, bs, sts, revs):
    C = qs[0].shape[0]
    ti = lax.broadcasted_iota(jnp.int32, (C, C), 0)
    si = lax.broadcasted_iota(jnp.int32, (C, C), 1)
    row = lax.broadcasted_iota(jnp.int32, (C, 1), 0)
    incl = {False: si <= ti, True: si >= ti}
    last = {False: row == C - 1, True: row == 0}
    mid = {False: row == C // 2 - 1, True: row == C // 2}
    bls = [jnp.sum(jnp.where(last[r], b, 0.0), axis=0, keepdims=True) for b, r in zip(bs, revs)]
    bms = [jnp.sum(jnp.where(mid[r], b, 0.0), axis=0, keepdims=True) for b, r in zip(bs, revs)]
    qss = [q * (HEAD_DIM ** -0.5) for q in qs]
    qi = [q * jnp.exp(b - bm) for q, b, bm in zip(qss, bs, bms)]
    ki = [k * jnp.exp(bm - b) for k, b, bm in zip(ks, bs, bms)]
    atts = [jnp.where(incl[r], _bdot(a, b, NT), 0.0) for a, b, r in zip(qi, ki, revs)]
    qe = [q * jnp.exp(b) for q, b in zip(qss, bs)]
    kl = [k * jnp.exp(bl - b) for k, b, bl in zip(ks, bs, bls)]
    o1 = [_bdot(a, v) for a, v in zip(atts, vs)]
    o2 = [_bdot(q, st, NT) for q, st in zip(qe, sts)]
    kvs = [_bdot(v, k, TN) for v, k in zip(vs, kl)]
    return [a + b for a, b in zip(o1, o2)], [st * jnp.exp(bl) + kv for st, bl, kv in zip(sts, bls, kvs)]


def _gla_specs(n, blocks, first):
    C = GLA_CHUNK
    at = (lambda i: i) if first else (lambda i: n - 1 - i)
    return [pl.BlockSpec((C, GROUP_WIDTH), lambda i, b=b: (at(i), b)) for b in blocks], at


def _gla_scan_fwd(p, la):
    L = p.shape[0]
    C, H, dh = GLA_CHUNK, GROUP_HEADS, HEAD_DIM
    n = L // C
    f_specs, f_at = _gla_specs(n, (P_QA, P_KA, P_VA), True)
    b_specs, b_at = _gla_specs(n, (P_QA, P_KA, P_VA), False)
    tile = lambda at, blk=0: pl.BlockSpec((C, GROUP_WIDTH), lambda i: (at(i), blk))
    st_spec = lambda at: pl.BlockSpec((None, H, dh, dh), lambda i: (at(i), 0, 0, 0))

    def body(qf, kf, vf, lf, qb, kb, vb, lb, of_ref, ob_ref, sf_ref, sb_ref, stf, stb):
        @pl.when(pl.program_id(0) == 0)
        def _():
            stf[...] = jnp.zeros_like(stf)
            stb[...] = jnp.zeros_like(stb)
        sf_ref[...] = stf[...]
        sb_ref[...] = stb[...]
        chains = [(t, h, sl) for t in ((qf, kf, vf, _cumsum_rows(lf[...], False), of_ref, stf, False),
                                       (qb, kb, vb, _cumsum_rows(lb[...], True), ob_ref, stb, True))
                  for h, sl in enumerate(HEADS)]
        os_, sts = _gla_chunks(*[[t[j][:, sl] for t, h, sl in chains] for j in range(4)],
                               [t[5][h] for t, h, sl in chains], [t[6] for t, h, sl in chains])
        for (t, h, sl), o, st_new in zip(chains, os_, sts):
            t[4][:, sl] = o
            t[5][h] = st_new

    return pl.pallas_call(
        body, name="gla_scan", grid=(n,),
        in_specs=f_specs + [tile(f_at, 0)] + b_specs + [tile(b_at, 1)],
        out_specs=[tile(f_at), tile(b_at), st_spec(f_at), st_spec(b_at)],
        out_shape=[jax.ShapeDtypeStruct((L, GROUP_WIDTH), F32)] * 2 + [jax.ShapeDtypeStruct((n, H, dh, dh), F32)] * 2,
        scratch_shapes=[pltpu.VMEM((H, dh, dh), F32)] * 2,
        compiler_params=_cparams(("arbitrary",)),
    )(p, p, p, la, p, p, p, la)


def _gla_scan_bwd(p, la, sf, sb, do):
    L = p.shape[0]
    C, H, dh = GLA_CHUNK, GROUP_HEADS, HEAD_DIM
    n = L // C
    f_specs, f_at = _gla_specs(n, (P_QA, P_KA, P_VA), False)
    b_specs, b_at = _gla_specs(n, (P_QA, P_KA, P_VA), True)
    tile = lambda at, blk=0: pl.BlockSpec((C, GROUP_WIDTH), lambda i: (at(i), blk))
    st_spec = lambda at: pl.BlockSpec((None, H, dh, dh), lambda i: (at(i), 0, 0, 0))

    def body(qf, kf, vf, lf, spf, dof, qb, kb, vb, lb, spb, dob, *rest):
        outs_f, outs_b, (dstf, dstb) = rest[0:4], rest[4:8], rest[8:10]

        @pl.when(pl.program_id(0) == 0)
        def _():
            dstf[...] = jnp.zeros_like(dstf)
            dstb[...] = jnp.zeros_like(dstb)
        chains = [(t, h, sl) for t in ((qf, kf, vf, _cumsum_rows(lf[...], False), spf, dof, outs_f, dstf, False),
                                       (qb, kb, vb, _cumsum_rows(lb[...], True), spb, dob, outs_b, dstb, True))
                  for h, sl in enumerate(HEADS)]
        nc = len(chains)
        revs = [t[8] for t, h, sl in chains]
        flat = [t[j][:, sl] for j in range(4) for t, h, sl in chains] + [t[4][h] for t, h, sl in chains]

        def f(*a):
            os_, sts = _gla_chunks(*[list(a[j * nc:(j + 1) * nc]) for j in range(5)], revs)
            return tuple(os_) + tuple(sts)

        _, vjp = jax.vjp(f, *flat)
        grads = vjp(tuple(t[5][:, sl] for t, h, sl in chains) + tuple(t[7][h] for t, h, sl in chains))
        for c_, (t, h, sl) in enumerate(chains):
            for j in range(4):
                t[6][j][:, sl] = grads[j * nc + c_].astype(t[6][j].dtype)
            t[7][h] = grads[4 * nc + c_]
        for outs, rev in ((outs_f, False), (outs_b, True)):
            outs[3][...] = _cumsum_rows(outs[3][...], not rev)

    return pl.pallas_call(
        body, name="gla_scan_b", grid=(n,),
        in_specs=f_specs + [tile(f_at, 0), st_spec(f_at), tile(f_at)] + b_specs + [tile(b_at, 1), st_spec(b_at), tile(b_at)],
        out_specs=[tile(f_at)] * 4 + [tile(b_at)] * 4,
        out_shape=[jax.ShapeDtypeStruct((L, GROUP_WIDTH), d) for d in (BF16, BF16, BF16, F32)] * 2,
        scratch_shapes=[pltpu.VMEM((H, dh, dh), F32)] * 2,
        compiler_params=_cparams(("arbitrary",)),
    )(p, p, p, la, sf, do, p, p, p, la, sb, do)


NA_W = NA_ROWS * GRID_W
NA_BW = (2 * NA_ROWS - 1) * GRID_W


def _na_start(i, rows):
    return jnp.clip(i - NA_ROWS // 2, 0, rows - NA_ROWS)


def _na_fwd(p, kb, vb, btab):
    L = p.shape[0]
    rows = L // GRID_W

    def body(q_ref, k_ref, v_ref, b_ref, o_ref):
        r = pl.program_id(0)
        s = _na_start(r, rows)
        start = pl.multiple_of(s * GRID_W, GRID_W)
        os_, _ = _attn_heads([q_ref[:, sl] for sl in HEADS], [k_ref[pl.ds(start, NA_W), sl] for sl in HEADS],
                             [v_ref[pl.ds(start, NA_W), sl] for sl in HEADS],
                             [b_ref[s - r + NA_ROWS - 1, h] for h in range(GROUP_HEADS)])
        for sl, o in zip(HEADS, os_):
            o_ref[:, sl] = o.astype(o_ref.dtype)

    whole = lambda a: pl.BlockSpec(a.shape, lambda i, nd=a.ndim: (0,) * nd)
    return pl.pallas_call(
        body, name="na_attn", grid=(rows,),
        in_specs=[pl.BlockSpec((GRID_W, GROUP_WIDTH), lambda i: (i, P_QB)), whole(kb), whole(vb), whole(btab)],
        out_specs=pl.BlockSpec((GRID_W, GROUP_WIDTH), lambda i: (i, 0)),
        out_shape=jax.ShapeDtypeStruct((L, GROUP_WIDTH), BF16),
        compiler_params=_cparams(("arbitrary",)),
    )(p, kb, vb, btab)


def _na_bwd(p, kb, vb, btab, dycat):
    L = p.shape[0]
    rows = L // GRID_W
    flush = NA_ROWS - 1
    emit = lambda i: jnp.where(i < rows, _na_start(i, rows), i - flush)

    def body(q_ref, k_ref, v_ref, b_ref, do_ref, dq_ref, dk_ref, dv_ref, db_ref, acc_k, acc_v):
        i = pl.program_id(0)

        @pl.when(i == 0)
        def _():
            acc_k[...] = jnp.zeros_like(acc_k)
            acc_v[...] = jnp.zeros_like(acc_v)
            db_ref[...] = jnp.zeros_like(db_ref)

        @pl.when((i > 0) & (emit(i) != emit(i - 1)))
        def _():
            for acc in (acc_k, acc_v):
                moved = acc[GRID_W:NA_W, :]
                acc[0:NA_W - GRID_W, :] = moved
                acc[NA_W - GRID_W:NA_W, :] = jnp.zeros((GRID_W, GROUP_WIDTH), F32)

        @pl.when(i < rows)
        def _():
            s = _na_start(i, rows)
            sv = s - i + NA_ROWS - 1
            start = pl.multiple_of(s * GRID_W, GRID_W)
            H = GROUP_HEADS
            flat = [q_ref[:, sl] for sl in HEADS] + [k_ref[pl.ds(start, NA_W), sl].astype(F32) for sl in HEADS] + \
                   [v_ref[pl.ds(start, NA_W), sl].astype(F32) for sl in HEADS] + [b_ref[sv, h] for h in range(H)]

            def f(*a):
                os_, lses = _attn_heads(a[0:H], a[H:2 * H], a[2 * H:3 * H], a[3 * H:4 * H])
                return tuple(os_) + tuple(lses)

            _, vjp = jax.vjp(f, *flat)
            grads = vjp(tuple(do_ref[:, sl] for sl in HEADS) + (jnp.zeros((GRID_W, 1), F32),) * H)
            for h, sl in enumerate(HEADS):
                dq_ref[:, sl] = grads[h].astype(dq_ref.dtype)
                acc_k[:, sl] += grads[H + h]
                acc_v[:, sl] += grads[2 * H + h]
                db_ref[sv, h] += grads[3 * H + h]

        dk_ref[...] = acc_k[0:GRID_W, :].astype(dk_ref.dtype)
        dv_ref[...] = acc_v[0:GRID_W, :].astype(dv_ref.dtype)

    whole = lambda a: pl.BlockSpec(a.shape, lambda i, nd=a.ndim: (0,) * nd)
    qrow = lambda blk: pl.BlockSpec((GRID_W, GROUP_WIDTH), lambda i: (jnp.minimum(i, rows - 1), blk))
    erow = pl.BlockSpec((GRID_W, GROUP_WIDTH), lambda i: (emit(i), 0))
    return pl.pallas_call(
        body, name="na_attn_b", grid=(rows + flush,),
        in_specs=[qrow(P_QB), whole(kb), whole(vb), whole(btab), qrow(1)],
        out_specs=[qrow(0), erow, erow, whole(btab)],
        out_shape=[jax.ShapeDtypeStruct((L, GROUP_WIDTH), BF16)] * 3 + [jax.ShapeDtypeStruct(btab.shape, F32)],
        scratch_shapes=[pltpu.VMEM((NA_W, GROUP_WIDTH), F32)] * 2,
        compiler_params=_cparams(("arbitrary",)),
    )(p, kb, vb, btab, dycat)


def _na_col_ok():
    qc = np.arange(GRID_W)[:, None]
    kc = (np.arange(NA_W) % GRID_W)[None, :]
    c0 = np.clip(qc - NA_COLS // 2, 0, GRID_W - NA_COLS)
    return (kc >= c0) & (kc < c0 + NA_COLS)


def _rpb_tables():
    c = np.arange(GRID_W)
    dc = np.clip(c[None, :] - c[:, None], -(NA_COLS - 1), NA_COLS - 1) + NA_COLS - 1
    t = np.zeros((2 * NA_COLS - 1, GRID_W, GRID_W), np.float32)
    t[dc, c[:, None], c[None, :]] = 1.0
    return jnp.asarray(t.reshape(2 * NA_COLS - 1, GRID_W * GRID_W))


def _rpb_expand(rpb, tab):
    H = rpb.shape[0]
    xt = _small_dot("na_bias", rpb.reshape(H * (2 * NA_ROWS - 1), 2 * NA_COLS - 1), tab, 'nn')
    b15 = xt.reshape(H, 2 * NA_ROWS - 1, GRID_W, GRID_W).transpose(0, 2, 1, 3).reshape(H, GRID_W, NA_BW)
    ok = jnp.asarray(_na_col_ok())
    return jnp.stack([jnp.where(ok, b15[:, :, sv * GRID_W:sv * GRID_W + NA_W], NEG) for sv in range(NA_ROWS)])


def _rpb_contract(dbv, tab):
    H = dbv.shape[1]
    db = sum(jnp.pad(dbv[sv], ((0, 0), (0, 0), (sv * GRID_W, NA_BW - NA_W - sv * GRID_W))) for sv in range(NA_ROWS))
    dx = db.reshape(H, GRID_W, 2 * NA_ROWS - 1, GRID_W).transpose(0, 2, 1, 3).reshape(H * (2 * NA_ROWS - 1), GRID_W * GRID_W)
    return _small_dot("na_bias_b", dx, tab, 'nt').reshape(H, 2 * NA_ROWS - 1, 2 * NA_COLS - 1)


def _band_bias(i, tq, w, halo, n):
    a = lax.broadcasted_iota(jnp.int32, (tq, w), 0)
    b = lax.broadcasted_iota(jnp.int32, (tq, w), 1)
    kpos = i * tq - halo + b
    d = b - halo - a
    return jnp.where((d <= halo) & (d >= -halo) & (kpos >= 0) & (kpos < n), 0.0, NEG)


def _band_fwd(name, q, kp, vp, tq, halo):
    G, n, _ = q.shape
    w = tq + 2 * halo

    def body(q_ref, k_ref, v_ref, o_ref, l_ref):
        i = pl.program_id(1)
        start = pl.multiple_of(i * tq, tq)
        bias = _band_bias(i, tq, w, halo, n)
        os_, lses = _attn_heads([q_ref[:, sl] for sl in HEADS], [k_ref[pl.ds(start, w), sl] for sl in HEADS],
                                [v_ref[pl.ds(start, w), sl] for sl in HEADS], [bias] * GROUP_HEADS)
        for sl, o, lse in zip(HEADS, os_, lses):
            o_ref[:, sl] = o.astype(o_ref.dtype)
            l_ref[:, sl] = jnp.broadcast_to(lse, (tq, HEAD_DIM))

    qblk = pl.BlockSpec((None, tq, GROUP_WIDTH), lambda g, i: (g, i, 0))
    kblk = pl.BlockSpec((None, n + 2 * halo, GROUP_WIDTH), lambda g, i: (g, 0, 0))
    return pl.pallas_call(
        body, name=name, grid=(G, n // tq), in_specs=[qblk, kblk, kblk], out_specs=[qblk, qblk],
        out_shape=[jax.ShapeDtypeStruct((G, n, GROUP_WIDTH), d) for d in (BF16, F32)],
        compiler_params=_cparams(("parallel", "arbitrary")),
    )(q, kp, vp)


def _band_bwd(name, q, kp, vp, do, dl, tq, halo):
    G, n, _ = q.shape
    w = tq + 2 * halo
    nq = n // tq

    def body(q_ref, k_ref, v_ref, do_ref, dl_ref, dq_ref, dk_ref, dv_ref, acc_k, acc_v):
        i = pl.program_id(1)

        @pl.when(i == 0)
        def _():
            acc_k[...] = jnp.zeros_like(acc_k)
            acc_v[...] = jnp.zeros_like(acc_v)

        @pl.when(i > 0)
        def _():
            for acc in (acc_k, acc_v):
                moved = acc[tq:w, :]
                acc[0:2 * halo, :] = moved
                acc[2 * halo:w, :] = jnp.zeros((tq, GROUP_WIDTH), F32)

        @pl.when(i < nq)
        def _():
            start = pl.multiple_of(i * tq, tq)
            bias = _band_bias(i, tq, w, halo, n)
            H = GROUP_HEADS
            flat = [q_ref[:, sl].astype(F32) for sl in HEADS] + [k_ref[pl.ds(start, w), sl].astype(F32) for sl in HEADS] + \
                   [v_ref[pl.ds(start, w), sl].astype(F32) for sl in HEADS]

            def f(*a):
                os_, lses = _attn_heads(a[0:H], a[H:2 * H], a[2 * H:3 * H], [bias] * H)
                return tuple(os_) + tuple(lses)

            _, vjp = jax.vjp(f, *flat)
            grads = vjp(tuple(do_ref[:, sl].astype(F32) for sl in HEADS) +
                        tuple(jnp.sum(dl_ref[:, sl], axis=1, keepdims=True) for sl in HEADS))
            for h, sl in enumerate(HEADS):
                dq_ref[:, sl] = grads[h].astype(dq_ref.dtype)
                acc_k[:, sl] += grads[H + h]
                acc_v[:, sl] += grads[2 * H + h]

        dk_ref[...] = acc_k[0:tq, :].astype(dk_ref.dtype)
        dv_ref[...] = acc_v[0:tq, :].astype(dv_ref.dtype)

    qblk = pl.BlockSpec((None, tq, GROUP_WIDTH), lambda g, i: (g, jnp.minimum(i, nq - 1), 0))
    kblk = pl.BlockSpec((None, n + 2 * halo, GROUP_WIDTH), lambda g, i: (g, 0, 0))
    eblk = pl.BlockSpec((None, tq, GROUP_WIDTH), lambda g, i: (g, i, 0))
    return pl.pallas_call(
        body, name=name, grid=(G, nq + 1), in_specs=[qblk, kblk, kblk, qblk, qblk], out_specs=[qblk, eblk, eblk],
        out_shape=[jax.ShapeDtypeStruct((G, n, GROUP_WIDTH), BF16)] + [jax.ShapeDtypeStruct((G, (nq + 1) * tq, GROUP_WIDTH), BF16)] * 2,
        scratch_shapes=[pltpu.VMEM((w, GROUP_WIDTH), F32)] * 2,
        compiler_params=_cparams(("parallel", "arbitrary")),
    )(q, kp, vp, do, dl)


def _lin_scan(name, coef, inp, rev, coef_shift=0):
    L, C = coef.shape
    tt = 256 if L % 256 == 0 else L
    nt = L // tt
    tile = (lambda i: nt - 1 - i) if rev else (lambda i: i)
    tidx = lambda i: (tile(i), 0)

    def body(*refs):
        u_ref, o_ref, carry = refs[-3:]

        @pl.when(pl.program_id(0) == 0)
        def _():
            carry[...] = jnp.zeros_like(carry)
        if coef_shift:
            a = _shift_tile(refs[0][...], refs[1][...], refs[2][...], coef_shift, tile(pl.program_id(0)), nt)
        else:
            a = refs[0][...]
        u = u_ref[...]
        row = lax.broadcasted_iota(jnp.int32, (tt, C), 0)
        s = 1
        while s < tt:
            ok = (row < tt - s) if rev else (row >= s)
            sh = tt - s if rev else s
            u = u + a * jnp.where(ok, pltpu.roll(u, sh, 0), 0.0)
            a = a * jnp.where(ok, pltpu.roll(a, sh, 0), 1.0)
            s *= 2
        out = u + a * carry[...]
        o_ref[...] = out
        carry[...] = out[0:1] if rev else out[tt - 1:tt]

    blk = pl.BlockSpec((tt, C), tidx)
    near = [pl.BlockSpec((tt, C), lambda i, off=off: (jnp.clip(tile(i) + off, 0, nt - 1), 0)) for off in (-1, 0, 1)]
    coef_specs = near if coef_shift else [blk]
    return pl.pallas_call(
        body, name=name, grid=(nt,), in_specs=coef_specs + [blk], out_specs=blk,
        out_shape=jax.ShapeDtypeStruct((L, C), F32), scratch_shapes=[pltpu.VMEM((1, C), F32)],
        compiler_params=_cparams(("arbitrary",)),
    )(*([coef] * len(coef_specs)), inp)


def _pieces(shape):
    n0 = max(d for d in range(1, DMA_PIECES + 1) if shape[0] % d == 0)
    n1 = 1
    if len(shape) >= 3:
        n1 = max(d for d in range(1, DMA_PIECES // n0 + 1) if shape[1] % d == 0)
    s0, s1 = shape[0] // n0, (shape[1] // n1 if len(shape) >= 3 else 0)
    out = []
    for i in range(n0):
        for j in range(n1):
            out.append((pl.ds(i * s0, s0),) + ((pl.ds(j * s1, s1),) if len(shape) >= 3 else ()))
    return out


def _exchange(name, src, axes, gather):
    flips = {'xy': [(1, 0, 0), (0, 1, 0), (1, 1, 0)], 'c': [(0, 0, 1)],
             'xyc': [(fx, fy, fc) for fx in (0, 1) for fy in (0, 1) for fc in (0, 1)][1:]}[axes]
    n = len(flips) + 1
    blk_shape = tuple(src.shape if gather else src.shape[1:])
    pieces = _pieces(blk_shape)

    def number(px, py, pc):
        return {'xy': 2 * px + py, 'c': pc, 'xyc': 4 * px + 2 * py + pc}[axes]

    def body(src_ref, out_ref, send_sems, recv_sems):
        x, y, c = lax.axis_index("x"), lax.axis_index("y"), lax.axis_index("c")
        me = number(x, y, c)
        piece = (lambda k: src_ref) if gather else (lambda k: src_ref.at[k])
        peers = []
        for s, (fx, fy, fc) in enumerate(flips):
            px, py, pc = (x + fx) % 2, (y + fy) % 2, (c + fc) % 2

            def copy(ix, s=s, px=px, py=py, pc=pc):
                part = (lambda r: r) if ix is None else (lambda r: r.at[ix])
                return pltpu.make_async_remote_copy(
                    src_ref=part(piece(number(px, py, pc))), dst_ref=part(out_ref.at[me]),
                    send_sem=send_sems.at[s], recv_sem=recv_sems.at[s],
                    device_id=(px, py, pc), device_id_type=MESH)

            for ix in pieces:
                copy(ix).start()
            peers.append(copy)
        for copy in peers:
            copy(None).wait()

    out = pl.pallas_call(
        body, name=name, out_shape=jax.ShapeDtypeStruct((n,) + blk_shape, src.dtype),
        in_specs=[pl.BlockSpec(memory_space=pl.ANY)], out_specs=pl.BlockSpec(memory_space=pl.ANY),
        scratch_shapes=[pltpu.SemaphoreType.DMA((n - 1,)), pltpu.SemaphoreType.DMA((n - 1,))],
    )(src)
    me = number(lax.axis_index("x"), lax.axis_index("y"), lax.axis_index("c"))
    own = src if gather else lax.dynamic_index_in_dim(src, me, 0, keepdims=False)
    return lax.dynamic_update_index_in_dim(out, own, me, 0)


def _ordered_sum(name, buf, dtype=F32):
    n = buf.shape[0]

    def fn(*t):
        acc = t[0].astype(F32)
        for v in t[1:]:
            acc = acc + v.astype(F32)
        return acc

    return _rowwise(name, fn, [Rows(buf, lead=k) for k in range(n)], [], [(buf.shape[-1], dtype)])[0]


def _reduce_big(name, g):
    mine = _ordered_sum(name + "_sum_c", _exchange(name + "_swap_c", g, 'c', False).reshape(2, -1, g.shape[-1]), BF16)
    mine = mine.reshape(g.shape[1:])
    tot = _ordered_sum(name + "_sum_xy", _exchange(name + "_a2a_xy", mine, 'xy', False))
    return _exchange(name + "_share_c", tot, 'c', True)


def _dilate(t, dil):
    L, C = t.shape
    return t.reshape(L // dil, dil, C).transpose(1, 0, 2)


def _undilate(t):
    dil, n, C = t.shape
    return t.transpose(1, 0, 2).reshape(dil * n, C)


def _pad_rows(t, halo):
    return jnp.pad(t, ((0, 0), (halo, halo), (0, 0)))


def _pcol(p, blk):
    return Rows(p, GROUP_WIDTH, blk)


def _pslice(p, blk):
    return p[:, blk * GROUP_WIDTH:(blk + 1) * GROUP_WIDTH]


def _conv_taps(p):
    return Rows(p, GROUP_WIDTH, P_XC, shifts=[LRU_CONV_LEFT - j for j in range(LRU_CONV)])


def _seg_matrix():
    h = np.arange(GROUP_WIDTH) // HEAD_DIM
    return jnp.asarray((h[:, None] == h[None, :]).astype(np.float32) / HEAD_DIM)


def _rope_tables(L):
    pos = jnp.arange(L, dtype=F32)
    inv_freq = ROPE_THETA ** (-jnp.arange(0, HEAD_DIM, 2, dtype=F32) / HEAD_DIM)
    ang = pos[:, None] * inv_freq[None, :]
    cos, sin = jnp.cos(ang), jnp.sin(ang)
    cs = jnp.tile(jnp.concatenate([cos, cos], axis=1), (1, GROUP_HEADS))
    sn = jnp.tile(jnp.concatenate([-sin, sin], axis=1), (1, GROUP_HEADS))
    return cs, sn


def _dil_branches(L):
    out = []
    for window, dil in DIL_PAIRS:
        radius = window // (2 * dil)
        n = L // dil
        out.append((dil, radius, min(256, n)))
    return out


def _dil_operands(qr, kr, p, dil, radius):
    q = _dilate(qr.astype(BF16), dil)
    k = _pad_rows(_dilate(kr.astype(BF16), dil), radius)
    v = _pad_rows(_dilate(_pslice(p, P_VD).astype(BF16), dil), radius)
    return q, k, v


def _layer_fwd(x, w, c):
    L, D = x.shape
    sv = {'x_in': x}
    h = _rowwise("mix_prenorm", _prenorm_fn, [x], [w['mix_norm_pre']], [(D, BF16)])[0]
    p = _matmul("mix_proj", h, w['w_in'], 'nn', F32)
    sv.update(p=p, h=h)

    la = _rowwise("gla_pre", _gla_pre_fn, [Rows(p, LANE, P_Z // LANE)], [w['gla_wg'], w['gla_bg']], [(2 * GROUP_WIDTH, F32)])[0]
    of, ob, sf, sb = _gla_scan_fwd(p, la)
    sv.update(la=la, sf=sf, sb=sb, of=of, ob=ob)

    yb = _na_fwd(p, _pslice(p, P_KB).astype(BF16), _pslice(p, P_VB).astype(BF16), _rpb_expand(w['na_rpb'], c['rpb_tab']))

    a0, a1, u0, u1 = _rowwise("lru_pre", _lru_pre_fn, [_conv_taps(p)], [w[k] for k in LRU_PARAMS], [(GROUP_WIDTH, F32)] * 4)
    hf = _lin_scan("lru_scan_f", a0, u0, False)
    hb = _lin_scan("lru_scan_b", a1, u1, True)
    sv.update(a0=a0, a1=a1, hf=hf, hb=hb)

    qr, kr = _rowwise("rope", _rope_fn, [_pcol(p, P_QD), _pcol(p, P_KD), c['cos'], c['sin']], [], [(GROUP_WIDTH, F32)] * 2)
    os_, ls_ = [], []
    for dil, radius, tq in _dil_branches(L):
        o, lse = _band_fwd(f"dil_attn{dil}", *_dil_operands(qr, kr, p, dil, radius), tq, radius)
        os_.append(_undilate(o))
        ls_.append(_undilate(lse))
    sv.update(qr=qr, kr=kr, dil_o=os_, dil_l=ls_)

    ycat = _rowwise("mix_post", _mix_post_fn, [of, ob, _pcol(p, P_GA), yb, hf, hb, _pcol(p, P_GC)] + os_ + ls_,
                    [w['gla_norm'], c['seg']], [(4 * GROUP_WIDTH, BF16)], tm_cap=128)[0]
    y = _matmul("mix_out", ycat, w['w_out'], 'nn', F32)
    xm = _rowwise("mix_postnorm", _postnorm_fn, [x, y], [w['mix_norm_post']], [(D, F32)])[0]
    sv.update(ycat=ycat, y=y, x_mid=xm)

    h2 = _rowwise("ffn_prenorm", _prenorm_fn, [xm], [w['ffn_norm_pre']], [(D, BF16)])[0]
    gate, up, act = _ffn_up("ffn_up", h2, w['ffn_wg'], w['ffn_wu'])
    f = _matmul("ffn_out", act, w['ffn_w_out'], 'nn', F32)
    xo = _rowwise("ffn_postnorm", _postnorm_fn, [xm, f], [w['ffn_norm_post']], [(D, F32)])[0]
    sv.update(gate=gate, up=up, act=act, f=f, h2=h2)
    return xo, sv


def _layer_bwd(dx, w, c, sv):
    L, D = dx.shape
    g = {}
    as_f32 = lambda t: (t.astype(F32),)
    df, g['ffn_norm_post'] = _rowwise_bwd("ffn_postnorm_b", lambda y, gn: _rms(y, gn), [sv['f']], [w['ffn_norm_post']],
                                          [dx], as_f32, [BF16], [True])
    dgate, dup = _ffn_down_bwd("ffn_out_bx", df, w['ffn_w_out'], sv['gate'], sv['up'])
    g['ffn_w_out'] = _matmul("ffn_out_bw", sv['act'], df, 'tn', F32)
    dh2 = _matmul("ffn_up_bx", dup, w['ffn_wu'], 'nt', F32, acc_in=_matmul("ffn_gate_bx", dgate, w['ffn_wg'], 'nt', F32))
    xm = sv['x_mid']
    g['ffn_wg'] = _matmul("ffn_gate_bw", sv['h2'], dgate, 'tn', F32)
    g['ffn_wu'] = _matmul("ffn_up_bw", sv['h2'], dup, 'tn', F32)
    dxm, g['ffn_norm_pre'] = _rowwise_bwd("ffn_prenorm_b", _prenorm_fn, [xm], [w['ffn_norm_pre']], [dh2], as_f32, [F32], [True],
                                          row_grad_add=[dx])

    dy, g['mix_norm_post'] = _rowwise_bwd("mix_postnorm_b", lambda y, gn: _rms(y, gn), [sv['y']], [w['mix_norm_post']],
                                          [dxm], as_f32, [BF16], [True])
    dycat = _matmul("mix_out_bx", dy, w['w_out'], 'nt', F32)
    g['w_out'] = _matmul("mix_out_bw", sv['ycat'], dy, 'tn', F32)
    p = sv['p']
    dya, dyb, dyc, dyd = (Rows(dycat, GROUP_WIDTH, k) for k in range(4))

    dof, dga, g['gla_norm'] = _rowwise_bwd("gla_post_b", _gla_post_fn, [sv['of'], sv['ob'], _pcol(p, P_GA)],
                                           [w['gla_norm'], c['seg']], [dya], as_f32, [F32, None, BF16], [True, False])
    la = sv['la']
    dqf, dkf, dvf, dlf, dqb_, dkb_, dvb_, dlb = _gla_scan_bwd(p, la, sv['sf'], sv['sb'], dof)
    dz, g['gla_wg'], g['gla_bg'] = _rowwise_bwd("gla_pre_b", _gla_pre_fn, [Rows(p, LANE, P_Z // LANE)], [w['gla_wg'], w['gla_bg']],
                                                [dlf, dlb], lambda a, b: (jnp.concatenate([a, b], axis=1),), [BF16], [True, True])

    btab = _rpb_expand(w['na_rpb'], c['rpb_tab'])
    dqn, dkn, dvn, dbt = _na_bwd(p, _pslice(p, P_KB).astype(BF16), _pslice(p, P_VB).astype(BF16), btab, dycat)
    g['na_rpb'] = _rpb_contract(dbt, c['rpb_tab'])

    dh, dgc = _rowwise_bwd("lru_post_b", _lru_post_fn, [sv['hf'], sv['hb'], _pcol(p, P_GC)], [], [dyc], as_f32, [F32, None, BF16], [])
    lam0 = _lin_scan("lru_scan_f_b", sv['a0'], dh, True, coef_shift=-1)
    lam1 = _lin_scan("lru_scan_b_b", sv['a1'], dh, False, coef_shift=1)
    res = _rowwise_bwd("lru_pre_b", _lru_pre_fn, [_conv_taps(p)], [w[k] for k in LRU_PARAMS],
                       [lam0, lam1, Rows(sv['hf'], shifts=[1]), Rows(sv['hb'], shifts=[-1])],
                       lambda l0, l1, hfp, hbn: (l0 * hfp, l1 * hbn, l0, l1), [F32] * 4, [True] * len(LRU_PARAMS))
    dxs = res[:4]
    for k, nm in enumerate(LRU_PARAMS):
        g[nm] = res[4 + k]
    dxc = [Rows(dxs[j], shifts=[j - LRU_CONV_LEFT]) for j in range(LRU_CONV)]

    comb = _rowwise_bwd("dil_comb_b", _dil_comb_fn, sv['dil_o'] + sv['dil_l'], [], [dyd], as_f32, [BF16] * 3 + [F32] * 3, [])
    qr, kr = sv['qr'], sv['kr']
    dqs, dks, dvs = [], [], []
    for k, (dil, radius, tq) in enumerate(_dil_branches(L)):
        n = L // dil
        dq_, dk_, dv_ = _band_bwd(f"dil_attn{dil}_b", *_dil_operands(qr, kr, p, dil, radius),
                                  _dilate(comb[k], dil), _dilate(comb[3 + k], dil), tq, radius)
        dqs.append(_undilate(dq_))
        dks.append(_undilate(dk_[:, radius:radius + n]))
        dvs.append(_undilate(dv_[:, radius:radius + n]))
    dqd, dkd = _rowwise_bwd("rope_b", _rope_fn, [_pcol(p, P_QD), _pcol(p, P_KD), c['cos'], c['sin']], [], dqs + dks,
                            lambda *t: (sum(v.astype(F32) for v in t[:3]), sum(v.astype(F32) for v in t[3:])), [BF16, BF16, None, None], [])

    dp = _assemble("mix_dp", [[dqf, dqb_], [dkf, dkb_], [dvf, dvb_], [dga], [dqn], [dkn], [dvn], dxc, [dgc], [dqd], [dkd], dvs, [dz]], BF16)
    dh1 = _matmul("mix_proj_bx", dp, w['w_in'], 'nt', F32)
    x_in = sv['x_in']
    g['w_in'] = _matmul("mix_proj_bw", sv['h'], dp, 'tn', F32)
    dxi, g['mix_norm_pre'] = _rowwise_bwd("mix_prenorm_b", _prenorm_fn, [x_in], [w['mix_norm_pre']], [dh1], as_f32, [F32], [True],
                                          row_grad_add=[dxm])
    return dxi, g


def _loss_fn(y, t):
    e = y - t
    return e * (1.0 / y.shape[1]), jnp.sum(e * e, axis=0, keepdims=True)


def _gather_cols(name, shard, axis):
    half = shard.shape[0] // 2
    mine = lax.dynamic_slice_in_dim(shard, lax.axis_index("c") * half, half, axis=0).astype(BF16)
    both = _exchange(name + "_c", _exchange(name + "_xy", mine, 'xy', True), 'c', True)
    shp = list(shard.shape)
    shp[axis] *= 4
    return jnp.moveaxis(both, 1, axis + 1).reshape(shp)


def _pack(arrs, mult=64 * LANE):
    flat = jnp.concatenate([a.reshape(-1) for a in arrs])
    pad = (-flat.shape[0]) % mult
    return jnp.pad(flat, (0, pad)).reshape(-1, LANE)


def _unpack(buf, shapes):
    flat, out, k = buf.reshape(-1), [], 0
    for s in shapes:
        sz = int(np.prod(s))
        out.append(flat[k:k + sz].reshape(s))
        k += sz
    return out


def _perm_in(w_in):
    pad = jnp.zeros(w_in.shape[:-1] + (D_INP - D_IN,), w_in.dtype)
    return jnp.concatenate([w_in[..., :P_QB * GROUP_WIDTH], w_in[..., P_QB * GROUP_WIDTH + 2 * GLA_RANK:],
                            w_in[..., P_QB * GROUP_WIDTH:P_QB * GROUP_WIDTH + 2 * GLA_RANK], pad], axis=-1)


def _unperm_in(g):
    return jnp.concatenate([g[..., :P_QB * GROUP_WIDTH], g[..., P_Z:P_Z + 2 * GLA_RANK], g[..., P_QB * GROUP_WIDTH:P_Z]], axis=-1)


def _block_diag(wb):
    l = wb.shape[0]
    eye = jnp.eye(GROUP_HEADS, dtype=wb.dtype)
    return jnp.einsum('lehij,hg->lehigj', wb, eye).reshape(l, 2, GROUP_WIDTH, GROUP_WIDTH)


def _block_diag_grad(gw):
    l = gw.shape[0]
    g6 = gw.reshape(l, 2, GROUP_HEADS, HEAD_DIM, GROUP_HEADS, HEAD_DIM)
    return jnp.stack([g6[:, :, h, :, h, :] for h in range(GROUP_HEADS)], axis=2)


def _gate_matrix(wg):
    l = wg.shape[0]
    m = jnp.zeros((l, LANE, 2 * GROUP_WIDTH), wg.dtype)
    for e in range(2):
        m = m.at[:, e * GLA_RANK:(e + 1) * GLA_RANK, e * GROUP_WIDTH:(e + 1) * GROUP_WIDTH].set(wg[:, e])
    return m


def _gate_matrix_grad(gm):
    return jnp.stack([gm[:, e * GLA_RANK:(e + 1) * GLA_RANK, e * GROUP_WIDTH:(e + 1) * GROUP_WIDTH] for e in range(2)], axis=1)


def _adam_fn(w, g, m, v):
    m = ADAM_B1 * m + (1.0 - ADAM_B1) * g
    v = ADAM_B2 * v + (1.0 - ADAM_B2) * (g * g)
    m_hat = m / (1.0 - ADAM_B1 ** ADAM_STEP)
    v_hat = v / (1.0 - ADAM_B2 ** ADAM_STEP)
    return -ADAM_LR * (m_hat / (jnp.sqrt(v_hat) + ADAM_EPS) + ADAM_WD * w), m, v


def _adam(name, w, g, m, v):
    shp = w.shape
    two = lambda t: t.reshape(-1, shp[-1])
    res = _rowwise(name, _adam_fn, [two(w), two(g), two(m), two(v)], [], [(shp[-1], F32)] * 3)
    return [r.reshape(shp) for r in res]


def _local_step(x, target, fw):
    L, D = x.shape
    depth = fw['w_in'].shape[0]
    cs, sn = _rope_tables(L)
    consts = {'seg': _seg_matrix(), 'rpb_tab': _rpb_tables(), 'cos': cs, 'sin': sn}
    layer = lambda l: {k: v[l] for k, v in fw.items()}
    saved = []
    for l in range(depth):
        x, sv = _layer_fwd(x, layer(l), consts)
        saved.append(sv)
    dx, sq = _rowwise("loss", _loss_fn, [x, target], [], [(D, F32)], acc_outs=[(1, D)])
    grads = [None] * depth
    for l in reversed(range(depth)):
        dx, grads[l] = _layer_bwd(dx, layer(l), consts, saved[l])
    return sq, dx, grads


def kernel(x, mix_norm_pre, mix_norm_post, w_in, gla_w_gate, gla_b_gate, gla_norm, na_rpb, lru_conv_w, lru_conv_b, lru_w_a, lru_b_a, lru_w_x, lru_b_x, lru_lambda, w_out, ffn_norm_pre, ffn_norm_post, ffn_w_in, ffn_w_out, loss_target, m_mix_norm_pre, m_mix_norm_post, m_w_in, m_gla_w_gate, m_gla_b_gate, m_gla_norm, m_na_rpb, m_lru_conv_w, m_lru_conv_b, m_lru_w_a, m_lru_b_a, m_lru_w_x, m_lru_b_x, m_lru_lambda, m_w_out, m_ffn_norm_pre, m_ffn_norm_post, m_ffn_w_in, m_ffn_w_out, v_mix_norm_pre, v_mix_norm_post, v_w_in, v_gla_w_gate, v_gla_b_gate, v_gla_norm, v_na_rpb, v_lru_conv_w, v_lru_conv_b, v_lru_w_a, v_lru_b_a, v_lru_w_x, v_lru_b_x, v_lru_lambda, v_w_out, v_ffn_norm_pre, v_ffn_norm_post, v_ffn_w_in, v_ffn_w_out):
    args = (mix_norm_pre, mix_norm_post, w_in, gla_w_gate, gla_b_gate, gla_norm, na_rpb, lru_conv_w, lru_conv_b, lru_w_a, lru_b_a, lru_w_x, lru_b_x, lru_lambda, w_out, ffn_norm_pre, ffn_norm_post, ffn_w_in, ffn_w_out,
            m_mix_norm_pre, m_mix_norm_post, m_w_in, m_gla_w_gate, m_gla_b_gate, m_gla_norm, m_na_rpb, m_lru_conv_w, m_lru_conv_b, m_lru_w_a, m_lru_b_a, m_lru_w_x, m_lru_b_x, m_lru_lambda, m_w_out, m_ffn_norm_pre, m_ffn_norm_post, m_ffn_w_in, m_ffn_w_out,
            v_mix_norm_pre, v_mix_norm_post, v_w_in, v_gla_w_gate, v_gla_b_gate, v_gla_norm, v_na_rpb, v_lru_conv_w, v_lru_conv_b, v_lru_w_a, v_lru_b_a, v_lru_w_x, v_lru_b_x, v_lru_lambda, v_w_out, v_ffn_norm_pre, v_ffn_norm_post, v_ffn_w_in, v_ffn_w_out)
    nw = len(WEIGHTS)
    W = dict(zip(WEIGHTS, args[:nw]))
    M = dict(zip(WEIGHTS, args[nw:2 * nw]))
    V = dict(zip(WEIGHTS, args[2 * nw:]))
    chip = 2 * lax.axis_index("x") + lax.axis_index("y")

    full = dict(W)
    full['w_in'] = _gather_cols("ag_w_in", w_in, 2)
    full['ffn_w_in'] = _gather_cols("ag_ffn_w_in", ffn_w_in, 2)
    full['w_out'] = _gather_cols("ag_w_out", w_out, 1)
    full['ffn_w_out'] = _gather_cols("ag_ffn_w_out", ffn_w_out, 1)
    small = list(SMALL_SHARDED)
    got = _exchange("ag_small", _pack([W[k] for k in small]), 'xy', True)
    for k, parts in zip(small, zip(*[_unpack(got[j], [W[k].shape for k in small]) for j in range(4)])):
        ax = SMALL_SHARDED[k]
        stacked = jnp.moveaxis(jnp.stack(parts), 0, ax)
        shp = list(W[k].shape)
        shp[ax] *= 4
        full[k] = stacked.reshape(shp)

    sq, dx0, g = _local_step(x[0], loss_target[0], _layer_weights(full))
    loss = lax.psum(0.5 * jnp.sum(sq) / x.shape[-1], ("x", "y", "c"))
    gfull = _stored_grads(g)

    grad = {}
    for k, cut in _big_cuts(g).items():
        grad[k] = _reduce_big("rs_" + k, cut).reshape(W[k].shape)
    rest = [k for k in WEIGHTS if k not in BIG]
    allg = _exchange("ar_small", _pack([gfull[k] for k in rest]), 'xyc', True)
    summed = _unpack(_ordered_sum("ar_small_sum", allg), [gfull[k].shape for k in rest])
    for k, s in zip(rest, summed):
        if k in SMALL_SHARDED:
            ax = SMALL_SHARDED[k]
            n = W[k].shape[ax]
            s = lax.dynamic_slice_in_dim(s, chip * n, n, axis=ax)
        grad[k] = s

    delta, new_m, new_v = {}, {}, {}
    for k in BIG:
        delta[k], new_m[k], new_v[k] = _adam("adam_" + k, W[k], grad[k], M[k], V[k])
    shapes = [W[k].shape for k in rest]
    res = _rowwise("adam_small", _adam_fn, [_pack([d[k] for k in rest]) for d in (W, grad, M, V)], [], [(LANE, F32)] * 3)
    for d, r in zip((delta, new_m, new_v), res):
        for k, t in zip(rest, _unpack(r, shapes)):
            d[k] = t

    return (loss, dx0[None], *[grad[k] for k in WEIGHTS], *[delta[k] for k in WEIGHTS],
            *[new_m[k] for k in WEIGHTS], *[new_v[k] for k in WEIGHTS])


def _layer_weights(full):
    depth = full['w_in'].shape[0]
    dff = full['ffn_w_in'].shape[-1] // 2
    row = lambda t: t[:, None, :]
    fw = {
        'mix_norm_pre': row(full['mix_norm_pre']), 'mix_norm_post': row(full['mix_norm_post']),
        'ffn_norm_pre': row(full['ffn_norm_pre']), 'ffn_norm_post': row(full['ffn_norm_post']),
        'w_in': _perm_in(full['w_in']), 'w_out': full['w_out'],
        'ffn_wg': full['ffn_w_in'][..., :dff], 'ffn_wu': full['ffn_w_in'][..., dff:], 'ffn_w_out': full['ffn_w_out'],
        'gla_wg': _gate_matrix(full['gla_w_gate']), 'gla_bg': full['gla_b_gate'].reshape(depth, 1, 2 * GROUP_WIDTH),
        'gla_norm': row(full['gla_norm']), 'na_rpb': full['na_rpb'],
        'lru_cb': row(full['lru_conv_b']),
    }
    wa_bd, wx_bd = _block_diag(full['lru_w_a']), _block_diag(full['lru_w_x'])
    for j in range(LRU_CONV):
        fw[f'lru_cw{j}'] = row(full['lru_conv_w'][:, j])
    for e in range(2):
        fw[f'lru_wa{e}'], fw[f'lru_wx{e}'] = wa_bd[:, e], wx_bd[:, e]
        fw[f'lru_ba{e}'], fw[f'lru_bx{e}'] = row(full['lru_b_a'][:, e]), row(full['lru_b_x'][:, e])
        fw[f'lru_lam{e}'] = row(full['lru_lambda'][:, e])
    return fw


def _orig_cols(gp, lo, hi):
    split, zend = P_QB * GROUP_WIDTH, P_QB * GROUP_WIDTH + 2 * GLA_RANK
    parts = []
    for a, b, at in ((0, split, 0), (split, zend, P_Z), (zend, D_IN, split)):
        s, e = max(lo, a), min(hi, b)
        if s < e:
            parts.append(gp[..., at + s - a:at + e - a])
    return parts[0] if len(parts) == 1 else jnp.concatenate(parts, axis=-1)


def _big_cuts(grads):
    depth = len(grads)
    halves = (range(0, depth // 2), range(depth // 2, depth))
    n_in, n_ff = D_IN // 4, grads[0]['ffn_wg'].shape[1] // 2

    def build(piece):
        return jnp.stack([jnp.stack([jnp.concatenate([piece(l, j) for l in hl], axis=0) for j in range(4)]) for hl in halves])

    def rows_of(key):
        return lambda l, j: jnp.split(grads[l][key], 4, axis=0)[j]

    return {
        'w_in': build(lambda l, j: _orig_cols(grads[l]['w_in'], j * n_in, (j + 1) * n_in)),
        'w_out': build(rows_of('w_out')),
        'ffn_w_in': build(lambda l, j: grads[l]['ffn_wg' if j < 2 else 'ffn_wu'][:, (j % 2) * n_ff:(j % 2 + 1) * n_ff]),
        'ffn_w_out': build(rows_of('ffn_w_out')),
    }


def _stored_grads(grads):
    depth = len(grads)
    g = {k: jnp.stack([gl[k] for gl in grads]) for k in grads[0]}
    return {
        'mix_norm_pre': g['mix_norm_pre'][:, 0], 'mix_norm_post': g['mix_norm_post'][:, 0],
        'ffn_norm_pre': g['ffn_norm_pre'][:, 0], 'ffn_norm_post': g['ffn_norm_post'][:, 0],
        'w_in': _unperm_in(g['w_in']), 'w_out': g['w_out'],
        'ffn_w_in': jnp.concatenate([g['ffn_wg'], g['ffn_wu']], axis=-1), 'ffn_w_out': g['ffn_w_out'],
        'gla_w_gate': _gate_matrix_grad(g['gla_wg']), 'gla_b_gate': g['gla_bg'].reshape(depth, 2, GROUP_WIDTH),
        'gla_norm': g['gla_norm'][:, 0], 'na_rpb': g['na_rpb'],
        'lru_conv_w': jnp.stack([g[f'lru_cw{j}'][:, 0] for j in range(LRU_CONV)], axis=1), 'lru_conv_b': g['lru_cb'][:, 0],
        'lru_w_a': _block_diag_grad(jnp.stack([g['lru_wa0'], g['lru_wa1']], axis=1)),
        'lru_w_x': _block_diag_grad(jnp.stack([g['lru_wx0'], g['lru_wx1']], axis=1)),
        'lru_b_a': jnp.stack([g['lru_ba0'][:, 0], g['lru_ba1'][:, 0]], axis=1),
        'lru_b_x': jnp.stack([g['lru_bx0'][:, 0], g['lru_bx1'][:, 0]], axis=1),
        'lru_lambda': jnp.stack([g['lru_lam0'][:, 0], g['lru_lam1'][:, 0]], axis=1),
    }
```

```python
import numpy as np
import jax
import jax.numpy as jnp
from jax import lax
from jax.experimental import pallas as pl
from jax.experimental.pallas import tpu as pltpu

F32, BF16 = jnp.float32, jnp.bfloat16
HIGHEST = lax.Precision.HIGHEST
MESH = pl.DeviceIdType.MESH

HEAD_DIM = 64
GROUP_HEADS = 4
GROUP_WIDTH = GROUP_HEADS * HEAD_DIM
GLA_RANK = 16
GLA_TAU = 16.0
GLA_CHUNK = 64
GRID_W = 64
NA_ROWS = 8
NA_COLS = 16
LRU_CONV = 4
LRU_CONV_LEFT = 2
LRU_C = 8.0
DIL_PAIRS = ((128, 1), (512, 4), (2048, 16))
ROPE_THETA = 10000.0
EPS = 1e-6
ADAM_LR, ADAM_B1, ADAM_B2, ADAM_EPS, ADAM_WD, ADAM_STEP = 0.001, 0.9, 0.999, 1e-08, 0.01, 10
NEG = -1e30

LANE = 128
VMEM_LIMIT = 56 * 1024 * 1024
ROW_BUDGET = 16 * 1024 * 1024
DMA_PIECES = 8

P_QA, P_KA, P_VA, P_GA, P_QB, P_KB, P_VB, P_XC, P_GC, P_QD, P_KD, P_VD = range(12)
P_Z = 12 * GROUP_WIDTH
D_IN = 12 * GROUP_WIDTH + 2 * GLA_RANK
D_INP = 12 * GROUP_WIDTH + LANE

WEIGHTS = ['mix_norm_pre', 'mix_norm_post', 'w_in', 'gla_w_gate', 'gla_b_gate', 'gla_norm', 'na_rpb',
           'lru_conv_w', 'lru_conv_b', 'lru_w_a', 'lru_b_a', 'lru_w_x', 'lru_b_x', 'lru_lambda', 'w_out',
           'ffn_norm_pre', 'ffn_norm_post', 'ffn_w_in', 'ffn_w_out']
BIG = ('w_in', 'w_out', 'ffn_w_in', 'ffn_w_out')
SMALL_SHARDED = {'gla_w_gate': 3, 'gla_b_gate': 2, 'lru_conv_w': 2, 'lru_b_a': 2, 'lru_b_x': 2, 'lru_lambda': 2}
HEADS = [slice(h * HEAD_DIM, (h + 1) * HEAD_DIM) for h in range(GROUP_HEADS)]


def _cparams(sem=None):
    return pltpu.CompilerParams(dimension_semantics=sem, vmem_limit_bytes=VMEM_LIMIT)


def _tile(dim, target, mult=LANE):
    best = None
    for t in range(mult, min(dim, target) + 1, mult):
        if dim % t == 0:
            best = t
    return best or dim


class Rows:
    def __init__(self, a, w=None, cb=0, lead=None, shifts=None):
        self.a, self.cb, self.lead, self.shifts = a, cb, lead, shifts
        self.w = a.shape[-1] if w is None else w
        self.nrows = a.shape[-2]

    def spec(self, tm, ncol=1, off=0):
        w = self.w // ncol
        last = self.nrows // tm - 1
        row = (lambda i: i) if off == 0 else (lambda i: jnp.clip(i + off, 0, last))
        if self.lead is None:
            return pl.BlockSpec((tm, w), lambda i, j, cb=self.cb: (row(i), cb * ncol + j))
        return pl.BlockSpec((None, tm, w), lambda i, j, cb=self.cb, k=self.lead: (k, row(i), cb * ncol + j))

    def nbytes(self):
        return self.w * self.a.dtype.itemsize * (1 if self.shifts is None else 3)


def _as_rows(rs):
    return [r if isinstance(r, Rows) else Rows(r) for r in rs]


def _shift_tile(prev, cur, nxt, k, t, nt):
    if k == 0:
        return cur
    tm = cur.shape[0]
    row = lax.broadcasted_iota(jnp.int32, cur.shape, 0)
    if k > 0:
        edge = jnp.where(t > 0, pltpu.roll(prev, k, 0), 0.0)
        return jnp.where(row < k, edge, pltpu.roll(cur, k, 0))
    edge = jnp.where(t < nt - 1, pltpu.roll(nxt, tm + k, 0), 0.0)
    return jnp.where(row >= tm + k, edge, pltpu.roll(cur, tm + k, 0))


def _operands(rows, tm, ncol):
    specs, arrs = [], []
    for r in rows:
        if r.shifts is None:
            specs.append(r.spec(tm, ncol))
            arrs.append(r.a)
        else:
            assert ncol == 1
            specs += [r.spec(tm, 1, off) for off in (-1, 0, 1)]
            arrs += [r.a] * 3

    def load(refs):
        vals, k = [], 0
        t, nt = pl.program_id(0), rows[0].nrows // tm
        for r in rows:
            if r.shifts is None:
                vals.append(refs[k][...])
                k += 1
            else:
                prev, cur, nxt = refs[k][...], refs[k + 1][...], refs[k + 2][...]
                vals += [_shift_tile(prev, cur, nxt, s, t, nt) for s in r.shifts]
                k += 3
        return vals

    return specs, arrs, load


def _expand(rows):
    return [r for r in rows for _ in (r.shifts or [0])]


def _pick_tm(nrows, row_bytes, scale, cap=512):
    tm = cap
    while tm > 16 and (tm * row_bytes * scale > ROW_BUDGET or nrows % tm):
        tm //= 2
    assert nrows % tm == 0, (nrows, tm)
    return tm


def _full_spec(a):
    nd = a.ndim
    return pl.BlockSpec(a.shape, lambda i, j, nd=nd: (0,) * nd)


def _rowwise(name, fn, rows, params, outs, acc_outs=(), ncol=1, tm_cap=512):
    rows = _as_rows(rows)
    nrows = rows[0].nrows
    assert ncol == 1 or not (acc_outs or params)
    tm = _pick_tm(nrows, (sum(r.nbytes() for r in rows) + sum(w * jnp.dtype(d).itemsize for w, d in outs)) // ncol, 2, tm_cap)
    specs, arrs, load = _operands(rows, tm, ncol)
    n_r, n_p, n_o = len(specs), len(params), len(outs)

    def body(*refs):
        vals = load(refs[:n_r]) + [r[...] for r in refs[n_r:n_r + n_p]]
        res = fn(*vals)
        res = res if isinstance(res, (tuple, list)) else (res,)
        orefs = refs[n_r + n_p:]
        for o, v in zip(orefs[:n_o], res[:n_o]):
            o[...] = v.astype(o.dtype)
        for o, v in zip(orefs[n_o:], res[n_o:]):
            @pl.when(pl.program_id(0) == 0)
            def _(o=o):
                o[...] = jnp.zeros_like(o)
            o[...] += v

    out_shape = [jax.ShapeDtypeStruct((nrows, w), d) for w, d in outs] + [jax.ShapeDtypeStruct(s, F32) for s in acc_outs]
    out_specs = [pl.BlockSpec((tm, w // ncol), lambda i, j: (i, j)) for w, _ in outs] + \
                [pl.BlockSpec(s, lambda i, j, nd=len(s): (0,) * nd) for s in acc_outs]
    return pl.pallas_call(
        body, name=name, grid=(nrows // tm, ncol),
        in_specs=specs + [_full_spec(p) for p in params],
        out_specs=out_specs, out_shape=out_shape,
        compiler_params=_cparams(("arbitrary", "arbitrary") if acc_outs else ("parallel", "parallel")),
    )(*arrs, *params)


def _rowwise_bwd(name, fn, rows, params, ct_rows, ct_fn, row_grads, param_grads, row_grad_add=None, ncol=1):
    rows, ct_rows = _as_rows(rows), _as_rows(ct_rows)
    nrows = rows[0].nrows
    n_rg = sum(d is not None for d in row_grads)
    adds = _as_rows([a for a in (row_grad_add or []) if a is not None])
    add_at = [k for k, a in enumerate(row_grad_add or []) if a is not None]
    assert ncol == 1 or not (params or adds)
    seen = _expand(rows)
    gbytes = sum(r.w * jnp.dtype(d).itemsize for r, d in zip(seen, row_grads) if d is not None)
    tm = _pick_tm(nrows, (sum(r.nbytes() for r in rows + ct_rows + adds) + gbytes) // ncol, 4)
    r_specs, r_arrs, r_load = _operands(rows, tm, ncol)
    c_specs, c_arrs, c_load = _operands(ct_rows, tm, ncol)
    a_specs, a_arrs, a_load = _operands(adds, tm, ncol)
    n_r, n_p, n_c, n_a = len(r_specs), len(params), len(c_specs), len(a_specs)
    diff = [k for k, d in enumerate(row_grads) if d is not None] + [len(seen) + k for k, g in enumerate(param_grads) if g]

    def body(*refs):
        vals = r_load(refs[:n_r]) + [r[...] for r in refs[n_r:n_r + n_p]]
        cts_in = c_load(refs[n_r + n_p:n_r + n_p + n_c])
        add_in = a_load(refs[n_r + n_p + n_c:n_r + n_p + n_c + n_a]) if n_a else []
        orefs = refs[n_r + n_p + n_c + n_a:]

        def f(*dv):
            full = list(vals)
            for k, v in zip(diff, dv):
                full[k] = v
            res = fn(*full)
            return tuple(res) if isinstance(res, (tuple, list)) else (res,)

        outs, vjp = jax.vjp(f, *[vals[k].astype(F32) for k in diff])
        cts = ct_fn(*cts_in)
        cts = cts if isinstance(cts, (tuple, list)) else (cts,)
        grads = list(vjp(tuple(c.astype(o.dtype) for c, o in zip(cts, outs))))
        for k, a in zip(add_at, add_in):
            grads[k] = grads[k] + a.astype(F32)
        for o, g in zip(orefs[:n_rg], grads[:n_rg]):
            o[...] = g.astype(o.dtype)
        for o, g in zip(orefs[n_rg:], grads[n_rg:]):
            @pl.when(pl.program_id(0) == 0)
            def _(o=o):
                o[...] = jnp.zeros_like(o)
            o[...] += g.astype(F32)

    out_shape = [jax.ShapeDtypeStruct((nrows, r.w), d) for r, d in zip(seen, row_grads) if d is not None] + \
                [jax.ShapeDtypeStruct(p.shape, F32) for p, g in zip(params, param_grads) if g]
    out_specs = [pl.BlockSpec((tm, r.w // ncol), lambda i, j: (i, j)) for r, d in zip(seen, row_grads) if d is not None] + \
                [_full_spec(p) for p, g in zip(params, param_grads) if g]
    return pl.pallas_call(
        body, name=name, grid=(nrows // tm, ncol),
        in_specs=r_specs + [_full_spec(p) for p in params] + c_specs + a_specs,
        out_specs=out_specs, out_shape=out_shape,
        compiler_params=_cparams(("arbitrary", "arbitrary")),
    )(*r_arrs, *params, *c_arrs, *a_arrs)


def _assemble(name, groups, dtype):
    sizes = [len(g) for g in groups]
    flat = [a for g in groups for a in g]

    def fn(*tiles):
        out, k = [], 0
        for s in sizes:
            acc = tiles[k].astype(F32)
            for t in tiles[k + 1:k + s]:
                acc = acc + t.astype(F32)
            out.append(acc.astype(dtype))
            k += s
        return out[0] if len(out) == 1 else jnp.concatenate(out, axis=1)

    width = sum(g[0].shape[-1] if not isinstance(g[0], Rows) else g[0].w for g in groups)
    return _rowwise(name, fn, flat, [], [(width, dtype)])[0]


def _matmul(name, a, b, mode, out_dtype, acc_in=None, also=None):
    if mode == 'nn':
        (M, K), N = a.shape, b.shape[1]
    elif mode == 'nt':
        (M, K), N = a.shape, b.shape[0]
    else:
        (K, M), N = a.shape, b.shape[1]
    tm, tn, tk = _tile(M, 1536), _tile(N, 1536), _tile(K, 2048 if mode == 'tn' else 3328)
    nk = K // tk
    dn = {'nn': NN, 'nt': NT, 'tn': TN}[mode]
    has_acc = acc_in is not None

    n_in = 3 if has_acc else 2
    dtypes = [out_dtype] + ([also] if also is not None else [])

    def body(*refs):
        a_ref, b_ref = refs[:2]
        o_refs = refs[n_in:n_in + len(dtypes)]
        part = lax.dot_general(a_ref[...].astype(BF16), b_ref[...].astype(BF16), dn, preferred_element_type=F32)
        if nk == 1:
            val = (refs[2][...] + part) if has_acc else part
            for o_ref in o_refs:
                o_ref[...] = val.astype(o_ref.dtype)
            return
        acc = refs[-1]

        @pl.when(pl.program_id(2) == 0)
        def _():
            acc[...] = refs[2][...] if has_acc else jnp.zeros_like(acc)
        acc[...] += part

        @pl.when(pl.program_id(2) == nk - 1)
        def _():
            for o_ref in o_refs:
                o_ref[...] = acc[...].astype(o_ref.dtype)

    a_spec = pl.BlockSpec((tk, tm), lambda i, j, k: (k, i)) if mode == 'tn' else pl.BlockSpec((tm, tk), lambda i, j, k: (i, k))
    b_spec = pl.BlockSpec((tn, tk), lambda i, j, k: (j, k)) if mode == 'nt' else pl.BlockSpec((tk, tn), lambda i, j, k: (k, j))
    o_spec = pl.BlockSpec((tm, tn), lambda i, j, k: (i, j))
    res = pl.pallas_call(
        body, name=name, grid=(M // tm, N // tn, nk),
        in_specs=[a_spec, b_spec] + ([o_spec] if has_acc else []), out_specs=[o_spec] * len(dtypes),
        out_shape=[jax.ShapeDtypeStruct((M, N), d) for d in dtypes],
        scratch_shapes=[] if nk == 1 else [pltpu.VMEM((tm, tn), F32)],
        compiler_params=_cparams(("parallel", "parallel", "arbitrary")),
    )(a, b, *([acc_in] if has_acc else []))
    return res[0] if also is None else res


def _ffn_up(name, h, wg, wu):
    (M, K), N = h.shape, wg.shape[1]
    tm, tn = _tile(M, 512), _tile(N, 1536)

    def body(h_ref, g_ref, u_ref, gate_ref, up_ref, act_ref):
        a = h_ref[...].astype(BF16)
        gate = _bdot(a, g_ref[...]).astype(BF16)
        up = _bdot(a, u_ref[...]).astype(BF16)
        gate_ref[...], up_ref[...] = gate, up
        act_ref[...] = _swiglu_fn(gate, up).astype(BF16)

    w_spec = pl.BlockSpec((K, tn), lambda i, j: (0, j))
    o_spec = pl.BlockSpec((tm, tn), lambda i, j: (i, j))
    return pl.pallas_call(
        body, name=name, grid=(M // tm, N // tn),
        in_specs=[pl.BlockSpec((tm, K), lambda i, j: (i, 0)), w_spec, w_spec], out_specs=[o_spec] * 3,
        out_shape=[jax.ShapeDtypeStruct((M, N), BF16)] * 3,
        compiler_params=_cparams(("parallel", "parallel")),
    )(h, wg, wu)


def _ffn_down_bwd(name, df, w_out, gate, up):
    (M, K), N = df.shape, w_out.shape[0]
    tm, tn = _tile(M, 512), _tile(N, 1536)

    def body(d_ref, w_ref, gate_ref, up_ref, dg_ref, du_ref):
        dact = _bdot(d_ref[...], w_ref[...], NT)
        dg, du = _swiglu_bwd((gate_ref[...], up_ref[...]), dact)
        dg_ref[...], du_ref[...] = dg, du

    o_spec = pl.BlockSpec((tm, tn), lambda i, j: (i, j))
    return pl.pallas_call(
        body, name=name, grid=(M // tm, N // tn),
        in_specs=[pl.BlockSpec((tm, K), lambda i, j: (i, 0)), pl.BlockSpec((tn, K), lambda i, j: (j, 0)), o_spec, o_spec],
        out_specs=[o_spec] * 2, out_shape=[jax.ShapeDtypeStruct((M, N), BF16)] * 2,
        compiler_params=_cparams(("parallel", "parallel")),
    )(df, w_out, gate, up)


def _small_dot(name, a, b, mode):
    dn = {'nn': (((1,), (0,)), ((), ())), 'nt': (((1,), (1,)), ((), ()))}[mode]
    M = a.shape[0]
    N = b.shape[1] if mode == 'nn' else b.shape[0]

    def body(a_ref, b_ref, o_ref):
        o_ref[...] = lax.dot_general(a_ref[...], b_ref[...], dn, precision=HIGHEST, preferred_element_type=F32)

    return pl.pallas_call(body, name=name, out_shape=jax.ShapeDtypeStruct((M, N), F32),
                          compiler_params=pltpu.CompilerParams(vmem_limit_bytes=VMEM_LIMIT))(a, b)


NN, NT, TN = (((1,), (0,)), ((), ())), (((1,), (1,)), ((), ())), (((0,), (0,)), ((), ()))


def _bdot(a, b, dn=NN):
    return lax.dot_general(a.astype(BF16), b.astype(BF16), dn, preferred_element_type=F32)


def _sigmoid(x):
    return 0.5 * jnp.tanh(0.5 * x) + 0.5


def _silu(x):
    return x * _sigmoid(x)


def _softplus(x):
    return jnp.maximum(x, 0.0) + jnp.log(1.0 + jnp.exp(-jnp.abs(x)))


def _gelu(x):
    return 0.5 * x * (1.0 + jnp.tanh(0.7978845608028654 * (x + 0.044715 * (x * x * x))))


def _rms(x, g):
    return x * lax.rsqrt(jnp.mean(x * x, axis=-1, keepdims=True) + EPS) * g


def _prenorm_fn(x, g):
    return _rms(x, g)


def _postnorm_fn(x, y, g):
    return x + _rms(y, g)


@jax.custom_vjp
def _swiglu_fn(gate, up):
    return _silu(gate.astype(F32)) * up.astype(F32)


def _swiglu_bwd(res, ct):
    g, u = res[0].astype(F32), res[1].astype(F32)
    s = _sigmoid(g)
    gs = g * s
    return (ct * u * (s + gs - gs * s)).astype(res[0].dtype), (ct * gs).astype(res[1].dtype)


_swiglu_fn.defvjp(lambda gate, up: (_swiglu_fn(gate, up), (gate, up)), _swiglu_bwd)


def _gla_pre_fn(z, wg, bg):
    logit = _bdot(z, wg) + bg
    return -_softplus(-logit) * (1.0 / GLA_TAU)


def _seg_mean(x, seg):
    return lax.dot_general(x, seg, NN, precision=HIGHEST, preferred_element_type=F32)


def _gla_post_fn(of, ob, g, norm, seg):
    o = of + ob
    o = o * lax.rsqrt(_seg_mean(o * o, seg) + EPS) * norm
    return o * _silu(g)


@jax.custom_vjp
def _swap_halves(x):
    n = x.shape[-1]
    lane = lax.broadcasted_iota(jnp.int32, x.shape, x.ndim - 1)
    lo = (lane & (HEAD_DIM - 1)) < HEAD_DIM // 2
    return jnp.where(lo, pltpu.roll(x, n - HEAD_DIM // 2, x.ndim - 1), pltpu.roll(x, HEAD_DIM // 2, x.ndim - 1))


_swap_halves.defvjp(lambda x: (_swap_halves(x), None), lambda _, g: (_swap_halves(g),))


def _rope_fn(q, k, cs, sn):
    return q * cs + _swap_halves(q) * sn, k * cs + _swap_halves(k) * sn


def _dil_comb_fn(o1, o2, o3, l1, l2, l3):
    m = jnp.maximum(jnp.maximum(l1, l2), l3)
    e1, e2, e3 = jnp.exp(l1 - m), jnp.exp(l2 - m), jnp.exp(l3 - m)
    return (e1 * o1 + e2 * o2 + e3 * o3) / (e1 + e2 + e3)


LRU_PARAMS = ['lru_cw0', 'lru_cw1', 'lru_cw2', 'lru_cw3', 'lru_cb', 'lru_wa0', 'lru_wa1', 'lru_ba0', 'lru_ba1',
              'lru_wx0', 'lru_wx1', 'lru_bx0', 'lru_bx1', 'lru_lam0', 'lru_lam1']


def _lru_pre_fn(x0, x1, x2, x3, cw0, cw1, cw2, cw3, cb, wa0, wa1, ba0, ba1, wx0, wx1, bx0, bx1, lam0, lam1):
    xc = cb + x0 * cw0 + x1 * cw1 + x2 * cw2 + x3 * cw3
    outs = []
    for wa, ba, wx, bx, lam in ((wa0, ba0, wx0, bx0, lam0), (wa1, ba1, wx1, bx1, lam1)):
        r = _sigmoid(_bdot(xc, wa) + ba)
        i = _sigmoid(_bdot(xc, wx) + bx)
        log_a = -LRU_C * r * _softplus(-lam)
        a = jnp.exp(log_a)
        u = jnp.sqrt(-jnp.tanh(log_a) * (a * a + 1.0)) * (i * xc)
        outs += [a, u]
    return outs[0], outs[2], outs[1], outs[3]


def _lru_post_fn(hf, hb, gate):
    return (hf + hb) * _gelu(gate)


def _mix_post_fn(of, ob, ga, yb, hf, hb, gc, o1, o2, o3, l1, l2, l3, norm, seg):
    ya = _gla_post_fn(of, ob, ga, norm, seg)
    yc = _lru_post_fn(hf, hb, gc)
    yd = _dil_comb_fn(o1, o2, o3, l1, l2, l3)
    return jnp.concatenate([ya.astype(BF16), yb.astype(BF16), yc.astype(BF16), yd.astype(BF16)], axis=1)


def _attn_heads(qs, kws, vws, biases):
    ss = [_bdot(q, kw, NT) * (HEAD_DIM ** -0.5) + b for q, kw, b in zip(qs, kws, biases)]
    ms = [lax.stop_gradient(jnp.max(s, axis=-1, keepdims=True)) for s in ss]
    es = [jnp.exp(s - m) for s, m in zip(ss, ms)]
    dens = [jnp.sum(e, axis=-1, keepdims=True) for e in es]
    ps = [e * (1.0 / d) for e, d in zip(es, dens)]
    os_ = [_bdot(p_, vw) for p_, vw in zip(ps, vws)]
    return os_, [m + jnp.log(d) for m, d in zip(ms, dens)]


def _cumsum_rows(x, rev):
    n = x.shape[0]
    row = lax.broadcasted_iota(jnp.int32, x.shape, 0)
    s = 1
    while s < n:
        if rev:
            x = x + jnp.where(row < n - s, pltpu.roll(x, n - s, 0), 0.0)
        else:
            x = x + jnp.where(row >= s, pltpu.roll(x, s, 0), 0.0)
        s *= 2
    return x


def _gla_chunks(qs, ks, vs, bs, sts, revs):
    C = qs[0].shape[0]
    ti = lax.broadcasted_iota(jnp.int32, (C, C), 0)
    si = lax.broadcasted_iota(jnp.int32, (C, C), 1)
    row = lax.broadcasted_iota(jnp.int32, (C, 1), 0)
    incl = {False: si <= ti, True: si >= ti}
    last = {False: row == C - 1, True: row == 0}
    mid = {False: row == C // 2 - 1, True: row == C // 2}
    bls = [jnp.sum(jnp.where(last[r], b, 0.0), axis=0, keepdims=True) for b, r in zip(bs, revs)]
    bms = [jnp.sum(jnp.where(mid[r], b, 0.0), axis=0, keepdims=True) for b, r in zip(bs, revs)]
    qss = [q * (HEAD_DIM ** -0.5) for q in qs]
    qi = [q * jnp.exp(b - bm) for q, b, bm in zip(qss, bs, bms)]
    ki = [k * jnp.exp(bm - b) for k, b, bm in zip(ks, bs, bms)]
    atts = [jnp.where(incl[r], _bdot(a, b, NT), 0.0) for a, b, r in zip(qi, ki, revs)]
    qe = [q * jnp.exp(b) for q, b in zip(qss, bs)]
    kl = [k * jnp.exp(bl - b) for k, b, bl in zip(ks, bs, bls)]
    o1 = [_bdot(a, v) for a, v in zip(atts, vs)]
    o2 = [_bdot(q, st, NT) for q, st in zip(qe, sts)]
    kvs = [_bdot(v, k, TN) for v, k in zip(vs, kl)]
    return [a + b for a, b in zip(o1, o2)], [st * jnp.exp(bl) + kv for st, bl, kv in zip(sts, bls, kvs)]


def _gla_specs(n, blocks, first):
    C = GLA_CHUNK
    at = (lambda i: i) if first else (lambda i: n - 1 - i)
    return [pl.BlockSpec((C, GROUP_WIDTH), lambda i, b=b: (at(i), b)) for b in blocks], at


def _gla_scan_fwd(p, la):
    L = p.shape[0]
    C, H, dh = GLA_CHUNK, GROUP_HEADS, HEAD_DIM
    n = L // C
    f_specs, f_at = _gla_specs(n, (P_QA, P_KA, P_VA), True)
    b_specs, b_at = _gla_specs(n, (P_QA, P_KA, P_VA), False)
    tile = lambda at, blk=0: pl.BlockSpec((C, GROUP_WIDTH), lambda i: (at(i), blk))
    st_spec = lambda at: pl.BlockSpec((None, H, dh, dh), lambda i: (at(i), 0, 0, 0))

    def body(qf, kf, vf, lf, qb, kb, vb, lb, of_ref, ob_ref, sf_ref, sb_ref, stf, stb):
        @pl.when(pl.program_id(0) == 0)
        def _():
            stf[...] = jnp.zeros_like(stf)
            stb[...] = jnp.zeros_like(stb)
        sf_ref[...] = stf[...]
        sb_ref[...] = stb[...]
        chains = [(t, h, sl) for t in ((qf, kf, vf, _cumsum_rows(lf[...], False), of_ref, stf, False),
                                       (qb, kb, vb, _cumsum_rows(lb[...], True), ob_ref, stb, True))
                  for h, sl in enumerate(HEADS)]
        os_, sts = _gla_chunks(*[[t[j][:, sl] for t, h, sl in chains] for j in range(4)],
                               [t[5][h] for t, h, sl in chains], [t[6] for t, h, sl in chains])
        for (t, h, sl), o, st_new in zip(chains, os_, sts):
            t[4][:, sl] = o
            t[5][h] = st_new

    return pl.pallas_call(
        body, name="gla_scan", grid=(n,),
        in_specs=f_specs + [tile(f_at, 0)] + b_specs + [tile(b_at, 1)],
        out_specs=[tile(f_at), tile(b_at), st_spec(f_at), st_spec(b_at)],
        out_shape=[jax.ShapeDtypeStruct((L, GROUP_WIDTH), F32)] * 2 + [jax.ShapeDtypeStruct((n, H, dh, dh), F32)] * 2,
        scratch_shapes=[pltpu.VMEM((H, dh, dh), F32)] * 2,
        compiler_params=_cparams(("arbitrary",)),
    )(p, p, p, la, p, p, p, la)


def _gla_scan_bwd(p, la, sf, sb, do):
    L = p.shape[0]
    C, H, dh = GLA_CHUNK, GROUP_HEADS, HEAD_DIM
    n = L // C
    f_specs, f_at = _gla_specs(n, (P_QA, P_KA, P_VA), False)
    b_specs, b_at = _gla_specs(n, (P_QA, P_KA, P_VA), True)
    tile = lambda at, blk=0: pl.BlockSpec((C, GROUP_WIDTH), lambda i: (at(i), blk))
    st_spec = lambda at: pl.BlockSpec((None, H, dh, dh), lambda i: (at(i), 0, 0, 0))

    def body(qf, kf, vf, lf, spf, dof, qb, kb, vb, lb, spb, dob, *rest):
        outs_f, outs_b, (dstf, dstb) = rest[0:4], rest[4:8], rest[8:10]

        @pl.when(pl.program_id(0) == 0)
        def _():
            dstf[...] = jnp.zeros_like(dstf)
            dstb[...] = jnp.zeros_like(dstb)
        chains = [(t, h, sl) for t in ((qf, kf, vf, _cumsum_rows(lf[...], False), spf, dof, outs_f, dstf, False),
                                       (qb, kb, vb, _cumsum_rows(lb[...], True), spb, dob, outs_b, dstb, True))
                  for h, sl in enumerate(HEADS)]
        nc = len(chains)
        revs = [t[8] for t, h, sl in chains]
        flat = [t[j][:, sl] for j in range(4) for t, h, sl in chains] + [t[4][h] for t, h, sl in chains]

        def f(*a):
            os_, sts = _gla_chunks(*[list(a[j * nc:(j + 1) * nc]) for j in range(5)], revs)
            return tuple(os_) + tuple(sts)

        _, vjp = jax.vjp(f, *flat)
        grads = vjp(tuple(t[5][:, sl] for t, h, sl in chains) + tuple(t[7][h] for t, h, sl in chains))
        for c_, (t, h, sl) in enumerate(chains):
            for j in range(4):
                t[6][j][:, sl] = grads[j * nc + c_].astype(t[6][j].dtype)
            t[7][h] = grads[4 * nc + c_]
        for outs, rev in ((outs_f, False), (outs_b, True)):
            outs[3][...] = _cumsum_rows(outs[3][...], not rev)

    return pl.pallas_call(
        body, name="gla_scan_b", grid=(n,),
        in_specs=f_specs + [tile(f_at, 0), st_spec(f_at), tile(f_at)] + b_specs + [tile(b_at, 1), st_spec(b_at), tile(b_at)],
        out_specs=[tile(f_at)] * 4 + [tile(b_at)] * 4,
        out_shape=[jax.ShapeDtypeStruct((L, GROUP_WIDTH), d) for d in (BF16, BF16, BF16, F32)] * 2,
        scratch_shapes=[pltpu.VMEM((H, dh, dh), F32)] * 2,
        compiler_params=_cparams(("arbitrary",)),
    )(p, p, p, la, sf, do, p, p, p, la, sb, do)


NA_W = NA_ROWS * GRID_W
NA_BW = (2 * NA_ROWS - 1) * GRID_W


def _na_start(i, rows):
    return jnp.clip(i - NA_ROWS // 2, 0, rows - NA_ROWS)


def _na_fwd(p16, btab):
    L = p16.shape[0]
    rows = L // GRID_W
    kv = lambda blk: pl.BlockSpec((L, GROUP_WIDTH), lambda i: (0, blk))

    def body(q_ref, k_ref, v_ref, b_ref, o_ref):
        r = pl.program_id(0)
        s = _na_start(r, rows)
        start = pl.multiple_of(s * GRID_W, GRID_W)
        os_, _ = _attn_heads([q_ref[:, sl] for sl in HEADS], [k_ref[pl.ds(start, NA_W), sl] for sl in HEADS],
                             [v_ref[pl.ds(start, NA_W), sl] for sl in HEADS],
                             [b_ref[s - r + NA_ROWS - 1, h] for h in range(GROUP_HEADS)])
        for sl, o in zip(HEADS, os_):
            o_ref[:, sl] = o.astype(o_ref.dtype)

    whole = lambda a: pl.BlockSpec(a.shape, lambda i, nd=a.ndim: (0,) * nd)
    return pl.pallas_call(
        body, name="na_attn", grid=(rows,),
        in_specs=[pl.BlockSpec((GRID_W, GROUP_WIDTH), lambda i: (i, P_QB)), kv(P_KB), kv(P_VB), whole(btab)],
        out_specs=pl.BlockSpec((GRID_W, GROUP_WIDTH), lambda i: (i, 0)),
        out_shape=jax.ShapeDtypeStruct((L, GROUP_WIDTH), BF16),
        compiler_params=_cparams(("arbitrary",)),
    )(p16, p16, p16, btab)


def _na_bwd(p16, btab, dycat):
    L = p16.shape[0]
    rows = L // GRID_W
    kv = lambda blk: pl.BlockSpec((L, GROUP_WIDTH), lambda i: (0, blk))
    flush = NA_ROWS - 1
    emit = lambda i: jnp.where(i < rows, _na_start(i, rows), i - flush)

    def body(q_ref, k_ref, v_ref, b_ref, do_ref, dq_ref, dk_ref, dv_ref, db_ref, acc_k, acc_v):
        i = pl.program_id(0)

        @pl.when(i == 0)
        def _():
            acc_k[...] = jnp.zeros_like(acc_k)
            acc_v[...] = jnp.zeros_like(acc_v)
            db_ref[...] = jnp.zeros_like(db_ref)

        @pl.when((i > 0) & (emit(i) != emit(i - 1)))
        def _():
            for acc in (acc_k, acc_v):
                moved = acc[GRID_W:NA_W, :]
                acc[0:NA_W - GRID_W, :] = moved
                acc[NA_W - GRID_W:NA_W, :] = jnp.zeros((GRID_W, GROUP_WIDTH), F32)

        @pl.when(i < rows)
        def _():
            s = _na_start(i, rows)
            sv = s - i + NA_ROWS - 1
            start = pl.multiple_of(s * GRID_W, GRID_W)
            H = GROUP_HEADS
            flat = [q_ref[:, sl].astype(F32) for sl in HEADS] + [k_ref[pl.ds(start, NA_W), sl].astype(F32) for sl in HEADS] + \
                   [v_ref[pl.ds(start, NA_W), sl].astype(F32) for sl in HEADS] + [b_ref[sv, h] for h in range(H)]

            def f(*a):
                os_, lses = _attn_heads(a[0:H], a[H:2 * H], a[2 * H:3 * H], a[3 * H:4 * H])
                return tuple(os_) + tuple(lses)

            _, vjp = jax.vjp(f, *flat)
            grads = vjp(tuple(do_ref[:, sl] for sl in HEADS) + (jnp.zeros((GRID_W, 1), F32),) * H)
            for h, sl in enumerate(HEADS):
                dq_ref[:, sl] = grads[h].astype(dq_ref.dtype)
                acc_k[:, sl] += grads[H + h]
                acc_v[:, sl] += grads[2 * H + h]
                db_ref[sv, h] += grads[3 * H + h]

        dk_ref[...] = acc_k[0:GRID_W, :].astype(dk_ref.dtype)
        dv_ref[...] = acc_v[0:GRID_W, :].astype(dv_ref.dtype)

    whole = lambda a: pl.BlockSpec(a.shape, lambda i, nd=a.ndim: (0,) * nd)
    qrow = lambda blk: pl.BlockSpec((GRID_W, GROUP_WIDTH), lambda i: (jnp.minimum(i, rows - 1), blk))
    erow = pl.BlockSpec((GRID_W, GROUP_WIDTH), lambda i: (emit(i), 0))
    return pl.pallas_call(
        body, name="na_attn_b", grid=(rows + flush,),
        in_specs=[qrow(P_QB), kv(P_KB), kv(P_VB), whole(btab), qrow(1)],
        out_specs=[qrow(0), erow, erow, whole(btab)],
        out_shape=[jax.ShapeDtypeStruct((L, GROUP_WIDTH), BF16)] * 3 + [jax.ShapeDtypeStruct(btab.shape, F32)],
        scratch_shapes=[pltpu.VMEM((NA_W, GROUP_WIDTH), F32)] * 2,
        compiler_params=_cparams(("arbitrary",)),
    )(p16, p16, p16, btab, dycat)


def _na_col_ok():
    qc = np.arange(GRID_W)[:, None]
    kc = (np.arange(NA_W) % GRID_W)[None, :]
    c0 = np.clip(qc - NA_COLS // 2, 0, GRID_W - NA_COLS)
    return (kc >= c0) & (kc < c0 + NA_COLS)


def _rpb_tables():
    c = np.arange(GRID_W)
    dc = np.clip(c[None, :] - c[:, None], -(NA_COLS - 1), NA_COLS - 1) + NA_COLS - 1
    t = np.zeros((2 * NA_COLS - 1, GRID_W, GRID_W), np.float32)
    t[dc, c[:, None], c[None, :]] = 1.0
    return jnp.asarray(t.reshape(2 * NA_COLS - 1, GRID_W * GRID_W))


def _rpb_expand(rpb, tab):
    H = rpb.shape[0]
    xt = _small_dot("na_bias", rpb.reshape(H * (2 * NA_ROWS - 1), 2 * NA_COLS - 1), tab, 'nn')
    b15 = xt.reshape(H, 2 * NA_ROWS - 1, GRID_W, GRID_W).transpose(0, 2, 1, 3).reshape(H, GRID_W, NA_BW)
    ok = jnp.asarray(_na_col_ok())
    return jnp.stack([jnp.where(ok, b15[:, :, sv * GRID_W:sv * GRID_W + NA_W], NEG) for sv in range(NA_ROWS)])


def _rpb_contract(dbv, tab):
    H = dbv.shape[1]
    db = sum(jnp.pad(dbv[sv], ((0, 0), (0, 0), (sv * GRID_W, NA_BW - NA_W - sv * GRID_W))) for sv in range(NA_ROWS))
    dx = db.reshape(H, GRID_W, 2 * NA_ROWS - 1, GRID_W).transpose(0, 2, 1, 3).reshape(H * (2 * NA_ROWS - 1), GRID_W * GRID_W)
    return _small_dot("na_bias_b", dx, tab, 'nt').reshape(H, 2 * NA_ROWS - 1, 2 * NA_COLS - 1)


def _band_bias(i, tq, w, halo, n):
    a = lax.broadcasted_iota(jnp.int32, (tq, w), 0)
    b = lax.broadcasted_iota(jnp.int32, (tq, w), 1)
    kpos = i * tq - halo + b
    d = b - halo - a
    return jnp.where((d <= halo) & (d >= -halo) & (kpos >= 0) & (kpos < n), 0.0, NEG)


def _band_fwd(name, q, kp, vp, tq, halo):
    G, n, _ = q.shape
    w = tq + 2 * halo

    def body(q_ref, k_ref, v_ref, o_ref, l_ref):
        i = pl.program_id(1)
        start = pl.multiple_of(i * tq, tq)
        bias = _band_bias(i, tq, w, halo, n)
        os_, lses = _attn_heads([q_ref[:, sl] for sl in HEADS], [k_ref[pl.ds(start, w), sl] for sl in HEADS],
                                [v_ref[pl.ds(start, w), sl] for sl in HEADS], [bias] * GROUP_HEADS)
        for sl, o, lse in zip(HEADS, os_, lses):
            o_ref[:, sl] = o.astype(o_ref.dtype)
            l_ref[:, sl] = jnp.broadcast_to(lse, (tq, HEAD_DIM))

    qblk = pl.BlockSpec((None, tq, GROUP_WIDTH), lambda g, i: (g, i, 0))
    kblk = pl.BlockSpec((None, n + 2 * halo, GROUP_WIDTH), lambda g, i: (g, 0, 0))
    return pl.pallas_call(
        body, name=name, grid=(G, n // tq), in_specs=[qblk, kblk, kblk], out_specs=[qblk, qblk],
        out_shape=[jax.ShapeDtypeStruct((G, n, GROUP_WIDTH), d) for d in (BF16, F32)],
        compiler_params=_cparams(("parallel", "arbitrary")),
    )(q, kp, vp)


def _band_bwd(name, q, kp, vp, do, dl, tq, halo):
    G, n, _ = q.shape
    w = tq + 2 * halo
    nq = n // tq

    def body(q_ref, k_ref, v_ref, do_ref, dl_ref, dq_ref, dk_ref, dv_ref, acc_k, acc_v):
        i = pl.program_id(1)

        @pl.when(i == 0)
        def _():
            acc_k[...] = jnp.zeros_like(acc_k)
            acc_v[...] = jnp.zeros_like(acc_v)

        @pl.when(i > 0)
        def _():
            for acc in (acc_k, acc_v):
                moved = acc[tq:w, :]
                acc[0:2 * halo, :] = moved
                acc[2 * halo:w, :] = jnp.zeros((tq, GROUP_WIDTH), F32)

        @pl.when(i < nq)
        def _():
            start = pl.multiple_of(i * tq, tq)
            bias = _band_bias(i, tq, w, halo, n)
            H = GROUP_HEADS
            flat = [q_ref[:, sl].astype(F32) for sl in HEADS] + [k_ref[pl.ds(start, w), sl].astype(F32) for sl in HEADS] + \
                   [v_ref[pl.ds(start, w), sl].astype(F32) for sl in HEADS]

            def f(*a):
                os_, lses = _attn_heads(a[0:H], a[H:2 * H], a[2 * H:3 * H], [bias] * H)
                return tuple(os_) + tuple(lses)

            _, vjp = jax.vjp(f, *flat)
            grads = vjp(tuple(do_ref[:, sl].astype(F32) for sl in HEADS) +
                        tuple(jnp.sum(dl_ref[:, sl], axis=1, keepdims=True) for sl in HEADS))
            for h, sl in enumerate(HEADS):
                dq_ref[:, sl] = grads[h].astype(dq_ref.dtype)
                acc_k[:, sl] += grads[H + h]
                acc_v[:, sl] += grads[2 * H + h]

        dk_ref[...] = acc_k[0:tq, :].astype(dk_ref.dtype)
        dv_ref[...] = acc_v[0:tq, :].astype(dv_ref.dtype)

    qblk = pl.BlockSpec((None, tq, GROUP_WIDTH), lambda g, i: (g, jnp.minimum(i, nq - 1), 0))
    kblk = pl.BlockSpec((None, n + 2 * halo, GROUP_WIDTH), lambda g, i: (g, 0, 0))
    eblk = pl.BlockSpec((None, tq, GROUP_WIDTH), lambda g, i: (g, i, 0))
    return pl.pallas_call(
        body, name=name, grid=(G, nq + 1), in_specs=[qblk, kblk, kblk, qblk, qblk], out_specs=[qblk, eblk, eblk],
        out_shape=[jax.ShapeDtypeStruct((G, n, GROUP_WIDTH), BF16)] + [jax.ShapeDtypeStruct((G, (nq + 1) * tq, GROUP_WIDTH), BF16)] * 2,
        scratch_shapes=[pltpu.VMEM((w, GROUP_WIDTH), F32)] * 2,
        compiler_params=_cparams(("parallel", "arbitrary")),
    )(q, kp, vp, do, dl)


def _lin_scan(name, coef, inp, rev, coef_shift=0):
    L, C = coef.shape
    tt = 256 if L % 256 == 0 else L
    nt = L // tt
    tile = (lambda i: nt - 1 - i) if rev else (lambda i: i)
    tidx = lambda i: (tile(i), 0)

    def body(*refs):
        u_ref, o_ref, carry = refs[-3:]

        @pl.when(pl.program_id(0) == 0)
        def _():
            carry[...] = jnp.zeros_like(carry)
        if coef_shift:
            a = _shift_tile(refs[0][...], refs[1][...], refs[2][...], coef_shift, tile(pl.program_id(0)), nt)
        else:
            a = refs[0][...]
        u = u_ref[...]
        row = lax.broadcasted_iota(jnp.int32, (tt, C), 0)
        s = 1
        while s < tt:
            ok = (row < tt - s) if rev else (row >= s)
            sh = tt - s if rev else s
            u = u + a * jnp.where(ok, pltpu.roll(u, sh, 0), 0.0)
            a = a * jnp.where(ok, pltpu.roll(a, sh, 0), 1.0)
            s *= 2
        out = u + a * carry[...]
        o_ref[...] = out
        carry[...] = out[0:1] if rev else out[tt - 1:tt]

    blk = pl.BlockSpec((tt, C), tidx)
    near = [pl.BlockSpec((tt, C), lambda i, off=off: (jnp.clip(tile(i) + off, 0, nt - 1), 0)) for off in (-1, 0, 1)]
    coef_specs = near if coef_shift else [blk]
    return pl.pallas_call(
        body, name=name, grid=(nt,), in_specs=coef_specs + [blk], out_specs=blk,
        out_shape=jax.ShapeDtypeStruct((L, C), F32), scratch_shapes=[pltpu.VMEM((1, C), F32)],
        compiler_params=_cparams(("arbitrary",)),
    )(*([coef] * len(coef_specs)), inp)


def _pieces(shape):
    n0 = max(d for d in range(1, DMA_PIECES + 1) if shape[0] % d == 0)
    n1 = 1
    if len(shape) >= 3:
        n1 = max(d for d in range(1, DMA_PIECES // n0 + 1) if shape[1] % d == 0)
    s0, s1 = shape[0] // n0, (shape[1] // n1 if len(shape) >= 3 else 0)
    out = []
    for i in range(n0):
        for j in range(n1):
            out.append((pl.ds(i * s0, s0),) + ((pl.ds(j * s1, s1),) if len(shape) >= 3 else ()))
    return out


def _exchange(name, src, axes, gather):
    flips = {'xy': [(1, 0, 0), (0, 1, 0), (1, 1, 0)], 'c': [(0, 0, 1)],
             'xyc': [(fx, fy, fc) for fx in (0, 1) for fy in (0, 1) for fc in (0, 1)][1:]}[axes]
    n = len(flips) + 1
    blk_shape = tuple(src.shape if gather else src.shape[1:])
    pieces = _pieces(blk_shape)

    def number(px, py, pc):
        return {'xy': 2 * px + py, 'c': pc, 'xyc': 4 * px + 2 * py + pc}[axes]

    def body(src_ref, out_ref, send_sems, recv_sems):
        x, y, c = lax.axis_index("x"), lax.axis_index("y"), lax.axis_index("c")
        me = number(x, y, c)
        piece = (lambda k: src_ref) if gather else (lambda k: src_ref.at[k])
        peers = []
        for s, (fx, fy, fc) in enumerate(flips):
            px, py, pc = (x + fx) % 2, (y + fy) % 2, (c + fc) % 2

            def copy(ix, s=s, px=px, py=py, pc=pc):
                part = (lambda r: r) if ix is None else (lambda r: r.at[ix])
                return pltpu.make_async_remote_copy(
                    src_ref=part(piece(number(px, py, pc))), dst_ref=part(out_ref.at[me]),
                    send_sem=send_sems.at[s], recv_sem=recv_sems.at[s],
                    device_id=(px, py, pc), device_id_type=MESH)

            for ix in pieces:
                copy(ix).start()
            peers.append(copy)
        for copy in peers:
            copy(None).wait()

    out = pl.pallas_call(
        body, name=name, out_shape=jax.ShapeDtypeStruct((n,) + blk_shape, src.dtype),
        in_specs=[pl.BlockSpec(memory_space=pl.ANY)], out_specs=pl.BlockSpec(memory_space=pl.ANY),
        scratch_shapes=[pltpu.SemaphoreType.DMA((n - 1,)), pltpu.SemaphoreType.DMA((n - 1,))],
    )(src)
    me = number(lax.axis_index("x"), lax.axis_index("y"), lax.axis_index("c"))
    own = src if gather else lax.dynamic_index_in_dim(src, me, 0, keepdims=False)
    return lax.dynamic_update_index_in_dim(out, own, me, 0)


def _ordered_sum(name, buf, dtype=F32):
    n = buf.shape[0]

    def fn(*t):
        acc = t[0].astype(F32)
        for v in t[1:]:
            acc = acc + v.astype(F32)
        return acc

    return _rowwise(name, fn, [Rows(buf, lead=k) for k in range(n)], [], [(buf.shape[-1], dtype)])[0]


def _reduce_big(name, g):
    mine = _ordered_sum(name + "_sum_c", _exchange(name + "_swap_c", g, 'c', False).reshape(2, -1, g.shape[-1]), BF16)
    mine = mine.reshape(g.shape[1:])
    tot = _ordered_sum(name + "_sum_xy", _exchange(name + "_a2a_xy", mine, 'xy', False))
    return _exchange(name + "_share_c", tot, 'c', True)


def _dilate(t, dil):
    L, C = t.shape
    return t.reshape(L // dil, dil, C).transpose(1, 0, 2)


def _undilate(t):
    dil, n, C = t.shape
    return t.transpose(1, 0, 2).reshape(dil * n, C)


def _pad_rows(t, halo):
    return jnp.pad(t, ((0, 0), (halo, halo), (0, 0)))


def _pcol(p, blk):
    return Rows(p, GROUP_WIDTH, blk)


def _pslice(p, blk):
    return p[:, blk * GROUP_WIDTH:(blk + 1) * GROUP_WIDTH]


def _conv_taps(p):
    return Rows(p, GROUP_WIDTH, P_XC, shifts=[LRU_CONV_LEFT - j for j in range(LRU_CONV)])


def _seg_matrix():
    h = np.arange(GROUP_WIDTH) // HEAD_DIM
    return jnp.asarray((h[:, None] == h[None, :]).astype(np.float32) / HEAD_DIM)


def _rope_tables(L):
    pos = jnp.arange(L, dtype=F32)
    inv_freq = ROPE_THETA ** (-jnp.arange(0, HEAD_DIM, 2, dtype=F32) / HEAD_DIM)
    ang = pos[:, None] * inv_freq[None, :]
    cos, sin = jnp.cos(ang), jnp.sin(ang)
    cs = jnp.tile(jnp.concatenate([cos, cos], axis=1), (1, GROUP_HEADS))
    sn = jnp.tile(jnp.concatenate([-sin, sin], axis=1), (1, GROUP_HEADS))
    return cs, sn


def _dil_branches(L):
    out = []
    for window, dil in DIL_PAIRS:
        radius = window // (2 * dil)
        n = L // dil
        out.append((dil, radius, min(256, n)))
    return out


def _dil_operands(qr, kr, p16, dil, radius):
    return _dilate(qr, dil), _pad_rows(_dilate(kr, dil), radius), _pad_rows(_dilate(_pslice(p16, P_VD), dil), radius)


def _layer_fwd(x, w, c):
    L, D = x.shape
    sv = {'x_in': x}
    h = _rowwise("mix_prenorm", _prenorm_fn, [x], [w['mix_norm_pre']], [(D, BF16)])[0]
    p, p16 = _matmul("mix_proj", h, w['w_in'], 'nn', F32, also=BF16)
    sv.update(p=p, p16=p16, h=h)

    la = _rowwise("gla_pre", _gla_pre_fn, [Rows(p, LANE, P_Z // LANE)], [w['gla_wg'], w['gla_bg']], [(2 * GROUP_WIDTH, F32)])[0]
    of, ob, sf, sb = _gla_scan_fwd(p, la)
    sv.update(la=la, sf=sf, sb=sb, of=of, ob=ob)

    yb = _na_fwd(p16, _rpb_expand(w['na_rpb'], c['rpb_tab']))

    a0, a1, u0, u1 = _rowwise("lru_pre", _lru_pre_fn, [_conv_taps(p)], [w[k] for k in LRU_PARAMS], [(GROUP_WIDTH, F32)] * 4)
    hf = _lin_scan("lru_scan_f", a0, u0, False)
    hb = _lin_scan("lru_scan_b", a1, u1, True)
    sv.update(a0=a0, a1=a1, hf=hf, hb=hb)

    qr, kr = _rowwise("rope", _rope_fn, [_pcol(p, P_QD), _pcol(p, P_KD), c['cos'], c['sin']], [], [(GROUP_WIDTH, BF16)] * 2)
    os_, ls_, ops = [], [], []
    for dil, radius, tq in _dil_branches(L):
        ops.append(_dil_operands(qr, kr, p16, dil, radius))
        o, lse = _band_fwd(f"dil_attn{dil}", *ops[-1], tq, radius)
        os_.append(_undilate(o))
        ls_.append(_undilate(lse))
    sv.update(dil_ops=ops, dil_o=os_, dil_l=ls_)

    ycat = _rowwise("mix_post", _mix_post_fn, [of, ob, _pcol(p, P_GA), yb, hf, hb, _pcol(p, P_GC)] + os_ + ls_,
                    [w['gla_norm'], c['seg']], [(4 * GROUP_WIDTH, BF16)])[0]
    y = _matmul("mix_out", ycat, w['w_out'], 'nn', F32)
    xm = _rowwise("mix_postnorm", _postnorm_fn, [x, y], [w['mix_norm_post']], [(D, F32)])[0]
    sv.update(ycat=ycat, y=y, x_mid=xm)

    h2 = _rowwise("ffn_prenorm", _prenorm_fn, [xm], [w['ffn_norm_pre']], [(D, BF16)])[0]
    gate, up, act = _ffn_up("ffn_up", h2, w['ffn_wg'], w['ffn_wu'])
    f = _matmul("ffn_out", act, w['ffn_w_out'], 'nn', F32)
    xo = _rowwise("ffn_postnorm", _postnorm_fn, [xm, f], [w['ffn_norm_post']], [(D, F32)])[0]
    sv.update(gate=gate, up=up, act=act, f=f, h2=h2)
    return xo, sv


def _layer_bwd(dx, w, c, sv):
    L, D = dx.shape
    g = {}
    as_f32 = lambda t: (t.astype(F32),)
    df, g['ffn_norm_post'] = _rowwise_bwd("ffn_postnorm_b", lambda y, gn: _rms(y, gn), [sv['f']], [w['ffn_norm_post']],
                                          [dx], as_f32, [BF16], [True])
    dgate, dup = _ffn_down_bwd("ffn_out_bx", df, w['ffn_w_out'], sv['gate'], sv['up'])
    g['ffn_w_out'] = _matmul("ffn_out_bw", sv['act'], df, 'tn', F32)
    dh2 = _matmul("ffn_up_bx", dup, w['ffn_wu'], 'nt', F32, acc_in=_matmul("ffn_gate_bx", dgate, w['ffn_wg'], 'nt', F32))
    xm = sv['x_mid']
    g['ffn_wg'] = _matmul("ffn_gate_bw", sv['h2'], dgate, 'tn', F32)
    g['ffn_wu'] = _matmul("ffn_up_bw", sv['h2'], dup, 'tn', F32)
    dxm, g['ffn_norm_pre'] = _rowwise_bwd("ffn_prenorm_b", _prenorm_fn, [xm], [w['ffn_norm_pre']], [dh2], as_f32, [F32], [True],
                                          row_grad_add=[dx])

    dy, g['mix_norm_post'] = _rowwise_bwd("mix_postnorm_b", lambda y, gn: _rms(y, gn), [sv['y']], [w['mix_norm_post']],
                                          [dxm], as_f32, [BF16], [True])
    dycat = _matmul("mix_out_bx", dy, w['w_out'], 'nt', F32)
    g['w_out'] = _matmul("mix_out_bw", sv['ycat'], dy, 'tn', F32)
    p = sv['p']
    dya, dyb, dyc, dyd = (Rows(dycat, GROUP_WIDTH, k) for k in range(4))

    dof, dga, g['gla_norm'] = _rowwise_bwd("gla_post_b", _gla_post_fn, [sv['of'], sv['ob'], _pcol(p, P_GA)],
                                           [w['gla_norm'], c['seg']], [dya], as_f32, [F32, None, BF16], [True, False])
    la = sv['la']
    dqf, dkf, dvf, dlf, dqb_, dkb_, dvb_, dlb = _gla_scan_bwd(p, la, sv['sf'], sv['sb'], dof)
    dz, g['gla_wg'], g['gla_bg'] = _rowwise_bwd("gla_pre_b", _gla_pre_fn, [Rows(p, LANE, P_Z // LANE)], [w['gla_wg'], w['gla_bg']],
                                                [dlf, dlb], lambda a, b: (jnp.concatenate([a, b], axis=1),), [BF16], [True, True])

    btab = _rpb_expand(w['na_rpb'], c['rpb_tab'])
    dqn, dkn, dvn, dbt = _na_bwd(sv['p16'], btab, dycat)
    g['na_rpb'] = _rpb_contract(dbt, c['rpb_tab'])

    dh, dgc = _rowwise_bwd("lru_post_b", _lru_post_fn, [sv['hf'], sv['hb'], _pcol(p, P_GC)], [], [dyc], as_f32, [F32, None, BF16], [])
    lam0 = _lin_scan("lru_scan_f_b", sv['a0'], dh, True, coef_shift=-1)
    lam1 = _lin_scan("lru_scan_b_b", sv['a1'], dh, False, coef_shift=1)
    res = _rowwise_bwd("lru_pre_b", _lru_pre_fn, [_conv_taps(p)], [w[k] for k in LRU_PARAMS],
                       [lam0, lam1, Rows(sv['hf'], shifts=[1]), Rows(sv['hb'], shifts=[-1])],
                       lambda l0, l1, hfp, hbn: (l0 * hfp, l1 * hbn, l0, l1), [F32] * 4, [True] * len(LRU_PARAMS))
    dxs = res[:4]
    for k, nm in enumerate(LRU_PARAMS):
        g[nm] = res[4 + k]
    dxc = [Rows(dxs[j], shifts=[j - LRU_CONV_LEFT]) for j in range(LRU_CONV)]

    comb = _rowwise_bwd("dil_comb_b", _dil_comb_fn, sv['dil_o'] + sv['dil_l'], [], [dyd], as_f32, [BF16] * 3 + [F32] * 3, [])
    dqs, dks, dvs = [], [], []
    for k, (dil, radius, tq) in enumerate(_dil_branches(L)):
        n = L // dil
        dq_, dk_, dv_ = _band_bwd(f"dil_attn{dil}_b", *sv['dil_ops'][k], _dilate(comb[k], dil), _dilate(comb[3 + k], dil), tq, radius)
        dqs.append(_undilate(dq_))
        dks.append(_undilate(dk_[:, radius:radius + n]))
        dvs.append(_undilate(dv_[:, radius:radius + n]))
    dqd, dkd = _rowwise_bwd("rope_b", _rope_fn, [_pcol(p, P_QD), _pcol(p, P_KD), c['cos'], c['sin']], [], dqs + dks,
                            lambda *t: (sum(v.astype(F32) for v in t[:3]), sum(v.astype(F32) for v in t[3:])), [BF16, BF16, None, None], [])

    dp = _assemble("mix_dp", [[dqf, dqb_], [dkf, dkb_], [dvf, dvb_], [dga], [dqn], [dkn], [dvn], dxc, [dgc], [dqd], [dkd], dvs, [dz]], BF16)
    dh1 = _matmul("mix_proj_bx", dp, w['w_in'], 'nt', F32)
    x_in = sv['x_in']
    g['w_in'] = _matmul("mix_proj_bw", sv['h'], dp, 'tn', F32)
    dxi, g['mix_norm_pre'] = _rowwise_bwd("mix_prenorm_b", _prenorm_fn, [x_in], [w['mix_norm_pre']], [dh1], as_f32, [F32], [True],
                                          row_grad_add=[dxm])
    return dxi, g


def _loss_fn(y, t):
    e = y - t
    return e * (1.0 / y.shape[1]), jnp.sum(e * e, axis=0, keepdims=True)


def _gather_cols(name, shard, axis):
    half = shard.shape[0] // 2
    mine = lax.dynamic_slice_in_dim(shard, lax.axis_index("c") * half, half, axis=0).astype(BF16)
    both = _exchange(name + "_c", _exchange(name + "_xy", mine, 'xy', True), 'c', True)
    shp = list(shard.shape)
    shp[axis] *= 4
    return jnp.moveaxis(both, 1, axis + 1).reshape(shp)


def _pack(arrs, mult=64 * LANE):
    flat = jnp.concatenate([a.reshape(-1) for a in arrs])
    pad = (-flat.shape[0]) % mult
    return jnp.pad(flat, (0, pad)).reshape(-1, LANE)


def _unpack(buf, shapes):
    flat, out, k = buf.reshape(-1), [], 0
    for s in shapes:
        sz = int(np.prod(s))
        out.append(flat[k:k + sz].reshape(s))
        k += sz
    return out


def _perm_in(w_in):
    pad = jnp.zeros(w_in.shape[:-1] + (D_INP - D_IN,), w_in.dtype)
    return jnp.concatenate([w_in[..., :P_QB * GROUP_WIDTH], w_in[..., P_QB * GROUP_WIDTH + 2 * GLA_RANK:],
                            w_in[..., P_QB * GROUP_WIDTH:P_QB * GROUP_WIDTH + 2 * GLA_RANK], pad], axis=-1)


def _unperm_in(g):
    return jnp.concatenate([g[..., :P_QB * GROUP_WIDTH], g[..., P_Z:P_Z + 2 * GLA_RANK], g[..., P_QB * GROUP_WIDTH:P_Z]], axis=-1)


def _block_diag(wb):
    l = wb.shape[0]
    eye = jnp.eye(GROUP_HEADS, dtype=wb.dtype)
    return jnp.einsum('lehij,hg->lehigj', wb, eye).reshape(l, 2, GROUP_WIDTH, GROUP_WIDTH)


def _block_diag_grad(gw):
    l = gw.shape[0]
    g6 = gw.reshape(l, 2, GROUP_HEADS, HEAD_DIM, GROUP_HEADS, HEAD_DIM)
    return jnp.stack([g6[:, :, h, :, h, :] for h in range(GROUP_HEADS)], axis=2)


def _gate_matrix(wg):
    l = wg.shape[0]
    m = jnp.zeros((l, LANE, 2 * GROUP_WIDTH), wg.dtype)
    for e in range(2):
        m = m.at[:, e * GLA_RANK:(e + 1) * GLA_RANK, e * GROUP_WIDTH:(e + 1) * GROUP_WIDTH].set(wg[:, e])
    return m


def _gate_matrix_grad(gm):
    return jnp.stack([gm[:, e * GLA_RANK:(e + 1) * GLA_RANK, e * GROUP_WIDTH:(e + 1) * GROUP_WIDTH] for e in range(2)], axis=1)


def _adam_fn(w, g, m, v):
    m = ADAM_B1 * m + (1.0 - ADAM_B1) * g
    v = ADAM_B2 * v + (1.0 - ADAM_B2) * (g * g)
    m_hat = m / (1.0 - ADAM_B1 ** ADAM_STEP)
    v_hat = v / (1.0 - ADAM_B2 ** ADAM_STEP)
    return -ADAM_LR * (m_hat / (jnp.sqrt(v_hat) + ADAM_EPS) + ADAM_WD * w), m, v


def _adam(name, w, g, m, v):
    shp = w.shape
    two = lambda t: t.reshape(-1, shp[-1])
    res = _rowwise(name, _adam_fn, [two(w), two(g), two(m), two(v)], [], [(shp[-1], F32)] * 3)
    return [r.reshape(shp) for r in res]


def _local_step(x, target, fw):
    L, D = x.shape
    depth = fw['w_in'].shape[0]
    cs, sn = _rope_tables(L)
    consts = {'seg': _seg_matrix(), 'rpb_tab': _rpb_tables(), 'cos': cs, 'sin': sn}
    layer = lambda l: {k: v[l] for k, v in fw.items()}
    saved = []
    for l in range(depth):
        x, sv = _layer_fwd(x, layer(l), consts)
        saved.append(sv)
    dx, sq = _rowwise("loss", _loss_fn, [x, target], [], [(D, F32)], acc_outs=[(1, D)])
    grads = [None] * depth
    for l in reversed(range(depth)):
        dx, grads[l] = _layer_bwd(dx, layer(l), consts, saved[l])
    return sq, dx, grads


def kernel(x, mix_norm_pre, mix_norm_post, w_in, gla_w_gate, gla_b_gate, gla_norm, na_rpb, lru_conv_w, lru_conv_b, lru_w_a, lru_b_a, lru_w_x, lru_b_x, lru_lambda, w_out, ffn_norm_pre, ffn_norm_post, ffn_w_in, ffn_w_out, loss_target, m_mix_norm_pre, m_mix_norm_post, m_w_in, m_gla_w_gate, m_gla_b_gate, m_gla_norm, m_na_rpb, m_lru_conv_w, m_lru_conv_b, m_lru_w_a, m_lru_b_a, m_lru_w_x, m_lru_b_x, m_lru_lambda, m_w_out, m_ffn_norm_pre, m_ffn_norm_post, m_ffn_w_in, m_ffn_w_out, v_mix_norm_pre, v_mix_norm_post, v_w_in, v_gla_w_gate, v_gla_b_gate, v_gla_norm, v_na_rpb, v_lru_conv_w, v_lru_conv_b, v_lru_w_a, v_lru_b_a, v_lru_w_x, v_lru_b_x, v_lru_lambda, v_w_out, v_ffn_norm_pre, v_ffn_norm_post, v_ffn_w_in, v_ffn_w_out):
    args = (mix_norm_pre, mix_norm_post, w_in, gla_w_gate, gla_b_gate, gla_norm, na_rpb, lru_conv_w, lru_conv_b, lru_w_a, lru_b_a, lru_w_x, lru_b_x, lru_lambda, w_out, ffn_norm_pre, ffn_norm_post, ffn_w_in, ffn_w_out,
            m_mix_norm_pre, m_mix_norm_post, m_w_in, m_gla_w_gate, m_gla_b_gate, m_gla_norm, m_na_rpb, m_lru_conv_w, m_lru_conv_b, m_lru_w_a, m_lru_b_a, m_lru_w_x, m_lru_b_x, m_lru_lambda, m_w_out, m_ffn_norm_pre, m_ffn_norm_post, m_ffn_w_in, m_ffn_w_out,
            v_mix_norm_pre, v_mix_norm_post, v_w_in, v_gla_w_gate, v_gla_b_gate, v_gla_norm, v_na_rpb, v_lru_conv_w, v_lru_conv_b, v_lru_w_a, v_lru_b_a, v_lru_w_x, v_lru_b_x, v_lru_lambda, v_w_out, v_ffn_norm_pre, v_ffn_norm_post, v_ffn_w_in, v_ffn_w_out)
    nw = len(WEIGHTS)
    W = dict(zip(WEIGHTS, args[:nw]))
    M = dict(zip(WEIGHTS, args[nw:2 * nw]))
    V = dict(zip(WEIGHTS, args[2 * nw:]))
    chip = 2 * lax.axis_index("x") + lax.axis_index("y")

    full = dict(W)
    full['w_in'] = _gather_cols("ag_w_in", w_in, 2)
    full['ffn_w_in'] = _gather_cols("ag_ffn_w_in", ffn_w_in, 2)
    full['w_out'] = _gather_cols("ag_w_out", w_out, 1)
    full['ffn_w_out'] = _gather_cols("ag_ffn_w_out", ffn_w_out, 1)
    small = list(SMALL_SHARDED)
    got = _exchange("ag_small", _pack([W[k] for k in small]), 'xy', True)
    for k, parts in zip(small, zip(*[_unpack(got[j], [W[k].shape for k in small]) for j in range(4)])):
        ax = SMALL_SHARDED[k]
        stacked = jnp.moveaxis(jnp.stack(parts), 0, ax)
        shp = list(W[k].shape)
        shp[ax] *= 4
        full[k] = stacked.reshape(shp)

    sq, dx0, g = _local_step(x[0], loss_target[0], _layer_weights(full))
    loss = lax.psum(0.5 * jnp.sum(sq) / x.shape[-1], ("x", "y", "c"))
    gfull = _stored_grads(g)

    grad = {}
    for k, cut in _big_cuts(g).items():
        grad[k] = _reduce_big("rs_" + k, cut).reshape(W[k].shape)
    rest = [k for k in WEIGHTS if k not in BIG]
    allg = _exchange("ar_small", _pack([gfull[k] for k in rest]), 'xyc', True)
    summed = _unpack(_ordered_sum("ar_small_sum", allg), [gfull[k].shape for k in rest])
    for k, s in zip(rest, summed):
        if k in SMALL_SHARDED:
            ax = SMALL_SHARDED[k]
            n = W[k].shape[ax]
            s = lax.dynamic_slice_in_dim(s, chip * n, n, axis=ax)
        grad[k] = s

    delta, new_m, new_v = {}, {}, {}
    for k in BIG:
        delta[k], new_m[k], new_v[k] = _adam("adam_" + k, W[k], grad[k], M[k], V[k])
    shapes = [W[k].shape for k in rest]
    res = _rowwise("adam_small", _adam_fn, [_pack([d[k] for k in rest]) for d in (W, grad, M, V)], [], [(LANE, F32)] * 3)
    for d, r in zip((delta, new_m, new_v), res):
        for k, t in zip(rest, _unpack(r, shapes)):
            d[k] = t

    return (loss, dx0[None], *[grad[k] for k in WEIGHTS], *[delta[k] for k in WEIGHTS],
            *[new_m[k] for k in WEIGHTS], *[new_v[k] for k in WEIGHTS])


def _layer_weights(full):
    depth = full['w_in'].shape[0]
    dff = full['ffn_w_in'].shape[-1] // 2
    row = lambda t: t[:, None, :]
    fw = {
        'mix_norm_pre': row(full['mix_norm_pre']), 'mix_norm_post': row(full['mix_norm_post']),
        'ffn_norm_pre': row(full['ffn_norm_pre']), 'ffn_norm_post': row(full['ffn_norm_post']),
        'w_in': _perm_in(full['w_in']), 'w_out': full['w_out'],
        'ffn_wg': full['ffn_w_in'][..., :dff], 'ffn_wu': full['ffn_w_in'][..., dff:], 'ffn_w_out': full['ffn_w_out'],
        'gla_wg': _gate_matrix(full['gla_w_gate']), 'gla_bg': full['gla_b_gate'].reshape(depth, 1, 2 * GROUP_WIDTH),
        'gla_norm': row(full['gla_norm']), 'na_rpb': full['na_rpb'],
        'lru_cb': row(full['lru_conv_b']),
    }
    wa_bd, wx_bd = _block_diag(full['lru_w_a']), _block_diag(full['lru_w_x'])
    for j in range(LRU_CONV):
        fw[f'lru_cw{j}'] = row(full['lru_conv_w'][:, j])
    for e in range(2):
        fw[f'lru_wa{e}'], fw[f'lru_wx{e}'] = wa_bd[:, e], wx_bd[:, e]
        fw[f'lru_ba{e}'], fw[f'lru_bx{e}'] = row(full['lru_b_a'][:, e]), row(full['lru_b_x'][:, e])
        fw[f'lru_lam{e}'] = row(full['lru_lambda'][:, e])
    return fw


def _orig_cols(gp, lo, hi):
    split, zend = P_QB * GROUP_WIDTH, P_QB * GROUP_WIDTH + 2 * GLA_RANK
    parts = []
    for a, b, at in ((0, split, 0), (split, zend, P_Z), (zend, D_IN, split)):
        s, e = max(lo, a), min(hi, b)
        if s < e:
            parts.append(gp[..., at + s - a:at + e - a])
    return parts[0] if len(parts) == 1 else jnp.concatenate(parts, axis=-1)


def _big_cuts(grads):
    depth = len(grads)
    halves = (range(0, depth // 2), range(depth // 2, depth))
    n_in, n_ff = D_IN // 4, grads[0]['ffn_wg'].shape[1] // 2

    def build(piece):
        return jnp.stack([jnp.stack([jnp.concatenate([piece(l, j) for l in hl], axis=0) for j in range(4)]) for hl in halves])

    def rows_of(key):
        return lambda l, j: jnp.split(grads[l][key], 4, axis=0)[j]

    return {
        'w_in': build(lambda l, j: _orig_cols(grads[l]['w_in'], j * n_in, (j + 1) * n_in)),
        'w_out': build(rows_of('w_out')),
        'ffn_w_in': build(lambda l, j: grads[l]['ffn_wg' if j < 2 else 'ffn_wu'][:, (j % 2) * n_ff:(j % 2 + 1) * n_ff]),
        'ffn_w_out': build(rows_of('ffn_w_out')),
    }


def _stored_grads(grads):
    depth = len(grads)
    g = {k: jnp.stack([gl[k] for gl in grads]) for k in grads[0]}
    return {
        'mix_norm_pre': g['mix_norm_pre'][:, 0], 'mix_norm_post': g['mix_norm_post'][:, 0],
        'ffn_norm_pre': g['ffn_norm_pre'][:, 0], 'ffn_norm_post': g['ffn_norm_post'][:, 0],
        'w_in': _unperm_in(g['w_in']), 'w_out': g['w_out'],
        'ffn_w_in': jnp.concatenate([g['ffn_wg'], g['ffn_wu']], axis=-1), 'ffn_w_out': g['ffn_w_out'],
        'gla_w_gate': _gate_matrix_grad(g['gla_wg']), 'gla_b_gate': g['gla_bg'].reshape(depth, 2, GROUP_WIDTH),
        'gla_norm': g['gla_norm'][:, 0], 'na_rpb': g['na_rpb'],
        'lru_conv_w': jnp.stack([g[f'lru_cw{j}'][:, 0] for j in range(LRU_CONV)], axis=1), 'lru_conv_b': g['lru_cb'][:, 0],
        'lru_w_a': _block_diag_grad(jnp.stack([g['lru_wa0'], g['lru_wa1']], axis=1)),
        'lru_w_x': _block_diag_grad(jnp.stack([g['lru_wx0'], g['lru_wx1']], axis=1)),
        'lru_b_a': jnp.stack([g['lru_ba0'][:, 0], g['lru_ba1'][:, 0]], axis=1),
        'lru_b_x': jnp.stack([g['lru_bx0'][:, 0], g['lru_bx1'][:, 0]], axis=1),
        'lru_lambda': jnp.stack([g['lru_lam0'][:, 0], g['lru_lam1'][:, 0]], axis=1),
    }
```

```python
import numpy as np
import jax
import jax.numpy as jnp
from jax import lax
from jax.experimental import pallas as pl
from jax.experimental.pallas import tpu as pltpu

F32, BF16 = jnp.float32, jnp.bfloat16
HIGHEST = lax.Precision.HIGHEST
MESH = pl.DeviceIdType.MESH

HEAD_DIM = 64
GROUP_HEADS = 4
GROUP_WIDTH = GROUP_HEADS * HEAD_DIM
GLA_RANK = 16
GLA_TAU = 16.0
GLA_CHUNK = 64
GRID_W = 64
NA_ROWS = 8
NA_COLS = 16
LRU_CONV = 4
LRU_CONV_LEFT = 2
LRU_C = 8.0
DIL_PAIRS = ((128, 1), (512, 4), (2048, 16))
ROPE_THETA = 10000.0
EPS = 1e-6
ADAM_LR, ADAM_B1, ADAM_B2, ADAM_EPS, ADAM_WD, ADAM_STEP = 0.001, 0.9, 0.999, 1e-08, 0.01, 10
NEG = -1e30

LANE = 128
VMEM_LIMIT = 56 * 1024 * 1024
ROW_BUDGET = 16 * 1024 * 1024
DMA_PIECES = 8

P_QA, P_KA, P_VA, P_GA, P_QB, P_KB, P_VB, P_XC, P_GC, P_QD, P_KD, P_VD = range(12)
P_Z = 12 * GROUP_WIDTH
D_IN = 12 * GROUP_WIDTH + 2 * GLA_RANK
D_INP = 12 * GROUP_WIDTH + LANE

WEIGHTS = ['mix_norm_pre', 'mix_norm_post', 'w_in', 'gla_w_gate', 'gla_b_gate', 'gla_norm', 'na_rpb',
           'lru_conv_w', 'lru_conv_b', 'lru_w_a', 'lru_b_a', 'lru_w_x', 'lru_b_x', 'lru_lambda', 'w_out',
           'ffn_norm_pre', 'ffn_norm_post', 'ffn_w_in', 'ffn_w_out']
BIG = ('w_in', 'w_out', 'ffn_w_in', 'ffn_w_out')
SMALL_SHARDED = {'gla_w_gate': 3, 'gla_b_gate': 2, 'lru_conv_w': 2, 'lru_b_a': 2, 'lru_b_x': 2, 'lru_lambda': 2}
HEADS = [slice(h * HEAD_DIM, (h + 1) * HEAD_DIM) for h in range(GROUP_HEADS)]


def _cparams(sem=None):
    return pltpu.CompilerParams(dimension_semantics=sem, vmem_limit_bytes=VMEM_LIMIT)


def _tile(dim, target, mult=LANE):
    best = None
    for t in range(mult, min(dim, target) + 1, mult):
        if dim % t == 0:
            best = t
    return best or dim


class Rows:
    def __init__(self, a, w=None, cb=0, lead=None, shifts=None):
        self.a, self.cb, self.lead, self.shifts = a, cb, lead, shifts
        self.w = a.shape[-1] if w is None else w
        self.nrows = a.shape[-2]

    def spec(self, tm, ncol=1, halo=0):
        w = self.w // ncol
        per, last = tm // HALO, self.nrows // HALO - 1
        rows, row = (tm, lambda i: i) if halo == 0 else (HALO, lambda i: jnp.clip(i * per + (per if halo > 0 else -1), 0, last))
        if self.lead is None:
            return pl.BlockSpec((rows, w), lambda i, j, cb=self.cb: (row(i), cb * ncol + j))
        return pl.BlockSpec((None, rows, w), lambda i, j, cb=self.cb, k=self.lead: (k, row(i), cb * ncol + j))

    def nbytes(self):
        return self.w * self.a.dtype.itemsize


def _as_rows(rs):
    return [r if isinstance(r, Rows) else Rows(r) for r in rs]


HALO = 8


def _shift_tile(before, cur, after, k, t, nt):
    if k == 0:
        return cur
    tm = cur.shape[0]
    row = lax.broadcasted_iota(jnp.int32, before.shape, 0)
    if k > 0:
        moved = pltpu.roll(cur, k, 0)
        edge = jnp.where(row < k, jnp.where(t > 0, pltpu.roll(before, k, 0), 0.0), moved[0:HALO])
        return jnp.concatenate([edge, moved[HALO:]], axis=0)
    moved = pltpu.roll(cur, tm + k, 0)
    edge = jnp.where(row >= HALO + k, jnp.where(t < nt - 1, pltpu.roll(after, HALO + k, 0), 0.0), moved[tm - HALO:])
    return jnp.concatenate([moved[:tm - HALO], edge], axis=0)


def _operands(rows, tm, ncol):
    specs, arrs = [], []
    for r in rows:
        if r.shifts is None:
            specs.append(r.spec(tm, ncol))
            arrs.append(r.a)
        else:
            assert ncol == 1
            specs += [r.spec(tm, 1, side) for side in (-1, 0, 1)]
            arrs += [r.a] * 3

    def load(refs):
        vals, k = [], 0
        t, nt = pl.program_id(0), rows[0].nrows // tm
        for r in rows:
            if r.shifts is None:
                vals.append(refs[k][...])
                k += 1
            else:
                prev, cur, nxt = refs[k][...], refs[k + 1][...], refs[k + 2][...]
                vals += [_shift_tile(prev, cur, nxt, s, t, nt) for s in r.shifts]
                k += 3
        return vals

    return specs, arrs, load


def _expand(rows):
    return [r for r in rows for _ in (r.shifts or [0])]


def _pick_tm(nrows, row_bytes, scale, cap=512):
    tm = cap
    while tm > 16 and (tm * row_bytes * scale > ROW_BUDGET or nrows % tm):
        tm //= 2
    assert nrows % tm == 0, (nrows, tm)
    return tm


def _full_spec(a):
    nd = a.ndim
    return pl.BlockSpec(a.shape, lambda i, j, nd=nd: (0,) * nd)


def _rowwise(name, fn, rows, params, outs, acc_outs=(), ncol=1, tm_cap=512):
    rows = _as_rows(rows)
    nrows = rows[0].nrows
    assert ncol == 1 or not (acc_outs or params)
    tm = _pick_tm(nrows, (sum(r.nbytes() for r in rows) + sum(w * jnp.dtype(d).itemsize for w, d in outs)) // ncol, 2, tm_cap)
    specs, arrs, load = _operands(rows, tm, ncol)
    n_r, n_p, n_o = len(specs), len(params), len(outs)

    def body(*refs):
        vals = load(refs[:n_r]) + [r[...] for r in refs[n_r:n_r + n_p]]
        res = fn(*vals)
        res = res if isinstance(res, (tuple, list)) else (res,)
        orefs = refs[n_r + n_p:]
        for o, v in zip(orefs[:n_o], res[:n_o]):
            o[...] = v.astype(o.dtype)
        for o, v in zip(orefs[n_o:], res[n_o:]):
            @pl.when(pl.program_id(0) == 0)
            def _(o=o):
                o[...] = jnp.zeros_like(o)
            o[...] += v

    out_shape = [jax.ShapeDtypeStruct((nrows, w), d) for w, d in outs] + [jax.ShapeDtypeStruct(s, F32) for s in acc_outs]
    out_specs = [pl.BlockSpec((tm, w // ncol), lambda i, j: (i, j)) for w, _ in outs] + \
                [pl.BlockSpec(s, lambda i, j, nd=len(s): (0,) * nd) for s in acc_outs]
    return pl.pallas_call(
        body, name=name, grid=(nrows // tm, ncol),
        in_specs=specs + [_full_spec(p) for p in params],
        out_specs=out_specs, out_shape=out_shape,
        compiler_params=_cparams(("arbitrary", "arbitrary") if acc_outs else ("parallel", "parallel")),
    )(*arrs, *params)


def _rowwise_bwd(name, fn, rows, params, ct_rows, ct_fn, row_grads, param_grads, row_grad_add=None, ncol=1):
    rows, ct_rows = _as_rows(rows), _as_rows(ct_rows)
    nrows = rows[0].nrows
    n_rg = sum(d is not None for d in row_grads)
    adds = _as_rows([a for a in (row_grad_add or []) if a is not None])
    add_at = [k for k, a in enumerate(row_grad_add or []) if a is not None]
    assert ncol == 1 or not (params or adds)
    seen = _expand(rows)
    gbytes = sum(r.w * jnp.dtype(d).itemsize for r, d in zip(seen, row_grads) if d is not None)
    tm = _pick_tm(nrows, (sum(r.nbytes() for r in rows + ct_rows + adds) + gbytes) // ncol, 4)
    r_specs, r_arrs, r_load = _operands(rows, tm, ncol)
    c_specs, c_arrs, c_load = _operands(ct_rows, tm, ncol)
    a_specs, a_arrs, a_load = _operands(adds, tm, ncol)
    n_r, n_p, n_c, n_a = len(r_specs), len(params), len(c_specs), len(a_specs)
    diff = [k for k, d in enumerate(row_grads) if d is not None] + [len(seen) + k for k, g in enumerate(param_grads) if g]

    def body(*refs):
        vals = r_load(refs[:n_r]) + [r[...] for r in refs[n_r:n_r + n_p]]
        cts_in = c_load(refs[n_r + n_p:n_r + n_p + n_c])
        add_in = a_load(refs[n_r + n_p + n_c:n_r + n_p + n_c + n_a]) if n_a else []
        orefs = refs[n_r + n_p + n_c + n_a:]

        def f(*dv):
            full = list(vals)
            for k, v in zip(diff, dv):
                full[k] = v
            res = fn(*full)
            return tuple(res) if isinstance(res, (tuple, list)) else (res,)

        outs, vjp = jax.vjp(f, *[vals[k].astype(F32) for k in diff])
        cts = ct_fn(*cts_in)
        cts = cts if isinstance(cts, (tuple, list)) else (cts,)
        grads = list(vjp(tuple(c.astype(o.dtype) for c, o in zip(cts, outs))))
        for k, a in zip(add_at, add_in):
            grads[k] = grads[k] + a.astype(F32)
        for o, g in zip(orefs[:n_rg], grads[:n_rg]):
            o[...] = g.astype(o.dtype)
        for o, g in zip(orefs[n_rg:], grads[n_rg:]):
            @pl.when(pl.program_id(0) == 0)
            def _(o=o):
                o[...] = jnp.zeros_like(o)
            o[...] += g.astype(F32)

    out_shape = [jax.ShapeDtypeStruct((nrows, r.w), d) for r, d in zip(seen, row_grads) if d is not None] + \
                [jax.ShapeDtypeStruct(p.shape, F32) for p, g in zip(params, param_grads) if g]
    out_specs = [pl.BlockSpec((tm, r.w // ncol), lambda i, j: (i, j)) for r, d in zip(seen, row_grads) if d is not None] + \
                [_full_spec(p) for p, g in zip(params, param_grads) if g]
    return pl.pallas_call(
        body, name=name, grid=(nrows // tm, ncol),
        in_specs=r_specs + [_full_spec(p) for p in params] + c_specs + a_specs,
        out_specs=out_specs, out_shape=out_shape,
        compiler_params=_cparams(("arbitrary", "arbitrary")),
    )(*r_arrs, *params, *c_arrs, *a_arrs)


def _assemble(name, groups, dtype):
    sizes = [len(g) for g in groups]
    flat = [a for g in groups for a in g]

    def fn(*tiles):
        out, k = [], 0
        for s in sizes:
            acc = tiles[k].astype(F32)
            for t in tiles[k + 1:k + s]:
                acc = acc + t.astype(F32)
            out.append(acc.astype(dtype))
            k += s
        return out[0] if len(out) == 1 else jnp.concatenate(out, axis=1)

    width = sum(g[0].shape[-1] if not isinstance(g[0], Rows) else g[0].w for g in groups)
    return _rowwise(name, fn, flat, [], [(width, dtype)])[0]


def _matmul(name, a, b, mode, out_dtype, acc_in=None, also=None):
    if mode == 'nn':
        (M, K), N = a.shape, b.shape[1]
    elif mode == 'nt':
        (M, K), N = a.shape, b.shape[0]
    else:
        (K, M), N = a.shape, b.shape[1]
    tm, tn, tk = _tile(M, 1536), _tile(N, 1536), _tile(K, 2048 if mode == 'tn' else 3328)
    nk = K // tk
    dn = {'nn': NN, 'nt': NT, 'tn': TN}[mode]
    has_acc = acc_in is not None

    n_in = 3 if has_acc else 2
    dtypes = [out_dtype] + ([also] if also is not None else [])

    def body(*refs):
        a_ref, b_ref = refs[:2]
        o_refs = refs[n_in:n_in + len(dtypes)]
        part = lax.dot_general(a_ref[...].astype(BF16), b_ref[...].astype(BF16), dn, preferred_element_type=F32)
        if nk == 1:
            val = (refs[2][...] + part) if has_acc else part
            for o_ref in o_refs:
                o_ref[...] = val.astype(o_ref.dtype)
            return
        acc = refs[-1]

        @pl.when(pl.program_id(2) == 0)
        def _():
            acc[...] = refs[2][...] if has_acc else jnp.zeros_like(acc)
        acc[...] += part

        @pl.when(pl.program_id(2) == nk - 1)
        def _():
            for o_ref in o_refs:
                o_ref[...] = acc[...].astype(o_ref.dtype)

    a_spec = pl.BlockSpec((tk, tm), lambda i, j, k: (k, i)) if mode == 'tn' else pl.BlockSpec((tm, tk), lambda i, j, k: (i, k))
    b_spec = pl.BlockSpec((tn, tk), lambda i, j, k: (j, k)) if mode == 'nt' else pl.BlockSpec((tk, tn), lambda i, j, k: (k, j))
    o_spec = pl.BlockSpec((tm, tn), lambda i, j, k: (i, j))
    res = pl.pallas_call(
        body, name=name, grid=(M // tm, N // tn, nk),
        in_specs=[a_spec, b_spec] + ([o_spec] if has_acc else []), out_specs=[o_spec] * len(dtypes),
        out_shape=[jax.ShapeDtypeStruct((M, N), d) for d in dtypes],
        scratch_shapes=[] if nk == 1 else [pltpu.VMEM((tm, tn), F32)],
        compiler_params=_cparams(("parallel", "parallel", "arbitrary")),
    )(a, b, *([acc_in] if has_acc else []))
    return res[0] if also is None else res


def _ffn_up(name, h, wg, wu):
    (M, K), N = h.shape, wg.shape[1]
    tm, tn = _tile(M, 512), _tile(N, 1536)

    def body(h_ref, g_ref, u_ref, gate_ref, up_ref, act_ref):
        a = h_ref[...].astype(BF16)
        gate = _bdot(a, g_ref[...]).astype(BF16)
        up = _bdot(a, u_ref[...]).astype(BF16)
        gate_ref[...], up_ref[...] = gate, up
        act_ref[...] = _swiglu_fn(gate, up).astype(BF16)

    w_spec = pl.BlockSpec((K, tn), lambda i, j: (0, j))
    o_spec = pl.BlockSpec((tm, tn), lambda i, j: (i, j))
    return pl.pallas_call(
        body, name=name, grid=(M // tm, N // tn),
        in_specs=[pl.BlockSpec((tm, K), lambda i, j: (i, 0)), w_spec, w_spec], out_specs=[o_spec] * 3,
        out_shape=[jax.ShapeDtypeStruct((M, N), BF16)] * 3,
        compiler_params=_cparams(("parallel", "parallel")),
    )(h, wg, wu)


def _ffn_down_bwd(name, df, w_out, gate, up):
    (M, K), N = df.shape, w_out.shape[0]
    tm, tn = _tile(M, 512), _tile(N, 1536)

    def body(d_ref, w_ref, gate_ref, up_ref, dg_ref, du_ref):
        dact = _bdot(d_ref[...], w_ref[...], NT)
        dg, du = _swiglu_bwd((gate_ref[...], up_ref[...]), dact)
        dg_ref[...], du_ref[...] = dg, du

    o_spec = pl.BlockSpec((tm, tn), lambda i, j: (i, j))
    return pl.pallas_call(
        body, name=name, grid=(M // tm, N // tn),
        in_specs=[pl.BlockSpec((tm, K), lambda i, j: (i, 0)), pl.BlockSpec((tn, K), lambda i, j: (j, 0)), o_spec, o_spec],
        out_specs=[o_spec] * 2, out_shape=[jax.ShapeDtypeStruct((M, N), BF16)] * 2,
        compiler_params=_cparams(("parallel", "parallel")),
    )(df, w_out, gate, up)


def _small_dot(name, a, b, mode):
    dn = {'nn': (((1,), (0,)), ((), ())), 'nt': (((1,), (1,)), ((), ()))}[mode]
    M = a.shape[0]
    N = b.shape[1] if mode == 'nn' else b.shape[0]

    def body(a_ref, b_ref, o_ref):
        o_ref[...] = lax.dot_general(a_ref[...], b_ref[...], dn, precision=HIGHEST, preferred_element_type=F32)

    return pl.pallas_call(body, name=name, out_shape=jax.ShapeDtypeStruct((M, N), F32),
                          compiler_params=pltpu.CompilerParams(vmem_limit_bytes=VMEM_LIMIT))(a, b)


NN, NT, TN = (((1,), (0,)), ((), ())), (((1,), (1,)), ((), ())), (((0,), (0,)), ((), ()))


def _bdot(a, b, dn=NN):
    return lax.dot_general(a.astype(BF16), b.astype(BF16), dn, preferred_element_type=F32)


def _sigmoid(x):
    return 0.5 * jnp.tanh(0.5 * x) + 0.5


def _silu(x):
    return x * _sigmoid(x)


def _softplus(x):
    return jnp.maximum(x, 0.0) + jnp.log(1.0 + jnp.exp(-jnp.abs(x)))


def _gelu(x):
    return 0.5 * x * (1.0 + jnp.tanh(0.7978845608028654 * (x + 0.044715 * (x * x * x))))


def _rms(x, g):
    return x * lax.rsqrt(jnp.mean(x * x, axis=-1, keepdims=True) + EPS) * g


def _prenorm_fn(x, g):
    return _rms(x, g)


def _postnorm_fn(x, y, g):
    return x + _rms(y, g)


@jax.custom_vjp
def _swiglu_fn(gate, up):
    return _silu(gate.astype(F32)) * up.astype(F32)


def _swiglu_bwd(res, ct):
    g, u = res[0].astype(F32), res[1].astype(F32)
    s = _sigmoid(g)
    gs = g * s
    return (ct * u * (s + gs - gs * s)).astype(res[0].dtype), (ct * gs).astype(res[1].dtype)


_swiglu_fn.defvjp(lambda gate, up: (_swiglu_fn(gate, up), (gate, up)), _swiglu_bwd)


def _gla_pre_fn(z, wg, bg):
    logit = _bdot(z, wg) + bg
    return -_softplus(-logit) * (1.0 / GLA_TAU)


def _seg_mean(x, seg):
    return lax.dot_general(x, seg, NN, precision=HIGHEST, preferred_element_type=F32)


def _gla_post_fn(of, ob, g, norm, seg):
    o = of + ob
    o = o * lax.rsqrt(_seg_mean(o * o, seg) + EPS) * norm
    return o * _silu(g)


@jax.custom_vjp
def _swap_halves(x):
    n = x.shape[-1]
    lane = lax.broadcasted_iota(jnp.int32, x.shape, x.ndim - 1)
    lo = (lane & (HEAD_DIM - 1)) < HEAD_DIM // 2
    return jnp.where(lo, pltpu.roll(x, n - HEAD_DIM // 2, x.ndim - 1), pltpu.roll(x, HEAD_DIM // 2, x.ndim - 1))


_swap_halves.defvjp(lambda x: (_swap_halves(x), None), lambda _, g: (_swap_halves(g),))


def _rope_fn(q, k, cs, sn):
    return q * cs + _swap_halves(q) * sn, k * cs + _swap_halves(k) * sn


def _dil_comb_fn(o1, o2, o3, l1, l2, l3):
    m = jnp.maximum(jnp.maximum(l1, l2), l3)
    e1, e2, e3 = jnp.exp(l1 - m), jnp.exp(l2 - m), jnp.exp(l3 - m)
    return (e1 * o1 + e2 * o2 + e3 * o3) / (e1 + e2 + e3)


LRU_PARAMS = ['lru_cw0', 'lru_cw1', 'lru_cw2', 'lru_cw3', 'lru_cb', 'lru_wa0', 'lru_wa1', 'lru_ba0', 'lru_ba1',
              'lru_wx0', 'lru_wx1', 'lru_bx0', 'lru_bx1', 'lru_lam0', 'lru_lam1']


def _lru_pre_fn(x0, x1, x2, x3, cw0, cw1, cw2, cw3, cb, wa0, wa1, ba0, ba1, wx0, wx1, bx0, bx1, lam0, lam1):
    xc = cb + x0 * cw0 + x1 * cw1 + x2 * cw2 + x3 * cw3
    outs = []
    for wa, ba, wx, bx, lam in ((wa0, ba0, wx0, bx0, lam0), (wa1, ba1, wx1, bx1, lam1)):
        r = _sigmoid(_bdot(xc, wa) + ba)
        i = _sigmoid(_bdot(xc, wx) + bx)
        log_a = -LRU_C * r * _softplus(-lam)
        a = jnp.exp(log_a)
        u = jnp.sqrt(-jnp.tanh(log_a) * (a * a + 1.0)) * (i * xc)
        outs += [a, u]
    return outs[0], outs[2], outs[1], outs[3]


def _lru_post_fn(hf, hb, gate):
    return (hf + hb) * _gelu(gate)


def _mix_post_fn(of, ob, ga, yb, hf, hb, gc, o1, o2, o3, l1, l2, l3, norm, seg):
    ya = _gla_post_fn(of, ob, ga, norm, seg)
    yc = _lru_post_fn(hf, hb, gc)
    yd = _dil_comb_fn(o1, o2, o3, l1, l2, l3)
    return jnp.concatenate([ya.astype(BF16), yb.astype(BF16), yc.astype(BF16), yd.astype(BF16)], axis=1)


def _attn_heads(qs, kws, vws, biases):
    ss = [_bdot(q, kw, NT) * (HEAD_DIM ** -0.5) + b for q, kw, b in zip(qs, kws, biases)]
    ms = [lax.stop_gradient(jnp.max(s, axis=-1, keepdims=True)) for s in ss]
    es = [jnp.exp(s - m) for s, m in zip(ss, ms)]
    dens = [jnp.sum(e, axis=-1, keepdims=True) for e in es]
    ps = [e * (1.0 / d) for e, d in zip(es, dens)]
    os_ = [_bdot(p_, vw) for p_, vw in zip(ps, vws)]
    return os_, [m + jnp.log(d) for m, d in zip(ms, dens)]


def _cumsum_rows(x, rev):
    n = x.shape[0]
    row = lax.broadcasted_iota(jnp.int32, x.shape, 0)
    s = 1
    while s < n:
        if rev:
            x = x + jnp.where(row < n - s, pltpu.roll(x, n - s, 0), 0.0)
        else:
            x = x + jnp.where(row >= s, pltpu.roll(x, s, 0), 0.0)
        s *= 2
    return x


def _gla_chunks(qs, ks, vs, bs, sts, revs):
    C = qs[0].shape[0]
    ti = lax.broadcasted_iota(jnp.int32, (C, C), 0)
    si = lax.broadcasted_iota(jnp.int32, (C, C), 1)
    row = lax.broadcasted_iota(jnp.int32, (C, 1), 0)
    incl = {False: si <= ti, True: si >= ti}
    last = {False: row == C - 1, True: row == 0}
    mid = {False: row == C // 2 - 1, True: row == C // 2}
    bls = [jnp.sum(jnp.where(last[r], b, 0.0), axis=0, keepdims=True) for b, r in zip(bs, revs)]
    bms = [jnp.sum(jnp.where(mid[r], b, 0.0), axis=0, keepdims=True) for b, r in zip(bs, revs)]
    qss = [q * (HEAD_DIM ** -0.5) for q in qs]
    qi = [q * jnp.exp(b - bm) for q, b, bm in zip(qss, bs, bms)]
    ki = [k * jnp.exp(bm - b) for k, b, bm in zip(ks, bs, bms)]
    atts = [jnp.where(incl[r], _bdot(a, b, NT), 0.0) for a, b, r in zip(qi, ki, revs)]
    qe = [q * jnp.exp(b) for q, b in zip(qss, bs)]
    kl = [k * jnp.exp(bl - b) for k, b, bl in zip(ks, bs, bls)]
    o1 = [_bdot(a, v) for a, v in zip(atts, vs)]
    o2 = [_bdot(q, st, NT) for q, st in zip(qe, sts)]
    kvs = [_bdot(v, k, TN) for v, k in zip(vs, kl)]
    return [a + b for a, b in zip(o1, o2)], [st * jnp.exp(bl) + kv for st, bl, kv in zip(sts, bls, kvs)]


def _gla_specs(n, blocks, first):
    C = GLA_CHUNK
    at = (lambda i: i) if first else (lambda i: n - 1 - i)
    return [pl.BlockSpec((C, GROUP_WIDTH), lambda i, b=b: (at(i), b)) for b in blocks], at


def _gla_scan_fwd(p, la):
    L = p.shape[0]
    C, H, dh = GLA_CHUNK, GROUP_HEADS, HEAD_DIM
    n = L // C
    f_specs, f_at = _gla_specs(n, (P_QA, P_KA, P_VA), True)
    b_specs, b_at = _gla_specs(n, (P_QA, P_KA, P_VA), False)
    tile = lambda at, blk=0: pl.BlockSpec((C, GROUP_WIDTH), lambda i: (at(i), blk))
    st_spec = lambda at: pl.BlockSpec((None, H, dh, dh), lambda i: (at(i), 0, 0, 0))

    def body(qf, kf, vf, lf, qb, kb, vb, lb, of_ref, ob_ref, sf_ref, sb_ref, stf, stb):
        @pl.when(pl.program_id(0) == 0)
        def _():
            stf[...] = jnp.zeros_like(stf)
            stb[...] = jnp.zeros_like(stb)
        sf_ref[...] = stf[...]
        sb_ref[...] = stb[...]
        chains = [(t, h, sl) for t in ((qf, kf, vf, _cumsum_rows(lf[...], False), of_ref, stf, False),
                                       (qb, kb, vb, _cumsum_rows(lb[...], True), ob_ref, stb, True))
                  for h, sl in enumerate(HEADS)]
        os_, sts = _gla_chunks(*[[t[j][:, sl] for t, h, sl in chains] for j in range(4)],
                               [t[5][h] for t, h, sl in chains], [t[6] for t, h, sl in chains])
        for (t, h, sl), o, st_new in zip(chains, os_, sts):
            t[4][:, sl] = o
            t[5][h] = st_new

    return pl.pallas_call(
        body, name="gla_scan", grid=(n,),
        in_specs=f_specs + [tile(f_at, 0)] + b_specs + [tile(b_at, 1)],
        out_specs=[tile(f_at), tile(b_at), st_spec(f_at), st_spec(b_at)],
        out_shape=[jax.ShapeDtypeStruct((L, GROUP_WIDTH), F32)] * 2 + [jax.ShapeDtypeStruct((n, H, dh, dh), F32)] * 2,
        scratch_shapes=[pltpu.VMEM((H, dh, dh), F32)] * 2,
        compiler_params=_cparams(("arbitrary",)),
    )(p, p, p, la, p, p, p, la)


def _gla_scan_bwd(p, la, sf, sb, do):
    L = p.shape[0]
    C, H, dh = GLA_CHUNK, GROUP_HEADS, HEAD_DIM
    n = L // C
    f_specs, f_at = _gla_specs(n, (P_QA, P_KA, P_VA), False)
    b_specs, b_at = _gla_specs(n, (P_QA, P_KA, P_VA), True)
    tile = lambda at, blk=0: pl.BlockSpec((C, GROUP_WIDTH), lambda i: (at(i), blk))
    st_spec = lambda at: pl.BlockSpec((None, H, dh, dh), lambda i: (at(i), 0, 0, 0))

    def body(qf, kf, vf, lf, spf, dof, qb, kb, vb, lb, spb, dob, *rest):
        outs_f, outs_b, (dstf, dstb) = rest[0:4], rest[4:8], rest[8:10]

        @pl.when(pl.program_id(0) == 0)
        def _():
            dstf[...] = jnp.zeros_like(dstf)
            dstb[...] = jnp.zeros_like(dstb)
        chains = [(t, h, sl) for t in ((qf, kf, vf, _cumsum_rows(lf[...], False), spf, dof, outs_f, dstf, False),
                                       (qb, kb, vb, _cumsum_rows(lb[...], True), spb, dob, outs_b, dstb, True))
                  for h, sl in enumerate(HEADS)]
        nc = len(chains)
        revs = [t[8] for t, h, sl in chains]
        flat = [t[j][:, sl] for j in range(4) for t, h, sl in chains] + [t[4][h] for t, h, sl in chains]

        def f(*a):
            os_, sts = _gla_chunks(*[list(a[j * nc:(j + 1) * nc]) for j in range(5)], revs)
            return tuple(os_) + tuple(sts)

        _, vjp = jax.vjp(f, *flat)
        grads = vjp(tuple(t[5][:, sl] for t, h, sl in chains) + tuple(t[7][h] for t, h, sl in chains))
        for c_, (t, h, sl) in enumerate(chains):
            for j in range(4):
                t[6][j][:, sl] = grads[j * nc + c_].astype(t[6][j].dtype)
            t[7][h] = grads[4 * nc + c_]
        for outs, rev in ((outs_f, False), (outs_b, True)):
            outs[3][...] = _cumsum_rows(outs[3][...], not rev)

    return pl.pallas_call(
        body, name="gla_scan_b", grid=(n,),
        in_specs=f_specs + [tile(f_at, 0), st_spec(f_at), tile(f_at)] + b_specs + [tile(b_at, 1), st_spec(b_at), tile(b_at)],
        out_specs=[tile(f_at)] * 4 + [tile(b_at)] * 4,
        out_shape=[jax.ShapeDtypeStruct((L, GROUP_WIDTH), d) for d in (BF16, BF16, BF16, F32)] * 2,
        scratch_shapes=[pltpu.VMEM((H, dh, dh), F32)] * 2,
        compiler_params=_cparams(("arbitrary",)),
    )(p, p, p, la, sf, do, p, p, p, la, sb, do)


NA_W = NA_ROWS * GRID_W
NA_BW = (2 * NA_ROWS - 1) * GRID_W


def _na_start(i, rows):
    return jnp.clip(i - NA_ROWS // 2, 0, rows - NA_ROWS)


def _na_fwd(p16, btab):
    L = p16.shape[0]
    rows = L // GRID_W
    kv = lambda blk: pl.BlockSpec((L, GROUP_WIDTH), lambda i: (0, blk))

    def body(q_ref, k_ref, v_ref, b_ref, o_ref):
        r = pl.program_id(0)
        s = _na_start(r, rows)
        start = pl.multiple_of(s * GRID_W, GRID_W)
        os_, _ = _attn_heads([q_ref[:, sl] for sl in HEADS], [k_ref[pl.ds(start, NA_W), sl] for sl in HEADS],
                             [v_ref[pl.ds(start, NA_W), sl] for sl in HEADS],
                             [b_ref[s - r + NA_ROWS - 1, h] for h in range(GROUP_HEADS)])
        for sl, o in zip(HEADS, os_):
            o_ref[:, sl] = o.astype(o_ref.dtype)

    whole = lambda a: pl.BlockSpec(a.shape, lambda i, nd=a.ndim: (0,) * nd)
    return pl.pallas_call(
        body, name="na_attn", grid=(rows,),
        in_specs=[pl.BlockSpec((GRID_W, GROUP_WIDTH), lambda i: (i, P_QB)), kv(P_KB), kv(P_VB), whole(btab)],
        out_specs=pl.BlockSpec((GRID_W, GROUP_WIDTH), lambda i: (i, 0)),
        out_shape=jax.ShapeDtypeStruct((L, GROUP_WIDTH), BF16),
        compiler_params=_cparams(("arbitrary",)),
    )(p16, p16, p16, btab)


def _na_bwd(p16, btab, dycat):
    L = p16.shape[0]
    rows = L // GRID_W
    kv = lambda blk: pl.BlockSpec((L, GROUP_WIDTH), lambda i: (0, blk))
    flush = NA_ROWS - 1
    emit = lambda i: jnp.where(i < rows, _na_start(i, rows), i - flush)

    def body(q_ref, k_ref, v_ref, b_ref, do_ref, dq_ref, dk_ref, dv_ref, db_ref, acc_k, acc_v):
        i = pl.program_id(0)

        @pl.when(i == 0)
        def _():
            acc_k[...] = jnp.zeros_like(acc_k)
            acc_v[...] = jnp.zeros_like(acc_v)
            db_ref[...] = jnp.zeros_like(db_ref)

        @pl.when((i > 0) & (emit(i) != emit(i - 1)))
        def _():
            for acc in (acc_k, acc_v):
                moved = acc[GRID_W:NA_W, :]
                acc[0:NA_W - GRID_W, :] = moved
                acc[NA_W - GRID_W:NA_W, :] = jnp.zeros((GRID_W, GROUP_WIDTH), F32)

        @pl.when(i < rows)
        def _():
            s = _na_start(i, rows)
            sv = s - i + NA_ROWS - 1
            start = pl.multiple_of(s * GRID_W, GRID_W)
            H = GROUP_HEADS
            flat = [q_ref[:, sl].astype(F32) for sl in HEADS] + [k_ref[pl.ds(start, NA_W), sl].astype(F32) for sl in HEADS] + \
                   [v_ref[pl.ds(start, NA_W), sl].astype(F32) for sl in HEADS] + [b_ref[sv, h] for h in range(H)]

            def f(*a):
                os_, lses = _attn_heads(a[0:H], a[H:2 * H], a[2 * H:3 * H], a[3 * H:4 * H])
                return tuple(os_) + tuple(lses)

            _, vjp = jax.vjp(f, *flat)
            grads = vjp(tuple(do_ref[:, sl] for sl in HEADS) + (jnp.zeros((GRID_W, 1), F32),) * H)
            for h, sl in enumerate(HEADS):
                dq_ref[:, sl] = grads[h].astype(dq_ref.dtype)
                acc_k[:, sl] += grads[H + h]
                acc_v[:, sl] += grads[2 * H + h]
                db_ref[sv, h] += grads[3 * H + h]

        dk_ref[...] = acc_k[0:GRID_W, :].astype(dk_ref.dtype)
        dv_ref[...] = acc_v[0:GRID_W, :].astype(dv_ref.dtype)

    whole = lambda a: pl.BlockSpec(a.shape, lambda i, nd=a.ndim: (0,) * nd)
    qrow = lambda blk: pl.BlockSpec((GRID_W, GROUP_WIDTH), lambda i: (jnp.minimum(i, rows - 1), blk))
    erow = pl.BlockSpec((GRID_W, GROUP_WIDTH), lambda i: (emit(i), 0))
    return pl.pallas_call(
        body, name="na_attn_b", grid=(rows + flush,),
        in_specs=[qrow(P_QB), kv(P_KB), kv(P_VB), whole(btab), qrow(1)],
        out_specs=[qrow(0), erow, erow, whole(btab)],
        out_shape=[jax.ShapeDtypeStruct((L, GROUP_WIDTH), BF16)] * 3 + [jax.ShapeDtypeStruct(btab.shape, F32)],
        scratch_shapes=[pltpu.VMEM((NA_W, GROUP_WIDTH), F32)] * 2,
        compiler_params=_cparams(("arbitrary",)),
    )(p16, p16, p16, btab, dycat)


def _na_col_ok():
    qc = np.arange(GRID_W)[:, None]
    kc = (np.arange(NA_W) % GRID_W)[None, :]
    c0 = np.clip(qc - NA_COLS // 2, 0, GRID_W - NA_COLS)
    return (kc >= c0) & (kc < c0 + NA_COLS)


def _rpb_tables():
    c = np.arange(GRID_W)
    dc = np.clip(c[None, :] - c[:, None], -(NA_COLS - 1), NA_COLS - 1) + NA_COLS - 1
    t = np.zeros((2 * NA_COLS - 1, GRID_W, GRID_W), np.float32)
    t[dc, c[:, None], c[None, :]] = 1.0
    return jnp.asarray(t.reshape(2 * NA_COLS - 1, GRID_W * GRID_W))


def _rpb_expand(rpb, tab):
    H = rpb.shape[0]
    xt = _small_dot("na_bias", rpb.reshape(H * (2 * NA_ROWS - 1), 2 * NA_COLS - 1), tab, 'nn')
    b15 = xt.reshape(H, 2 * NA_ROWS - 1, GRID_W, GRID_W).transpose(0, 2, 1, 3).reshape(H, GRID_W, NA_BW)
    ok = jnp.asarray(_na_col_ok())
    return jnp.stack([jnp.where(ok, b15[:, :, sv * GRID_W:sv * GRID_W + NA_W], NEG) for sv in range(NA_ROWS)])


def _rpb_contract(dbv, tab):
    H = dbv.shape[1]
    db = sum(jnp.pad(dbv[sv], ((0, 0), (0, 0), (sv * GRID_W, NA_BW - NA_W - sv * GRID_W))) for sv in range(NA_ROWS))
    dx = db.reshape(H, GRID_W, 2 * NA_ROWS - 1, GRID_W).transpose(0, 2, 1, 3).reshape(H * (2 * NA_ROWS - 1), GRID_W * GRID_W)
    return _small_dot("na_bias_b", dx, tab, 'nt').reshape(H, 2 * NA_ROWS - 1, 2 * NA_COLS - 1)


def _band_bias(i, tq, w, halo, n):
    a = lax.broadcasted_iota(jnp.int32, (tq, w), 0)
    b = lax.broadcasted_iota(jnp.int32, (tq, w), 1)
    kpos = i * tq - halo + b
    d = b - halo - a
    return jnp.where((d <= halo) & (d >= -halo) & (kpos >= 0) & (kpos < n), 0.0, NEG)


def _band_fwd(name, q, kp, vp, tq, halo):
    G, n, _ = q.shape
    w = tq + 2 * halo

    def body(q_ref, k_ref, v_ref, o_ref, l_ref):
        i = pl.program_id(1)
        start = pl.multiple_of(i * tq, tq)
        bias = _band_bias(i, tq, w, halo, n)
        os_, lses = _attn_heads([q_ref[:, sl] for sl in HEADS], [k_ref[pl.ds(start, w), sl] for sl in HEADS],
                                [v_ref[pl.ds(start, w), sl] for sl in HEADS], [bias] * GROUP_HEADS)
        for sl, o, lse in zip(HEADS, os_, lses):
            o_ref[:, sl] = o.astype(o_ref.dtype)
            l_ref[:, sl] = jnp.broadcast_to(lse, (tq, HEAD_DIM))

    qblk = pl.BlockSpec((None, tq, GROUP_WIDTH), lambda g, i: (g, i, 0))
    kblk = pl.BlockSpec((None, n + 2 * halo, GROUP_WIDTH), lambda g, i: (g, 0, 0))
    return pl.pallas_call(
        body, name=name, grid=(G, n // tq), in_specs=[qblk, kblk, kblk], out_specs=[qblk, qblk],
        out_shape=[jax.ShapeDtypeStruct((G, n, GROUP_WIDTH), d) for d in (BF16, F32)],
        compiler_params=_cparams(("parallel", "arbitrary")),
    )(q, kp, vp)


def _band_bwd(name, q, kp, vp, do, dl, tq, halo):
    G, n, _ = q.shape
    w = tq + 2 * halo
    nq = n // tq

    def body(q_ref, k_ref, v_ref, do_ref, dl_ref, dq_ref, dk_ref, dv_ref, acc_k, acc_v):
        i = pl.program_id(1)

        @pl.when(i == 0)
        def _():
            acc_k[...] = jnp.zeros_like(acc_k)
            acc_v[...] = jnp.zeros_like(acc_v)

        @pl.when(i > 0)
        def _():
            for acc in (acc_k, acc_v):
                moved = acc[tq:w, :]
                acc[0:2 * halo, :] = moved
                acc[2 * halo:w, :] = jnp.zeros((tq, GROUP_WIDTH), F32)

        @pl.when(i < nq)
        def _():
            start = pl.multiple_of(i * tq, tq)
            bias = _band_bias(i, tq, w, halo, n)
            H = GROUP_HEADS
            flat = [q_ref[:, sl].astype(F32) for sl in HEADS] + [k_ref[pl.ds(start, w), sl].astype(F32) for sl in HEADS] + \
                   [v_ref[pl.ds(start, w), sl].astype(F32) for sl in HEADS]

            def f(*a):
                os_, lses = _attn_heads(a[0:H], a[H:2 * H], a[2 * H:3 * H], [bias] * H)
                return tuple(os_) + tuple(lses)

            _, vjp = jax.vjp(f, *flat)
            grads = vjp(tuple(do_ref[:, sl].astype(F32) for sl in HEADS) +
                        tuple(jnp.sum(dl_ref[:, sl], axis=1, keepdims=True) for sl in HEADS))
            for h, sl in enumerate(HEADS):
                dq_ref[:, sl] = grads[h].astype(dq_ref.dtype)
                acc_k[:, sl] += grads[H + h]
                acc_v[:, sl] += grads[2 * H + h]

        dk_ref[...] = acc_k[0:tq, :].astype(dk_ref.dtype)
        dv_ref[...] = acc_v[0:tq, :].astype(dv_ref.dtype)

    qblk = pl.BlockSpec((None, tq, GROUP_WIDTH), lambda g, i: (g, jnp.minimum(i, nq - 1), 0))
    kblk = pl.BlockSpec((None, n + 2 * halo, GROUP_WIDTH), lambda g, i: (g, 0, 0))
    eblk = pl.BlockSpec((None, tq, GROUP_WIDTH), lambda g, i: (g, i, 0))
    return pl.pallas_call(
        body, name=name, grid=(G, nq + 1), in_specs=[qblk, kblk, kblk, qblk, qblk], out_specs=[qblk, eblk, eblk],
        out_shape=[jax.ShapeDtypeStruct((G, n, GROUP_WIDTH), BF16)] + [jax.ShapeDtypeStruct((G, (nq + 1) * tq, GROUP_WIDTH), BF16)] * 2,
        scratch_shapes=[pltpu.VMEM((w, GROUP_WIDTH), F32)] * 2,
        compiler_params=_cparams(("parallel", "arbitrary")),
    )(q, kp, vp, do, dl)


def _lin_scans(name, jobs):
    L, C = jobs[0][0].shape
    tt = 256 if L % 256 == 0 else L
    nt, per, last = L // tt, tt // HALO, L // HALO - 1
    specs, arrs, plan = [], [], []
    for coef, inp, rev, shift in jobs:
        tile = (lambda i: nt - 1 - i) if rev else (lambda i: i)
        blk = pl.BlockSpec((tt, C), lambda i, tile=tile: (tile(i), 0))
        if shift:
            side = lambda d, tile=tile: pl.BlockSpec(
                (HALO, C), lambda i: (jnp.clip(tile(i) * per + (per if d > 0 else -1), 0, last), 0))
            specs += [side(-1), blk, side(1), blk]
            arrs += [coef, coef, coef, inp]
        else:
            specs += [blk, blk]
            arrs += [coef, inp]
        plan.append((tile, blk, rev, shift))
    n_in = len(specs)

    def body(*refs):
        o_refs, carries = refs[n_in:n_in + len(jobs)], refs[n_in + len(jobs):]

        @pl.when(pl.program_id(0) == 0)
        def _():
            for carry in carries:
                carry[...] = jnp.zeros_like(carry)
        row = lax.broadcasted_iota(jnp.int32, (tt, C), 0)
        au, k = [], 0
        for tile, _, rev, shift in plan:
            if shift:
                a = _shift_tile(refs[k][...], refs[k + 1][...], refs[k + 2][...], shift, tile(pl.program_id(0)), nt)
                k += 3
            else:
                a = refs[k][...]
                k += 1
            au.append([a, refs[k][...]])
            k += 1
        s = 1
        while s < tt:
            for (tile, _, rev, shift), st in zip(plan, au):
                a, u = st
                ok = (row < tt - s) if rev else (row >= s)
                sh = tt - s if rev else s
                st[1] = u + a * jnp.where(ok, pltpu.roll(u, sh, 0), 0.0)
                st[0] = a * jnp.where(ok, pltpu.roll(a, sh, 0), 1.0)
            s *= 2
        for (tile, _, rev, shift), (a, u), o_ref, carry in zip(plan, au, o_refs, carries):
            out = u + a * carry[...]
            o_ref[...] = out
            carry[...] = out[0:1] if rev else out[tt - 1:tt]

    return pl.pallas_call(
        body, name=name, grid=(nt,), in_specs=specs, out_specs=[p_[1] for p_ in plan],
        out_shape=[jax.ShapeDtypeStruct((L, C), F32)] * len(jobs), scratch_shapes=[pltpu.VMEM((1, C), F32)] * len(jobs),
        compiler_params=_cparams(("arbitrary",)),
    )(*arrs)


def _pieces(shape):
    n0 = max(d for d in range(1, DMA_PIECES + 1) if shape[0] % d == 0)
    n1 = 1
    if len(shape) >= 3:
        n1 = max(d for d in range(1, DMA_PIECES // n0 + 1) if shape[1] % d == 0)
    s0, s1 = shape[0] // n0, (shape[1] // n1 if len(shape) >= 3 else 0)
    out = []
    for i in range(n0):
        for j in range(n1):
            out.append((pl.ds(i * s0, s0),) + ((pl.ds(j * s1, s1),) if len(shape) >= 3 else ()))
    return out


def _exchange(name, src, axes, gather):
    flips = {'xy': [(1, 0, 0), (0, 1, 0), (1, 1, 0)], 'c': [(0, 0, 1)],
             'xyc': [(fx, fy, fc) for fx in (0, 1) for fy in (0, 1) for fc in (0, 1)][1:]}[axes]
    n = len(flips) + 1
    blk_shape = tuple(src.shape if gather else src.shape[1:])
    pieces = _pieces(blk_shape)

    def number(px, py, pc):
        return {'xy': 2 * px + py, 'c': pc, 'xyc': 4 * px + 2 * py + pc}[axes]

    def body(src_ref, out_ref, send_sems, recv_sems):
        x, y, c = lax.axis_index("x"), lax.axis_index("y"), lax.axis_index("c")
        me = number(x, y, c)
        piece = (lambda k: src_ref) if gather else (lambda k: src_ref.at[k])
        peers = []
        for s, (fx, fy, fc) in enumerate(flips):
            px, py, pc = (x + fx) % 2, (y + fy) % 2, (c + fc) % 2

            def copy(ix, s=s, px=px, py=py, pc=pc):
                part = (lambda r: r) if ix is None else (lambda r: r.at[ix])
                return pltpu.make_async_remote_copy(
                    src_ref=part(piece(number(px, py, pc))), dst_ref=part(out_ref.at[me]),
                    send_sem=send_sems.at[s], recv_sem=recv_sems.at[s],
                    device_id=(px, py, pc), device_id_type=MESH)

            for ix in pieces:
                copy(ix).start()
            peers.append(copy)
        for copy in peers:
            copy(None).wait()

    out = pl.pallas_call(
        body, name=name, out_shape=jax.ShapeDtypeStruct((n,) + blk_shape, src.dtype),
        in_specs=[pl.BlockSpec(memory_space=pl.ANY)], out_specs=pl.BlockSpec(memory_space=pl.ANY),
        scratch_shapes=[pltpu.SemaphoreType.DMA((n - 1,)), pltpu.SemaphoreType.DMA((n - 1,))],
    )(src)
    me = number(lax.axis_index("x"), lax.axis_index("y"), lax.axis_index("c"))
    own = src if gather else lax.dynamic_index_in_dim(src, me, 0, keepdims=False)
    return lax.dynamic_update_index_in_dim(out, own, me, 0)


def _ordered_sum(name, buf, dtype=F32):
    n = buf.shape[0]

    def fn(*t):
        acc = t[0].astype(F32)
        for v in t[1:]:
            acc = acc + v.astype(F32)
        return acc

    return _rowwise(name, fn, [Rows(buf, lead=k) for k in range(n)], [], [(buf.shape[-1], dtype)])[0]


def _reduce_big(name, g):
    mine = _ordered_sum(name + "_sum_c", _exchange(name + "_swap_c", g, 'c', False).reshape(2, -1, g.shape[-1]), BF16)
    mine = mine.reshape(g.shape[1:])
    tot = _ordered_sum(name + "_sum_xy", _exchange(name + "_a2a_xy", mine, 'xy', False))
    return _exchange(name + "_share_c", tot, 'c', True)


def _dilate(t, dil):
    L, C = t.shape
    return t.reshape(L // dil, dil, C).transpose(1, 0, 2)


def _undilate(t):
    dil, n, C = t.shape
    return t.transpose(1, 0, 2).reshape(dil * n, C)


def _pad_rows(t, halo):
    return jnp.pad(t, ((0, 0), (halo, halo), (0, 0)))


def _pcol(p, blk):
    return Rows(p, GROUP_WIDTH, blk)


def _pslice(p, blk):
    return p[:, blk * GROUP_WIDTH:(blk + 1) * GROUP_WIDTH]


def _conv_taps(p):
    return Rows(p, GROUP_WIDTH, P_XC, shifts=[LRU_CONV_LEFT - j for j in range(LRU_CONV)])


def _seg_matrix():
    h = np.arange(GROUP_WIDTH) // HEAD_DIM
    return jnp.asarray((h[:, None] == h[None, :]).astype(np.float32) / HEAD_DIM)


def _rope_tables(L):
    pos = jnp.arange(L, dtype=F32)
    inv_freq = ROPE_THETA ** (-jnp.arange(0, HEAD_DIM, 2, dtype=F32) / HEAD_DIM)
    ang = pos[:, None] * inv_freq[None, :]
    cos, sin = jnp.cos(ang), jnp.sin(ang)
    cs = jnp.tile(jnp.concatenate([cos, cos], axis=1), (1, GROUP_HEADS))
    sn = jnp.tile(jnp.concatenate([-sin, sin], axis=1), (1, GROUP_HEADS))
    return cs, sn


def _dil_branches(L):
    out = []
    for window, dil in DIL_PAIRS:
        radius = window // (2 * dil)
        n = L // dil
        out.append((dil, radius, min(256, n)))
    return out


def _dil_operands(qr, kr, p16, dil, radius):
    return _dilate(qr, dil), _pad_rows(_dilate(kr, dil), radius), _pad_rows(_dilate(_pslice(p16, P_VD), dil), radius)


def _layer_fwd(x, w, c):
    L, D = x.shape
    sv = {'x_in': x}
    h = _rowwise("mix_prenorm", _prenorm_fn, [x], [w['mix_norm_pre']], [(D, BF16)])[0]
    p, p16 = _matmul("mix_proj", h, w['w_in'], 'nn', F32, also=BF16)
    sv.update(p=p, p16=p16, h=h)

    la = _rowwise("gla_pre", _gla_pre_fn, [Rows(p, LANE, P_Z // LANE)], [w['gla_wg'], w['gla_bg']], [(2 * GROUP_WIDTH, F32)])[0]
    of, ob, sf, sb = _gla_scan_fwd(p, la)
    sv.update(la=la, sf=sf, sb=sb, of=of, ob=ob)

    yb = _na_fwd(p16, _rpb_expand(w['na_rpb'], c['rpb_tab']))

    a0, a1, u0, u1 = _rowwise("lru_pre", _lru_pre_fn, [_conv_taps(p)], [w[k] for k in LRU_PARAMS], [(GROUP_WIDTH, F32)] * 4)
    hf, hb = _lin_scans("lru_scan", [(a0, u0, False, 0), (a1, u1, True, 0)])
    sv.update(a0=a0, a1=a1, hf=hf, hb=hb)

    qr, kr = _rowwise("rope", _rope_fn, [_pcol(p, P_QD), _pcol(p, P_KD), c['cos'], c['sin']], [], [(GROUP_WIDTH, BF16)] * 2)
    os_, ls_, ops = [], [], []
    for dil, radius, tq in _dil_branches(L):
        ops.append(_dil_operands(qr, kr, p16, dil, radius))
        o, lse = _band_fwd(f"dil_attn{dil}", *ops[-1], tq, radius)
        os_.append(_undilate(o))
        ls_.append(_undilate(lse))
    sv.update(dil_ops=ops, dil_o=os_, dil_l=ls_)

    ycat = _rowwise("mix_post", _mix_post_fn, [of, ob, _pcol(p, P_GA), yb, hf, hb, _pcol(p, P_GC)] + os_ + ls_,
                    [w['gla_norm'], c['seg']], [(4 * GROUP_WIDTH, BF16)])[0]
    y = _matmul("mix_out", ycat, w['w_out'], 'nn', F32)
    xm = _rowwise("mix_postnorm", _postnorm_fn, [x, y], [w['mix_norm_post']], [(D, F32)])[0]
    sv.update(ycat=ycat, y=y, x_mid=xm)

    h2 = _rowwise("ffn_prenorm", _prenorm_fn, [xm], [w['ffn_norm_pre']], [(D, BF16)])[0]
    gate, up, act = _ffn_up("ffn_up", h2, w['ffn_wg'], w['ffn_wu'])
    f = _matmul("ffn_out", act, w['ffn_w_out'], 'nn', F32)
    xo = _rowwise("ffn_postnorm", _postnorm_fn, [xm, f], [w['ffn_norm_post']], [(D, F32)])[0]
    sv.update(gate=gate, up=up, act=act, f=f, h2=h2)
    return xo, sv


def _layer_bwd(dx, w, c, sv):
    L, D = dx.shape
    g = {}
    as_f32 = lambda t: (t.astype(F32),)
    df, g['ffn_norm_post'] = _rowwise_bwd("ffn_postnorm_b", lambda y, gn: _rms(y, gn), [sv['f']], [w['ffn_norm_post']],
                                          [dx], as_f32, [BF16], [True])
    dgate, dup = _ffn_down_bwd("ffn_out_bx", df, w['ffn_w_out'], sv['gate'], sv['up'])
    g['ffn_w_out'] = _matmul("ffn_out_bw", sv['act'], df, 'tn', F32)
    dh2 = _matmul("ffn_up_bx", dup, w['ffn_wu'], 'nt', F32, acc_in=_matmul("ffn_gate_bx", dgate, w['ffn_wg'], 'nt', F32))
    xm = sv['x_mid']
    g['ffn_wg'] = _matmul("ffn_gate_bw", sv['h2'], dgate, 'tn', F32)
    g['ffn_wu'] = _matmul("ffn_up_bw", sv['h2'], dup, 'tn', F32)
    dxm, g['ffn_norm_pre'] = _rowwise_bwd("ffn_prenorm_b", _prenorm_fn, [xm], [w['ffn_norm_pre']], [dh2], as_f32, [F32], [True],
                                          row_grad_add=[dx])

    dy, g['mix_norm_post'] = _rowwise_bwd("mix_postnorm_b", lambda y, gn: _rms(y, gn), [sv['y']], [w['mix_norm_post']],
                                          [dxm], as_f32, [BF16], [True])
    dycat = _matmul("mix_out_bx", dy, w['w_out'], 'nt', F32)
    g['w_out'] = _matmul("mix_out_bw", sv['ycat'], dy, 'tn', F32)
    p = sv['p']
    dya, dyb, dyc, dyd = (Rows(dycat, GROUP_WIDTH, k) for k in range(4))

    dof, dga, g['gla_norm'] = _rowwise_bwd("gla_post_b", _gla_post_fn, [sv['of'], sv['ob'], _pcol(p, P_GA)],
                                           [w['gla_norm'], c['seg']], [dya], as_f32, [F32, None, BF16], [True, False])
    la = sv['la']
    dqf, dkf, dvf, dlf, dqb_, dkb_, dvb_, dlb = _gla_scan_bwd(p, la, sv['sf'], sv['sb'], dof)
    dz, g['gla_wg'], g['gla_bg'] = _rowwise_bwd("gla_pre_b", _gla_pre_fn, [Rows(p, LANE, P_Z // LANE)], [w['gla_wg'], w['gla_bg']],
                                                [dlf, dlb], lambda a, b: (jnp.concatenate([a, b], axis=1),), [BF16], [True, True])

    btab = _rpb_expand(w['na_rpb'], c['rpb_tab'])
    dqn, dkn, dvn, dbt = _na_bwd(sv['p16'], btab, dycat)
    g['na_rpb'] = _rpb_contract(dbt, c['rpb_tab'])

    dh, dgc = _rowwise_bwd("lru_post_b", _lru_post_fn, [sv['hf'], sv['hb'], _pcol(p, P_GC)], [], [dyc], as_f32, [F32, None, BF16], [])
    lam0, lam1 = _lin_scans("lru_scan_b", [(sv['a0'], dh, True, -1), (sv['a1'], dh, False, 1)])
    res = _rowwise_bwd("lru_pre_b", _lru_pre_fn, [_conv_taps(p)], [w[k] for k in LRU_PARAMS],
                       [lam0, lam1, Rows(sv['hf'], shifts=[1]), Rows(sv['hb'], shifts=[-1])],
                       lambda l0, l1, hfp, hbn: (l0 * hfp, l1 * hbn, l0, l1), [F32] * 4, [True] * len(LRU_PARAMS))
    dxs = res[:4]
    for k, nm in enumerate(LRU_PARAMS):
        g[nm] = res[4 + k]
    dxc = [Rows(dxs[j], shifts=[j - LRU_CONV_LEFT]) for j in range(LRU_CONV)]

    comb = _rowwise_bwd("dil_comb_b", _dil_comb_fn, sv['dil_o'] + sv['dil_l'], [], [dyd], as_f32, [BF16] * 3 + [F32] * 3, [])
    dqs, dks, dvs = [], [], []
    for k, (dil, radius, tq) in enumerate(_dil_branches(L)):
        n = L // dil
        dq_, dk_, dv_ = _band_bwd(f"dil_attn{dil}_b", *sv['dil_ops'][k], _dilate(comb[k], dil), _dilate(comb[3 + k], dil), tq, radius)
        dqs.append(_undilate(dq_))
        dks.append(_undilate(dk_[:, radius:radius + n]))
        dvs.append(_undilate(dv_[:, radius:radius + n]))
    dqd, dkd = _rowwise_bwd("rope_b", _rope_fn, [_pcol(p, P_QD), _pcol(p, P_KD), c['cos'], c['sin']], [], dqs + dks,
                            lambda *t: (sum(v.astype(F32) for v in t[:3]), sum(v.astype(F32) for v in t[3:])), [BF16, BF16, None, None], [])

    dp = _assemble("mix_dp", [[dqf, dqb_], [dkf, dkb_], [dvf, dvb_], [dga], [dqn], [dkn], [dvn], dxc, [dgc], [dqd], [dkd], dvs, [dz]], BF16)
    dh1 = _matmul("mix_proj_bx", dp, w['w_in'], 'nt', F32)
    x_in = sv['x_in']
    g['w_in'] = _matmul("mix_proj_bw", sv['h'], dp, 'tn', F32)
    dxi, g['mix_norm_pre'] = _rowwise_bwd("mix_prenorm_b", _prenorm_fn, [x_in], [w['mix_norm_pre']], [dh1], as_f32, [F32], [True],
                                          row_grad_add=[dxm])
    return dxi, g


def _loss_fn(y, t):
    e = y - t
    return e * (1.0 / y.shape[1]), jnp.sum(e * e, axis=0, keepdims=True)


def _gather_cols(name, shard, axis):
    half = shard.shape[0] // 2
    mine = lax.dynamic_slice_in_dim(shard, lax.axis_index("c") * half, half, axis=0).astype(BF16)
    both = _exchange(name + "_c", _exchange(name + "_xy", mine, 'xy', True), 'c', True)
    shp = list(shard.shape)
    shp[axis] *= 4
    return jnp.moveaxis(both, 1, axis + 1).reshape(shp)


def _pack(arrs, mult=64 * LANE):
    flat = jnp.concatenate([a.reshape(-1) for a in arrs])
    pad = (-flat.shape[0]) % mult
    return jnp.pad(flat, (0, pad)).reshape(-1, LANE)


def _unpack(buf, shapes):
    flat, out, k = buf.reshape(-1), [], 0
    for s in shapes:
        sz = int(np.prod(s))
        out.append(flat[k:k + sz].reshape(s))
        k += sz
    return out


def _perm_in(w_in):
    pad = jnp.zeros(w_in.shape[:-1] + (D_INP - D_IN,), w_in.dtype)
    return jnp.concatenate([w_in[..., :P_QB * GROUP_WIDTH], w_in[..., P_QB * GROUP_WIDTH + 2 * GLA_RANK:],
                            w_in[..., P_QB * GROUP_WIDTH:P_QB * GROUP_WIDTH + 2 * GLA_RANK], pad], axis=-1)


def _unperm_in(g):
    return jnp.concatenate([g[..., :P_QB * GROUP_WIDTH], g[..., P_Z:P_Z + 2 * GLA_RANK], g[..., P_QB * GROUP_WIDTH:P_Z]], axis=-1)


def _block_diag(wb):
    l = wb.shape[0]
    eye = jnp.eye(GROUP_HEADS, dtype=wb.dtype)
    return jnp.einsum('lehij,hg->lehigj', wb, eye).reshape(l, 2, GROUP_WIDTH, GROUP_WIDTH)


def _block_diag_grad(gw):
    l = gw.shape[0]
    g6 = gw.reshape(l, 2, GROUP_HEADS, HEAD_DIM, GROUP_HEADS, HEAD_DIM)
    return jnp.stack([g6[:, :, h, :, h, :] for h in range(GROUP_HEADS)], axis=2)


def _gate_matrix(wg):
    l = wg.shape[0]
    m = jnp.zeros((l, LANE, 2 * GROUP_WIDTH), wg.dtype)
    for e in range(2):
        m = m.at[:, e * GLA_RANK:(e + 1) * GLA_RANK, e * GROUP_WIDTH:(e + 1) * GROUP_WIDTH].set(wg[:, e])
    return m


def _gate_matrix_grad(gm):
    return jnp.stack([gm[:, e * GLA_RANK:(e + 1) * GLA_RANK, e * GROUP_WIDTH:(e + 1) * GROUP_WIDTH] for e in range(2)], axis=1)


def _adam_fn(w, g, m, v):
    m = ADAM_B1 * m + (1.0 - ADAM_B1) * g
    v = ADAM_B2 * v + (1.0 - ADAM_B2) * (g * g)
    m_hat = m / (1.0 - ADAM_B1 ** ADAM_STEP)
    v_hat = v / (1.0 - ADAM_B2 ** ADAM_STEP)
    return -ADAM_LR * (m_hat / (jnp.sqrt(v_hat) + ADAM_EPS) + ADAM_WD * w), m, v


def _adam(name, w, g, m, v):
    shp = w.shape
    two = lambda t: t.reshape(-1, shp[-1])
    res = _rowwise(name, _adam_fn, [two(w), two(g), two(m), two(v)], [], [(shp[-1], F32)] * 3)
    return [r.reshape(shp) for r in res]


def _local_step(x, target, fw):
    L, D = x.shape
    depth = fw['w_in'].shape[0]
    cs, sn = _rope_tables(L)
    consts = {'seg': _seg_matrix(), 'rpb_tab': _rpb_tables(), 'cos': cs, 'sin': sn}
    layer = lambda l: {k: v[l] for k, v in fw.items()}
    saved = []
    for l in range(depth):
        x, sv = _layer_fwd(x, layer(l), consts)
        saved.append(sv)
    dx, sq = _rowwise("loss", _loss_fn, [x, target], [], [(D, F32)], acc_outs=[(1, D)])
    grads = [None] * depth
    for l in reversed(range(depth)):
        dx, grads[l] = _layer_bwd(dx, layer(l), consts, saved[l])
    return sq, dx, grads


def kernel(x, mix_norm_pre, mix_norm_post, w_in, gla_w_gate, gla_b_gate, gla_norm, na_rpb, lru_conv_w, lru_conv_b, lru_w_a, lru_b_a, lru_w_x, lru_b_x, lru_lambda, w_out, ffn_norm_pre, ffn_norm_post, ffn_w_in, ffn_w_out, loss_target, m_mix_norm_pre, m_mix_norm_post, m_w_in, m_gla_w_gate, m_gla_b_gate, m_gla_norm, m_na_rpb, m_lru_conv_w, m_lru_conv_b, m_lru_w_a, m_lru_b_a, m_lru_w_x, m_lru_b_x, m_lru_lambda, m_w_out, m_ffn_norm_pre, m_ffn_norm_post, m_ffn_w_in, m_ffn_w_out, v_mix_norm_pre, v_mix_norm_post, v_w_in, v_gla_w_gate, v_gla_b_gate, v_gla_norm, v_na_rpb, v_lru_conv_w, v_lru_conv_b, v_lru_w_a, v_lru_b_a, v_lru_w_x, v_lru_b_x, v_lru_lambda, v_w_out, v_ffn_norm_pre, v_ffn_norm_post, v_ffn_w_in, v_ffn_w_out):
    args = (mix_norm_pre, mix_norm_post, w_in, gla_w_gate, gla_b_gate, gla_norm, na_rpb, lru_conv_w, lru_conv_b, lru_w_a, lru_b_a, lru_w_x, lru_b_x, lru_lambda, w_out, ffn_norm_pre, ffn_norm_post, ffn_w_in, ffn_w_out,
            m_mix_norm_pre, m_mix_norm_post, m_w_in, m_gla_w_gate, m_gla_b_gate, m_gla_norm, m_na_rpb, m_lru_conv_w, m_lru_conv_b, m_lru_w_a, m_lru_b_a, m_lru_w_x, m_lru_b_x, m_lru_lambda, m_w_out, m_ffn_norm_pre, m_ffn_norm_post, m_ffn_w_in, m_ffn_w_out,
            v_mix_norm_pre, v_mix_norm_post, v_w_in, v_gla_w_gate, v_gla_b_gate, v_gla_norm, v_na_rpb, v_lru_conv_w, v_lru_conv_b, v_lru_w_a, v_lru_b_a, v_lru_w_x, v_lru_b_x, v_lru_lambda, v_w_out, v_ffn_norm_pre, v_ffn_norm_post, v_ffn_w_in, v_ffn_w_out)
    nw = len(WEIGHTS)
    W = dict(zip(WEIGHTS, args[:nw]))
    M = dict(zip(WEIGHTS, args[nw:2 * nw]))
    V = dict(zip(WEIGHTS, args[2 * nw:]))
    chip = 2 * lax.axis_index("x") + lax.axis_index("y")

    full = dict(W)
    full['w_in'] = _gather_cols("ag_w_in", w_in, 2)
    full['ffn_w_in'] = _gather_cols("ag_ffn_w_in", ffn_w_in, 2)
    full['w_out'] = _gather_cols("ag_w_out", w_out, 1)
    full['ffn_w_out'] = _gather_cols("ag_ffn_w_out", ffn_w_out, 1)
    small = list(SMALL_SHARDED)
    got = _exchange("ag_small", _pack([W[k] for k in small]), 'xy', True)
    for k, parts in zip(small, zip(*[_unpack(got[j], [W[k].shape for k in small]) for j in range(4)])):
        ax = SMALL_SHARDED[k]
        stacked = jnp.moveaxis(jnp.stack(parts), 0, ax)
        shp = list(W[k].shape)
        shp[ax] *= 4
        full[k] = stacked.reshape(shp)

    sq, dx0, g = _local_step(x[0], loss_target[0], _layer_weights(full))
    loss = lax.psum(0.5 * jnp.sum(sq) / x.shape[-1], ("x", "y", "c"))
    gfull = _stored_grads(g)

    grad = {}
    for k, cut in _big_cuts(g).items():
        grad[k] = _reduce_big("rs_" + k, cut).reshape(W[k].shape)
    rest = [k for k in WEIGHTS if k not in BIG]
    allg = _exchange("ar_small", _pack([gfull[k] for k in rest]), 'xyc', True)
    summed = _unpack(_ordered_sum("ar_small_sum", allg), [gfull[k].shape for k in rest])
    for k, s in zip(rest, summed):
        if k in SMALL_SHARDED:
            ax = SMALL_SHARDED[k]
            n = W[k].shape[ax]
            s = lax.dynamic_slice_in_dim(s, chip * n, n, axis=ax)
        grad[k] = s

    delta, new_m, new_v = {}, {}, {}
    for k in BIG:
        delta[k], new_m[k], new_v[k] = _adam("adam_" + k, W[k], grad[k], M[k], V[k])
    shapes = [W[k].shape for k in rest]
    res = _rowwise("adam_small", _adam_fn, [_pack([d[k] for k in rest]) for d in (W, grad, M, V)], [], [(LANE, F32)] * 3)
    for d, r in zip((delta, new_m, new_v), res):
        for k, t in zip(rest, _unpack(r, shapes)):
            d[k] = t

    return (loss, dx0[None], *[grad[k] for k in WEIGHTS], *[delta[k] for k in WEIGHTS],
            *[new_m[k] for k in WEIGHTS], *[new_v[k] for k in WEIGHTS])


def _layer_weights(full):
    depth = full['w_in'].shape[0]
    dff = full['ffn_w_in'].shape[-1] // 2
    row = lambda t: t[:, None, :]
    fw = {
        'mix_norm_pre': row(full['mix_norm_pre']), 'mix_norm_post': row(full['mix_norm_post']),
        'ffn_norm_pre': row(full['ffn_norm_pre']), 'ffn_norm_post': row(full['ffn_norm_post']),
        'w_in': _perm_in(full['w_in']), 'w_out': full['w_out'],
        'ffn_wg': full['ffn_w_in'][..., :dff], 'ffn_wu': full['ffn_w_in'][..., dff:], 'ffn_w_out': full['ffn_w_out'],
        'gla_wg': _gate_matrix(full['gla_w_gate']), 'gla_bg': full['gla_b_gate'].reshape(depth, 1, 2 * GROUP_WIDTH),
        'gla_norm': row(full['gla_norm']), 'na_rpb': full['na_rpb'],
        'lru_cb': row(full['lru_conv_b']),
    }
    wa_bd, wx_bd = _block_diag(full['lru_w_a']), _block_diag(full['lru_w_x'])
    for j in range(LRU_CONV):
        fw[f'lru_cw{j}'] = row(full['lru_conv_w'][:, j])
    for e in range(2):
        fw[f'lru_wa{e}'], fw[f'lru_wx{e}'] = wa_bd[:, e], wx_bd[:, e]
        fw[f'lru_ba{e}'], fw[f'lru_bx{e}'] = row(full['lru_b_a'][:, e]), row(full['lru_b_x'][:, e])
        fw[f'lru_lam{e}'] = row(full['lru_lambda'][:, e])
    return fw


def _orig_cols(gp, lo, hi):
    split, zend = P_QB * GROUP_WIDTH, P_QB * GROUP_WIDTH + 2 * GLA_RANK
    parts = []
    for a, b, at in ((0, split, 0), (split, zend, P_Z), (zend, D_IN, split)):
        s, e = max(lo, a), min(hi, b)
        if s < e:
            parts.append(gp[..., at + s - a:at + e - a])
    return parts[0] if len(parts) == 1 else jnp.concatenate(parts, axis=-1)


def _big_cuts(grads):
    depth = len(grads)
    halves = (range(0, depth // 2), range(depth // 2, depth))
    n_in, n_ff = D_IN // 4, grads[0]['ffn_wg'].shape[1] // 2

    def build(piece):
        return jnp.stack([jnp.stack([jnp.concatenate([piece(l, j) for l in hl], axis=0) for j in range(4)]) for hl in halves])

    def rows_of(key):
        return lambda l, j: jnp.split(grads[l][key], 4, axis=0)[j]

    return {
        'w_in': build(lambda l, j: _orig_cols(grads[l]['w_in'], j * n_in, (j + 1) * n_in)),
        'w_out': build(rows_of('w_out')),
        'ffn_w_in': build(lambda l, j: grads[l]['ffn_wg' if j < 2 else 'ffn_wu'][:, (j % 2) * n_ff:(j % 2 + 1) * n_ff]),
        'ffn_w_out': build(rows_of('ffn_w_out')),
    }


def _stored_grads(grads):
    depth = len(grads)
    g = {k: jnp.stack([gl[k] for gl in grads]) for k in grads[0]}
    return {
        'mix_norm_pre': g['mix_norm_pre'][:, 0], 'mix_norm_post': g['mix_norm_post'][:, 0],
        'ffn_norm_pre': g['ffn_norm_pre'][:, 0], 'ffn_norm_post': g['ffn_norm_post'][:, 0],
        'w_in': _unperm_in(g['w_in']), 'w_out': g['w_out'],
        'ffn_w_in': jnp.concatenate([g['ffn_wg'], g['ffn_wu']], axis=-1), 'ffn_w_out': g['ffn_w_out'],
        'gla_w_gate': _gate_matrix_grad(g['gla_wg']), 'gla_b_gate': g['gla_bg'].reshape(depth, 2, GROUP_WIDTH),
        'gla_norm': g['gla_norm'][:, 0], 'na_rpb': g['na_rpb'],
        'lru_conv_w': jnp.stack([g[f'lru_cw{j}'][:, 0] for j in range(LRU_CONV)], axis=1), 'lru_conv_b': g['lru_cb'][:, 0],
        'lru_w_a': _block_diag_grad(jnp.stack([g['lru_wa0'], g['lru_wa1']], axis=1)),
        'lru_w_x': _block_diag_grad(jnp.stack([g['lru_wx0'], g['lru_wx1']], axis=1)),
        'lru_b_a': jnp.stack([g['lru_ba0'][:, 0], g['lru_ba1'][:, 0]], axis=1),
        'lru_b_x': jnp.stack([g['lru_bx0'][:, 0], g['lru_bx1'][:, 0]], axis=1),
        'lru_lambda': jnp.stack([g['lru_lam0'][:, 0], g['lru_lam1'][:, 0]], axis=1),
    }
```

```python
import numpy as np
import jax
import jax.numpy as jnp
from jax import lax
from jax.experimental import pallas as pl
from jax.experimental.pallas import tpu as pltpu

F32, BF16 = jnp.float32, jnp.bfloat16
HIGHEST = lax.Precision.HIGHEST
MESH = pl.DeviceIdType.MESH

HEAD_DIM = 64
GROUP_HEADS = 4
GROUP_WIDTH = GROUP_HEADS * HEAD_DIM
GLA_RANK = 16
GLA_TAU = 16.0
GLA_CHUNK = 64
GRID_W = 64
NA_ROWS = 8
NA_COLS = 16
LRU_CONV = 4
LRU_CONV_LEFT = 2
LRU_C = 8.0
DIL_PAIRS = ((128, 1), (512, 4), (2048, 16))
ROPE_THETA = 10000.0
EPS = 1e-6
ADAM_LR, ADAM_B1, ADAM_B2, ADAM_EPS, ADAM_WD, ADAM_STEP = 0.001, 0.9, 0.999, 1e-08, 0.01, 10
NEG = -1e30

LANE = 128
VMEM_LIMIT = 56 * 1024 * 1024
ROW_BUDGET = 16 * 1024 * 1024
DMA_PIECES = 8

P_QA, P_KA, P_VA, P_GA, P_QB, P_KB, P_VB, P_XC, P_GC, P_QD, P_KD, P_VD = range(12)
P_Z = 12 * GROUP_WIDTH
D_IN = 12 * GROUP_WIDTH + 2 * GLA_RANK
D_INP = 12 * GROUP_WIDTH + LANE

WEIGHTS = ['mix_norm_pre', 'mix_norm_post', 'w_in', 'gla_w_gate', 'gla_b_gate', 'gla_norm', 'na_rpb',
           'lru_conv_w', 'lru_conv_b', 'lru_w_a', 'lru_b_a', 'lru_w_x', 'lru_b_x', 'lru_lambda', 'w_out',
           'ffn_norm_pre', 'ffn_norm_post', 'ffn_w_in', 'ffn_w_out']
BIG = ('w_in', 'w_out', 'ffn_w_in', 'ffn_w_out')
SMALL_SHARDED = {'gla_w_gate': 3, 'gla_b_gate': 2, 'lru_conv_w': 2, 'lru_b_a': 2, 'lru_b_x': 2, 'lru_lambda': 2}
HEADS = [slice(h * HEAD_DIM, (h + 1) * HEAD_DIM) for h in range(GROUP_HEADS)]


def _cparams(sem=None):
    return pltpu.CompilerParams(dimension_semantics=sem, vmem_limit_bytes=VMEM_LIMIT)


def _tile(dim, target, mult=LANE):
    best = None
    for t in range(mult, min(dim, target) + 1, mult):
        if dim % t == 0:
            best = t
    return best or dim


class Rows:
    def __init__(self, a, w=None, cb=0, lead=None, shifts=None):
        self.a, self.cb, self.lead, self.shifts = a, cb, lead, shifts
        self.w = a.shape[-1] if w is None else w
        self.nrows = a.shape[-2]

    def spec(self, tm, ncol=1, halo=0):
        w = self.w // ncol
        per, last = tm // HALO, self.nrows // HALO - 1
        rows, row = (tm, lambda i: i) if halo == 0 else (HALO, lambda i: jnp.clip(i * per + (per if halo > 0 else -1), 0, last))
        if self.lead is None:
            return pl.BlockSpec((rows, w), lambda i, j, cb=self.cb: (row(i), cb * ncol + j))
        return pl.BlockSpec((None, rows, w), lambda i, j, cb=self.cb, k=self.lead: (k, row(i), cb * ncol + j))

    def nbytes(self):
        return self.w * self.a.dtype.itemsize


def _as_rows(rs):
    return [r if isinstance(r, Rows) else Rows(r) for r in rs]


HALO = 8


def _shift_tile(before, cur, after, k, t, nt):
    if k == 0:
        return cur
    tm = cur.shape[0]
    row = lax.broadcasted_iota(jnp.int32, before.shape, 0)
    if k > 0:
        moved = pltpu.roll(cur, k, 0)
        edge = jnp.where(row < k, jnp.where(t > 0, pltpu.roll(before, k, 0), 0.0), moved[0:HALO])
        return jnp.concatenate([edge, moved[HALO:]], axis=0)
    moved = pltpu.roll(cur, tm + k, 0)
    edge = jnp.where(row >= HALO + k, jnp.where(t < nt - 1, pltpu.roll(after, HALO + k, 0), 0.0), moved[tm - HALO:])
    return jnp.concatenate([moved[:tm - HALO], edge], axis=0)


def _operands(rows, tm, ncol):
    specs, arrs = [], []
    for r in rows:
        if r.shifts is None:
            specs.append(r.spec(tm, ncol))
            arrs.append(r.a)
        else:
            assert ncol == 1
            specs += [r.spec(tm, 1, side) for side in (-1, 0, 1)]
            arrs += [r.a] * 3

    def load(refs):
        vals, k = [], 0
        t, nt = pl.program_id(0), rows[0].nrows // tm
        for r in rows:
            if r.shifts is None:
                vals.append(refs[k][...])
                k += 1
            else:
                prev, cur, nxt = refs[k][...], refs[k + 1][...], refs[k + 2][...]
                vals += [_shift_tile(prev, cur, nxt, s, t, nt) for s in r.shifts]
                k += 3
        return vals

    return specs, arrs, load


def _expand(rows):
    return [r for r in rows for _ in (r.shifts or [0])]


def _pick_tm(nrows, row_bytes, scale, cap=512):
    tm = cap
    while tm > 16 and (tm * row_bytes * scale > ROW_BUDGET or nrows % tm):
        tm //= 2
    assert nrows % tm == 0, (nrows, tm)
    return tm


def _full_spec(a):
    nd = a.ndim
    return pl.BlockSpec(a.shape, lambda i, j, nd=nd: (0,) * nd)


def _rowwise(name, fn, rows, params, outs, acc_outs=(), ncol=1, tm_cap=512):
    rows = _as_rows(rows)
    nrows = rows[0].nrows
    assert ncol == 1 or not (acc_outs or params)
    tm = _pick_tm(nrows, (sum(r.nbytes() for r in rows) + sum(w * jnp.dtype(d).itemsize for w, d in outs)) // ncol, 2, tm_cap)
    specs, arrs, load = _operands(rows, tm, ncol)
    n_r, n_p, n_o = len(specs), len(params), len(outs)

    def body(*refs):
        vals = load(refs[:n_r]) + [r[...] for r in refs[n_r:n_r + n_p]]
        res = fn(*vals)
        res = res if isinstance(res, (tuple, list)) else (res,)
        orefs = refs[n_r + n_p:]
        for o, v in zip(orefs[:n_o], res[:n_o]):
            o[...] = v.astype(o.dtype)
        for o, v in zip(orefs[n_o:], res[n_o:]):
            @pl.when(pl.program_id(0) == 0)
            def _(o=o):
                o[...] = jnp.zeros_like(o)
            o[...] += v

    out_shape = [jax.ShapeDtypeStruct((nrows, w), d) for w, d in outs] + [jax.ShapeDtypeStruct(s, F32) for s in acc_outs]
    out_specs = [pl.BlockSpec((tm, w // ncol), lambda i, j: (i, j)) for w, _ in outs] + \
                [pl.BlockSpec(s, lambda i, j, nd=len(s): (0,) * nd) for s in acc_outs]
    return pl.pallas_call(
        body, name=name, grid=(nrows // tm, ncol),
        in_specs=specs + [_full_spec(p) for p in params],
        out_specs=out_specs, out_shape=out_shape,
        compiler_params=_cparams(("arbitrary", "arbitrary") if acc_outs else ("parallel", "parallel")),
    )(*arrs, *params)


def _rowwise_bwd(name, fn, rows, params, ct_rows, ct_fn, row_grads, param_grads, row_grad_add=None, ncol=1):
    rows, ct_rows = _as_rows(rows), _as_rows(ct_rows)
    nrows = rows[0].nrows
    n_rg = sum(d is not None for d in row_grads)
    adds = _as_rows([a for a in (row_grad_add or []) if a is not None])
    add_at = [k for k, a in enumerate(row_grad_add or []) if a is not None]
    assert ncol == 1 or not (params or adds)
    seen = _expand(rows)
    gbytes = sum(r.w * jnp.dtype(d).itemsize for r, d in zip(seen, row_grads) if d is not None)
    tm = _pick_tm(nrows, (sum(r.nbytes() for r in rows + ct_rows + adds) + gbytes) // ncol, 4)
    r_specs, r_arrs, r_load = _operands(rows, tm, ncol)
    c_specs, c_arrs, c_load = _operands(ct_rows, tm, ncol)
    a_specs, a_arrs, a_load = _operands(adds, tm, ncol)
    n_r, n_p, n_c, n_a = len(r_specs), len(params), len(c_specs), len(a_specs)
    diff = [k for k, d in enumerate(row_grads) if d is not None] + [len(seen) + k for k, g in enumerate(param_grads) if g]

    def body(*refs):
        vals = r_load(refs[:n_r]) + [r[...] for r in refs[n_r:n_r + n_p]]
        cts_in = c_load(refs[n_r + n_p:n_r + n_p + n_c])
        add_in = a_load(refs[n_r + n_p + n_c:n_r + n_p + n_c + n_a]) if n_a else []
        orefs = refs[n_r + n_p + n_c + n_a:]

        def f(*dv):
            full = list(vals)
            for k, v in zip(diff, dv):
                full[k] = v
            res = fn(*full)
            return tuple(res) if isinstance(res, (tuple, list)) else (res,)

        outs, vjp = jax.vjp(f, *[vals[k].astype(F32) for k in diff])
        cts = ct_fn(*cts_in)
        cts = cts if isinstance(cts, (tuple, list)) else (cts,)
        grads = list(vjp(tuple(c.astype(o.dtype) for c, o in zip(cts, outs))))
        for k, a in zip(add_at, add_in):
            grads[k] = grads[k] + a.astype(F32)
        for o, g in zip(orefs[:n_rg], grads[:n_rg]):
            o[...] = g.astype(o.dtype)
        for o, g in zip(orefs[n_rg:], grads[n_rg:]):
            @pl.when(pl.program_id(0) == 0)
            def _(o=o):
                o[...] = jnp.zeros_like(o)
            o[...] += g.astype(F32)

    out_shape = [jax.ShapeDtypeStruct((nrows, r.w), d) for r, d in zip(seen, row_grads) if d is not None] + \
                [jax.ShapeDtypeStruct(p.shape, F32) for p, g in zip(params, param_grads) if g]
    out_specs = [pl.BlockSpec((tm, r.w // ncol), lambda i, j: (i, j)) for r, d in zip(seen, row_grads) if d is not None] + \
                [_full_spec(p) for p, g in zip(params, param_grads) if g]
    return pl.pallas_call(
        body, name=name, grid=(nrows // tm, ncol),
        in_specs=r_specs + [_full_spec(p) for p in params] + c_specs + a_specs,
        out_specs=out_specs, out_shape=out_shape,
        compiler_params=_cparams(("arbitrary", "arbitrary")),
    )(*r_arrs, *params, *c_arrs, *a_arrs)


def _assemble(name, groups, dtype):
    sizes = [len(g) for g in groups]
    flat = [a for g in groups for a in g]

    def fn(*tiles):
        out, k = [], 0
        for s in sizes:
            acc = tiles[k].astype(F32)
            for t in tiles[k + 1:k + s]:
                acc = acc + t.astype(F32)
            out.append(acc.astype(dtype))
            k += s
        return out[0] if len(out) == 1 else jnp.concatenate(out, axis=1)

    width = sum(g[0].shape[-1] if not isinstance(g[0], Rows) else g[0].w for g in groups)
    return _rowwise(name, fn, flat, [], [(width, dtype)])[0]


def _matmul(name, a, b, mode, out_dtype, acc_in=None, also=None):
    if mode == 'nn':
        (M, K), N = a.shape, b.shape[1]
    elif mode == 'nt':
        (M, K), N = a.shape, b.shape[0]
    else:
        (K, M), N = a.shape, b.shape[1]
    tm, tn, tk = _tile(M, 1536), _tile(N, 1536), _tile(K, 2048 if mode == 'tn' else 3328)
    nk = K // tk
    dn = {'nn': NN, 'nt': NT, 'tn': TN}[mode]
    has_acc = acc_in is not None

    n_in = 3 if has_acc else 2
    dtypes = [out_dtype] + ([also] if also is not None else [])

    def body(*refs):
        a_ref, b_ref = refs[:2]
        o_refs = refs[n_in:n_in + len(dtypes)]
        part = lax.dot_general(a_ref[...].astype(BF16), b_ref[...].astype(BF16), dn, preferred_element_type=F32)
        if nk == 1:
            val = (refs[2][...] + part) if has_acc else part
            for o_ref in o_refs:
                o_ref[...] = val.astype(o_ref.dtype)
            return
        acc = refs[-1]

        @pl.when(pl.program_id(2) == 0)
        def _():
            acc[...] = refs[2][...] if has_acc else jnp.zeros_like(acc)
        acc[...] += part

        @pl.when(pl.program_id(2) == nk - 1)
        def _():
            for o_ref in o_refs:
                o_ref[...] = acc[...].astype(o_ref.dtype)

    a_spec = pl.BlockSpec((tk, tm), lambda i, j, k: (k, i)) if mode == 'tn' else pl.BlockSpec((tm, tk), lambda i, j, k: (i, k))
    b_spec = pl.BlockSpec((tn, tk), lambda i, j, k: (j, k)) if mode == 'nt' else pl.BlockSpec((tk, tn), lambda i, j, k: (k, j))
    o_spec = pl.BlockSpec((tm, tn), lambda i, j, k: (i, j))
    res = pl.pallas_call(
        body, name=name, grid=(M // tm, N // tn, nk),
        in_specs=[a_spec, b_spec] + ([o_spec] if has_acc else []), out_specs=[o_spec] * len(dtypes),
        out_shape=[jax.ShapeDtypeStruct((M, N), d) for d in dtypes],
        scratch_shapes=[] if nk == 1 else [pltpu.VMEM((tm, tn), F32)],
        compiler_params=_cparams(("parallel", "parallel", "arbitrary")),
    )(a, b, *([acc_in] if has_acc else []))
    return res[0] if also is None else res


def _ffn_up(name, x, gain, wg, wu):
    (M, K), N = x.shape, wg.shape[1]
    tm, tn = _tile(M, 512), _tile(N, 1536)

    def body(x_ref, n_ref, g_ref, u_ref, h_ref, gate_ref, up_ref, act_ref, hs):
        @pl.when(pl.program_id(1) == 0)
        def _():
            hs[...] = _rms(x_ref[...], n_ref[...]).astype(BF16)
            h_ref[...] = hs[...]
        a = hs[...]
        gate = _bdot(a, g_ref[...]).astype(BF16)
        up = _bdot(a, u_ref[...]).astype(BF16)
        gate_ref[...], up_ref[...] = gate, up
        act_ref[...] = _swiglu_fn(gate, up).astype(BF16)

    x_spec = pl.BlockSpec((tm, K), lambda i, j: (i, 0))
    w_spec = pl.BlockSpec((K, tn), lambda i, j: (0, j))
    o_spec = pl.BlockSpec((tm, tn), lambda i, j: (i, j))
    return pl.pallas_call(
        body, name=name, grid=(M // tm, N // tn),
        in_specs=[x_spec, pl.BlockSpec(gain.shape, lambda i, j: (0, 0)), w_spec, w_spec], out_specs=[x_spec] + [o_spec] * 3,
        out_shape=[jax.ShapeDtypeStruct((M, K), BF16)] + [jax.ShapeDtypeStruct((M, N), BF16)] * 3,
        scratch_shapes=[pltpu.VMEM((tm, K), BF16)],
        compiler_params=_cparams(("parallel", "arbitrary")),
    )(x, gain, wg, wu)


def _ffn_down_bwd(name, df, w_out, gate, up):
    (M, K), N = df.shape, w_out.shape[0]
    tm, tn = _tile(M, 512), _tile(N, 1536)

    def body(d_ref, w_ref, gate_ref, up_ref, dg_ref, du_ref):
        dact = _bdot(d_ref[...], w_ref[...], NT)
        dg, du = _swiglu_bwd((gate_ref[...], up_ref[...]), dact)
        dg_ref[...], du_ref[...] = dg, du

    o_spec = pl.BlockSpec((tm, tn), lambda i, j: (i, j))
    return pl.pallas_call(
        body, name=name, grid=(M // tm, N // tn),
        in_specs=[pl.BlockSpec((tm, K), lambda i, j: (i, 0)), pl.BlockSpec((tn, K), lambda i, j: (j, 0)), o_spec, o_spec],
        out_specs=[o_spec] * 2, out_shape=[jax.ShapeDtypeStruct((M, N), BF16)] * 2,
        compiler_params=_cparams(("parallel", "parallel")),
    )(df, w_out, gate, up)


def _small_dot(name, a, b, mode):
    dn = {'nn': (((1,), (0,)), ((), ())), 'nt': (((1,), (1,)), ((), ()))}[mode]
    M = a.shape[0]
    N = b.shape[1] if mode == 'nn' else b.shape[0]

    def body(a_ref, b_ref, o_ref):
        o_ref[...] = lax.dot_general(a_ref[...], b_ref[...], dn, precision=HIGHEST, preferred_element_type=F32)

    return pl.pallas_call(body, name=name, out_shape=jax.ShapeDtypeStruct((M, N), F32),
                          compiler_params=pltpu.CompilerParams(vmem_limit_bytes=VMEM_LIMIT))(a, b)


NN, NT, TN = (((1,), (0,)), ((), ())), (((1,), (1,)), ((), ())), (((0,), (0,)), ((), ()))


def _bdot(a, b, dn=NN):
    return lax.dot_general(a.astype(BF16), b.astype(BF16), dn, preferred_element_type=F32)


def _sigmoid(x):
    return 0.5 * jnp.tanh(0.5 * x) + 0.5


def _silu(x):
    return x * _sigmoid(x)


def _softplus(x):
    return jnp.maximum(x, 0.0) + jnp.log(1.0 + jnp.exp(-jnp.abs(x)))


def _gelu(x):
    return 0.5 * x * (1.0 + jnp.tanh(0.7978845608028654 * (x + 0.044715 * (x * x * x))))


def _rms(x, g):
    return x * lax.rsqrt(jnp.mean(x * x, axis=-1, keepdims=True) + EPS) * g


def _prenorm_fn(x, g):
    return _rms(x, g)


def _postnorm_fn(x, y, g):
    return x + _rms(y, g)


@jax.custom_vjp
def _swiglu_fn(gate, up):
    return _silu(gate.astype(F32)) * up.astype(F32)


def _swiglu_bwd(res, ct):
    g, u = res[0].astype(F32), res[1].astype(F32)
    s = _sigmoid(g)
    gs = g * s
    return (ct * u * (s + gs - gs * s)).astype(res[0].dtype), (ct * gs).astype(res[1].dtype)


_swiglu_fn.defvjp(lambda gate, up: (_swiglu_fn(gate, up), (gate, up)), _swiglu_bwd)


def _gla_pre_fn(z, wg, bg):
    logit = _bdot(z, wg) + bg
    return -_softplus(-logit) * (1.0 / GLA_TAU)


def _seg_mean(x, seg):
    return lax.dot_general(x, seg, NN, precision=HIGHEST, preferred_element_type=F32)


def _gla_post_fn(of, ob, g, norm, seg):
    o = of + ob
    o = o * lax.rsqrt(_seg_mean(o * o, seg) + EPS) * norm
    return o * _silu(g)


@jax.custom_vjp
def _swap_halves(x):
    n = x.shape[-1]
    lane = lax.broadcasted_iota(jnp.int32, x.shape, x.ndim - 1)
    lo = (lane & (HEAD_DIM - 1)) < HEAD_DIM // 2
    return jnp.where(lo, pltpu.roll(x, n - HEAD_DIM // 2, x.ndim - 1), pltpu.roll(x, HEAD_DIM // 2, x.ndim - 1))


_swap_halves.defvjp(lambda x: (_swap_halves(x), None), lambda _, g: (_swap_halves(g),))


def _rope_fn(q, k, cs, sn):
    return q * cs + _swap_halves(q) * sn, k * cs + _swap_halves(k) * sn


def _dil_comb_fn(o1, o2, o3, l1, l2, l3):
    m = jnp.maximum(jnp.maximum(l1, l2), l3)
    e1, e2, e3 = jnp.exp(l1 - m), jnp.exp(l2 - m), jnp.exp(l3 - m)
    return (e1 * o1 + e2 * o2 + e3 * o3) / (e1 + e2 + e3)


LRU_PARAMS = ['lru_cw0', 'lru_cw1', 'lru_cw2', 'lru_cw3', 'lru_cb', 'lru_wa0', 'lru_wa1', 'lru_ba0', 'lru_ba1',
              'lru_wx0', 'lru_wx1', 'lru_bx0', 'lru_bx1', 'lru_lam0', 'lru_lam1']


def _lru_pre_fn(x0, x1, x2, x3, cw0, cw1, cw2, cw3, cb, wa0, wa1, ba0, ba1, wx0, wx1, bx0, bx1, lam0, lam1):
    xc = cb + x0 * cw0 + x1 * cw1 + x2 * cw2 + x3 * cw3
    outs = []
    for wa, ba, wx, bx, lam in ((wa0, ba0, wx0, bx0, lam0), (wa1, ba1, wx1, bx1, lam1)):
        r = _sigmoid(_bdot(xc, wa) + ba)
        i = _sigmoid(_bdot(xc, wx) + bx)
        log_a = -LRU_C * r * _softplus(-lam)
        a = jnp.exp(log_a)
        u = jnp.sqrt(-jnp.tanh(log_a) * (a * a + 1.0)) * (i * xc)
        outs += [a, u]
    return outs[0], outs[2], outs[1], outs[3]


def _lru_post_fn(hf, hb, gate):
    return (hf + hb) * _gelu(gate)


def _mix_post_fn(of, ob, ga, yb, hf, hb, gc, o1, o2, o3, l1, l2, l3, norm, seg):
    ya = _gla_post_fn(of, ob, ga, norm, seg)
    yc = _lru_post_fn(hf, hb, gc)
    yd = _dil_comb_fn(o1, o2, o3, l1, l2, l3)
    return jnp.concatenate([ya.astype(BF16), yb.astype(BF16), yc.astype(BF16), yd.astype(BF16)], axis=1)


def _attn_heads(qs, kws, vws, biases):
    ss = [_bdot(q, kw, NT) * (HEAD_DIM ** -0.5) + b for q, kw, b in zip(qs, kws, biases)]
    ms = [lax.stop_gradient(jnp.max(s, axis=-1, keepdims=True)) for s in ss]
    es = [jnp.exp(s - m) for s, m in zip(ss, ms)]
    dens = [jnp.sum(e, axis=-1, keepdims=True) for e in es]
    ps = [e * (1.0 / d) for e, d in zip(es, dens)]
    os_ = [_bdot(p_, vw) for p_, vw in zip(ps, vws)]
    return os_, [m + jnp.log(d) for m, d in zip(ms, dens)]


def _cumsum_rows(x, rev):
    n = x.shape[0]
    row = lax.broadcasted_iota(jnp.int32, x.shape, 0)
    s = 1
    while s < n:
        if rev:
            x = x + jnp.where(row < n - s, pltpu.roll(x, n - s, 0), 0.0)
        else:
            x = x + jnp.where(row >= s, pltpu.roll(x, s, 0), 0.0)
        s *= 2
    return x


def _gla_chunks(qs, ks, vs, bs, sts, revs):
    C = qs[0].shape[0]
    ti = lax.broadcasted_iota(jnp.int32, (C, C), 0)
    si = lax.broadcasted_iota(jnp.int32, (C, C), 1)
    row = lax.broadcasted_iota(jnp.int32, (C, 1), 0)
    incl = {False: si <= ti, True: si >= ti}
    last = {False: row == C - 1, True: row == 0}
    mid = {False: row == C // 2 - 1, True: row == C // 2}
    bls = [jnp.sum(jnp.where(last[r], b, 0.0), axis=0, keepdims=True) for b, r in zip(bs, revs)]
    bms = [jnp.sum(jnp.where(mid[r], b, 0.0), axis=0, keepdims=True) for b, r in zip(bs, revs)]
    qss = [q * (HEAD_DIM ** -0.5) for q in qs]
    qi = [q * jnp.exp(b - bm) for q, b, bm in zip(qss, bs, bms)]
    ki = [k * jnp.exp(bm - b) for k, b, bm in zip(ks, bs, bms)]
    atts = [jnp.where(incl[r], _bdot(a, b, NT), 0.0) for a, b, r in zip(qi, ki, revs)]
    qe = [q * jnp.exp(b) for q, b in zip(qss, bs)]
    kl = [k * jnp.exp(bl - b) for k, b, bl in zip(ks, bs, bls)]
    o1 = [_bdot(a, v) for a, v in zip(atts, vs)]
    o2 = [_bdot(q, st, NT) for q, st in zip(qe, sts)]
    kvs = [_bdot(v, k, TN) for v, k in zip(vs, kl)]
    return [a + b for a, b in zip(o1, o2)], [st * jnp.exp(bl) + kv for st, bl, kv in zip(sts, bls, kvs)]


def _gla_specs(n, blocks, first):
    C = GLA_CHUNK
    at = (lambda i: i) if first else (lambda i: n - 1 - i)
    return [pl.BlockSpec((C, GROUP_WIDTH), lambda i, b=b: (at(i), b)) for b in blocks], at


def _gla_scan_fwd(p, la):
    L = p.shape[0]
    C, H, dh = GLA_CHUNK, GROUP_HEADS, HEAD_DIM
    n = L // C
    f_specs, f_at = _gla_specs(n, (P_QA, P_KA, P_VA), True)
    b_specs, b_at = _gla_specs(n, (P_QA, P_KA, P_VA), False)
    tile = lambda at, blk=0: pl.BlockSpec((C, GROUP_WIDTH), lambda i: (at(i), blk))
    st_spec = lambda at: pl.BlockSpec((None, H, dh, dh), lambda i: (at(i), 0, 0, 0))

    def body(qf, kf, vf, lf, qb, kb, vb, lb, of_ref, ob_ref, sf_ref, sb_ref, stf, stb):
        @pl.when(pl.program_id(0) == 0)
        def _():
            stf[...] = jnp.zeros_like(stf)
            stb[...] = jnp.zeros_like(stb)
        sf_ref[...] = stf[...]
        sb_ref[...] = stb[...]
        chains = [(t, h, sl) for t in ((qf, kf, vf, _cumsum_rows(lf[...], False), of_ref, stf, False),
                                       (qb, kb, vb, _cumsum_rows(lb[...], True), ob_ref, stb, True))
                  for h, sl in enumerate(HEADS)]
        os_, sts = _gla_chunks(*[[t[j][:, sl] for t, h, sl in chains] for j in range(4)],
                               [t[5][h] for t, h, sl in chains], [t[6] for t, h, sl in chains])
        for (t, h, sl), o, st_new in zip(chains, os_, sts):
            t[4][:, sl] = o
            t[5][h] = st_new

    return pl.pallas_call(
        body, name="gla_scan", grid=(n,),
        in_specs=f_specs + [tile(f_at, 0)] + b_specs + [tile(b_at, 1)],
        out_specs=[tile(f_at), tile(b_at), st_spec(f_at), st_spec(b_at)],
        out_shape=[jax.ShapeDtypeStruct((L, GROUP_WIDTH), F32)] * 2 + [jax.ShapeDtypeStruct((n, H, dh, dh), F32)] * 2,
        scratch_shapes=[pltpu.VMEM((H, dh, dh), F32)] * 2,
        compiler_params=_cparams(("arbitrary",)),
    )(p, p, p, la, p, p, p, la)


def _gla_scan_bwd(p, la, sf, sb, do):
    L = p.shape[0]
    C, H, dh = GLA_CHUNK, GROUP_HEADS, HEAD_DIM
    n = L // C
    f_specs, f_at = _gla_specs(n, (P_QA, P_KA, P_VA), False)
    b_specs, b_at = _gla_specs(n, (P_QA, P_KA, P_VA), True)
    tile = lambda at, blk=0: pl.BlockSpec((C, GROUP_WIDTH), lambda i: (at(i), blk))
    st_spec = lambda at: pl.BlockSpec((None, H, dh, dh), lambda i: (at(i), 0, 0, 0))

    def body(qf, kf, vf, lf, spf, dof, qb, kb, vb, lb, spb, dob, *rest):
        outs_f, outs_b, (dstf, dstb) = rest[0:4], rest[4:8], rest[8:10]

        @pl.when(pl.program_id(0) == 0)
        def _():
            dstf[...] = jnp.zeros_like(dstf)
            dstb[...] = jnp.zeros_like(dstb)
        chains = [(t, h, sl) for t in ((qf, kf, vf, _cumsum_rows(lf[...], False), spf, dof, outs_f, dstf, False),
                                       (qb, kb, vb, _cumsum_rows(lb[...], True), spb, dob, outs_b, dstb, True))
                  for h, sl in enumerate(HEADS)]
        nc = len(chains)
        revs = [t[8] for t, h, sl in chains]
        flat = [t[j][:, sl] for j in range(4) for t, h, sl in chains] + [t[4][h] for t, h, sl in chains]

        def f(*a):
            os_, sts = _gla_chunks(*[list(a[j * nc:(j + 1) * nc]) for j in range(5)], revs)
            return tuple(os_) + tuple(sts)

        _, vjp = jax.vjp(f, *flat)
        grads = vjp(tuple(t[5][:, sl] for t, h, sl in chains) + tuple(t[7][h] for t, h, sl in chains))
        for c_, (t, h, sl) in enumerate(chains):
            for j in range(4):
                t[6][j][:, sl] = grads[j * nc + c_].astype(t[6][j].dtype)
            t[7][h] = grads[4 * nc + c_]
        for outs, rev in ((outs_f, False), (outs_b, True)):
            outs[3][...] = _cumsum_rows(outs[3][...], not rev)

    return pl.pallas_call(
        body, name="gla_scan_b", grid=(n,),
        in_specs=f_specs + [tile(f_at, 0), st_spec(f_at), tile(f_at)] + b_specs + [tile(b_at, 1), st_spec(b_at), tile(b_at)],
        out_specs=[tile(f_at)] * 4 + [tile(b_at)] * 4,
        out_shape=[jax.ShapeDtypeStruct((L, GROUP_WIDTH), d) for d in (BF16, BF16, BF16, F32)] * 2,
        scratch_shapes=[pltpu.VMEM((H, dh, dh), F32)] * 2,
        compiler_params=_cparams(("arbitrary",)),
    )(p, p, p, la, sf, do, p, p, p, la, sb, do)


NA_W = NA_ROWS * GRID_W
NA_BW = (2 * NA_ROWS - 1) * GRID_W


def _na_start(i, rows):
    return jnp.clip(i - NA_ROWS // 2, 0, rows - NA_ROWS)


def _na_fwd(p16, btab):
    L = p16.shape[0]
    rows = L // GRID_W
    kv = lambda blk: pl.BlockSpec((L, GROUP_WIDTH), lambda i: (0, blk))

    def body(q_ref, k_ref, v_ref, b_ref, o_ref):
        r = pl.program_id(0)
        s = _na_start(r, rows)
        start = pl.multiple_of(s * GRID_W, GRID_W)
        os_, _ = _attn_heads([q_ref[:, sl] for sl in HEADS], [k_ref[pl.ds(start, NA_W), sl] for sl in HEADS],
                             [v_ref[pl.ds(start, NA_W), sl] for sl in HEADS],
                             [b_ref[s - r + NA_ROWS - 1, h] for h in range(GROUP_HEADS)])
        for sl, o in zip(HEADS, os_):
            o_ref[:, sl] = o.astype(o_ref.dtype)

    whole = lambda a: pl.BlockSpec(a.shape, lambda i, nd=a.ndim: (0,) * nd)
    return pl.pallas_call(
        body, name="na_attn", grid=(rows,),
        in_specs=[pl.BlockSpec((GRID_W, GROUP_WIDTH), lambda i: (i, P_QB)), kv(P_KB), kv(P_VB), whole(btab)],
        out_specs=pl.BlockSpec((GRID_W, GROUP_WIDTH), lambda i: (i, 0)),
        out_shape=jax.ShapeDtypeStruct((L, GROUP_WIDTH), BF16),
        compiler_params=_cparams(("arbitrary",)),
    )(p16, p16, p16, btab)


def _na_bwd(p16, btab, dycat):
    L = p16.shape[0]
    rows = L // GRID_W
    kv = lambda blk: pl.BlockSpec((L, GROUP_WIDTH), lambda i: (0, blk))
    flush = NA_ROWS - 1
    emit = lambda i: jnp.where(i < rows, _na_start(i, rows), i - flush)

    def body(q_ref, k_ref, v_ref, b_ref, do_ref, dq_ref, dk_ref, dv_ref, db_ref, acc_k, acc_v):
        i = pl.program_id(0)

        @pl.when(i == 0)
        def _():
            acc_k[...] = jnp.zeros_like(acc_k)
            acc_v[...] = jnp.zeros_like(acc_v)
            db_ref[...] = jnp.zeros_like(db_ref)

        @pl.when((i > 0) & (emit(i) != emit(i - 1)))
        def _():
            for acc in (acc_k, acc_v):
                moved = acc[GRID_W:NA_W, :]
                acc[0:NA_W - GRID_W, :] = moved
                acc[NA_W - GRID_W:NA_W, :] = jnp.zeros((GRID_W, GROUP_WIDTH), F32)

        @pl.when(i < rows)
        def _():
            s = _na_start(i, rows)
            sv = s - i + NA_ROWS - 1
            start = pl.multiple_of(s * GRID_W, GRID_W)
            H = GROUP_HEADS
            flat = [q_ref[:, sl].astype(F32) for sl in HEADS] + [k_ref[pl.ds(start, NA_W), sl].astype(F32) for sl in HEADS] + \
                   [v_ref[pl.ds(start, NA_W), sl].astype(F32) for sl in HEADS] + [b_ref[sv, h] for h in range(H)]

            def f(*a):
                os_, lses = _attn_heads(a[0:H], a[H:2 * H], a[2 * H:3 * H], a[3 * H:4 * H])
                return tuple(os_) + tuple(lses)

            _, vjp = jax.vjp(f, *flat)
            grads = vjp(tuple(do_ref[:, sl] for sl in HEADS) + (jnp.zeros((GRID_W, 1), F32),) * H)
            for h, sl in enumerate(HEADS):
                dq_ref[:, sl] = grads[h].astype(dq_ref.dtype)
                acc_k[:, sl] += grads[H + h]
                acc_v[:, sl] += grads[2 * H + h]
                db_ref[sv, h] += grads[3 * H + h]

        dk_ref[...] = acc_k[0:GRID_W, :].astype(dk_ref.dtype)
        dv_ref[...] = acc_v[0:GRID_W, :].astype(dv_ref.dtype)

    whole = lambda a: pl.BlockSpec(a.shape, lambda i, nd=a.ndim: (0,) * nd)
    qrow = lambda blk: pl.BlockSpec((GRID_W, GROUP_WIDTH), lambda i: (jnp.minimum(i, rows - 1), blk))
    erow = pl.BlockSpec((GRID_W, GROUP_WIDTH), lambda i: (emit(i), 0))
    return pl.pallas_call(
        body, name="na_attn_b", grid=(rows + flush,),
        in_specs=[qrow(P_QB), kv(P_KB), kv(P_VB), whole(btab), qrow(1)],
        out_specs=[qrow(0), erow, erow, whole(btab)],
        out_shape=[jax.ShapeDtypeStruct((L, GROUP_WIDTH), BF16)] * 3 + [jax.ShapeDtypeStruct(btab.shape, F32)],
        scratch_shapes=[pltpu.VMEM((NA_W, GROUP_WIDTH), F32)] * 2,
        compiler_params=_cparams(("arbitrary",)),
    )(p16, p16, p16, btab, dycat)


def _na_col_ok():
    qc = np.arange(GRID_W)[:, None]
    kc = (np.arange(NA_W) % GRID_W)[None, :]
    c0 = np.clip(qc - NA_COLS // 2, 0, GRID_W - NA_COLS)
    return (kc >= c0) & (kc < c0 + NA_COLS)


def _rpb_tables():
    c = np.arange(GRID_W)
    dc = np.clip(c[None, :] - c[:, None], -(NA_COLS - 1), NA_COLS - 1) + NA_COLS - 1
    t = np.zeros((2 * NA_COLS - 1, GRID_W, GRID_W), np.float32)
    t[dc, c[:, None], c[None, :]] = 1.0
    return jnp.asarray(t.reshape(2 * NA_COLS - 1, GRID_W * GRID_W))


def _rpb_expand(rpb, tab):
    H = rpb.shape[0]
    xt = _small_dot("na_bias", rpb.reshape(H * (2 * NA_ROWS - 1), 2 * NA_COLS - 1), tab, 'nn')
    b15 = xt.reshape(H, 2 * NA_ROWS - 1, GRID_W, GRID_W).transpose(0, 2, 1, 3).reshape(H, GRID_W, NA_BW)
    ok = jnp.asarray(_na_col_ok())
    return jnp.stack([jnp.where(ok, b15[:, :, sv * GRID_W:sv * GRID_W + NA_W], NEG) for sv in range(NA_ROWS)])


def _rpb_contract(dbv, tab):
    H = dbv.shape[1]
    db = sum(jnp.pad(dbv[sv], ((0, 0), (0, 0), (sv * GRID_W, NA_BW - NA_W - sv * GRID_W))) for sv in range(NA_ROWS))
    dx = db.reshape(H, GRID_W, 2 * NA_ROWS - 1, GRID_W).transpose(0, 2, 1, 3).reshape(H * (2 * NA_ROWS - 1), GRID_W * GRID_W)
    return _small_dot("na_bias_b", dx, tab, 'nt').reshape(H, 2 * NA_ROWS - 1, 2 * NA_COLS - 1)


def _band_bias(i, tq, w, halo, n):
    a = lax.broadcasted_iota(jnp.int32, (tq, w), 0)
    b = lax.broadcasted_iota(jnp.int32, (tq, w), 1)
    kpos = i * tq - halo + b
    d = b - halo - a
    return jnp.where((d <= halo) & (d >= -halo) & (kpos >= 0) & (kpos < n), 0.0, NEG)


def _band_fwd(name, q, kp, vp, tq, halo):
    G, n, _ = q.shape
    w = tq + 2 * halo

    def body(q_ref, k_ref, v_ref, o_ref, l_ref):
        i = pl.program_id(1)
        start = pl.multiple_of(i * tq, tq)
        bias = _band_bias(i, tq, w, halo, n)
        os_, lses = _attn_heads([q_ref[:, sl] for sl in HEADS], [k_ref[pl.ds(start, w), sl] for sl in HEADS],
                                [v_ref[pl.ds(start, w), sl] for sl in HEADS], [bias] * GROUP_HEADS)
        for sl, o, lse in zip(HEADS, os_, lses):
            o_ref[:, sl] = o.astype(o_ref.dtype)
            l_ref[:, sl] = jnp.broadcast_to(lse, (tq, HEAD_DIM))

    qblk = pl.BlockSpec((None, tq, GROUP_WIDTH), lambda g, i: (g, i, 0))
    kblk = pl.BlockSpec((None, n + 2 * halo, GROUP_WIDTH), lambda g, i: (g, 0, 0))
    return pl.pallas_call(
        body, name=name, grid=(G, n // tq), in_specs=[qblk, kblk, kblk], out_specs=[qblk, qblk],
        out_shape=[jax.ShapeDtypeStruct((G, n, GROUP_WIDTH), d) for d in (BF16, F32)],
        compiler_params=_cparams(("parallel", "arbitrary")),
    )(q, kp, vp)


def _band_bwd(name, q, kp, vp, do, dl, tq, halo):
    G, n, _ = q.shape
    w = tq + 2 * halo
    nq = n // tq

    def body(q_ref, k_ref, v_ref, do_ref, dl_ref, dq_ref, dk_ref, dv_ref, acc_k, acc_v):
        i = pl.program_id(1)

        @pl.when(i == 0)
        def _():
            acc_k[...] = jnp.zeros_like(acc_k)
            acc_v[...] = jnp.zeros_like(acc_v)

        @pl.when(i > 0)
        def _():
            for acc in (acc_k, acc_v):
                moved = acc[tq:w, :]
                acc[0:2 * halo, :] = moved
                acc[2 * halo:w, :] = jnp.zeros((tq, GROUP_WIDTH), F32)

        @pl.when(i < nq)
        def _():
            start = pl.multiple_of(i * tq, tq)
            bias = _band_bias(i, tq, w, halo, n)
            H = GROUP_HEADS
            flat = [q_ref[:, sl].astype(F32) for sl in HEADS] + [k_ref[pl.ds(start, w), sl].astype(F32) for sl in HEADS] + \
                   [v_ref[pl.ds(start, w), sl].astype(F32) for sl in HEADS]

            def f(*a):
                os_, lses = _attn_heads(a[0:H], a[H:2 * H], a[2 * H:3 * H], [bias] * H)
                return tuple(os_) + tuple(lses)

            _, vjp = jax.vjp(f, *flat)
            grads = vjp(tuple(do_ref[:, sl].astype(F32) for sl in HEADS) +
                        tuple(jnp.sum(dl_ref[:, sl], axis=1, keepdims=True) for sl in HEADS))
            for h, sl in enumerate(HEADS):
                dq_ref[:, sl] = grads[h].astype(dq_ref.dtype)
                acc_k[:, sl] += grads[H + h]
                acc_v[:, sl] += grads[2 * H + h]

        dk_ref[...] = acc_k[0:tq, :].astype(dk_ref.dtype)
        dv_ref[...] = acc_v[0:tq, :].astype(dv_ref.dtype)

    qblk = pl.BlockSpec((None, tq, GROUP_WIDTH), lambda g, i: (g, jnp.minimum(i, nq - 1), 0))
    kblk = pl.BlockSpec((None, n + 2 * halo, GROUP_WIDTH), lambda g, i: (g, 0, 0))
    eblk = pl.BlockSpec((None, tq, GROUP_WIDTH), lambda g, i: (g, i, 0))
    return pl.pallas_call(
        body, name=name, grid=(G, nq + 1), in_specs=[qblk, kblk, kblk, qblk, qblk], out_specs=[qblk, eblk, eblk],
        out_shape=[jax.ShapeDtypeStruct((G, n, GROUP_WIDTH), BF16)] + [jax.ShapeDtypeStruct((G, (nq + 1) * tq, GROUP_WIDTH), BF16)] * 2,
        scratch_shapes=[pltpu.VMEM((w, GROUP_WIDTH), F32)] * 2,
        compiler_params=_cparams(("parallel", "arbitrary")),
    )(q, kp, vp, do, dl)


def _lin_scans(name, jobs):
    L, C = jobs[0][0].shape
    tt = 256 if L % 256 == 0 else L
    nt, per, last = L // tt, tt // HALO, L // HALO - 1
    specs, arrs, plan = [], [], []
    for coef, inp, rev, shift in jobs:
        tile = (lambda i: nt - 1 - i) if rev else (lambda i: i)
        blk = pl.BlockSpec((tt, C), lambda i, tile=tile: (tile(i), 0))
        if shift:
            side = lambda d, tile=tile: pl.BlockSpec(
                (HALO, C), lambda i: (jnp.clip(tile(i) * per + (per if d > 0 else -1), 0, last), 0))
            specs += [side(-1), blk, side(1), blk]
            arrs += [coef, coef, coef, inp]
        else:
            specs += [blk, blk]
            arrs += [coef, inp]
        plan.append((tile, blk, rev, shift))
    n_in = len(specs)

    def body(*refs):
        o_refs, carries = refs[n_in:n_in + len(jobs)], refs[n_in + len(jobs):]

        @pl.when(pl.program_id(0) == 0)
        def _():
            for carry in carries:
                carry[...] = jnp.zeros_like(carry)
        row = lax.broadcasted_iota(jnp.int32, (tt, C), 0)
        au, k = [], 0
        for tile, _, rev, shift in plan:
            if shift:
                a = _shift_tile(refs[k][...], refs[k + 1][...], refs[k + 2][...], shift, tile(pl.program_id(0)), nt)
                k += 3
            else:
                a = refs[k][...]
                k += 1
            au.append([a, refs[k][...]])
            k += 1
        s = 1
        while s < tt:
            for (tile, _, rev, shift), st in zip(plan, au):
                a, u = st
                ok = (row < tt - s) if rev else (row >= s)
                sh = tt - s if rev else s
                st[1] = u + a * jnp.where(ok, pltpu.roll(u, sh, 0), 0.0)
                st[0] = a * jnp.where(ok, pltpu.roll(a, sh, 0), 1.0)
            s *= 2
        for (tile, _, rev, shift), (a, u), o_ref, carry in zip(plan, au, o_refs, carries):
            out = u + a * carry[...]
            o_ref[...] = out
            carry[...] = out[0:1] if rev else out[tt - 1:tt]

    return pl.pallas_call(
        body, name=name, grid=(nt,), in_specs=specs, out_specs=[p_[1] for p_ in plan],
        out_shape=[jax.ShapeDtypeStruct((L, C), F32)] * len(jobs), scratch_shapes=[pltpu.VMEM((1, C), F32)] * len(jobs),
        compiler_params=_cparams(("arbitrary",)),
    )(*arrs)


def _pieces(shape):
    n0 = max(d for d in range(1, DMA_PIECES + 1) if shape[0] % d == 0)
    n1 = 1
    if len(shape) >= 3:
        n1 = max(d for d in range(1, DMA_PIECES // n0 + 1) if shape[1] % d == 0)
    s0, s1 = shape[0] // n0, (shape[1] // n1 if len(shape) >= 3 else 0)
    out = []
    for i in range(n0):
        for j in range(n1):
            out.append((pl.ds(i * s0, s0),) + ((pl.ds(j * s1, s1),) if len(shape) >= 3 else ()))
    return out


def _exchange(name, src, axes, gather):
    flips = {'xy': [(1, 0, 0), (0, 1, 0), (1, 1, 0)], 'c': [(0, 0, 1)],
             'xyc': [(fx, fy, fc) for fx in (0, 1) for fy in (0, 1) for fc in (0, 1)][1:]}[axes]
    n = len(flips) + 1
    blk_shape = tuple(src.shape if gather else src.shape[1:])
    pieces = _pieces(blk_shape)

    def number(px, py, pc):
        return {'xy': 2 * px + py, 'c': pc, 'xyc': 4 * px + 2 * py + pc}[axes]

    def body(src_ref, out_ref, send_sems, recv_sems):
        x, y, c = lax.axis_index("x"), lax.axis_index("y"), lax.axis_index("c")
        me = number(x, y, c)
        piece = (lambda k: src_ref) if gather else (lambda k: src_ref.at[k])
        peers = []
        for s, (fx, fy, fc) in enumerate(flips):
            px, py, pc = (x + fx) % 2, (y + fy) % 2, (c + fc) % 2

            def copy(ix, s=s, px=px, py=py, pc=pc):
                part = (lambda r: r) if ix is None else (lambda r: r.at[ix])
                return pltpu.make_async_remote_copy(
                    src_ref=part(piece(number(px, py, pc))), dst_ref=part(out_ref.at[me]),
                    send_sem=send_sems.at[s], recv_sem=recv_sems.at[s],
                    device_id=(px, py, pc), device_id_type=MESH)

            for ix in pieces:
                copy(ix).start()
            peers.append(copy)
        for copy in peers:
            copy(None).wait()

    out = pl.pallas_call(
        body, name=name, out_shape=jax.ShapeDtypeStruct((n,) + blk_shape, src.dtype),
        in_specs=[pl.BlockSpec(memory_space=pl.ANY)], out_specs=pl.BlockSpec(memory_space=pl.ANY),
        scratch_shapes=[pltpu.SemaphoreType.DMA((n - 1,)), pltpu.SemaphoreType.DMA((n - 1,))],
    )(src)
    me = number(lax.axis_index("x"), lax.axis_index("y"), lax.axis_index("c"))
    own = src if gather else lax.dynamic_index_in_dim(src, me, 0, keepdims=False)
    return lax.dynamic_update_index_in_dim(out, own, me, 0)


def _ordered_sum(name, buf, dtype=F32):
    n = buf.shape[0]

    def fn(*t):
        acc = t[0].astype(F32)
        for v in t[1:]:
            acc = acc + v.astype(F32)
        return acc

    return _rowwise(name, fn, [Rows(buf, lead=k) for k in range(n)], [], [(buf.shape[-1], dtype)])[0]


def _reduce_big(name, g):
    mine = _ordered_sum(name + "_sum_c", _exchange(name + "_swap_c", g, 'c', False).reshape(2, -1, g.shape[-1]), BF16)
    mine = mine.reshape(g.shape[1:])
    tot = _ordered_sum(name + "_sum_xy", _exchange(name + "_a2a_xy", mine, 'xy', False))
    return _exchange(name + "_share_c", tot, 'c', True)


def _dilate(t, dil):
    L, C = t.shape
    return t.reshape(L // dil, dil, C).transpose(1, 0, 2)


def _undilate(t):
    dil, n, C = t.shape
    return t.transpose(1, 0, 2).reshape(dil * n, C)


def _pad_rows(t, halo):
    return jnp.pad(t, ((0, 0), (halo, halo), (0, 0)))


def _pcol(p, blk):
    return Rows(p, GROUP_WIDTH, blk)


def _pslice(p, blk):
    return p[:, blk * GROUP_WIDTH:(blk + 1) * GROUP_WIDTH]


def _conv_taps(p):
    return Rows(p, GROUP_WIDTH, P_XC, shifts=[LRU_CONV_LEFT - j for j in range(LRU_CONV)])


def _seg_matrix():
    h = np.arange(GROUP_WIDTH) // HEAD_DIM
    return jnp.asarray((h[:, None] == h[None, :]).astype(np.float32) / HEAD_DIM)


def _rope_tables(L):
    pos = jnp.arange(L, dtype=F32)
    inv_freq = ROPE_THETA ** (-jnp.arange(0, HEAD_DIM, 2, dtype=F32) / HEAD_DIM)
    ang = pos[:, None] * inv_freq[None, :]
    cos, sin = jnp.cos(ang), jnp.sin(ang)
    cs = jnp.tile(jnp.concatenate([cos, cos], axis=1), (1, GROUP_HEADS))
    sn = jnp.tile(jnp.concatenate([-sin, sin], axis=1), (1, GROUP_HEADS))
    return cs, sn


def _dil_branches(L):
    out = []
    for window, dil in DIL_PAIRS:
        radius = window // (2 * dil)
        n = L // dil
        out.append((dil, radius, min(256, n)))
    return out


def _dil_operands(qr, kr, p16, dil, radius):
    return _dilate(qr, dil), _pad_rows(_dilate(kr, dil), radius), _pad_rows(_dilate(_pslice(p16, P_VD), dil), radius)


def _layer_fwd(x, w, c):
    L, D = x.shape
    sv = {'x_in': x}
    h = _rowwise("mix_prenorm", _prenorm_fn, [x], [w['mix_norm_pre']], [(D, BF16)])[0]
    p, p16 = _matmul("mix_proj", h, w['w_in'], 'nn', F32, also=BF16)
    sv.update(p=p, p16=p16, h=h)

    la = _rowwise("gla_pre", _gla_pre_fn, [Rows(p, LANE, P_Z // LANE)], [w['gla_wg'], w['gla_bg']], [(2 * GROUP_WIDTH, F32)])[0]
    of, ob, sf, sb = _gla_scan_fwd(p, la)
    sv.update(la=la, sf=sf, sb=sb, of=of, ob=ob)

    yb = _na_fwd(p16, _rpb_expand(w['na_rpb'], c['rpb_tab']))

    a0, a1, u0, u1 = _rowwise("lru_pre", _lru_pre_fn, [_conv_taps(p)], [w[k] for k in LRU_PARAMS], [(GROUP_WIDTH, F32)] * 4)
    hf, hb = _lin_scans("lru_scan", [(a0, u0, False, 0), (a1, u1, True, 0)])
    sv.update(a0=a0, a1=a1, hf=hf, hb=hb)

    qr, kr = _rowwise("rope", _rope_fn, [_pcol(p, P_QD), _pcol(p, P_KD), c['cos'], c['sin']], [], [(GROUP_WIDTH, BF16)] * 2)
    os_, ls_, ops = [], [], []
    for dil, radius, tq in _dil_branches(L):
        ops.append(_dil_operands(qr, kr, p16, dil, radius))
        o, lse = _band_fwd(f"dil_attn{dil}", *ops[-1], tq, radius)
        os_.append(_undilate(o))
        ls_.append(_undilate(lse))
    sv.update(dil_ops=ops, dil_o=os_, dil_l=ls_)

    ycat = _rowwise("mix_post", _mix_post_fn, [of, ob, _pcol(p, P_GA), yb, hf, hb, _pcol(p, P_GC)] + os_ + ls_,
                    [w['gla_norm'], c['seg']], [(4 * GROUP_WIDTH, BF16)])[0]
    y = _matmul("mix_out", ycat, w['w_out'], 'nn', F32)
    xm = _rowwise("mix_postnorm", _postnorm_fn, [x, y], [w['mix_norm_post']], [(D, F32)])[0]
    sv.update(ycat=ycat, y=y, x_mid=xm)

    h2, gate, up, act = _ffn_up("ffn_up", xm, w['ffn_norm_pre'], w['ffn_wg'], w['ffn_wu'])
    f = _matmul("ffn_out", act, w['ffn_w_out'], 'nn', F32)
    xo = _rowwise("ffn_postnorm", _postnorm_fn, [xm, f], [w['ffn_norm_post']], [(D, F32)])[0]
    sv.update(gate=gate, up=up, act=act, f=f, h2=h2)
    return xo, sv


def _layer_bwd(dx, w, c, sv):
    L, D = dx.shape
    g = {}
    as_f32 = lambda t: (t.astype(F32),)
    df, g['ffn_norm_post'] = _rowwise_bwd("ffn_postnorm_b", lambda y, gn: _rms(y, gn), [sv['f']], [w['ffn_norm_post']],
                                          [dx], as_f32, [BF16], [True])
    dgate, dup = _ffn_down_bwd("ffn_out_bx", df, w['ffn_w_out'], sv['gate'], sv['up'])
    g['ffn_w_out'] = _matmul("ffn_out_bw", sv['act'], df, 'tn', F32)
    dh2 = _matmul("ffn_up_bx", dup, w['ffn_wu'], 'nt', F32, acc_in=_matmul("ffn_gate_bx", dgate, w['ffn_wg'], 'nt', F32))
    xm = sv['x_mid']
    g['ffn_wg'] = _matmul("ffn_gate_bw", sv['h2'], dgate, 'tn', F32)
    g['ffn_wu'] = _matmul("ffn_up_bw", sv['h2'], dup, 'tn', F32)
    dxm, g['ffn_norm_pre'] = _rowwise_bwd("ffn_prenorm_b", _prenorm_fn, [xm], [w['ffn_norm_pre']], [dh2], as_f32, [F32], [True],
                                          row_grad_add=[dx])

    dy, g['mix_norm_post'] = _rowwise_bwd("mix_postnorm_b", lambda y, gn: _rms(y, gn), [sv['y']], [w['mix_norm_post']],
                                          [dxm], as_f32, [BF16], [True])
    dycat = _matmul("mix_out_bx", dy, w['w_out'], 'nt', F32)
    g['w_out'] = _matmul("mix_out_bw", sv['ycat'], dy, 'tn', F32)
    p = sv['p']
    dya, dyb, dyc, dyd = (Rows(dycat, GROUP_WIDTH, k) for k in range(4))

    dof, dga, g['gla_norm'] = _rowwise_bwd("gla_post_b", _gla_post_fn, [sv['of'], sv['ob'], _pcol(p, P_GA)],
                                           [w['gla_norm'], c['seg']], [dya], as_f32, [F32, None, BF16], [True, False])
    la = sv['la']
    dqf, dkf, dvf, dlf, dqb_, dkb_, dvb_, dlb = _gla_scan_bwd(p, la, sv['sf'], sv['sb'], dof)
    dz, g['gla_wg'], g['gla_bg'] = _rowwise_bwd("gla_pre_b", _gla_pre_fn, [Rows(p, LANE, P_Z // LANE)], [w['gla_wg'], w['gla_bg']],
                                                [dlf, dlb], lambda a, b: (jnp.concatenate([a, b], axis=1),), [BF16], [True, True])

    btab = _rpb_expand(w['na_rpb'], c['rpb_tab'])
    dqn, dkn, dvn, dbt = _na_bwd(sv['p16'], btab, dycat)
    g['na_rpb'] = _rpb_contract(dbt, c['rpb_tab'])

    dh, dgc = _rowwise_bwd("lru_post_b", _lru_post_fn, [sv['hf'], sv['hb'], _pcol(p, P_GC)], [], [dyc], as_f32, [F32, None, BF16], [])
    lam0, lam1 = _lin_scans("lru_scan_b", [(sv['a0'], dh, True, -1), (sv['a1'], dh, False, 1)])
    res = _rowwise_bwd("lru_pre_b", _lru_pre_fn, [_conv_taps(p)], [w[k] for k in LRU_PARAMS],
                       [lam0, lam1, Rows(sv['hf'], shifts=[1]), Rows(sv['hb'], shifts=[-1])],
                       lambda l0, l1, hfp, hbn: (l0 * hfp, l1 * hbn, l0, l1), [F32] * 4, [True] * len(LRU_PARAMS))
    dxs = res[:4]
    for k, nm in enumerate(LRU_PARAMS):
        g[nm] = res[4 + k]
    dxc = [Rows(dxs[j], shifts=[j - LRU_CONV_LEFT]) for j in range(LRU_CONV)]

    comb = _rowwise_bwd("dil_comb_b", _dil_comb_fn, sv['dil_o'] + sv['dil_l'], [], [dyd], as_f32, [BF16] * 3 + [F32] * 3, [])
    dqs, dks, dvs = [], [], []
    for k, (dil, radius, tq) in enumerate(_dil_branches(L)):
        n = L // dil
        dq_, dk_, dv_ = _band_bwd(f"dil_attn{dil}_b", *sv['dil_ops'][k], _dilate(comb[k], dil), _dilate(comb[3 + k], dil), tq, radius)
        dqs.append(_undilate(dq_))
        dks.append(_undilate(dk_[:, radius:radius + n]))
        dvs.append(_undilate(dv_[:, radius:radius + n]))
    dqd, dkd = _rowwise_bwd("rope_b", _rope_fn, [_pcol(p, P_QD), _pcol(p, P_KD), c['cos'], c['sin']], [], dqs + dks,
                            lambda *t: (sum(v.astype(F32) for v in t[:3]), sum(v.astype(F32) for v in t[3:])), [BF16, BF16, None, None], [])

    dp = _assemble("mix_dp", [[dqf, dqb_], [dkf, dkb_], [dvf, dvb_], [dga], [dqn], [dkn], [dvn], dxc, [dgc], [dqd], [dkd], dvs, [dz]], BF16)
    dh1 = _matmul("mix_proj_bx", dp, w['w_in'], 'nt', F32)
    x_in = sv['x_in']
    g['w_in'] = _matmul("mix_proj_bw", sv['h'], dp, 'tn', F32)
    dxi, g['mix_norm_pre'] = _rowwise_bwd("mix_prenorm_b", _prenorm_fn, [x_in], [w['mix_norm_pre']], [dh1], as_f32, [F32], [True],
                                          row_grad_add=[dxm])
    return dxi, g


def _loss_fn(y, t):
    e = y - t
    return e * (1.0 / y.shape[1]), jnp.sum(e * e, axis=0, keepdims=True)


def _gather_cols(name, shard, axis):
    half = shard.shape[0] // 2
    mine = lax.dynamic_slice_in_dim(shard, lax.axis_index("c") * half, half, axis=0).astype(BF16)
    both = _exchange(name + "_c", _exchange(name + "_xy", mine, 'xy', True), 'c', True)
    shp = list(shard.shape)
    shp[axis] *= 4
    return jnp.moveaxis(both, 1, axis + 1).reshape(shp)


def _pack(arrs, mult=64 * LANE):
    flat = jnp.concatenate([a.reshape(-1) for a in arrs])
    pad = (-flat.shape[0]) % mult
    return jnp.pad(flat, (0, pad)).reshape(-1, LANE)


def _unpack(buf, shapes):
    flat, out, k = buf.reshape(-1), [], 0
    for s in shapes:
        sz = int(np.prod(s))
        out.append(flat[k:k + sz].reshape(s))
        k += sz
    return out


def _perm_in(w_in):
    pad = jnp.zeros(w_in.shape[:-1] + (D_INP - D_IN,), w_in.dtype)
    return jnp.concatenate([w_in[..., :P_QB * GROUP_WIDTH], w_in[..., P_QB * GROUP_WIDTH + 2 * GLA_RANK:],
                            w_in[..., P_QB * GROUP_WIDTH:P_QB * GROUP_WIDTH + 2 * GLA_RANK], pad], axis=-1)


def _unperm_in(g):
    return jnp.concatenate([g[..., :P_QB * GROUP_WIDTH], g[..., P_Z:P_Z + 2 * GLA_RANK], g[..., P_QB * GROUP_WIDTH:P_Z]], axis=-1)


def _block_diag(wb):
    l = wb.shape[0]
    eye = jnp.eye(GROUP_HEADS, dtype=wb.dtype)
    return jnp.einsum('lehij,hg->lehigj', wb, eye).reshape(l, 2, GROUP_WIDTH, GROUP_WIDTH)


def _block_diag_grad(gw):
    l = gw.shape[0]
    g6 = gw.reshape(l, 2, GROUP_HEADS, HEAD_DIM, GROUP_HEADS, HEAD_DIM)
    return jnp.stack([g6[:, :, h, :, h, :] for h in range(GROUP_HEADS)], axis=2)


def _gate_matrix(wg):
    l = wg.shape[0]
    m = jnp.zeros((l, LANE, 2 * GROUP_WIDTH), wg.dtype)
    for e in range(2):
        m = m.at[:, e * GLA_RANK:(e + 1) * GLA_RANK, e * GROUP_WIDTH:(e + 1) * GROUP_WIDTH].set(wg[:, e])
    return m


def _gate_matrix_grad(gm):
    return jnp.stack([gm[:, e * GLA_RANK:(e + 1) * GLA_RANK, e * GROUP_WIDTH:(e + 1) * GROUP_WIDTH] for e in range(2)], axis=1)


def _adam_fn(w, g, m, v):
    m = ADAM_B1 * m + (1.0 - ADAM_B1) * g
    v = ADAM_B2 * v + (1.0 - ADAM_B2) * (g * g)
    m_hat = m / (1.0 - ADAM_B1 ** ADAM_STEP)
    v_hat = v / (1.0 - ADAM_B2 ** ADAM_STEP)
    return -ADAM_LR * (m_hat / (jnp.sqrt(v_hat) + ADAM_EPS) + ADAM_WD * w), m, v


def _adam(name, w, g, m, v):
    shp = w.shape
    two = lambda t: t.reshape(-1, shp[-1])
    res = _rowwise(name, _adam_fn, [two(w), two(g), two(m), two(v)], [], [(shp[-1], F32)] * 3)
    return [r.reshape(shp) for r in res]


def _local_step(x, target, fw):
    L, D = x.shape
    depth = fw['w_in'].shape[0]
    cs, sn = _rope_tables(L)
    consts = {'seg': _seg_matrix(), 'rpb_tab': _rpb_tables(), 'cos': cs, 'sin': sn}
    layer = lambda l: {k: v[l] for k, v in fw.items()}
    saved = []
    for l in range(depth):
        x, sv = _layer_fwd(x, layer(l), consts)
        saved.append(sv)
    dx, sq = _rowwise("loss", _loss_fn, [x, target], [], [(D, F32)], acc_outs=[(1, D)])
    grads = [None] * depth
    for l in reversed(range(depth)):
        dx, grads[l] = _layer_bwd(dx, layer(l), consts, saved[l])
    return sq, dx, grads


def kernel(x, mix_norm_pre, mix_norm_post, w_in, gla_w_gate, gla_b_gate, gla_norm, na_rpb, lru_conv_w, lru_conv_b, lru_w_a, lru_b_a, lru_w_x, lru_b_x, lru_lambda, w_out, ffn_norm_pre, ffn_norm_post, ffn_w_in, ffn_w_out, loss_target, m_mix_norm_pre, m_mix_norm_post, m_w_in, m_gla_w_gate, m_gla_b_gate, m_gla_norm, m_na_rpb, m_lru_conv_w, m_lru_conv_b, m_lru_w_a, m_lru_b_a, m_lru_w_x, m_lru_b_x, m_lru_lambda, m_w_out, m_ffn_norm_pre, m_ffn_norm_post, m_ffn_w_in, m_ffn_w_out, v_mix_norm_pre, v_mix_norm_post, v_w_in, v_gla_w_gate, v_gla_b_gate, v_gla_norm, v_na_rpb, v_lru_conv_w, v_lru_conv_b, v_lru_w_a, v_lru_b_a, v_lru_w_x, v_lru_b_x, v_lru_lambda, v_w_out, v_ffn_norm_pre, v_ffn_norm_post, v_ffn_w_in, v_ffn_w_out):
    args = (mix_norm_pre, mix_norm_post, w_in, gla_w_gate, gla_b_gate, gla_norm, na_rpb, lru_conv_w, lru_conv_b, lru_w_a, lru_b_a, lru_w_x, lru_b_x, lru_lambda, w_out, ffn_norm_pre, ffn_norm_post, ffn_w_in, ffn_w_out,
            m_mix_norm_pre, m_mix_norm_post, m_w_in, m_gla_w_gate, m_gla_b_gate, m_gla_norm, m_na_rpb, m_lru_conv_w, m_lru_conv_b, m_lru_w_a, m_lru_b_a, m_lru_w_x, m_lru_b_x, m_lru_lambda, m_w_out, m_ffn_norm_pre, m_ffn_norm_post, m_ffn_w_in, m_ffn_w_out,
            v_mix_norm_pre, v_mix_norm_post, v_w_in, v_gla_w_gate, v_gla_b_gate, v_gla_norm, v_na_rpb, v_lru_conv_w, v_lru_conv_b, v_lru_w_a, v_lru_b_a, v_lru_w_x, v_lru_b_x, v_lru_lambda, v_w_out, v_ffn_norm_pre, v_ffn_norm_post, v_ffn_w_in, v_ffn_w_out)
    nw = len(WEIGHTS)
    W = dict(zip(WEIGHTS, args[:nw]))
    M = dict(zip(WEIGHTS, args[nw:2 * nw]))
    V = dict(zip(WEIGHTS, args[2 * nw:]))
    chip = 2 * lax.axis_index("x") + lax.axis_index("y")

    full = dict(W)
    full['w_in'] = _gather_cols("ag_w_in", w_in, 2)
    full['ffn_w_in'] = _gather_cols("ag_ffn_w_in", ffn_w_in, 2)
    full['w_out'] = _gather_cols("ag_w_out", w_out, 1)
    full['ffn_w_out'] = _gather_cols("ag_ffn_w_out", ffn_w_out, 1)
    small = list(SMALL_SHARDED)
    got = _exchange("ag_small", _pack([W[k] for k in small]), 'xy', True)
    for k, parts in zip(small, zip(*[_unpack(got[j], [W[k].shape for k in small]) for j in range(4)])):
        ax = SMALL_SHARDED[k]
        stacked = jnp.moveaxis(jnp.stack(parts), 0, ax)
        shp = list(W[k].shape)
        shp[ax] *= 4
        full[k] = stacked.reshape(shp)

    sq, dx0, g = _local_step(x[0], loss_target[0], _layer_weights(full))
    loss = lax.psum(0.5 * jnp.sum(sq) / x.shape[-1], ("x", "y", "c"))
    gfull = _stored_grads(g)

    grad = {}
    for k, cut in _big_cuts(g).items():
        grad[k] = _reduce_big("rs_" + k, cut).reshape(W[k].shape)
    rest = [k for k in WEIGHTS if k not in BIG]
    allg = _exchange("ar_small_c", _exchange("ar_small_xy", _pack([gfull[k] for k in rest]), 'xy', True), 'c', True)
    allg = allg.transpose(1, 0, 2, 3).reshape(8, -1, LANE)
    summed = _unpack(_ordered_sum("ar_small_sum", allg), [gfull[k].shape for k in rest])
    for k, s in zip(rest, summed):
        if k in SMALL_SHARDED:
            ax = SMALL_SHARDED[k]
            n = W[k].shape[ax]
            s = lax.dynamic_slice_in_dim(s, chip * n, n, axis=ax)
        grad[k] = s

    delta, new_m, new_v = {}, {}, {}
    for k in BIG:
        delta[k], new_m[k], new_v[k] = _adam("adam_" + k, W[k], grad[k], M[k], V[k])
    shapes = [W[k].shape for k in rest]
    res = _rowwise("adam_small", _adam_fn, [_pack([d[k] for k in rest]) for d in (W, grad, M, V)], [], [(LANE, F32)] * 3)
    for d, r in zip((delta, new_m, new_v), res):
        for k, t in zip(rest, _unpack(r, shapes)):
            d[k] = t

    return (loss, dx0[None], *[grad[k] for k in WEIGHTS], *[delta[k] for k in WEIGHTS],
            *[new_m[k] for k in WEIGHTS], *[new_v[k] for k in WEIGHTS])


def _layer_weights(full):
    depth = full['w_in'].shape[0]
    dff = full['ffn_w_in'].shape[-1] // 2
    row = lambda t: t[:, None, :]
    fw = {
        'mix_norm_pre': row(full['mix_norm_pre']), 'mix_norm_post': row(full['mix_norm_post']),
        'ffn_norm_pre': row(full['ffn_norm_pre']), 'ffn_norm_post': row(full['ffn_norm_post']),
        'w_in': _perm_in(full['w_in']), 'w_out': full['w_out'],
        'ffn_wg': full['ffn_w_in'][..., :dff], 'ffn_wu': full['ffn_w_in'][..., dff:], 'ffn_w_out': full['ffn_w_out'],
        'gla_wg': _gate_matrix(full['gla_w_gate']), 'gla_bg': full['gla_b_gate'].reshape(depth, 1, 2 * GROUP_WIDTH),
        'gla_norm': row(full['gla_norm']), 'na_rpb': full['na_rpb'],
        'lru_cb': row(full['lru_conv_b']),
    }
    wa_bd, wx_bd = _block_diag(full['lru_w_a']), _block_diag(full['lru_w_x'])
    for j in range(LRU_CONV):
        fw[f'lru_cw{j}'] = row(full['lru_conv_w'][:, j])
    for e in range(2):
        fw[f'lru_wa{e}'], fw[f'lru_wx{e}'] = wa_bd[:, e], wx_bd[:, e]
        fw[f'lru_ba{e}'], fw[f'lru_bx{e}'] = row(full['lru_b_a'][:, e]), row(full['lru_b_x'][:, e])
        fw[f'lru_lam{e}'] = row(full['lru_lambda'][:, e])
    return fw


def _orig_cols(gp, lo, hi):
    split, zend = P_QB * GROUP_WIDTH, P_QB * GROUP_WIDTH + 2 * GLA_RANK
    parts = []
    for a, b, at in ((0, split, 0), (split, zend, P_Z), (zend, D_IN, split)):
        s, e = max(lo, a), min(hi, b)
        if s < e:
            parts.append(gp[..., at + s - a:at + e - a])
    return parts[0] if len(parts) == 1 else jnp.concatenate(parts, axis=-1)


def _big_cuts(grads):
    depth = len(grads)
    halves = (range(0, depth // 2), range(depth // 2, depth))
    n_in, n_ff = D_IN // 4, grads[0]['ffn_wg'].shape[1] // 2

    def build(piece):
        return jnp.stack([jnp.stack([jnp.concatenate([piece(l, j) for l in hl], axis=0) for j in range(4)]) for hl in halves])

    def rows_of(key):
        return lambda l, j: jnp.split(grads[l][key], 4, axis=0)[j]

    return {
        'w_in': build(lambda l, j: _orig_cols(grads[l]['w_in'], j * n_in, (j + 1) * n_in)),
        'w_out': build(rows_of('w_out')),
        'ffn_w_in': build(lambda l, j: grads[l]['ffn_wg' if j < 2 else 'ffn_wu'][:, (j % 2) * n_ff:(j % 2 + 1) * n_ff]),
        'ffn_w_out': build(rows_of('ffn_w_out')),
    }


def _stored_grads(grads):
    depth = len(grads)
    g = {k: jnp.stack([gl[k] for gl in grads]) for k in grads[0]}
    return {
        'mix_norm_pre': g['mix_norm_pre'][:, 0], 'mix_norm_post': g['mix_norm_post'][:, 0],
        'ffn_norm_pre': g['ffn_norm_pre'][:, 0], 'ffn_norm_post': g['ffn_norm_post'][:, 0],
        'w_in': _unperm_in(g['w_in']), 'w_out': g['w_out'],
        'ffn_w_in': jnp.concatenate([g['ffn_wg'], g['ffn_wu']], axis=-1), 'ffn_w_out': g['ffn_w_out'],
        'gla_w_gate': _gate_matrix_grad(g['gla_wg']), 'gla_b_gate': g['gla_bg'].reshape(depth, 2, GROUP_WIDTH),
        'gla_norm': g['gla_norm'][:, 0], 'na_rpb': g['na_rpb'],
        'lru_conv_w': jnp.stack([g[f'lru_cw{j}'][:, 0] for j in range(LRU_CONV)], axis=1), 'lru_conv_b': g['lru_cb'][:, 0],
        'lru_w_a': _block_diag_grad(jnp.stack([g['lru_wa0'], g['lru_wa1']], axis=1)),
        'lru_w_x': _block_diag_grad(jnp.stack([g['lru_wx0'], g['lru_wx1']], axis=1)),
        'lru_b_a': jnp.stack([g['lru_ba0'][:, 0], g['lru_ba1'][:, 0]], axis=1),
        'lru_b_x': jnp.stack([g['lru_bx0'][:, 0], g['lru_bx1'][:, 0]], axis=1),
        'lru_lambda': jnp.stack([g['lru_lam0'][:, 0], g['lru_lam1'][:, 0]], axis=1),
    }
```

```python
import numpy as np
import jax
import jax.numpy as jnp
from jax import lax
from jax.experimental import pallas as pl
from jax.experimental.pallas import tpu as pltpu

F32, BF16 = jnp.float32, jnp.bfloat16
HIGHEST = lax.Precision.HIGHEST
MESH = pl.DeviceIdType.MESH

HEAD_DIM = 64
GROUP_HEADS = 4
GROUP_WIDTH = GROUP_HEADS * HEAD_DIM
GLA_RANK = 16
GLA_TAU = 16.0
GLA_CHUNK = 64
GRID_W = 64
NA_ROWS = 8
NA_COLS = 16
LRU_CONV = 4
LRU_CONV_LEFT = 2
LRU_C = 8.0
DIL_PAIRS = ((128, 1), (512, 4), (2048, 16))
ROPE_THETA = 10000.0
EPS = 1e-6
ADAM_LR, ADAM_B1, ADAM_B2, ADAM_EPS, ADAM_WD, ADAM_STEP = 0.001, 0.9, 0.999, 1e-08, 0.01, 10
NEG = -1e30

LANE = 128
VMEM_LIMIT = 56 * 1024 * 1024
ROW_BUDGET = 16 * 1024 * 1024
DMA_PIECES = 8

P_QA, P_KA, P_VA, P_GA, P_QB, P_KB, P_VB, P_XC, P_GC, P_QD, P_KD, P_VD = range(12)
P_Z = 12 * GROUP_WIDTH
D_IN = 12 * GROUP_WIDTH + 2 * GLA_RANK
D_INP = 12 * GROUP_WIDTH + LANE

WEIGHTS = ['mix_norm_pre', 'mix_norm_post', 'w_in', 'gla_w_gate', 'gla_b_gate', 'gla_norm', 'na_rpb',
           'lru_conv_w', 'lru_conv_b', 'lru_w_a', 'lru_b_a', 'lru_w_x', 'lru_b_x', 'lru_lambda', 'w_out',
           'ffn_norm_pre', 'ffn_norm_post', 'ffn_w_in', 'ffn_w_out']
BIG = ('w_in', 'w_out', 'ffn_w_in', 'ffn_w_out')
SMALL_SHARDED = {'gla_w_gate': 3, 'gla_b_gate': 2, 'lru_conv_w': 2, 'lru_b_a': 2, 'lru_b_x': 2, 'lru_lambda': 2}
HEADS = [slice(h * HEAD_DIM, (h + 1) * HEAD_DIM) for h in range(GROUP_HEADS)]


def _cparams(sem=None):
    return pltpu.CompilerParams(dimension_semantics=sem, vmem_limit_bytes=VMEM_LIMIT)


def _tile(dim, target, mult=LANE):
    best = None
    for t in range(mult, min(dim, target) + 1, mult):
        if dim % t == 0:
            best = t
    return best or dim


class Rows:
    def __init__(self, a, w=None, cb=0, lead=None, shifts=None):
        self.a, self.cb, self.lead, self.shifts = a, cb, lead, shifts
        self.w = a.shape[-1] if w is None else w
        self.nrows = a.shape[-2]

    def spec(self, tm, ncol=1, halo=0):
        w = self.w // ncol
        per, last = tm // HALO, self.nrows // HALO - 1
        rows, row = (tm, lambda i: i) if halo == 0 else (HALO, lambda i: jnp.clip(i * per + (per if halo > 0 else -1), 0, last))
        if self.lead is None:
            return pl.BlockSpec((rows, w), lambda i, j, cb=self.cb: (row(i), cb * ncol + j))
        return pl.BlockSpec((None, rows, w), lambda i, j, cb=self.cb, k=self.lead: (k, row(i), cb * ncol + j))

    def nbytes(self):
        return self.w * self.a.dtype.itemsize


def _as_rows(rs):
    return [r if isinstance(r, Rows) else Rows(r) for r in rs]


HALO = 8


def _shift_tile(before, cur, after, k, t, nt):
    if k == 0:
        return cur
    tm = cur.shape[0]
    row = lax.broadcasted_iota(jnp.int32, before.shape, 0)
    if k > 0:
        moved = pltpu.roll(cur, k, 0)
        edge = jnp.where(row < k, jnp.where(t > 0, pltpu.roll(before, k, 0), 0.0), moved[0:HALO])
        return jnp.concatenate([edge, moved[HALO:]], axis=0)
    moved = pltpu.roll(cur, tm + k, 0)
    edge = jnp.where(row >= HALO + k, jnp.where(t < nt - 1, pltpu.roll(after, HALO + k, 0), 0.0), moved[tm - HALO:])
    return jnp.concatenate([moved[:tm - HALO], edge], axis=0)


def _operands(rows, tm, ncol):
    specs, arrs = [], []
    for r in rows:
        if r.shifts is None:
            specs.append(r.spec(tm, ncol))
            arrs.append(r.a)
        else:
            assert ncol == 1
            specs += [r.spec(tm, 1, side) for side in (-1, 0, 1)]
            arrs += [r.a] * 3

    def load(refs):
        vals, k = [], 0
        t, nt = pl.program_id(0), rows[0].nrows // tm
        for r in rows:
            if r.shifts is None:
                vals.append(refs[k][...])
                k += 1
            else:
                prev, cur, nxt = refs[k][...], refs[k + 1][...], refs[k + 2][...]
                vals += [_shift_tile(prev, cur, nxt, s, t, nt) for s in r.shifts]
                k += 3
        return vals

    return specs, arrs, load


def _expand(rows):
    return [r for r in rows for _ in (r.shifts or [0])]


def _pick_tm(nrows, row_bytes, scale, cap=512):
    tm = cap
    while tm > 16 and (tm * row_bytes * scale > ROW_BUDGET or nrows % tm):
        tm //= 2
    assert nrows % tm == 0, (nrows, tm)
    return tm


def _full_spec(a):
    nd = a.ndim
    return pl.BlockSpec(a.shape, lambda i, j, nd=nd: (0,) * nd)


def _rowwise(name, fn, rows, params, outs, acc_outs=(), ncol=1, tm_cap=512):
    rows = _as_rows(rows)
    nrows = rows[0].nrows
    assert ncol == 1 or not (acc_outs or params)
    tm = _pick_tm(nrows, (sum(r.nbytes() for r in rows) + sum(w * jnp.dtype(d).itemsize for w, d in outs)) // ncol, 2, tm_cap)
    specs, arrs, load = _operands(rows, tm, ncol)
    n_r, n_p, n_o = len(specs), len(params), len(outs)

    def body(*refs):
        vals = load(refs[:n_r]) + [r[...] for r in refs[n_r:n_r + n_p]]
        res = fn(*vals)
        res = res if isinstance(res, (tuple, list)) else (res,)
        orefs = refs[n_r + n_p:]
        for o, v in zip(orefs[:n_o], res[:n_o]):
            o[...] = v.astype(o.dtype)
        for o, v in zip(orefs[n_o:], res[n_o:]):
            @pl.when(pl.program_id(0) == 0)
            def _(o=o):
                o[...] = jnp.zeros_like(o)
            o[...] += v

    out_shape = [jax.ShapeDtypeStruct((nrows, w), d) for w, d in outs] + [jax.ShapeDtypeStruct(s, F32) for s in acc_outs]
    out_specs = [pl.BlockSpec((tm, w // ncol), lambda i, j: (i, j)) for w, _ in outs] + \
                [pl.BlockSpec(s, lambda i, j, nd=len(s): (0,) * nd) for s in acc_outs]
    return pl.pallas_call(
        body, name=name, grid=(nrows // tm, ncol),
        in_specs=specs + [_full_spec(p) for p in params],
        out_specs=out_specs, out_shape=out_shape,
        compiler_params=_cparams(("arbitrary", "arbitrary") if acc_outs else ("parallel", "parallel")),
    )(*arrs, *params)


def _rowwise_bwd(name, fn, rows, params, ct_rows, ct_fn, row_grads, param_grads, row_grad_add=None, ncol=1):
    rows, ct_rows = _as_rows(rows), _as_rows(ct_rows)
    nrows = rows[0].nrows
    n_rg = sum(d is not None for d in row_grads)
    adds = _as_rows([a for a in (row_grad_add or []) if a is not None])
    add_at = [k for k, a in enumerate(row_grad_add or []) if a is not None]
    assert ncol == 1 or not (params or adds)
    seen = _expand(rows)
    gbytes = sum(r.w * jnp.dtype(d).itemsize for r, d in zip(seen, row_grads) if d is not None)
    tm = _pick_tm(nrows, (sum(r.nbytes() for r in rows + ct_rows + adds) + gbytes) // ncol, 4)
    r_specs, r_arrs, r_load = _operands(rows, tm, ncol)
    c_specs, c_arrs, c_load = _operands(ct_rows, tm, ncol)
    a_specs, a_arrs, a_load = _operands(adds, tm, ncol)
    n_r, n_p, n_c, n_a = len(r_specs), len(params), len(c_specs), len(a_specs)
    diff = [k for k, d in enumerate(row_grads) if d is not None] + [len(seen) + k for k, g in enumerate(param_grads) if g]

    def body(*refs):
        vals = r_load(refs[:n_r]) + [r[...] for r in refs[n_r:n_r + n_p]]
        cts_in = c_load(refs[n_r + n_p:n_r + n_p + n_c])
        add_in = a_load(refs[n_r + n_p + n_c:n_r + n_p + n_c + n_a]) if n_a else []
        orefs = refs[n_r + n_p + n_c + n_a:]

        def f(*dv):
            full = list(vals)
            for k, v in zip(diff, dv):
                full[k] = v
            res = fn(*full)
            return tuple(res) if isinstance(res, (tuple, list)) else (res,)

        outs, vjp = jax.vjp(f, *[vals[k].astype(F32) for k in diff])
        cts = ct_fn(*cts_in)
        cts = cts if isinstance(cts, (tuple, list)) else (cts,)
        grads = list(vjp(tuple(c.astype(o.dtype) for c, o in zip(cts, outs))))
        for k, a in zip(add_at, add_in):
            grads[k] = grads[k] + a.astype(F32)
        for o, g in zip(orefs[:n_rg], grads[:n_rg]):
            o[...] = g.astype(o.dtype)
        for o, g in zip(orefs[n_rg:], grads[n_rg:]):
            @pl.when(pl.program_id(0) == 0)
            def _(o=o):
                o[...] = jnp.zeros_like(o)
            o[...] += g.astype(F32)

    out_shape = [jax.ShapeDtypeStruct((nrows, r.w), d) for r, d in zip(seen, row_grads) if d is not None] + \
                [jax.ShapeDtypeStruct(p.shape, F32) for p, g in zip(params, param_grads) if g]
    out_specs = [pl.BlockSpec((tm, r.w // ncol), lambda i, j: (i, j)) for r, d in zip(seen, row_grads) if d is not None] + \
                [_full_spec(p) for p, g in zip(params, param_grads) if g]
    return pl.pallas_call(
        body, name=name, grid=(nrows // tm, ncol),
        in_specs=r_specs + [_full_spec(p) for p in params] + c_specs + a_specs,
        out_specs=out_specs, out_shape=out_shape,
        compiler_params=_cparams(("arbitrary", "arbitrary")),
    )(*r_arrs, *params, *c_arrs, *a_arrs)


def _assemble(name, groups, dtype):
    sizes = [len(g) for g in groups]
    flat = [a for g in groups for a in g]

    def fn(*tiles):
        out, k = [], 0
        for s in sizes:
            acc = tiles[k].astype(F32)
            for t in tiles[k + 1:k + s]:
                acc = acc + t.astype(F32)
            out.append(acc.astype(dtype))
            k += s
        return out[0] if len(out) == 1 else jnp.concatenate(out, axis=1)

    width = sum(g[0].shape[-1] if not isinstance(g[0], Rows) else g[0].w for g in groups)
    return _rowwise(name, fn, flat, [], [(width, dtype)])[0]


def _matmul(name, a, b, mode, out_dtype, acc_in=None, also=None):
    if mode == 'nn':
        (M, K), N = a.shape, b.shape[1]
    elif mode == 'nt':
        (M, K), N = a.shape, b.shape[0]
    else:
        (K, M), N = a.shape, b.shape[1]
    tm, tn, tk = _tile(M, 1536), _tile(N, 1536), _tile(K, 2048 if mode == 'tn' else 3328)
    nk = K // tk
    dn = {'nn': NN, 'nt': NT, 'tn': TN}[mode]
    has_acc = acc_in is not None

    n_in = 3 if has_acc else 2
    dtypes = [out_dtype] + ([also] if also is not None else [])

    def body(*refs):
        a_ref, b_ref = refs[:2]
        o_refs = refs[n_in:n_in + len(dtypes)]
        part = lax.dot_general(a_ref[...].astype(BF16), b_ref[...].astype(BF16), dn, preferred_element_type=F32)
        if nk == 1:
            val = (refs[2][...] + part) if has_acc else part
            for o_ref in o_refs:
                o_ref[...] = val.astype(o_ref.dtype)
            return
        acc = refs[-1]

        @pl.when(pl.program_id(2) == 0)
        def _():
            acc[...] = refs[2][...] if has_acc else jnp.zeros_like(acc)
        acc[...] += part

        @pl.when(pl.program_id(2) == nk - 1)
        def _():
            for o_ref in o_refs:
                o_ref[...] = acc[...].astype(o_ref.dtype)

    a_spec = pl.BlockSpec((tk, tm), lambda i, j, k: (k, i)) if mode == 'tn' else pl.BlockSpec((tm, tk), lambda i, j, k: (i, k))
    b_spec = pl.BlockSpec((tn, tk), lambda i, j, k: (j, k)) if mode == 'nt' else pl.BlockSpec((tk, tn), lambda i, j, k: (k, j))
    o_spec = pl.BlockSpec((tm, tn), lambda i, j, k: (i, j))
    res = pl.pallas_call(
        body, name=name, grid=(M // tm, N // tn, nk),
        in_specs=[a_spec, b_spec] + ([o_spec] if has_acc else []), out_specs=[o_spec] * len(dtypes),
        out_shape=[jax.ShapeDtypeStruct((M, N), d) for d in dtypes],
        scratch_shapes=[] if nk == 1 else [pltpu.VMEM((tm, tn), F32)],
        compiler_params=_cparams(("parallel", "parallel", "arbitrary")),
    )(a, b, *([acc_in] if has_acc else []))
    return res[0] if also is None else res


def _ffn_up(name, x, gain, wg, wu):
    (M, K), N = x.shape, wg.shape[1]
    tm, tn = _tile(M, 512), _tile(N, 1536)

    def body(x_ref, n_ref, g_ref, u_ref, h_ref, gate_ref, up_ref, act_ref, hs):
        @pl.when(pl.program_id(1) == 0)
        def _():
            hs[...] = _rms(x_ref[...], n_ref[...]).astype(BF16)
            h_ref[...] = hs[...]
        a = hs[...]
        gate = _bdot(a, g_ref[...]).astype(BF16)
        up = _bdot(a, u_ref[...]).astype(BF16)
        gate_ref[...], up_ref[...] = gate, up
        act_ref[...] = _swiglu_fn(gate, up).astype(BF16)

    x_spec = pl.BlockSpec((tm, K), lambda i, j: (i, 0))
    w_spec = pl.BlockSpec((K, tn), lambda i, j: (0, j))
    o_spec = pl.BlockSpec((tm, tn), lambda i, j: (i, j))
    return pl.pallas_call(
        body, name=name, grid=(M // tm, N // tn),
        in_specs=[x_spec, pl.BlockSpec(gain.shape, lambda i, j: (0, 0)), w_spec, w_spec], out_specs=[x_spec] + [o_spec] * 3,
        out_shape=[jax.ShapeDtypeStruct((M, K), BF16)] + [jax.ShapeDtypeStruct((M, N), BF16)] * 3,
        scratch_shapes=[pltpu.VMEM((tm, K), BF16)],
        compiler_params=_cparams(("parallel", "arbitrary")),
    )(x, gain, wg, wu)


def _ffn_down_bwd(name, df, w_out, gate, up):
    (M, K), N = df.shape, w_out.shape[0]
    tm, tn = _tile(M, 512), _tile(N, 1536)

    def body(d_ref, w_ref, gate_ref, up_ref, dg_ref, du_ref):
        dact = _bdot(d_ref[...], w_ref[...], NT)
        dg, du = _swiglu_bwd((gate_ref[...], up_ref[...]), dact)
        dg_ref[...], du_ref[...] = dg, du

    o_spec = pl.BlockSpec((tm, tn), lambda i, j: (i, j))
    return pl.pallas_call(
        body, name=name, grid=(M // tm, N // tn),
        in_specs=[pl.BlockSpec((tm, K), lambda i, j: (i, 0)), pl.BlockSpec((tn, K), lambda i, j: (j, 0)), o_spec, o_spec],
        out_specs=[o_spec] * 2, out_shape=[jax.ShapeDtypeStruct((M, N), BF16)] * 2,
        compiler_params=_cparams(("parallel", "parallel")),
    )(df, w_out, gate, up)


def _small_dot(name, a, b, mode):
    dn = {'nn': (((1,), (0,)), ((), ())), 'nt': (((1,), (1,)), ((), ()))}[mode]
    M = a.shape[0]
    N = b.shape[1] if mode == 'nn' else b.shape[0]

    def body(a_ref, b_ref, o_ref):
        o_ref[...] = lax.dot_general(a_ref[...], b_ref[...], dn, precision=HIGHEST, preferred_element_type=F32)

    return pl.pallas_call(body, name=name, out_shape=jax.ShapeDtypeStruct((M, N), F32),
                          compiler_params=pltpu.CompilerParams(vmem_limit_bytes=VMEM_LIMIT))(a, b)


NN, NT, TN = (((1,), (0,)), ((), ())), (((1,), (1,)), ((), ())), (((0,), (0,)), ((), ()))


def _bdot(a, b, dn=NN):
    return lax.dot_general(a.astype(BF16), b.astype(BF16), dn, preferred_element_type=F32)


def _sigmoid(x):
    return 0.5 * jnp.tanh(0.5 * x) + 0.5


def _silu(x):
    return x * _sigmoid(x)


def _softplus(x):
    return jnp.maximum(x, 0.0) + jnp.log(1.0 + jnp.exp(-jnp.abs(x)))


def _gelu(x):
    return 0.5 * x * (1.0 + jnp.tanh(0.7978845608028654 * (x + 0.044715 * (x * x * x))))


def _rms(x, g):
    return x * lax.rsqrt(jnp.mean(x * x, axis=-1, keepdims=True) + EPS) * g


def _prenorm_fn(x, g):
    return _rms(x, g)


def _postnorm_fn(x, y, g):
    return x + _rms(y, g)


@jax.custom_vjp
def _swiglu_fn(gate, up):
    return _silu(gate.astype(F32)) * up.astype(F32)


def _swiglu_bwd(res, ct):
    g, u = res[0].astype(F32), res[1].astype(F32)
    s = _sigmoid(g)
    gs = g * s
    return (ct * u * (s + gs - gs * s)).astype(res[0].dtype), (ct * gs).astype(res[1].dtype)


_swiglu_fn.defvjp(lambda gate, up: (_swiglu_fn(gate, up), (gate, up)), _swiglu_bwd)


def _gla_pre_fn(z, wg, bg):
    logit = _bdot(z, wg) + bg
    return -_softplus(-logit) * (1.0 / GLA_TAU)


def _seg_mean(x, seg):
    return lax.dot_general(x, seg, NN, precision=HIGHEST, preferred_element_type=F32)


def _gla_post_fn(of, ob, g, norm, seg):
    o = of + ob
    o = o * lax.rsqrt(_seg_mean(o * o, seg) + EPS) * norm
    return o * _silu(g)


@jax.custom_vjp
def _swap_halves(x):
    n = x.shape[-1]
    lane = lax.broadcasted_iota(jnp.int32, x.shape, x.ndim - 1)
    lo = (lane & (HEAD_DIM - 1)) < HEAD_DIM // 2
    return jnp.where(lo, pltpu.roll(x, n - HEAD_DIM // 2, x.ndim - 1), pltpu.roll(x, HEAD_DIM // 2, x.ndim - 1))


_swap_halves.defvjp(lambda x: (_swap_halves(x), None), lambda _, g: (_swap_halves(g),))


def _rope_fn(q, k, cs, sn):
    return q * cs + _swap_halves(q) * sn, k * cs + _swap_halves(k) * sn


def _dil_comb_fn(o1, o2, o3, l1, l2, l3):
    m = jnp.maximum(jnp.maximum(l1, l2), l3)
    e1, e2, e3 = jnp.exp(l1 - m), jnp.exp(l2 - m), jnp.exp(l3 - m)
    return (e1 * o1 + e2 * o2 + e3 * o3) / (e1 + e2 + e3)


LRU_PARAMS = ['lru_cw0', 'lru_cw1', 'lru_cw2', 'lru_cw3', 'lru_cb', 'lru_wa0', 'lru_wa1', 'lru_ba0', 'lru_ba1',
              'lru_wx0', 'lru_wx1', 'lru_bx0', 'lru_bx1', 'lru_lam0', 'lru_lam1']


def _lru_pre_fn(x0, x1, x2, x3, cw0, cw1, cw2, cw3, cb, wa0, wa1, ba0, ba1, wx0, wx1, bx0, bx1, lam0, lam1):
    xc = cb + x0 * cw0 + x1 * cw1 + x2 * cw2 + x3 * cw3
    outs = []
    for wa, ba, wx, bx, lam in ((wa0, ba0, wx0, bx0, lam0), (wa1, ba1, wx1, bx1, lam1)):
        r = _sigmoid(_bdot(xc, wa) + ba)
        i = _sigmoid(_bdot(xc, wx) + bx)
        log_a = -LRU_C * r * _softplus(-lam)
        a = jnp.exp(log_a)
        u = jnp.sqrt(-jnp.tanh(log_a) * (a * a + 1.0)) * (i * xc)
        outs += [a, u]
    return outs[0], outs[2], outs[1], outs[3]


def _lru_post_fn(hf, hb, gate):
    return (hf + hb) * _gelu(gate)


def _mix_post_fn(of, ob, ga, yb, hf, hb, gc, o1, o2, o3, l1, l2, l3, norm, seg):
    ya = _gla_post_fn(of, ob, ga, norm, seg)
    yc = _lru_post_fn(hf, hb, gc)
    yd = _dil_comb_fn(o1, o2, o3, l1, l2, l3)
    return jnp.concatenate([ya.astype(BF16), yb.astype(BF16), yc.astype(BF16), yd.astype(BF16)], axis=1)


def _attn_heads(qs, kws, vws, biases):
    ss = [_bdot(q, kw, NT) * (HEAD_DIM ** -0.5) + b for q, kw, b in zip(qs, kws, biases)]
    ms = [lax.stop_gradient(jnp.max(s, axis=-1, keepdims=True)) for s in ss]
    es = [jnp.exp(s - m) for s, m in zip(ss, ms)]
    dens = [jnp.sum(e, axis=-1, keepdims=True) for e in es]
    ps = [e * (1.0 / d) for e, d in zip(es, dens)]
    os_ = [_bdot(p_, vw) for p_, vw in zip(ps, vws)]
    return os_, [m + jnp.log(d) for m, d in zip(ms, dens)]


def _cumsum_rows(x, rev):
    n = x.shape[0]
    row = lax.broadcasted_iota(jnp.int32, x.shape, 0)
    s = 1
    while s < n:
        if rev:
            x = x + jnp.where(row < n - s, pltpu.roll(x, n - s, 0), 0.0)
        else:
            x = x + jnp.where(row >= s, pltpu.roll(x, s, 0), 0.0)
        s *= 2
    return x


def _gla_chunks(qs, ks, vs, bs, sts, revs):
    C = qs[0].shape[0]
    ti = lax.broadcasted_iota(jnp.int32, (C, C), 0)
    si = lax.broadcasted_iota(jnp.int32, (C, C), 1)
    row = lax.broadcasted_iota(jnp.int32, (C, 1), 0)
    incl = {False: si <= ti, True: si >= ti}
    last = {False: row == C - 1, True: row == 0}
    mid = {False: row == C // 2 - 1, True: row == C // 2}
    bls = [jnp.sum(jnp.where(last[r], b, 0.0), axis=0, keepdims=True) for b, r in zip(bs, revs)]
    bms = [jnp.sum(jnp.where(mid[r], b, 0.0), axis=0, keepdims=True) for b, r in zip(bs, revs)]
    qss = [q * (HEAD_DIM ** -0.5) for q in qs]
    qi = [q * jnp.exp(b - bm) for q, b, bm in zip(qss, bs, bms)]
    ki = [k * jnp.exp(bm - b) for k, b, bm in zip(ks, bs, bms)]
    atts = [jnp.where(incl[r], _bdot(a, b, NT), 0.0) for a, b, r in zip(qi, ki, revs)]
    qe = [q * jnp.exp(b) for q, b in zip(qss, bs)]
    kl = [k * jnp.exp(bl - b) for k, b, bl in zip(ks, bs, bls)]
    o1 = [_bdot(a, v) for a, v in zip(atts, vs)]
    o2 = [_bdot(q, st, NT) for q, st in zip(qe, sts)]
    kvs = [_bdot(v, k, TN) for v, k in zip(vs, kl)]
    return [a + b for a, b in zip(o1, o2)], [st * jnp.exp(bl) + kv for st, bl, kv in zip(sts, bls, kvs)]


def _gla_specs(n, blocks, first):
    C = GLA_CHUNK
    at = (lambda i: i) if first else (lambda i: n - 1 - i)
    return [pl.BlockSpec((C, GROUP_WIDTH), lambda i, b=b: (at(i), b)) for b in blocks], at


def _gla_scan_fwd(p, la):
    L = p.shape[0]
    C, H, dh = GLA_CHUNK, GROUP_HEADS, HEAD_DIM
    n = L // C
    f_specs, f_at = _gla_specs(n, (P_QA, P_KA, P_VA), True)
    b_specs, b_at = _gla_specs(n, (P_QA, P_KA, P_VA), False)
    tile = lambda at, blk=0: pl.BlockSpec((C, GROUP_WIDTH), lambda i: (at(i), blk))
    st_spec = lambda at: pl.BlockSpec((None, H, dh, dh), lambda i: (at(i), 0, 0, 0))

    def body(qf, kf, vf, lf, qb, kb, vb, lb, of_ref, ob_ref, sf_ref, sb_ref, stf, stb):
        @pl.when(pl.program_id(0) == 0)
        def _():
            stf[...] = jnp.zeros_like(stf)
            stb[...] = jnp.zeros_like(stb)
        sf_ref[...] = stf[...]
        sb_ref[...] = stb[...]
        chains = [(t, h, sl) for t in ((qf, kf, vf, _cumsum_rows(lf[...], False), of_ref, stf, False),
                                       (qb, kb, vb, _cumsum_rows(lb[...], True), ob_ref, stb, True))
                  for h, sl in enumerate(HEADS)]
        os_, sts = _gla_chunks(*[[t[j][:, sl] for t, h, sl in chains] for j in range(4)],
                               [t[5][h] for t, h, sl in chains], [t[6] for t, h, sl in chains])
        for (t, h, sl), o, st_new in zip(chains, os_, sts):
            t[4][:, sl] = o
            t[5][h] = st_new

    return pl.pallas_call(
        body, name="gla_scan", grid=(n,),
        in_specs=f_specs + [tile(f_at, 0)] + b_specs + [tile(b_at, 1)],
        out_specs=[tile(f_at), tile(b_at), st_spec(f_at), st_spec(b_at)],
        out_shape=[jax.ShapeDtypeStruct((L, GROUP_WIDTH), F32)] * 2 + [jax.ShapeDtypeStruct((n, H, dh, dh), F32)] * 2,
        scratch_shapes=[pltpu.VMEM((H, dh, dh), F32)] * 2,
        compiler_params=_cparams(("arbitrary",)),
    )(p, p, p, la, p, p, p, la)


def _gla_scan_bwd(p, la, sf, sb, do):
    L = p.shape[0]
    C, H, dh = GLA_CHUNK, GROUP_HEADS, HEAD_DIM
    n = L // C
    f_specs, f_at = _gla_specs(n, (P_QA, P_KA, P_VA), False)
    b_specs, b_at = _gla_specs(n, (P_QA, P_KA, P_VA), True)
    tile = lambda at, blk=0: pl.BlockSpec((C, GROUP_WIDTH), lambda i: (at(i), blk))
    st_spec = lambda at: pl.BlockSpec((None, H, dh, dh), lambda i: (at(i), 0, 0, 0))

    def body(qf, kf, vf, lf, spf, dof, qb, kb, vb, lb, spb, dob, *rest):
        outs_f, outs_b, (dstf, dstb) = rest[0:4], rest[4:8], rest[8:10]

        @pl.when(pl.program_id(0) == 0)
        def _():
            dstf[...] = jnp.zeros_like(dstf)
            dstb[...] = jnp.zeros_like(dstb)
        chains = [(t, h, sl) for t in ((qf, kf, vf, _cumsum_rows(lf[...], False), spf, dof, outs_f, dstf, False),
                                       (qb, kb, vb, _cumsum_rows(lb[...], True), spb, dob, outs_b, dstb, True))
                  for h, sl in enumerate(HEADS)]
        nc = len(chains)
        revs = [t[8] for t, h, sl in chains]
        flat = [t[j][:, sl] for j in range(4) for t, h, sl in chains] + [t[4][h] for t, h, sl in chains]

        def f(*a):
            os_, sts = _gla_chunks(*[list(a[j * nc:(j + 1) * nc]) for j in range(5)], revs)
            return tuple(os_) + tuple(sts)

        _, vjp = jax.vjp(f, *flat)
        grads = vjp(tuple(t[5][:, sl] for t, h, sl in chains) + tuple(t[7][h] for t, h, sl in chains))
        for c_, (t, h, sl) in enumerate(chains):
            for j in range(4):
                t[6][j][:, sl] = grads[j * nc + c_].astype(t[6][j].dtype)
            t[7][h] = grads[4 * nc + c_]
        for outs, rev in ((outs_f, False), (outs_b, True)):
            outs[3][...] = _cumsum_rows(outs[3][...], not rev)

    return pl.pallas_call(
        body, name="gla_scan_b", grid=(n,),
        in_specs=f_specs + [tile(f_at, 0), st_spec(f_at), tile(f_at)] + b_specs + [tile(b_at, 1), st_spec(b_at), tile(b_at)],
        out_specs=[tile(f_at)] * 4 + [tile(b_at)] * 4,
        out_shape=[jax.ShapeDtypeStruct((L, GROUP_WIDTH), d) for d in (BF16, BF16, BF16, F32)] * 2,
        scratch_shapes=[pltpu.VMEM((H, dh, dh), F32)] * 2,
        compiler_params=_cparams(("arbitrary",)),
    )(p, p, p, la, sf, do, p, p, p, la, sb, do)


NA_W = NA_ROWS * GRID_W
NA_BW = (2 * NA_ROWS - 1) * GRID_W


NA_PER = 2


def _na_start(i, rows):
    return jnp.clip(i - NA_ROWS // 2, 0, rows - NA_ROWS)


def _na_chains(step, rows):
    first = _na_start(step * NA_PER, rows)
    out = []
    for rho in range(NA_PER):
        r = step * NA_PER + rho
        s = _na_start(r, rows)
        ks = pl.ds(pl.multiple_of(s * GRID_W, GRID_W), NA_W)
        off = pl.multiple_of((s - first) * GRID_W, GRID_W)
        out += [(pl.ds(rho * GRID_W, GRID_W), sl, h, ks, s - r + NA_ROWS - 1, off) for h, sl in enumerate(HEADS)]
    return out


def _na_fwd(p16, btab):
    L = p16.shape[0]
    rows = L // GRID_W
    assert rows % NA_PER == 0
    kv = lambda blk: pl.BlockSpec((L, GROUP_WIDTH), lambda i: (0, blk))

    def body(q_ref, k_ref, v_ref, b_ref, o_ref):
        chains = _na_chains(pl.program_id(0), rows)
        os_, _ = _attn_heads([q_ref[qs, sl] for qs, sl, h, ks, sv, off in chains], [k_ref[ks, sl] for qs, sl, h, ks, sv, off in chains],
                             [v_ref[ks, sl] for qs, sl, h, ks, sv, off in chains], [b_ref[sv, h] for qs, sl, h, ks, sv, off in chains])
        for (qs, sl, h, ks, sv, off), o in zip(chains, os_):
            o_ref[qs, sl] = o.astype(o_ref.dtype)

    whole = lambda a: pl.BlockSpec(a.shape, lambda i, nd=a.ndim: (0,) * nd)
    return pl.pallas_call(
        body, name="na_attn", grid=(rows // NA_PER,),
        in_specs=[pl.BlockSpec((NA_PER * GRID_W, GROUP_WIDTH), lambda i: (i, P_QB)), kv(P_KB), kv(P_VB), whole(btab)],
        out_specs=pl.BlockSpec((NA_PER * GRID_W, GROUP_WIDTH), lambda i: (i, 0)),
        out_shape=jax.ShapeDtypeStruct((L, GROUP_WIDTH), BF16),
        compiler_params=_cparams(("arbitrary",)),
    )(p16, p16, p16, btab)


def _na_bwd(p16, btab, dycat):
    L = p16.shape[0]
    rows = L // GRID_W
    assert rows % NA_PER == 0 and NA_ROWS % NA_PER == 0
    kv = lambda blk: pl.BlockSpec((L, GROUP_WIDTH), lambda i: (0, blk))
    steps, blk = rows // NA_PER, NA_PER * GRID_W
    span = NA_W + blk - GRID_W
    span += (-span) % blk
    flush = NA_W // blk - 1
    half = NA_ROWS // 2 // NA_PER
    emit = lambda i: jnp.where(i < steps, jnp.clip(i - half, 0, steps - NA_ROWS // NA_PER), i - flush)

    def body(q_ref, k_ref, v_ref, b_ref, do_ref, dq_ref, dk_ref, dv_ref, db_ref, acc_k, acc_v):
        i = pl.program_id(0)

        @pl.when(i == 0)
        def _():
            acc_k[...] = jnp.zeros_like(acc_k)
            acc_v[...] = jnp.zeros_like(acc_v)
            db_ref[...] = jnp.zeros_like(db_ref)

        @pl.when((i > 0) & (emit(i) != emit(i - 1)))
        def _():
            for acc in (acc_k, acc_v):
                moved = acc[blk:span, :]
                acc[0:span - blk, :] = moved
                acc[span - blk:span, :] = jnp.zeros((blk, GROUP_WIDTH), F32)

        @pl.when(i < steps)
        def _():
            chains = _na_chains(i, rows)
            H = len(chains)
            flat = [q_ref[qs, sl].astype(F32) for qs, sl, h, ks, sv, off in chains] + \
                   [k_ref[ks, sl].astype(F32) for qs, sl, h, ks, sv, off in chains] + \
                   [v_ref[ks, sl].astype(F32) for qs, sl, h, ks, sv, off in chains] + [b_ref[sv, h] for qs, sl, h, ks, sv, off in chains]

            def f(*a):
                os_, lses = _attn_heads(a[0:H], a[H:2 * H], a[2 * H:3 * H], a[3 * H:4 * H])
                return tuple(os_) + tuple(lses)

            _, vjp = jax.vjp(f, *flat)
            grads = vjp(tuple(do_ref[qs, sl] for qs, sl, h, ks, sv, off in chains) + (jnp.zeros((GRID_W, 1), F32),) * H)
            for c_, (qs, sl, h, ks, sv, off) in enumerate(chains):
                dq_ref[qs, sl] = grads[c_].astype(dq_ref.dtype)
                acc_k[pl.ds(off, NA_W), sl] += grads[H + c_]
                acc_v[pl.ds(off, NA_W), sl] += grads[2 * H + c_]
                db_ref[sv, h] += grads[3 * H + c_]

        dk_ref[...] = acc_k[0:blk, :].astype(dk_ref.dtype)
        dv_ref[...] = acc_v[0:blk, :].astype(dv_ref.dtype)

    whole = lambda a: pl.BlockSpec(a.shape, lambda i, nd=a.ndim: (0,) * nd)
    qrow = lambda cb: pl.BlockSpec((blk, GROUP_WIDTH), lambda i: (jnp.minimum(i, steps - 1), cb))
    erow = pl.BlockSpec((blk, GROUP_WIDTH), lambda i: (emit(i), 0))
    return pl.pallas_call(
        body, name="na_attn_b", grid=(steps + flush,),
        in_specs=[qrow(P_QB), kv(P_KB), kv(P_VB), whole(btab), qrow(1)],
        out_specs=[qrow(0), erow, erow, whole(btab)],
        out_shape=[jax.ShapeDtypeStruct((L, GROUP_WIDTH), BF16)] * 3 + [jax.ShapeDtypeStruct(btab.shape, F32)],
        scratch_shapes=[pltpu.VMEM((span, GROUP_WIDTH), F32)] * 2,
        compiler_params=_cparams(("arbitrary",)),
    )(p16, p16, p16, btab, dycat)


def _na_col_ok():
    qc = np.arange(GRID_W)[:, None]
    kc = (np.arange(NA_W) % GRID_W)[None, :]
    c0 = np.clip(qc - NA_COLS // 2, 0, GRID_W - NA_COLS)
    return (kc >= c0) & (kc < c0 + NA_COLS)


def _rpb_tables():
    c = np.arange(GRID_W)
    dc = np.clip(c[None, :] - c[:, None], -(NA_COLS - 1), NA_COLS - 1) + NA_COLS - 1
    t = np.zeros((2 * NA_COLS - 1, GRID_W, GRID_W), np.float32)
    t[dc, c[:, None], c[None, :]] = 1.0
    return jnp.asarray(t.reshape(2 * NA_COLS - 1, GRID_W * GRID_W))


def _rpb_expand(rpb, tab):
    H = rpb.shape[0]
    xt = _small_dot("na_bias", rpb.reshape(H * (2 * NA_ROWS - 1), 2 * NA_COLS - 1), tab, 'nn')
    b15 = xt.reshape(H, 2 * NA_ROWS - 1, GRID_W, GRID_W).transpose(0, 2, 1, 3).reshape(H, GRID_W, NA_BW)
    ok = jnp.asarray(_na_col_ok())
    return jnp.stack([jnp.where(ok, b15[:, :, sv * GRID_W:sv * GRID_W + NA_W], NEG) for sv in range(NA_ROWS)])


def _rpb_contract(dbv, tab):
    H = dbv.shape[1]
    db = sum(jnp.pad(dbv[sv], ((0, 0), (0, 0), (sv * GRID_W, NA_BW - NA_W - sv * GRID_W))) for sv in range(NA_ROWS))
    dx = db.reshape(H, GRID_W, 2 * NA_ROWS - 1, GRID_W).transpose(0, 2, 1, 3).reshape(H * (2 * NA_ROWS - 1), GRID_W * GRID_W)
    return _small_dot("na_bias_b", dx, tab, 'nt').reshape(H, 2 * NA_ROWS - 1, 2 * NA_COLS - 1)


def _band_bias(i, tq, w, halo, n):
    a = lax.broadcasted_iota(jnp.int32, (tq, w), 0)
    b = lax.broadcasted_iota(jnp.int32, (tq, w), 1)
    kpos = i * tq - halo + b
    d = b - halo - a
    return jnp.where((d <= halo) & (d >= -halo) & (kpos >= 0) & (kpos < n), 0.0, NEG)


def _band_fwd(name, q, kp, vp, tq, halo):
    G, n, _ = q.shape
    w = tq + 2 * halo

    def body(q_ref, k_ref, v_ref, o_ref, l_ref):
        i = pl.program_id(1)
        start = pl.multiple_of(i * tq, tq)
        bias = _band_bias(i, tq, w, halo, n)
        os_, lses = _attn_heads([q_ref[:, sl] for sl in HEADS], [k_ref[pl.ds(start, w), sl] for sl in HEADS],
                                [v_ref[pl.ds(start, w), sl] for sl in HEADS], [bias] * GROUP_HEADS)
        for sl, o, lse in zip(HEADS, os_, lses):
            o_ref[:, sl] = o.astype(o_ref.dtype)
            l_ref[:, sl] = jnp.broadcast_to(lse, (tq, HEAD_DIM))

    qblk = pl.BlockSpec((None, tq, GROUP_WIDTH), lambda g, i: (g, i, 0))
    kblk = pl.BlockSpec((None, n + 2 * halo, GROUP_WIDTH), lambda g, i: (g, 0, 0))
    return pl.pallas_call(
        body, name=name, grid=(G, n // tq), in_specs=[qblk, kblk, kblk], out_specs=[qblk, qblk],
        out_shape=[jax.ShapeDtypeStruct((G, n, GROUP_WIDTH), d) for d in (BF16, F32)],
        compiler_params=_cparams(("parallel", "arbitrary")),
    )(q, kp, vp)


def _band_bwd(name, q, kp, vp, do, dl, tq, halo):
    G, n, _ = q.shape
    w = tq + 2 * halo
    nq = n // tq

    def body(q_ref, k_ref, v_ref, do_ref, dl_ref, dq_ref, dk_ref, dv_ref, acc_k, acc_v):
        i = pl.program_id(1)

        @pl.when(i == 0)
        def _():
            acc_k[...] = jnp.zeros_like(acc_k)
            acc_v[...] = jnp.zeros_like(acc_v)

        @pl.when(i > 0)
        def _():
            for acc in (acc_k, acc_v):
                moved = acc[tq:w, :]
                acc[0:2 * halo, :] = moved
                acc[2 * halo:w, :] = jnp.zeros((tq, GROUP_WIDTH), F32)

        @pl.when(i < nq)
        def _():
            start = pl.multiple_of(i * tq, tq)
            bias = _band_bias(i, tq, w, halo, n)
            H = GROUP_HEADS
            flat = [q_ref[:, sl].astype(F32) for sl in HEADS] + [k_ref[pl.ds(start, w), sl].astype(F32) for sl in HEADS] + \
                   [v_ref[pl.ds(start, w), sl].astype(F32) for sl in HEADS]

            def f(*a):
                os_, lses = _attn_heads(a[0:H], a[H:2 * H], a[2 * H:3 * H], [bias] * H)
                return tuple(os_) + tuple(lses)

            _, vjp = jax.vjp(f, *flat)
            grads = vjp(tuple(do_ref[:, sl].astype(F32) for sl in HEADS) +
                        tuple(jnp.sum(dl_ref[:, sl], axis=1, keepdims=True) for sl in HEADS))
            for h, sl in enumerate(HEADS):
                dq_ref[:, sl] = grads[h].astype(dq_ref.dtype)
                acc_k[:, sl] += grads[H + h]
                acc_v[:, sl] += grads[2 * H + h]

        dk_ref[...] = acc_k[0:tq, :].astype(dk_ref.dtype)
        dv_ref[...] = acc_v[0:tq, :].astype(dv_ref.dtype)

    qblk = pl.BlockSpec((None, tq, GROUP_WIDTH), lambda g, i: (g, jnp.minimum(i, nq - 1), 0))
    kblk = pl.BlockSpec((None, n + 2 * halo, GROUP_WIDTH), lambda g, i: (g, 0, 0))
    eblk = pl.BlockSpec((None, tq, GROUP_WIDTH), lambda g, i: (g, i, 0))
    return pl.pallas_call(
        body, name=name, grid=(G, nq + 1), in_specs=[qblk, kblk, kblk, qblk, qblk], out_specs=[qblk, eblk, eblk],
        out_shape=[jax.ShapeDtypeStruct((G, n, GROUP_WIDTH), BF16)] + [jax.ShapeDtypeStruct((G, (nq + 1) * tq, GROUP_WIDTH), BF16)] * 2,
        scratch_shapes=[pltpu.VMEM((w, GROUP_WIDTH), F32)] * 2,
        compiler_params=_cparams(("parallel", "arbitrary")),
    )(q, kp, vp, do, dl)


def _lin_scans(name, jobs):
    L, C = jobs[0][0].shape
    tt = 256 if L % 256 == 0 else L
    nt, per, last = L // tt, tt // HALO, L // HALO - 1
    specs, arrs, plan = [], [], []
    for coef, inp, rev, shift in jobs:
        tile = (lambda i: nt - 1 - i) if rev else (lambda i: i)
        blk = pl.BlockSpec((tt, C), lambda i, tile=tile: (tile(i), 0))
        if shift:
            side = lambda d, tile=tile: pl.BlockSpec(
                (HALO, C), lambda i: (jnp.clip(tile(i) * per + (per if d > 0 else -1), 0, last), 0))
            specs += [side(-1), blk, side(1), blk]
            arrs += [coef, coef, coef, inp]
        else:
            specs += [blk, blk]
            arrs += [coef, inp]
        plan.append((tile, blk, rev, shift))
    n_in = len(specs)

    def body(*refs):
        o_refs, carries = refs[n_in:n_in + len(jobs)], refs[n_in + len(jobs):]

        @pl.when(pl.program_id(0) == 0)
        def _():
            for carry in carries:
                carry[...] = jnp.zeros_like(carry)
        row = lax.broadcasted_iota(jnp.int32, (tt, C), 0)
        au, k = [], 0
        for tile, _, rev, shift in plan:
            if shift:
                a = _shift_tile(refs[k][...], refs[k + 1][...], refs[k + 2][...], shift, tile(pl.program_id(0)), nt)
                k += 3
            else:
                a = refs[k][...]
                k += 1
            au.append([a, refs[k][...]])
            k += 1
        s = 1
        while s < tt:
            for (tile, _, rev, shift), st in zip(plan, au):
                a, u = st
                ok = (row < tt - s) if rev else (row >= s)
                sh = tt - s if rev else s
                st[1] = u + a * jnp.where(ok, pltpu.roll(u, sh, 0), 0.0)
                st[0] = a * jnp.where(ok, pltpu.roll(a, sh, 0), 1.0)
            s *= 2
        for (tile, _, rev, shift), (a, u), o_ref, carry in zip(plan, au, o_refs, carries):
            out = u + a * carry[...]
            o_ref[...] = out
            carry[...] = out[0:1] if rev else out[tt - 1:tt]

    return pl.pallas_call(
        body, name=name, grid=(nt,), in_specs=specs, out_specs=[p_[1] for p_ in plan],
        out_shape=[jax.ShapeDtypeStruct((L, C), F32)] * len(jobs), scratch_shapes=[pltpu.VMEM((1, C), F32)] * len(jobs),
        compiler_params=_cparams(("arbitrary",)),
    )(*arrs)


def _pieces(shape):
    n0 = max(d for d in range(1, DMA_PIECES + 1) if shape[0] % d == 0)
    n1 = 1
    if len(shape) >= 3:
        n1 = max(d for d in range(1, DMA_PIECES // n0 + 1) if shape[1] % d == 0)
    s0, s1 = shape[0] // n0, (shape[1] // n1 if len(shape) >= 3 else 0)
    out = []
    for i in range(n0):
        for j in range(n1):
            out.append((pl.ds(i * s0, s0),) + ((pl.ds(j * s1, s1),) if len(shape) >= 3 else ()))
    return out


def _exchange(name, src, axes, gather):
    flips = {'xy': [(1, 0, 0), (0, 1, 0), (1, 1, 0)], 'c': [(0, 0, 1)],
             'xyc': [(fx, fy, fc) for fx in (0, 1) for fy in (0, 1) for fc in (0, 1)][1:]}[axes]
    n = len(flips) + 1
    blk_shape = tuple(src.shape if gather else src.shape[1:])
    pieces = _pieces(blk_shape)

    def number(px, py, pc):
        return {'xy': 2 * px + py, 'c': pc, 'xyc': 4 * px + 2 * py + pc}[axes]

    def body(src_ref, out_ref, send_sems, recv_sems):
        x, y, c = lax.axis_index("x"), lax.axis_index("y"), lax.axis_index("c")
        me = number(x, y, c)
        piece = (lambda k: src_ref) if gather else (lambda k: src_ref.at[k])
        peers = []
        for s, (fx, fy, fc) in enumerate(flips):
            px, py, pc = (x + fx) % 2, (y + fy) % 2, (c + fc) % 2

            def copy(ix, s=s, px=px, py=py, pc=pc):
                part = (lambda r: r) if ix is None else (lambda r: r.at[ix])
                return pltpu.make_async_remote_copy(
                    src_ref=part(piece(number(px, py, pc))), dst_ref=part(out_ref.at[me]),
                    send_sem=send_sems.at[s], recv_sem=recv_sems.at[s],
                    device_id=(px, py, pc), device_id_type=MESH)

            for ix in pieces:
                copy(ix).start()
            peers.append(copy)
        for copy in peers:
            copy(None).wait()

    out = pl.pallas_call(
        body, name=name, out_shape=jax.ShapeDtypeStruct((n,) + blk_shape, src.dtype),
        in_specs=[pl.BlockSpec(memory_space=pl.ANY)], out_specs=pl.BlockSpec(memory_space=pl.ANY),
        scratch_shapes=[pltpu.SemaphoreType.DMA((n - 1,)), pltpu.SemaphoreType.DMA((n - 1,))],
    )(src)
    me = number(lax.axis_index("x"), lax.axis_index("y"), lax.axis_index("c"))
    own = src if gather else lax.dynamic_index_in_dim(src, me, 0, keepdims=False)
    return lax.dynamic_update_index_in_dim(out, own, me, 0)


def _ordered_sum(name, buf, dtype=F32):
    n = buf.shape[0]

    def fn(*t):
        acc = t[0].astype(F32)
        for v in t[1:]:
            acc = acc + v.astype(F32)
        return acc

    return _rowwise(name, fn, [Rows(buf, lead=k) for k in range(n)], [], [(buf.shape[-1], dtype)])[0]


def _reduce_big(name, g):
    mine = _ordered_sum(name + "_sum_c", _exchange(name + "_swap_c", g, 'c', False).reshape(2, -1, g.shape[-1]), BF16)
    mine = mine.reshape(g.shape[1:])
    tot = _ordered_sum(name + "_sum_xy", _exchange(name + "_a2a_xy", mine, 'xy', False))
    return _exchange(name + "_share_c", tot, 'c', True)


def _dilate(t, dil):
    L, C = t.shape
    return t.reshape(L // dil, dil, C).transpose(1, 0, 2)


def _undilate(t):
    dil, n, C = t.shape
    return t.transpose(1, 0, 2).reshape(dil * n, C)


def _pad_rows(t, halo):
    return jnp.pad(t, ((0, 0), (halo, halo), (0, 0)))


def _pcol(p, blk):
    return Rows(p, GROUP_WIDTH, blk)


def _pslice(p, blk):
    return p[:, blk * GROUP_WIDTH:(blk + 1) * GROUP_WIDTH]


def _conv_taps(p):
    return Rows(p, GROUP_WIDTH, P_XC, shifts=[LRU_CONV_LEFT - j for j in range(LRU_CONV)])


def _seg_matrix():
    h = np.arange(GROUP_WIDTH) // HEAD_DIM
    return jnp.asarray((h[:, None] == h[None, :]).astype(np.float32) / HEAD_DIM)


def _rope_tables(L):
    pos = jnp.arange(L, dtype=F32)
    inv_freq = ROPE_THETA ** (-jnp.arange(0, HEAD_DIM, 2, dtype=F32) / HEAD_DIM)
    ang = pos[:, None] * inv_freq[None, :]
    cos, sin = jnp.cos(ang), jnp.sin(ang)
    cs = jnp.tile(jnp.concatenate([cos, cos], axis=1), (1, GROUP_HEADS))
    sn = jnp.tile(jnp.concatenate([-sin, sin], axis=1), (1, GROUP_HEADS))
    return cs, sn


def _dil_branches(L):
    out = []
    for window, dil in DIL_PAIRS:
        radius = window // (2 * dil)
        n = L // dil
        out.append((dil, radius, min(256, n)))
    return out


def _dil_operands(qr, kr, p16, dil, radius):
    return _dilate(qr, dil), _pad_rows(_dilate(kr, dil), radius), _pad_rows(_dilate(_pslice(p16, P_VD), dil), radius)


def _layer_fwd(x, w, c):
    L, D = x.shape
    sv = {'x_in': x}
    h = _rowwise("mix_prenorm", _prenorm_fn, [x], [w['mix_norm_pre']], [(D, BF16)])[0]
    p, p16 = _matmul("mix_proj", h, w['w_in'], 'nn', F32, also=BF16)
    sv.update(p=p, p16=p16, h=h)

    la = _rowwise("gla_pre", _gla_pre_fn, [Rows(p, LANE, P_Z // LANE)], [w['gla_wg'], w['gla_bg']], [(2 * GROUP_WIDTH, F32)])[0]
    of, ob, sf, sb = _gla_scan_fwd(p, la)
    sv.update(la=la, sf=sf, sb=sb, of=of, ob=ob)

    yb = _na_fwd(p16, _rpb_expand(w['na_rpb'], c['rpb_tab']))

    a0, a1, u0, u1 = _rowwise("lru_pre", _lru_pre_fn, [_conv_taps(p)], [w[k] for k in LRU_PARAMS], [(GROUP_WIDTH, F32)] * 4)
    hf, hb = _lin_scans("lru_scan", [(a0, u0, False, 0), (a1, u1, True, 0)])
    sv.update(a0=a0, a1=a1, hf=hf, hb=hb)

    qr, kr = _rowwise("rope", _rope_fn, [_pcol(p, P_QD), _pcol(p, P_KD), c['cos'], c['sin']], [], [(GROUP_WIDTH, BF16)] * 2)
    os_, ls_, ops = [], [], []
    for dil, radius, tq in _dil_branches(L):
        ops.append(_dil_operands(qr, kr, p16, dil, radius))
        o, lse = _band_fwd(f"dil_attn{dil}", *ops[-1], tq, radius)
        os_.append(_undilate(o))
        ls_.append(_undilate(lse))
    sv.update(dil_ops=ops, dil_o=os_, dil_l=ls_)

    ycat = _rowwise("mix_post", _mix_post_fn, [of, ob, _pcol(p, P_GA), yb, hf, hb, _pcol(p, P_GC)] + os_ + ls_,
                    [w['gla_norm'], c['seg']], [(4 * GROUP_WIDTH, BF16)])[0]
    y = _matmul("mix_out", ycat, w['w_out'], 'nn', F32)
    xm = _rowwise("mix_postnorm", _postnorm_fn, [x, y], [w['mix_norm_post']], [(D, F32)])[0]
    sv.update(ycat=ycat, y=y, x_mid=xm)

    h2, gate, up, act = _ffn_up("ffn_up", xm, w['ffn_norm_pre'], w['ffn_wg'], w['ffn_wu'])
    f = _matmul("ffn_out", act, w['ffn_w_out'], 'nn', F32)
    xo = _rowwise("ffn_postnorm", _postnorm_fn, [xm, f], [w['ffn_norm_post']], [(D, F32)])[0]
    sv.update(gate=gate, up=up, act=act, f=f, h2=h2)
    return xo, sv


def _layer_bwd(dx, w, c, sv):
    L, D = dx.shape
    g = {}
    as_f32 = lambda t: (t.astype(F32),)
    df, g['ffn_norm_post'] = _rowwise_bwd("ffn_postnorm_b", lambda y, gn: _rms(y, gn), [sv['f']], [w['ffn_norm_post']],
                                          [dx], as_f32, [BF16], [True])
    dgate, dup = _ffn_down_bwd("ffn_out_bx", df, w['ffn_w_out'], sv['gate'], sv['up'])
    g['ffn_w_out'] = _matmul("ffn_out_bw", sv['act'], df, 'tn', F32)
    dh2 = _matmul("ffn_up_bx", dup, w['ffn_wu'], 'nt', F32, acc_in=_matmul("ffn_gate_bx", dgate, w['ffn_wg'], 'nt', F32))
    xm = sv['x_mid']
    g['ffn_wg'] = _matmul("ffn_gate_bw", sv['h2'], dgate, 'tn', F32)
    g['ffn_wu'] = _matmul("ffn_up_bw", sv['h2'], dup, 'tn', F32)
    dxm, g['ffn_norm_pre'] = _rowwise_bwd("ffn_prenorm_b", _prenorm_fn, [xm], [w['ffn_norm_pre']], [dh2], as_f32, [F32], [True],
                                          row_grad_add=[dx])

    dy, g['mix_norm_post'] = _rowwise_bwd("mix_postnorm_b", lambda y, gn: _rms(y, gn), [sv['y']], [w['mix_norm_post']],
                                          [dxm], as_f32, [BF16], [True])
    dycat = _matmul("mix_out_bx", dy, w['w_out'], 'nt', F32)
    g['w_out'] = _matmul("mix_out_bw", sv['ycat'], dy, 'tn', F32)
    p = sv['p']
    dya, dyb, dyc, dyd = (Rows(dycat, GROUP_WIDTH, k) for k in range(4))

    dof, dga, g['gla_norm'] = _rowwise_bwd("gla_post_b", _gla_post_fn, [sv['of'], sv['ob'], _pcol(p, P_GA)],
                                           [w['gla_norm'], c['seg']], [dya], as_f32, [F32, None, BF16], [True, False])
    la = sv['la']
    dqf, dkf, dvf, dlf, dqb_, dkb_, dvb_, dlb = _gla_scan_bwd(p, la, sv['sf'], sv['sb'], dof)
    dz, g['gla_wg'], g['gla_bg'] = _rowwise_bwd("gla_pre_b", _gla_pre_fn, [Rows(p, LANE, P_Z // LANE)], [w['gla_wg'], w['gla_bg']],
                                                [dlf, dlb], lambda a, b: (jnp.concatenate([a, b], axis=1),), [BF16], [True, True])

    btab = _rpb_expand(w['na_rpb'], c['rpb_tab'])
    dqn, dkn, dvn, dbt = _na_bwd(sv['p16'], btab, dycat)
    g['na_rpb'] = _rpb_contract(dbt, c['rpb_tab'])

    dh, dgc = _rowwise_bwd("lru_post_b", _lru_post_fn, [sv['hf'], sv['hb'], _pcol(p, P_GC)], [], [dyc], as_f32, [F32, None, BF16], [])
    lam0, lam1 = _lin_scans("lru_scan_b", [(sv['a0'], dh, True, -1), (sv['a1'], dh, False, 1)])
    res = _rowwise_bwd("lru_pre_b", _lru_pre_fn, [_conv_taps(p)], [w[k] for k in LRU_PARAMS],
                       [lam0, lam1, Rows(sv['hf'], shifts=[1]), Rows(sv['hb'], shifts=[-1])],
                       lambda l0, l1, hfp, hbn: (l0 * hfp, l1 * hbn, l0, l1), [F32] * 4, [True] * len(LRU_PARAMS))
    dxs = res[:4]
    for k, nm in enumerate(LRU_PARAMS):
        g[nm] = res[4 + k]
    dxc = [Rows(dxs[j], shifts=[j - LRU_CONV_LEFT]) for j in range(LRU_CONV)]

    comb = _rowwise_bwd("dil_comb_b", _dil_comb_fn, sv['dil_o'] + sv['dil_l'], [], [dyd], as_f32, [BF16] * 3 + [F32] * 3, [])
    dqs, dks, dvs = [], [], []
    for k, (dil, radius, tq) in enumerate(_dil_branches(L)):
        n = L // dil
        dq_, dk_, dv_ = _band_bwd(f"dil_attn{dil}_b", *sv['dil_ops'][k], _dilate(comb[k], dil), _dilate(comb[3 + k], dil), tq, radius)
        dqs.append(_undilate(dq_))
        dks.append(_undilate(dk_[:, radius:radius + n]))
        dvs.append(_undilate(dv_[:, radius:radius + n]))
    dqd, dkd = _rowwise_bwd("rope_b", _rope_fn, [_pcol(p, P_QD), _pcol(p, P_KD), c['cos'], c['sin']], [], dqs + dks,
                            lambda *t: (sum(v.astype(F32) for v in t[:3]), sum(v.astype(F32) for v in t[3:])), [BF16, BF16, None, None], [])

    dp = _assemble("mix_dp", [[dqf, dqb_], [dkf, dkb_], [dvf, dvb_], [dga], [dqn], [dkn], [dvn], dxc, [dgc], [dqd], [dkd], dvs, [dz]], BF16)
    dh1 = _matmul("mix_proj_bx", dp, w['w_in'], 'nt', F32)
    x_in = sv['x_in']
    g['w_in'] = _matmul("mix_proj_bw", sv['h'], dp, 'tn', F32)
    dxi, g['mix_norm_pre'] = _rowwise_bwd("mix_prenorm_b", _prenorm_fn, [x_in], [w['mix_norm_pre']], [dh1], as_f32, [F32], [True],
                                          row_grad_add=[dxm])
    return dxi, g


def _loss_fn(y, t):
    e = y - t
    return e * (1.0 / y.shape[1]), jnp.sum(e * e, axis=0, keepdims=True)


def _gather_cols(name, shard, axis):
    half = shard.shape[0] // 2
    mine = lax.dynamic_slice_in_dim(shard, lax.axis_index("c") * half, half, axis=0).astype(BF16)
    both = _exchange(name + "_c", _exchange(name + "_xy", mine, 'xy', True), 'c', True)
    shp = list(shard.shape)
    shp[axis] *= 4
    return jnp.moveaxis(both, 1, axis + 1).reshape(shp)


def _pack(arrs, mult=64 * LANE):
    flat = jnp.concatenate([a.reshape(-1) for a in arrs])
    pad = (-flat.shape[0]) % mult
    return jnp.pad(flat, (0, pad)).reshape(-1, LANE)


def _unpack(buf, shapes):
    flat, out, k = buf.reshape(-1), [], 0
    for s in shapes:
        sz = int(np.prod(s))
        out.append(flat[k:k + sz].reshape(s))
        k += sz
    return out


def _perm_in(w_in):
    pad = jnp.zeros(w_in.shape[:-1] + (D_INP - D_IN,), w_in.dtype)
    return jnp.concatenate([w_in[..., :P_QB * GROUP_WIDTH], w_in[..., P_QB * GROUP_WIDTH + 2 * GLA_RANK:],
                            w_in[..., P_QB * GROUP_WIDTH:P_QB * GROUP_WIDTH + 2 * GLA_RANK], pad], axis=-1)


def _unperm_in(g):
    return jnp.concatenate([g[..., :P_QB * GROUP_WIDTH], g[..., P_Z:P_Z + 2 * GLA_RANK], g[..., P_QB * GROUP_WIDTH:P_Z]], axis=-1)


def _block_diag(wb):
    l = wb.shape[0]
    eye = jnp.eye(GROUP_HEADS, dtype=wb.dtype)
    return jnp.einsum('lehij,hg->lehigj', wb, eye).reshape(l, 2, GROUP_WIDTH, GROUP_WIDTH)


def _block_diag_grad(gw):
    l = gw.shape[0]
    g6 = gw.reshape(l, 2, GROUP_HEADS, HEAD_DIM, GROUP_HEADS, HEAD_DIM)
    return jnp.stack([g6[:, :, h, :, h, :] for h in range(GROUP_HEADS)], axis=2)


def _gate_matrix(wg):
    l = wg.shape[0]
    m = jnp.zeros((l, LANE, 2 * GROUP_WIDTH), wg.dtype)
    for e in range(2):
        m = m.at[:, e * GLA_RANK:(e + 1) * GLA_RANK, e * GROUP_WIDTH:(e + 1) * GROUP_WIDTH].set(wg[:, e])
    return m


def _gate_matrix_grad(gm):
    return jnp.stack([gm[:, e * GLA_RANK:(e + 1) * GLA_RANK, e * GROUP_WIDTH:(e + 1) * GROUP_WIDTH] for e in range(2)], axis=1)


def _adam_fn(w, g, m, v):
    m = ADAM_B1 * m + (1.0 - ADAM_B1) * g
    v = ADAM_B2 * v + (1.0 - ADAM_B2) * (g * g)
    m_hat = m / (1.0 - ADAM_B1 ** ADAM_STEP)
    v_hat = v / (1.0 - ADAM_B2 ** ADAM_STEP)
    return -ADAM_LR * (m_hat / (jnp.sqrt(v_hat) + ADAM_EPS) + ADAM_WD * w), m, v


def _adam(name, w, g, m, v):
    shp = w.shape
    two = lambda t: t.reshape(-1, shp[-1])
    res = _rowwise(name, _adam_fn, [two(w), two(g), two(m), two(v)], [], [(shp[-1], F32)] * 3)
    return [r.reshape(shp) for r in res]


def _local_step(x, target, fw):
    L, D = x.shape
    depth = fw['w_in'].shape[0]
    cs, sn = _rope_tables(L)
    consts = {'seg': _seg_matrix(), 'rpb_tab': _rpb_tables(), 'cos': cs, 'sin': sn}
    layer = lambda l: {k: v[l] for k, v in fw.items()}
    saved = []
    for l in range(depth):
        x, sv = _layer_fwd(x, layer(l), consts)
        saved.append(sv)
    dx, sq = _rowwise("loss", _loss_fn, [x, target], [], [(D, F32)], acc_outs=[(1, D)])
    grads = [None] * depth
    for l in reversed(range(depth)):
        dx, grads[l] = _layer_bwd(dx, layer(l), consts, saved[l])
    return sq, dx, grads


def kernel(x, mix_norm_pre, mix_norm_post, w_in, gla_w_gate, gla_b_gate, gla_norm, na_rpb, lru_conv_w, lru_conv_b, lru_w_a, lru_b_a, lru_w_x, lru_b_x, lru_lambda, w_out, ffn_norm_pre, ffn_norm_post, ffn_w_in, ffn_w_out, loss_target, m_mix_norm_pre, m_mix_norm_post, m_w_in, m_gla_w_gate, m_gla_b_gate, m_gla_norm, m_na_rpb, m_lru_conv_w, m_lru_conv_b, m_lru_w_a, m_lru_b_a, m_lru_w_x, m_lru_b_x, m_lru_lambda, m_w_out, m_ffn_norm_pre, m_ffn_norm_post, m_ffn_w_in, m_ffn_w_out, v_mix_norm_pre, v_mix_norm_post, v_w_in, v_gla_w_gate, v_gla_b_gate, v_gla_norm, v_na_rpb, v_lru_conv_w, v_lru_conv_b, v_lru_w_a, v_lru_b_a, v_lru_w_x, v_lru_b_x, v_lru_lambda, v_w_out, v_ffn_norm_pre, v_ffn_norm_post, v_ffn_w_in, v_ffn_w_out):
    args = (mix_norm_pre, mix_norm_post, w_in, gla_w_gate, gla_b_gate, gla_norm, na_rpb, lru_conv_w, lru_conv_b, lru_w_a, lru_b_a, lru_w_x, lru_b_x, lru_lambda, w_out, ffn_norm_pre, ffn_norm_post, ffn_w_in, ffn_w_out,
            m_mix_norm_pre, m_mix_norm_post, m_w_in, m_gla_w_gate, m_gla_b_gate, m_gla_norm, m_na_rpb, m_lru_conv_w, m_lru_conv_b, m_lru_w_a, m_lru_b_a, m_lru_w_x, m_lru_b_x, m_lru_lambda, m_w_out, m_ffn_norm_pre, m_ffn_norm_post, m_ffn_w_in, m_ffn_w_out,
            v_mix_norm_pre, v_mix_norm_post, v_w_in, v_gla_w_gate, v_gla_b_gate, v_gla_norm, v_na_rpb, v_lru_conv_w, v_lru_conv_b, v_lru_w_a, v_lru_b_a, v_lru_w_x, v_lru_b_x, v_lru_lambda, v_w_out, v_ffn_norm_pre, v_ffn_norm_post, v_ffn_w_in, v_ffn_w_out)
    nw = len(WEIGHTS)
    W = dict(zip(WEIGHTS, args[:nw]))
    M = dict(zip(WEIGHTS, args[nw:2 * nw]))
    V = dict(zip(WEIGHTS, args[2 * nw:]))
    chip = 2 * lax.axis_index("x") + lax.axis_index("y")

    full = dict(W)
    full['w_in'] = _gather_cols("ag_w_in", w_in, 2)
    full['ffn_w_in'] = _gather_cols("ag_ffn_w_in", ffn_w_in, 2)
    full['w_out'] = _gather_cols("ag_w_out", w_out, 1)
    full['ffn_w_out'] = _gather_cols("ag_ffn_w_out", ffn_w_out, 1)
    small = list(SMALL_SHARDED)
    got = _exchange("ag_small", _pack([W[k] for k in small]), 'xy', True)
    for k, parts in zip(small, zip(*[_unpack(got[j], [W[k].shape for k in small]) for j in range(4)])):
        ax = SMALL_SHARDED[k]
        stacked = jnp.moveaxis(jnp.stack(parts), 0, ax)
        shp = list(W[k].shape)
        shp[ax] *= 4
        full[k] = stacked.reshape(shp)

    sq, dx0, g = _local_step(x[0], loss_target[0], _layer_weights(full))
    loss = lax.psum(0.5 * jnp.sum(sq) / x.shape[-1], ("x", "y", "c"))
    gfull = _stored_grads(g)

    grad = {}
    for k, cut in _big_cuts(g).items():
        grad[k] = _reduce_big("rs_" + k, cut).reshape(W[k].shape)
    rest = [k for k in WEIGHTS if k not in BIG]
    allg = _exchange("ar_small_c", _exchange("ar_small_xy", _pack([gfull[k] for k in rest]), 'xy', True), 'c', True)
    allg = allg.transpose(1, 0, 2, 3).reshape(8, -1, LANE)
    summed = _unpack(_ordered_sum("ar_small_sum", allg), [gfull[k].shape for k in rest])
    for k, s in zip(rest, summed):
        if k in SMALL_SHARDED:
            ax = SMALL_SHARDED[k]
            n = W[k].shape[ax]
            s = lax.dynamic_slice_in_dim(s, chip * n, n, axis=ax)
        grad[k] = s

    delta, new_m, new_v = {}, {}, {}
    for k in BIG:
        delta[k], new_m[k], new_v[k] = _adam("adam_" + k, W[k], grad[k], M[k], V[k])
    shapes = [W[k].shape for k in rest]
    res = _rowwise("adam_small", _adam_fn, [_pack([d[k] for k in rest]) for d in (W, grad, M, V)], [], [(LANE, F32)] * 3)
    for d, r in zip((delta, new_m, new_v), res):
        for k, t in zip(rest, _unpack(r, shapes)):
            d[k] = t

    return (loss, dx0[None], *[grad[k] for k in WEIGHTS], *[delta[k] for k in WEIGHTS],
            *[new_m[k] for k in WEIGHTS], *[new_v[k] for k in WEIGHTS])


def _layer_weights(full):
    depth = full['w_in'].shape[0]
    dff = full['ffn_w_in'].shape[-1] // 2
    row = lambda t: t[:, None, :]
    fw = {
        'mix_norm_pre': row(full['mix_norm_pre']), 'mix_norm_post': row(full['mix_norm_post']),
        'ffn_norm_pre': row(full['ffn_norm_pre']), 'ffn_norm_post': row(full['ffn_norm_post']),
        'w_in': _perm_in(full['w_in']), 'w_out': full['w_out'],
        'ffn_wg': full['ffn_w_in'][..., :dff], 'ffn_wu': full['ffn_w_in'][..., dff:], 'ffn_w_out': full['ffn_w_out'],
        'gla_wg': _gate_matrix(full['gla_w_gate']), 'gla_bg': full['gla_b_gate'].reshape(depth, 1, 2 * GROUP_WIDTH),
        'gla_norm': row(full['gla_norm']), 'na_rpb': full['na_rpb'],
        'lru_cb': row(full['lru_conv_b']),
    }
    wa_bd, wx_bd = _block_diag(full['lru_w_a']), _block_diag(full['lru_w_x'])
    for j in range(LRU_CONV):
        fw[f'lru_cw{j}'] = row(full['lru_conv_w'][:, j])
    for e in range(2):
        fw[f'lru_wa{e}'], fw[f'lru_wx{e}'] = wa_bd[:, e], wx_bd[:, e]
        fw[f'lru_ba{e}'], fw[f'lru_bx{e}'] = row(full['lru_b_a'][:, e]), row(full['lru_b_x'][:, e])
        fw[f'lru_lam{e}'] = row(full['lru_lambda'][:, e])
    return fw


def _orig_cols(gp, lo, hi):
    split, zend = P_QB * GROUP_WIDTH, P_QB * GROUP_WIDTH + 2 * GLA_RANK
    parts = []
    for a, b, at in ((0, split, 0), (split, zend, P_Z), (zend, D_IN, split)):
        s, e = max(lo, a), min(hi, b)
        if s < e:
            parts.append(gp[..., at + s - a:at + e - a])
    return parts[0] if len(parts) == 1 else jnp.concatenate(parts, axis=-1)


def _big_cuts(grads):
    depth = len(grads)
    halves = (range(0, depth // 2), range(depth // 2, depth))
    n_in, n_ff = D_IN // 4, grads[0]['ffn_wg'].shape[1] // 2

    def build(piece):
        return jnp.stack([jnp.stack([jnp.concatenate([piece(l, j) for l in hl], axis=0) for j in range(4)]) for hl in halves])

    def rows_of(key):
        return lambda l, j: jnp.split(grads[l][key], 4, axis=0)[j]

    return {
        'w_in': build(lambda l, j: _orig_cols(grads[l]['w_in'], j * n_in, (j + 1) * n_in)),
        'w_out': build(rows_of('w_out')),
        'ffn_w_in': build(lambda l, j: grads[l]['ffn_wg' if j < 2 else 'ffn_wu'][:, (j % 2) * n_ff:(j % 2 + 1) * n_ff]),
        'ffn_w_out': build(rows_of('ffn_w_out')),
    }


def _stored_grads(grads):
    depth = len(grads)
    g = {k: jnp.stack([gl[k] for gl in grads]) for k in grads[0]}
    return {
        'mix_norm_pre': g['mix_norm_pre'][:, 0], 'mix_norm_post': g['mix_norm_post'][:, 0],
        'ffn_norm_pre': g['ffn_norm_pre'][:, 0], 'ffn_norm_post': g['ffn_norm_post'][:, 0],
        'w_in': _unperm_in(g['w_in']), 'w_out': g['w_out'],
        'ffn_w_in': jnp.concatenate([g['ffn_wg'], g['ffn_wu']], axis=-1), 'ffn_w_out': g['ffn_w_out'],
        'gla_w_gate': _gate_matrix_grad(g['gla_wg']), 'gla_b_gate': g['gla_bg'].reshape(depth, 2, GROUP_WIDTH),
        'gla_norm': g['gla_norm'][:, 0], 'na_rpb': g['na_rpb'],
        'lru_conv_w': jnp.stack([g[f'lru_cw{j}'][:, 0] for j in range(LRU_CONV)], axis=1), 'lru_conv_b': g['lru_cb'][:, 0],
        'lru_w_a': _block_diag_grad(jnp.stack([g['lru_wa0'], g['lru_wa1']], axis=1)),
        'lru_w_x': _block_diag_grad(jnp.stack([g['lru_wx0'], g['lru_wx1']], axis=1)),
        'lru_b_a': jnp.stack([g['lru_ba0'][:, 0], g['lru_ba1'][:, 0]], axis=1),
        'lru_b_x': jnp.stack([g['lru_bx0'][:, 0], g['lru_bx1'][:, 0]], axis=1),
        'lru_lambda': jnp.stack([g['lru_lam0'][:, 0], g['lru_lam1'][:, 0]], axis=1),
    }
```

```python
import numpy as np
import jax
import jax.numpy as jnp
from jax import lax
from jax.experimental import pallas as pl
from jax.experimental.pallas import tpu as pltpu

F32, BF16 = jnp.float32, jnp.bfloat16
HIGHEST = lax.Precision.HIGHEST
MESH = pl.DeviceIdType.MESH

HEAD_DIM = 64
GROUP_HEADS = 4
GROUP_WIDTH = GROUP_HEADS * HEAD_DIM
GLA_RANK = 16
GLA_TAU = 16.0
GLA_CHUNK = 64
GRID_W = 64
NA_ROWS = 8
NA_COLS = 16
LRU_CONV = 4
LRU_CONV_LEFT = 2
LRU_C = 8.0
DIL_PAIRS = ((128, 1), (512, 4), (2048, 16))
ROPE_THETA = 10000.0
EPS = 1e-6
ADAM_LR, ADAM_B1, ADAM_B2, ADAM_EPS, ADAM_WD, ADAM_STEP = 0.001, 0.9, 0.999, 1e-08, 0.01, 10
NEG = -1e30

LANE = 128
VMEM_LIMIT = 56 * 1024 * 1024
ROW_BUDGET = 16 * 1024 * 1024
DMA_PIECES = 8

P_QA, P_KA, P_VA, P_GA, P_QB, P_KB, P_VB, P_XC, P_GC, P_QD, P_KD, P_VD = range(12)
P_Z = 12 * GROUP_WIDTH
D_IN = 12 * GROUP_WIDTH + 2 * GLA_RANK
D_INP = 12 * GROUP_WIDTH + LANE

WEIGHTS = ['mix_norm_pre', 'mix_norm_post', 'w_in', 'gla_w_gate', 'gla_b_gate', 'gla_norm', 'na_rpb',
           'lru_conv_w', 'lru_conv_b', 'lru_w_a', 'lru_b_a', 'lru_w_x', 'lru_b_x', 'lru_lambda', 'w_out',
           'ffn_norm_pre', 'ffn_norm_post', 'ffn_w_in', 'ffn_w_out']
BIG = ('w_in', 'w_out', 'ffn_w_in', 'ffn_w_out')
SMALL_SHARDED = {'gla_w_gate': 3, 'gla_b_gate': 2, 'lru_conv_w': 2, 'lru_b_a': 2, 'lru_b_x': 2, 'lru_lambda': 2}
HEADS = [slice(h * HEAD_DIM, (h + 1) * HEAD_DIM) for h in range(GROUP_HEADS)]


def _cparams(sem=None):
    return pltpu.CompilerParams(dimension_semantics=sem, vmem_limit_bytes=VMEM_LIMIT)


def _tile(dim, target, mult=LANE):
    best = None
    for t in range(mult, min(dim, target) + 1, mult):
        if dim % t == 0:
            best = t
    return best or dim


class Rows:
    def __init__(self, a, w=None, cb=0, lead=None, shifts=None):
        self.a, self.cb, self.lead, self.shifts = a, cb, lead, shifts
        self.w = a.shape[-1] if w is None else w
        self.nrows = a.shape[-2]

    def spec(self, tm, ncol=1, halo=0):
        w = self.w // ncol
        per, last = tm // HALO, self.nrows // HALO - 1
        rows, row = (tm, lambda i: i) if halo == 0 else (HALO, lambda i: jnp.clip(i * per + (per if halo > 0 else -1), 0, last))
        if self.lead is None:
            return pl.BlockSpec((rows, w), lambda i, j, cb=self.cb: (row(i), cb * ncol + j))
        return pl.BlockSpec((None, rows, w), lambda i, j, cb=self.cb, k=self.lead: (k, row(i), cb * ncol + j))

    def nbytes(self):
        return self.w * self.a.dtype.itemsize


def _as_rows(rs):
    return [r if isinstance(r, Rows) else Rows(r) for r in rs]


HALO = 8


def _shift_tile(before, cur, after, k, t, nt):
    if k == 0:
        return cur
    tm = cur.shape[0]
    row = lax.broadcasted_iota(jnp.int32, before.shape, 0)
    if k > 0:
        moved = pltpu.roll(cur, k, 0)
        edge = jnp.where(row < k, jnp.where(t > 0, pltpu.roll(before, k, 0), 0.0), moved[0:HALO])
        return jnp.concatenate([edge, moved[HALO:]], axis=0)
    moved = pltpu.roll(cur, tm + k, 0)
    edge = jnp.where(row >= HALO + k, jnp.where(t < nt - 1, pltpu.roll(after, HALO + k, 0), 0.0), moved[tm - HALO:])
    return jnp.concatenate([moved[:tm - HALO], edge], axis=0)


def _operands(rows, tm, ncol):
    specs, arrs = [], []
    for r in rows:
        if r.shifts is None:
            specs.append(r.spec(tm, ncol))
            arrs.append(r.a)
        else:
            assert ncol == 1
            specs += [r.spec(tm, 1, side) for side in (-1, 0, 1)]
            arrs += [r.a] * 3

    def load(refs):
        vals, k = [], 0
        t, nt = pl.program_id(0), rows[0].nrows // tm
        for r in rows:
            if r.shifts is None:
                vals.append(refs[k][...])
                k += 1
            else:
                prev, cur, nxt = refs[k][...], refs[k + 1][...], refs[k + 2][...]
                vals += [_shift_tile(prev, cur, nxt, s, t, nt) for s in r.shifts]
                k += 3
        return vals

    return specs, arrs, load


def _expand(rows):
    return [r for r in rows for _ in (r.shifts or [0])]


def _pick_tm(nrows, row_bytes, scale, cap=512):
    tm = cap
    while tm > 16 and (tm * row_bytes * scale > ROW_BUDGET or nrows % tm):
        tm //= 2
    assert nrows % tm == 0, (nrows, tm)
    return tm


def _full_spec(a):
    nd = a.ndim
    return pl.BlockSpec(a.shape, lambda i, j, nd=nd: (0,) * nd)


def _rowwise(name, fn, rows, params, outs, acc_outs=(), ncol=1, tm_cap=512):
    rows = _as_rows(rows)
    nrows = rows[0].nrows
    assert ncol == 1 or not (acc_outs or params)
    tm = _pick_tm(nrows, (sum(r.nbytes() for r in rows) + sum(w * jnp.dtype(d).itemsize for w, d in outs)) // ncol, 2, tm_cap)
    specs, arrs, load = _operands(rows, tm, ncol)
    n_r, n_p, n_o = len(specs), len(params), len(outs)

    def body(*refs):
        vals = load(refs[:n_r]) + [r[...] for r in refs[n_r:n_r + n_p]]
        res = fn(*vals)
        res = res if isinstance(res, (tuple, list)) else (res,)
        orefs = refs[n_r + n_p:]
        for o, v in zip(orefs[:n_o], res[:n_o]):
            o[...] = v.astype(o.dtype)
        for o, v in zip(orefs[n_o:], res[n_o:]):
            @pl.when(pl.program_id(0) == 0)
            def _(o=o):
                o[...] = jnp.zeros_like(o)
            o[...] += v

    out_shape = [jax.ShapeDtypeStruct((nrows, w), d) for w, d in outs] + [jax.ShapeDtypeStruct(s, F32) for s in acc_outs]
    out_specs = [pl.BlockSpec((tm, w // ncol), lambda i, j: (i, j)) for w, _ in outs] + \
                [pl.BlockSpec(s, lambda i, j, nd=len(s): (0,) * nd) for s in acc_outs]
    return pl.pallas_call(
        body, name=name, grid=(nrows // tm, ncol),
        in_specs=specs + [_full_spec(p) for p in params],
        out_specs=out_specs, out_shape=out_shape,
        compiler_params=_cparams(("arbitrary", "arbitrary") if acc_outs else ("parallel", "parallel")),
    )(*arrs, *params)


def _rowwise_bwd(name, fn, rows, params, ct_rows, ct_fn, row_grads, param_grads, row_grad_add=None, ncol=1):
    rows, ct_rows = _as_rows(rows), _as_rows(ct_rows)
    nrows = rows[0].nrows
    n_rg = sum(d is not None for d in row_grads)
    adds = _as_rows([a for a in (row_grad_add or []) if a is not None])
    add_at = [k for k, a in enumerate(row_grad_add or []) if a is not None]
    assert ncol == 1 or not (params or adds)
    seen = _expand(rows)
    gbytes = sum(r.w * jnp.dtype(d).itemsize for r, d in zip(seen, row_grads) if d is not None)
    tm = _pick_tm(nrows, (sum(r.nbytes() for r in rows + ct_rows + adds) + gbytes) // ncol, 4)
    r_specs, r_arrs, r_load = _operands(rows, tm, ncol)
    c_specs, c_arrs, c_load = _operands(ct_rows, tm, ncol)
    a_specs, a_arrs, a_load = _operands(adds, tm, ncol)
    n_r, n_p, n_c, n_a = len(r_specs), len(params), len(c_specs), len(a_specs)
    diff = [k for k, d in enumerate(row_grads) if d is not None] + [len(seen) + k for k, g in enumerate(param_grads) if g]

    def body(*refs):
        vals = r_load(refs[:n_r]) + [r[...] for r in refs[n_r:n_r + n_p]]
        cts_in = c_load(refs[n_r + n_p:n_r + n_p + n_c])
        add_in = a_load(refs[n_r + n_p + n_c:n_r + n_p + n_c + n_a]) if n_a else []
        orefs = refs[n_r + n_p + n_c + n_a:]

        def f(*dv):
            full = list(vals)
            for k, v in zip(diff, dv):
                full[k] = v
            res = fn(*full)
            return tuple(res) if isinstance(res, (tuple, list)) else (res,)

        outs, vjp = jax.vjp(f, *[vals[k].astype(F32) for k in diff])
        cts = ct_fn(*cts_in)
        cts = cts if isinstance(cts, (tuple, list)) else (cts,)
        grads = list(vjp(tuple(c.astype(o.dtype) for c, o in zip(cts, outs))))
        for k, a in zip(add_at, add_in):
            grads[k] = grads[k] + a.astype(F32)
        for o, g in zip(orefs[:n_rg], grads[:n_rg]):
            o[...] = g.astype(o.dtype)
        for o, g in zip(orefs[n_rg:], grads[n_rg:]):
            @pl.when(pl.program_id(0) == 0)
            def _(o=o):
                o[...] = jnp.zeros_like(o)
            o[...] += g.astype(F32)

    out_shape = [jax.ShapeDtypeStruct((nrows, r.w), d) for r, d in zip(seen, row_grads) if d is not None] + \
                [jax.ShapeDtypeStruct(p.shape, F32) for p, g in zip(params, param_grads) if g]
    out_specs = [pl.BlockSpec((tm, r.w // ncol), lambda i, j: (i, j)) for r, d in zip(seen, row_grads) if d is not None] + \
                [_full_spec(p) for p, g in zip(params, param_grads) if g]
    return pl.pallas_call(
        body, name=name, grid=(nrows // tm, ncol),
        in_specs=r_specs + [_full_spec(p) for p in params] + c_specs + a_specs,
        out_specs=out_specs, out_shape=out_shape,
        compiler_params=_cparams(("arbitrary", "arbitrary")),
    )(*r_arrs, *params, *c_arrs, *a_arrs)


def _assemble(name, groups, dtype):
    sizes = [len(g) for g in groups]
    flat = [a for g in groups for a in g]

    def fn(*tiles):
        out, k = [], 0
        for s in sizes:
            acc = tiles[k].astype(F32)
            for t in tiles[k + 1:k + s]:
                acc = acc + t.astype(F32)
            out.append(acc.astype(dtype))
            k += s
        return out[0] if len(out) == 1 else jnp.concatenate(out, axis=1)

    width = sum(g[0].shape[-1] if not isinstance(g[0], Rows) else g[0].w for g in groups)
    return _rowwise(name, fn, flat, [], [(width, dtype)])[0]


def _matmul(name, a, b, mode, out_dtype, acc_in=None, also=None):
    if mode == 'nn':
        (M, K), N = a.shape, b.shape[1]
    elif mode == 'nt':
        (M, K), N = a.shape, b.shape[0]
    else:
        (K, M), N = a.shape, b.shape[1]
    tm, tn, tk = _tile(M, 1536), _tile(N, 1536), _tile(K, 2048 if mode == 'tn' else 3328)
    nk = K // tk
    dn = {'nn': NN, 'nt': NT, 'tn': TN}[mode]
    has_acc = acc_in is not None

    n_in = 3 if has_acc else 2
    dtypes = [out_dtype] + ([also] if also is not None else [])

    def body(*refs):
        a_ref, b_ref = refs[:2]
        o_refs = refs[n_in:n_in + len(dtypes)]
        part = lax.dot_general(a_ref[...].astype(BF16), b_ref[...].astype(BF16), dn, preferred_element_type=F32)
        if nk == 1:
            val = (refs[2][...] + part) if has_acc else part
            for o_ref in o_refs:
                o_ref[...] = val.astype(o_ref.dtype)
            return
        acc = refs[-1]

        @pl.when(pl.program_id(2) == 0)
        def _():
            acc[...] = refs[2][...] if has_acc else jnp.zeros_like(acc)
        acc[...] += part

        @pl.when(pl.program_id(2) == nk - 1)
        def _():
            for o_ref in o_refs:
                o_ref[...] = acc[...].astype(o_ref.dtype)

    a_spec = pl.BlockSpec((tk, tm), lambda i, j, k: (k, i)) if mode == 'tn' else pl.BlockSpec((tm, tk), lambda i, j, k: (i, k))
    b_spec = pl.BlockSpec((tn, tk), lambda i, j, k: (j, k)) if mode == 'nt' else pl.BlockSpec((tk, tn), lambda i, j, k: (k, j))
    o_spec = pl.BlockSpec((tm, tn), lambda i, j, k: (i, j))
    res = pl.pallas_call(
        body, name=name, grid=(M // tm, N // tn, nk),
        in_specs=[a_spec, b_spec] + ([o_spec] if has_acc else []), out_specs=[o_spec] * len(dtypes),
        out_shape=[jax.ShapeDtypeStruct((M, N), d) for d in dtypes],
        scratch_shapes=[] if nk == 1 else [pltpu.VMEM((tm, tn), F32)],
        compiler_params=_cparams(("parallel", "parallel", "arbitrary")),
    )(a, b, *([acc_in] if has_acc else []))
    return res[0] if also is None else res


def _ffn_up(name, x, gain, wg, wu):
    (M, K), N = x.shape, wg.shape[1]
    tm, tn = _tile(M, 512), _tile(N, 1536)

    def body(x_ref, n_ref, g_ref, u_ref, h_ref, gate_ref, up_ref, act_ref, hs):
        @pl.when(pl.program_id(1) == 0)
        def _():
            hs[...] = _rms(x_ref[...], n_ref[...]).astype(BF16)
            h_ref[...] = hs[...]
        a = hs[...]
        gate = _bdot(a, g_ref[...]).astype(BF16)
        up = _bdot(a, u_ref[...]).astype(BF16)
        gate_ref[...], up_ref[...] = gate, up
        act_ref[...] = _swiglu_fn(gate, up).astype(BF16)

    x_spec = pl.BlockSpec((tm, K), lambda i, j: (i, 0))
    w_spec = pl.BlockSpec((K, tn), lambda i, j: (0, j))
    o_spec = pl.BlockSpec((tm, tn), lambda i, j: (i, j))
    return pl.pallas_call(
        body, name=name, grid=(M // tm, N // tn),
        in_specs=[x_spec, pl.BlockSpec(gain.shape, lambda i, j: (0, 0)), w_spec, w_spec], out_specs=[x_spec] + [o_spec] * 3,
        out_shape=[jax.ShapeDtypeStruct((M, K), BF16)] + [jax.ShapeDtypeStruct((M, N), BF16)] * 3,
        scratch_shapes=[pltpu.VMEM((tm, K), BF16)],
        compiler_params=_cparams(("parallel", "arbitrary")),
    )(x, gain, wg, wu)


def _ffn_down_bwd(name, df, w_out, gate, up):
    (M, K), N = df.shape, w_out.shape[0]
    tm, tn = _tile(M, 512), _tile(N, 1536)

    def body(d_ref, w_ref, gate_ref, up_ref, dg_ref, du_ref):
        dact = _bdot(d_ref[...], w_ref[...], NT)
        dg, du = _swiglu_bwd((gate_ref[...], up_ref[...]), dact)
        dg_ref[...], du_ref[...] = dg, du

    o_spec = pl.BlockSpec((tm, tn), lambda i, j: (i, j))
    return pl.pallas_call(
        body, name=name, grid=(M // tm, N // tn),
        in_specs=[pl.BlockSpec((tm, K), lambda i, j: (i, 0)), pl.BlockSpec((tn, K), lambda i, j: (j, 0)), o_spec, o_spec],
        out_specs=[o_spec] * 2, out_shape=[jax.ShapeDtypeStruct((M, N), BF16)] * 2,
        compiler_params=_cparams(("parallel", "parallel")),
    )(df, w_out, gate, up)


def _small_dot(name, a, b, mode):
    dn = {'nn': (((1,), (0,)), ((), ())), 'nt': (((1,), (1,)), ((), ()))}[mode]
    M = a.shape[0]
    N = b.shape[1] if mode == 'nn' else b.shape[0]

    def body(a_ref, b_ref, o_ref):
        o_ref[...] = lax.dot_general(a_ref[...], b_ref[...], dn, precision=HIGHEST, preferred_element_type=F32)

    return pl.pallas_call(body, name=name, out_shape=jax.ShapeDtypeStruct((M, N), F32),
                          compiler_params=pltpu.CompilerParams(vmem_limit_bytes=VMEM_LIMIT))(a, b)


NN, NT, TN = (((1,), (0,)), ((), ())), (((1,), (1,)), ((), ())), (((0,), (0,)), ((), ()))


def _bdot(a, b, dn=NN):
    return lax.dot_general(a.astype(BF16), b.astype(BF16), dn, preferred_element_type=F32)


def _sigmoid(x):
    return 0.5 * jnp.tanh(0.5 * x) + 0.5


def _silu(x):
    return x * _sigmoid(x)


def _softplus(x):
    return jnp.maximum(x, 0.0) + jnp.log(1.0 + jnp.exp(-jnp.abs(x)))


def _gelu(x):
    return 0.5 * x * (1.0 + jnp.tanh(0.7978845608028654 * (x + 0.044715 * (x * x * x))))


def _rms(x, g):
    return x * lax.rsqrt(jnp.mean(x * x, axis=-1, keepdims=True) + EPS) * g


def _prenorm_fn(x, g):
    return _rms(x, g)


def _postnorm_fn(x, y, g):
    return x + _rms(y, g)


@jax.custom_vjp
def _swiglu_fn(gate, up):
    return _silu(gate.astype(F32)) * up.astype(F32)


def _swiglu_bwd(res, ct):
    g, u = res[0].astype(F32), res[1].astype(F32)
    s = _sigmoid(g)
    gs = g * s
    return (ct * u * (s + gs - gs * s)).astype(res[0].dtype), (ct * gs).astype(res[1].dtype)


_swiglu_fn.defvjp(lambda gate, up: (_swiglu_fn(gate, up), (gate, up)), _swiglu_bwd)


def _gla_pre_fn(z, wg, bg):
    logit = _bdot(z, wg) + bg
    return -_softplus(-logit) * (1.0 / GLA_TAU)


def _seg_mean(x, seg):
    return lax.dot_general(x, seg, NN, precision=HIGHEST, preferred_element_type=F32)


def _gla_post_fn(of, ob, g, norm, seg):
    o = of + ob
    o = o * lax.rsqrt(_seg_mean(o * o, seg) + EPS) * norm
    return o * _silu(g)


@jax.custom_vjp
def _swap_halves(x):
    n = x.shape[-1]
    lane = lax.broadcasted_iota(jnp.int32, x.shape, x.ndim - 1)
    lo = (lane & (HEAD_DIM - 1)) < HEAD_DIM // 2
    return jnp.where(lo, pltpu.roll(x, n - HEAD_DIM // 2, x.ndim - 1), pltpu.roll(x, HEAD_DIM // 2, x.ndim - 1))


_swap_halves.defvjp(lambda x: (_swap_halves(x), None), lambda _, g: (_swap_halves(g),))


def _rope_fn(q, k, cs, sn):
    return q * cs + _swap_halves(q) * sn, k * cs + _swap_halves(k) * sn


def _dil_comb_fn(o1, o2, o3, l1, l2, l3):
    m = jnp.maximum(jnp.maximum(l1, l2), l3)
    e1, e2, e3 = jnp.exp(l1 - m), jnp.exp(l2 - m), jnp.exp(l3 - m)
    return (e1 * o1 + e2 * o2 + e3 * o3) / (e1 + e2 + e3)


LRU_PARAMS = ['lru_cw0', 'lru_cw1', 'lru_cw2', 'lru_cw3', 'lru_cb', 'lru_wa0', 'lru_wa1', 'lru_ba0', 'lru_ba1',
              'lru_wx0', 'lru_wx1', 'lru_bx0', 'lru_bx1', 'lru_lam0', 'lru_lam1']


def _lru_pre_fn(x0, x1, x2, x3, cw0, cw1, cw2, cw3, cb, wa0, wa1, ba0, ba1, wx0, wx1, bx0, bx1, lam0, lam1):
    xc = cb + x0 * cw0 + x1 * cw1 + x2 * cw2 + x3 * cw3
    outs = []
    for wa, ba, wx, bx, lam in ((wa0, ba0, wx0, bx0, lam0), (wa1, ba1, wx1, bx1, lam1)):
        r = _sigmoid(_bdot(xc, wa) + ba)
        i = _sigmoid(_bdot(xc, wx) + bx)
        log_a = -LRU_C * r * _softplus(-lam)
        a = jnp.exp(log_a)
        u = jnp.sqrt(-jnp.tanh(log_a) * (a * a + 1.0)) * (i * xc)
        outs += [a, u]
    return outs[0], outs[2], outs[1], outs[3]


def _lru_post_fn(hf, hb, gate):
    return (hf + hb) * _gelu(gate)


def _mix_post_fn(of, ob, ga, yb, hf, hb, gc, o1, o2, o3, l1, l2, l3, norm, seg):
    ya = _gla_post_fn(of, ob, ga, norm, seg)
    yc = _lru_post_fn(hf, hb, gc)
    yd = _dil_comb_fn(o1, o2, o3, l1, l2, l3)
    return jnp.concatenate([ya.astype(BF16), yb.astype(BF16), yc.astype(BF16), yd.astype(BF16)], axis=1)


def _attn_heads(qs, kws, vws, biases):
    ss = [_bdot(q, kw, NT) * (HEAD_DIM ** -0.5) + b for q, kw, b in zip(qs, kws, biases)]
    ms = [lax.stop_gradient(jnp.max(s, axis=-1, keepdims=True)) for s in ss]
    es = [jnp.exp(s - m) for s, m in zip(ss, ms)]
    dens = [jnp.sum(e, axis=-1, keepdims=True) for e in es]
    ps = [e * (1.0 / d) for e, d in zip(es, dens)]
    os_ = [_bdot(p_, vw) for p_, vw in zip(ps, vws)]
    return os_, [m + jnp.log(d) for m, d in zip(ms, dens)]


def _cumsum_rows(x, rev):
    n = x.shape[0]
    row = lax.broadcasted_iota(jnp.int32, x.shape, 0)
    s = 1
    while s < n:
        if rev:
            x = x + jnp.where(row < n - s, pltpu.roll(x, n - s, 0), 0.0)
        else:
            x = x + jnp.where(row >= s, pltpu.roll(x, s, 0), 0.0)
        s *= 2
    return x


def _gla_chunks(qs, ks, vs, bs, sts, revs):
    C = qs[0].shape[0]
    ti = lax.broadcasted_iota(jnp.int32, (C, C), 0)
    si = lax.broadcasted_iota(jnp.int32, (C, C), 1)
    row = lax.broadcasted_iota(jnp.int32, (C, 1), 0)
    incl = {False: si <= ti, True: si >= ti}
    last = {False: row == C - 1, True: row == 0}
    mid = {False: row == C // 2 - 1, True: row == C // 2}
    bls = [jnp.sum(jnp.where(last[r], b, 0.0), axis=0, keepdims=True) for b, r in zip(bs, revs)]
    bms = [jnp.sum(jnp.where(mid[r], b, 0.0), axis=0, keepdims=True) for b, r in zip(bs, revs)]
    qss = [q * (HEAD_DIM ** -0.5) for q in qs]
    qi = [q * jnp.exp(b - bm) for q, b, bm in zip(qss, bs, bms)]
    ki = [k * jnp.exp(bm - b) for k, b, bm in zip(ks, bs, bms)]
    atts = [jnp.where(incl[r], _bdot(a, b, NT), 0.0) for a, b, r in zip(qi, ki, revs)]
    qe = [q * jnp.exp(b) for q, b in zip(qss, bs)]
    kl = [k * jnp.exp(bl - b) for k, b, bl in zip(ks, bs, bls)]
    o1 = [_bdot(a, v) for a, v in zip(atts, vs)]
    o2 = [_bdot(q, st, NT) for q, st in zip(qe, sts)]
    kvs = [_bdot(v, k, TN) for v, k in zip(vs, kl)]
    return [a + b for a, b in zip(o1, o2)], [st * jnp.exp(bl) + kv for st, bl, kv in zip(sts, bls, kvs)]


def _gla_specs(n, blocks, first):
    C = GLA_CHUNK
    at = (lambda i: i) if first else (lambda i: n - 1 - i)
    return [pl.BlockSpec((C, GROUP_WIDTH), lambda i, b=b: (at(i), b)) for b in blocks], at


def _gla_scan_fwd(p, la):
    L = p.shape[0]
    C, H, dh = GLA_CHUNK, GROUP_HEADS, HEAD_DIM
    n = L // C
    f_specs, f_at = _gla_specs(n, (P_QA, P_KA, P_VA), True)
    b_specs, b_at = _gla_specs(n, (P_QA, P_KA, P_VA), False)
    tile = lambda at, blk=0: pl.BlockSpec((C, GROUP_WIDTH), lambda i: (at(i), blk))
    st_spec = lambda at: pl.BlockSpec((None, H, dh, dh), lambda i: (at(i), 0, 0, 0))

    def body(qf, kf, vf, lf, qb, kb, vb, lb, of_ref, ob_ref, sf_ref, sb_ref, stf, stb):
        @pl.when(pl.program_id(0) == 0)
        def _():
            stf[...] = jnp.zeros_like(stf)
            stb[...] = jnp.zeros_like(stb)
        sf_ref[...] = stf[...]
        sb_ref[...] = stb[...]
        chains = [(t, h, sl) for t in ((qf, kf, vf, _cumsum_rows(lf[...], False), of_ref, stf, False),
                                       (qb, kb, vb, _cumsum_rows(lb[...], True), ob_ref, stb, True))
                  for h, sl in enumerate(HEADS)]
        os_, sts = _gla_chunks(*[[t[j][:, sl] for t, h, sl in chains] for j in range(4)],
                               [t[5][h] for t, h, sl in chains], [t[6] for t, h, sl in chains])
        for (t, h, sl), o, st_new in zip(chains, os_, sts):
            t[4][:, sl] = o
            t[5][h] = st_new

    return pl.pallas_call(
        body, name="gla_scan", grid=(n,),
        in_specs=f_specs + [tile(f_at, 0)] + b_specs + [tile(b_at, 1)],
        out_specs=[tile(f_at), tile(b_at), st_spec(f_at), st_spec(b_at)],
        out_shape=[jax.ShapeDtypeStruct((L, GROUP_WIDTH), F32)] * 2 + [jax.ShapeDtypeStruct((n, H, dh, dh), F32)] * 2,
        scratch_shapes=[pltpu.VMEM((H, dh, dh), F32)] * 2,
        compiler_params=_cparams(("arbitrary",)),
    )(p, p, p, la, p, p, p, la)


def _gla_scan_bwd(p, la, sf, sb, do):
    L = p.shape[0]
    C, H, dh = GLA_CHUNK, GROUP_HEADS, HEAD_DIM
    n = L // C
    f_specs, f_at = _gla_specs(n, (P_QA, P_KA, P_VA), False)
    b_specs, b_at = _gla_specs(n, (P_QA, P_KA, P_VA), True)
    tile = lambda at, blk=0: pl.BlockSpec((C, GROUP_WIDTH), lambda i: (at(i), blk))
    st_spec = lambda at: pl.BlockSpec((None, H, dh, dh), lambda i: (at(i), 0, 0, 0))

    def body(qf, kf, vf, lf, spf, dof, qb, kb, vb, lb, spb, dob, *rest):
        outs_f, outs_b, (dstf, dstb) = rest[0:4], rest[4:8], rest[8:10]

        @pl.when(pl.program_id(0) == 0)
        def _():
            dstf[...] = jnp.zeros_like(dstf)
            dstb[...] = jnp.zeros_like(dstb)
        chains = [(t, h, sl) for t in ((qf, kf, vf, _cumsum_rows(lf[...], False), spf, dof, outs_f, dstf, False),
                                       (qb, kb, vb, _cumsum_rows(lb[...], True), spb, dob, outs_b, dstb, True))
                  for h, sl in enumerate(HEADS)]
        nc = len(chains)
        revs = [t[8] for t, h, sl in chains]
        flat = [t[j][:, sl] for j in range(4) for t, h, sl in chains] + [t[4][h] for t, h, sl in chains]

        def f(*a):
            os_, sts = _gla_chunks(*[list(a[j * nc:(j + 1) * nc]) for j in range(5)], revs)
            return tuple(os_) + tuple(sts)

        _, vjp = jax.vjp(f, *flat)
        grads = vjp(tuple(t[5][:, sl] for t, h, sl in chains) + tuple(t[7][h] for t, h, sl in chains))
        for c_, (t, h, sl) in enumerate(chains):
            for j in range(4):
                t[6][j][:, sl] = grads[j * nc + c_].astype(t[6][j].dtype)
            t[7][h] = grads[4 * nc + c_]
        for outs, rev in ((outs_f, False), (outs_b, True)):
            outs[3][...] = _cumsum_rows(outs[3][...], not rev)

    return pl.pallas_call(
        body, name="gla_scan_b", grid=(n,),
        in_specs=f_specs + [tile(f_at, 0), st_spec(f_at), tile(f_at)] + b_specs + [tile(b_at, 1), st_spec(b_at), tile(b_at)],
        out_specs=[tile(f_at)] * 4 + [tile(b_at)] * 4,
        out_shape=[jax.ShapeDtypeStruct((L, GROUP_WIDTH), d) for d in (BF16, BF16, BF16, F32)] * 2,
        scratch_shapes=[pltpu.VMEM((H, dh, dh), F32)] * 2,
        compiler_params=_cparams(("arbitrary",)),
    )(p, p, p, la, sf, do, p, p, p, la, sb, do)


NA_W = NA_ROWS * GRID_W
NA_BW = (2 * NA_ROWS - 1) * GRID_W


NA_PER = 4


def _na_start(i, rows):
    return jnp.clip(i - NA_ROWS // 2, 0, rows - NA_ROWS)


def _na_chains(step, rows):
    first = _na_start(step * NA_PER, rows)
    out = []
    for rho in range(NA_PER):
        r = step * NA_PER + rho
        s = _na_start(r, rows)
        ks = pl.ds(pl.multiple_of(s * GRID_W, GRID_W), NA_W)
        off = pl.multiple_of((s - first) * GRID_W, GRID_W)
        out += [(pl.ds(rho * GRID_W, GRID_W), sl, h, ks, s - r + NA_ROWS - 1, off) for h, sl in enumerate(HEADS)]
    return out


def _na_fwd(p16, btab):
    L = p16.shape[0]
    rows = L // GRID_W
    assert rows % NA_PER == 0
    kv = lambda blk: pl.BlockSpec((L, GROUP_WIDTH), lambda i: (0, blk))

    def body(q_ref, k_ref, v_ref, b_ref, o_ref):
        chains = _na_chains(pl.program_id(0), rows)
        os_, _ = _attn_heads([q_ref[qs, sl] for qs, sl, h, ks, sv, off in chains], [k_ref[ks, sl] for qs, sl, h, ks, sv, off in chains],
                             [v_ref[ks, sl] for qs, sl, h, ks, sv, off in chains], [b_ref[sv, h] for qs, sl, h, ks, sv, off in chains])
        for (qs, sl, h, ks, sv, off), o in zip(chains, os_):
            o_ref[qs, sl] = o.astype(o_ref.dtype)

    whole = lambda a: pl.BlockSpec(a.shape, lambda i, nd=a.ndim: (0,) * nd)
    return pl.pallas_call(
        body, name="na_attn", grid=(rows // NA_PER,),
        in_specs=[pl.BlockSpec((NA_PER * GRID_W, GROUP_WIDTH), lambda i: (i, P_QB)), kv(P_KB), kv(P_VB), whole(btab)],
        out_specs=pl.BlockSpec((NA_PER * GRID_W, GROUP_WIDTH), lambda i: (i, 0)),
        out_shape=jax.ShapeDtypeStruct((L, GROUP_WIDTH), BF16),
        compiler_params=_cparams(("arbitrary",)),
    )(p16, p16, p16, btab)


def _na_bwd(p16, btab, dycat):
    L = p16.shape[0]
    rows = L // GRID_W
    assert rows % NA_PER == 0 and NA_ROWS % NA_PER == 0
    kv = lambda blk: pl.BlockSpec((L, GROUP_WIDTH), lambda i: (0, blk))
    steps, blk = rows // NA_PER, NA_PER * GRID_W
    span = NA_W + blk - GRID_W
    span += (-span) % blk
    flush = NA_W // blk - 1
    half = NA_ROWS // 2 // NA_PER
    emit = lambda i: jnp.where(i < steps, jnp.clip(i - half, 0, steps - NA_ROWS // NA_PER), i - flush)

    def body(q_ref, k_ref, v_ref, b_ref, do_ref, dq_ref, dk_ref, dv_ref, db_ref, acc_k, acc_v):
        i = pl.program_id(0)

        @pl.when(i == 0)
        def _():
            acc_k[...] = jnp.zeros_like(acc_k)
            acc_v[...] = jnp.zeros_like(acc_v)
            db_ref[...] = jnp.zeros_like(db_ref)

        @pl.when((i > 0) & (emit(i) != emit(i - 1)))
        def _():
            for acc in (acc_k, acc_v):
                moved = acc[blk:span, :]
                acc[0:span - blk, :] = moved
                acc[span - blk:span, :] = jnp.zeros((blk, GROUP_WIDTH), F32)

        @pl.when(i < steps)
        def _():
            chains = _na_chains(i, rows)
            H = len(chains)
            flat = [q_ref[qs, sl].astype(F32) for qs, sl, h, ks, sv, off in chains] + \
                   [k_ref[ks, sl].astype(F32) for qs, sl, h, ks, sv, off in chains] + \
                   [v_ref[ks, sl].astype(F32) for qs, sl, h, ks, sv, off in chains] + [b_ref[sv, h] for qs, sl, h, ks, sv, off in chains]

            def f(*a):
                os_, lses = _attn_heads(a[0:H], a[H:2 * H], a[2 * H:3 * H], a[3 * H:4 * H])
                return tuple(os_) + tuple(lses)

            _, vjp = jax.vjp(f, *flat)
            grads = vjp(tuple(do_ref[qs, sl] for qs, sl, h, ks, sv, off in chains) + (jnp.zeros((GRID_W, 1), F32),) * H)
            for c_, (qs, sl, h, ks, sv, off) in enumerate(chains):
                dq_ref[qs, sl] = grads[c_].astype(dq_ref.dtype)
                acc_k[pl.ds(off, NA_W), sl] += grads[H + c_]
                acc_v[pl.ds(off, NA_W), sl] += grads[2 * H + c_]
                db_ref[sv, h] += grads[3 * H + c_]

        dk_ref[...] = acc_k[0:blk, :].astype(dk_ref.dtype)
        dv_ref[...] = acc_v[0:blk, :].astype(dv_ref.dtype)

    whole = lambda a: pl.BlockSpec(a.shape, lambda i, nd=a.ndim: (0,) * nd)
    qrow = lambda cb: pl.BlockSpec((blk, GROUP_WIDTH), lambda i: (jnp.minimum(i, steps - 1), cb))
    erow = pl.BlockSpec((blk, GROUP_WIDTH), lambda i: (emit(i), 0))
    return pl.pallas_call(
        body, name="na_attn_b", grid=(steps + flush,),
        in_specs=[qrow(P_QB), kv(P_KB), kv(P_VB), whole(btab), qrow(1)],
        out_specs=[qrow(0), erow, erow, whole(btab)],
        out_shape=[jax.ShapeDtypeStruct((L, GROUP_WIDTH), BF16)] * 3 + [jax.ShapeDtypeStruct(btab.shape, F32)],
        scratch_shapes=[pltpu.VMEM((span, GROUP_WIDTH), F32)] * 2,
        compiler_params=_cparams(("arbitrary",)),
    )(p16, p16, p16, btab, dycat)


def _na_col_ok():
    qc = np.arange(GRID_W)[:, None]
    kc = (np.arange(NA_W) % GRID_W)[None, :]
    c0 = np.clip(qc - NA_COLS // 2, 0, GRID_W - NA_COLS)
    return (kc >= c0) & (kc < c0 + NA_COLS)


def _rpb_tables():
    c = np.arange(GRID_W)
    dc = np.clip(c[None, :] - c[:, None], -(NA_COLS - 1), NA_COLS - 1) + NA_COLS - 1
    t = np.zeros((2 * NA_COLS - 1, GRID_W, GRID_W), np.float32)
    t[dc, c[:, None], c[None, :]] = 1.0
    return jnp.asarray(t.reshape(2 * NA_COLS - 1, GRID_W * GRID_W))


def _rpb_expand(rpb, tab):
    H = rpb.shape[0]
    xt = _small_dot("na_bias", rpb.reshape(H * (2 * NA_ROWS - 1), 2 * NA_COLS - 1), tab, 'nn')
    b15 = xt.reshape(H, 2 * NA_ROWS - 1, GRID_W, GRID_W).transpose(0, 2, 1, 3).reshape(H, GRID_W, NA_BW)
    ok = jnp.asarray(_na_col_ok())
    return jnp.stack([jnp.where(ok, b15[:, :, sv * GRID_W:sv * GRID_W + NA_W], NEG) for sv in range(NA_ROWS)])


def _rpb_contract(dbv, tab):
    H = dbv.shape[1]
    db = sum(jnp.pad(dbv[sv], ((0, 0), (0, 0), (sv * GRID_W, NA_BW - NA_W - sv * GRID_W))) for sv in range(NA_ROWS))
    dx = db.reshape(H, GRID_W, 2 * NA_ROWS - 1, GRID_W).transpose(0, 2, 1, 3).reshape(H * (2 * NA_ROWS - 1), GRID_W * GRID_W)
    return _small_dot("na_bias_b", dx, tab, 'nt').reshape(H, 2 * NA_ROWS - 1, 2 * NA_COLS - 1)


def _band_bias(i, tq, w, halo, n):
    a = lax.broadcasted_iota(jnp.int32, (tq, w), 0)
    b = lax.broadcasted_iota(jnp.int32, (tq, w), 1)
    kpos = i * tq - halo + b
    d = b - halo - a
    return jnp.where((d <= halo) & (d >= -halo) & (kpos >= 0) & (kpos < n), 0.0, NEG)


def _band_fwd(name, q, kp, vp, tq, halo):
    G, n, _ = q.shape
    w = tq + 2 * halo

    def body(q_ref, k_ref, v_ref, o_ref, l_ref):
        i = pl.program_id(1)
        start = pl.multiple_of(i * tq, tq)
        bias = _band_bias(i, tq, w, halo, n)
        os_, lses = _attn_heads([q_ref[:, sl] for sl in HEADS], [k_ref[pl.ds(start, w), sl] for sl in HEADS],
                                [v_ref[pl.ds(start, w), sl] for sl in HEADS], [bias] * GROUP_HEADS)
        for sl, o, lse in zip(HEADS, os_, lses):
            o_ref[:, sl] = o.astype(o_ref.dtype)
            l_ref[:, sl] = jnp.broadcast_to(lse, (tq, HEAD_DIM))

    qblk = pl.BlockSpec((None, tq, GROUP_WIDTH), lambda g, i: (g, i, 0))
    kblk = pl.BlockSpec((None, n + 2 * halo, GROUP_WIDTH), lambda g, i: (g, 0, 0))
    return pl.pallas_call(
        body, name=name, grid=(G, n // tq), in_specs=[qblk, kblk, kblk], out_specs=[qblk, qblk],
        out_shape=[jax.ShapeDtypeStruct((G, n, GROUP_WIDTH), d) for d in (BF16, F32)],
        compiler_params=_cparams(("parallel", "arbitrary")),
    )(q, kp, vp)


def _band_bwd(name, q, kp, vp, do, dl, tq, halo):
    G, n, _ = q.shape
    w = tq + 2 * halo
    nq = n // tq

    def body(q_ref, k_ref, v_ref, do_ref, dl_ref, dq_ref, dk_ref, dv_ref, acc_k, acc_v):
        i = pl.program_id(1)

        @pl.when(i == 0)
        def _():
            acc_k[...] = jnp.zeros_like(acc_k)
            acc_v[...] = jnp.zeros_like(acc_v)

        @pl.when(i > 0)
        def _():
            for acc in (acc_k, acc_v):
                moved = acc[tq:w, :]
                acc[0:2 * halo, :] = moved
                acc[2 * halo:w, :] = jnp.zeros((tq, GROUP_WIDTH), F32)

        @pl.when(i < nq)
        def _():
            start = pl.multiple_of(i * tq, tq)
            bias = _band_bias(i, tq, w, halo, n)
            H = GROUP_HEADS
            flat = [q_ref[:, sl].astype(F32) for sl in HEADS] + [k_ref[pl.ds(start, w), sl].astype(F32) for sl in HEADS] + \
                   [v_ref[pl.ds(start, w), sl].astype(F32) for sl in HEADS]

            def f(*a):
                os_, lses = _attn_heads(a[0:H], a[H:2 * H], a[2 * H:3 * H], [bias] * H)
                return tuple(os_) + tuple(lses)

            _, vjp = jax.vjp(f, *flat)
            grads = vjp(tuple(do_ref[:, sl].astype(F32) for sl in HEADS) +
                        tuple(jnp.sum(dl_ref[:, sl], axis=1, keepdims=True) for sl in HEADS))
            for h, sl in enumerate(HEADS):
                dq_ref[:, sl] = grads[h].astype(dq_ref.dtype)
                acc_k[:, sl] += grads[H + h]
                acc_v[:, sl] += grads[2 * H + h]

        dk_ref[...] = acc_k[0:tq, :].astype(dk_ref.dtype)
        dv_ref[...] = acc_v[0:tq, :].astype(dv_ref.dtype)

    qblk = pl.BlockSpec((None, tq, GROUP_WIDTH), lambda g, i: (g, jnp.minimum(i, nq - 1), 0))
    kblk = pl.BlockSpec((None, n + 2 * halo, GROUP_WIDTH), lambda g, i: (g, 0, 0))
    eblk = pl.BlockSpec((None, tq, GROUP_WIDTH), lambda g, i: (g, i, 0))
    return pl.pallas_call(
        body, name=name, grid=(G, nq + 1), in_specs=[qblk, kblk, kblk, qblk, qblk], out_specs=[qblk, eblk, eblk],
        out_shape=[jax.ShapeDtypeStruct((G, n, GROUP_WIDTH), BF16)] + [jax.ShapeDtypeStruct((G, (nq + 1) * tq, GROUP_WIDTH), BF16)] * 2,
        scratch_shapes=[pltpu.VMEM((w, GROUP_WIDTH), F32)] * 2,
        compiler_params=_cparams(("parallel", "arbitrary")),
    )(q, kp, vp, do, dl)


def _lin_scans(name, jobs):
    L, C = jobs[0][0].shape
    tt = 256 if L % 256 == 0 else L
    nt, per, last = L // tt, tt // HALO, L // HALO - 1
    specs, arrs, plan = [], [], []
    for coef, inp, rev, shift in jobs:
        tile = (lambda i: nt - 1 - i) if rev else (lambda i: i)
        blk = pl.BlockSpec((tt, C), lambda i, tile=tile: (tile(i), 0))
        if shift:
            side = lambda d, tile=tile: pl.BlockSpec(
                (HALO, C), lambda i: (jnp.clip(tile(i) * per + (per if d > 0 else -1), 0, last), 0))
            specs += [side(-1), blk, side(1), blk]
            arrs += [coef, coef, coef, inp]
        else:
            specs += [blk, blk]
            arrs += [coef, inp]
        plan.append((tile, blk, rev, shift))
    n_in = len(specs)

    def body(*refs):
        o_refs, carries = refs[n_in:n_in + len(jobs)], refs[n_in + len(jobs):]

        @pl.when(pl.program_id(0) == 0)
        def _():
            for carry in carries:
                carry[...] = jnp.zeros_like(carry)
        row = lax.broadcasted_iota(jnp.int32, (tt, C), 0)
        au, k = [], 0
        for tile, _, rev, shift in plan:
            if shift:
                a = _shift_tile(refs[k][...], refs[k + 1][...], refs[k + 2][...], shift, tile(pl.program_id(0)), nt)
                k += 3
            else:
                a = refs[k][...]
                k += 1
            au.append([a, refs[k][...]])
            k += 1
        s = 1
        while s < tt:
            for (tile, _, rev, shift), st in zip(plan, au):
                a, u = st
                ok = (row < tt - s) if rev else (row >= s)
                sh = tt - s if rev else s
                st[1] = u + a * jnp.where(ok, pltpu.roll(u, sh, 0), 0.0)
                st[0] = a * jnp.where(ok, pltpu.roll(a, sh, 0), 1.0)
            s *= 2
        for (tile, _, rev, shift), (a, u), o_ref, carry in zip(plan, au, o_refs, carries):
            out = u + a * carry[...]
            o_ref[...] = out
            carry[...] = out[0:1] if rev else out[tt - 1:tt]

    return pl.pallas_call(
        body, name=name, grid=(nt,), in_specs=specs, out_specs=[p_[1] for p_ in plan],
        out_shape=[jax.ShapeDtypeStruct((L, C), F32)] * len(jobs), scratch_shapes=[pltpu.VMEM((1, C), F32)] * len(jobs),
        compiler_params=_cparams(("arbitrary",)),
    )(*arrs)


def _pieces(shape):
    n0 = max(d for d in range(1, DMA_PIECES + 1) if shape[0] % d == 0)
    n1 = 1
    if len(shape) >= 3:
        n1 = max(d for d in range(1, DMA_PIECES // n0 + 1) if shape[1] % d == 0)
    s0, s1 = shape[0] // n0, (shape[1] // n1 if len(shape) >= 3 else 0)
    out = []
    for i in range(n0):
        for j in range(n1):
            out.append((pl.ds(i * s0, s0),) + ((pl.ds(j * s1, s1),) if len(shape) >= 3 else ()))
    return out


def _exchange(name, src, axes, gather):
    flips = {'xy': [(1, 0, 0), (0, 1, 0), (1, 1, 0)], 'c': [(0, 0, 1)],
             'xyc': [(fx, fy, fc) for fx in (0, 1) for fy in (0, 1) for fc in (0, 1)][1:]}[axes]
    n = len(flips) + 1
    blk_shape = tuple(src.shape if gather else src.shape[1:])
    pieces = _pieces(blk_shape)

    def number(px, py, pc):
        return {'xy': 2 * px + py, 'c': pc, 'xyc': 4 * px + 2 * py + pc}[axes]

    def body(src_ref, out_ref, send_sems, recv_sems):
        x, y, c = lax.axis_index("x"), lax.axis_index("y"), lax.axis_index("c")
        me = number(x, y, c)
        piece = (lambda k: src_ref) if gather else (lambda k: src_ref.at[k])
        peers = []
        for s, (fx, fy, fc) in enumerate(flips):
            px, py, pc = (x + fx) % 2, (y + fy) % 2, (c + fc) % 2

            def copy(ix, s=s, px=px, py=py, pc=pc):
                part = (lambda r: r) if ix is None else (lambda r: r.at[ix])
                return pltpu.make_async_remote_copy(
                    src_ref=part(piece(number(px, py, pc))), dst_ref=part(out_ref.at[me]),
                    send_sem=send_sems.at[s], recv_sem=recv_sems.at[s],
                    device_id=(px, py, pc), device_id_type=MESH)

            for ix in pieces:
                copy(ix).start()
            peers.append(copy)
        for copy in peers:
            copy(None).wait()

    out = pl.pallas_call(
        body, name=name, out_shape=jax.ShapeDtypeStruct((n,) + blk_shape, src.dtype),
        in_specs=[pl.BlockSpec(memory_space=pl.ANY)], out_specs=pl.BlockSpec(memory_space=pl.ANY),
        scratch_shapes=[pltpu.SemaphoreType.DMA((n - 1,)), pltpu.SemaphoreType.DMA((n - 1,))],
    )(src)
    me = number(lax.axis_index("x"), lax.axis_index("y"), lax.axis_index("c"))
    own = src if gather else lax.dynamic_index_in_dim(src, me, 0, keepdims=False)
    return lax.dynamic_update_index_in_dim(out, own, me, 0)


def _ordered_sum(name, buf, dtype=F32):
    n = buf.shape[0]

    def fn(*t):
        acc = t[0].astype(F32)
        for v in t[1:]:
            acc = acc + v.astype(F32)
        return acc

    return _rowwise(name, fn, [Rows(buf, lead=k) for k in range(n)], [], [(buf.shape[-1], dtype)])[0]


def _reduce_big(name, g):
    mine = _ordered_sum(name + "_sum_c", _exchange(name + "_swap_c", g, 'c', False).reshape(2, -1, g.shape[-1]), BF16)
    mine = mine.reshape(g.shape[1:])
    tot = _ordered_sum(name + "_sum_xy", _exchange(name + "_a2a_xy", mine, 'xy', False))
    return _exchange(name + "_share_c", tot, 'c', True)


def _dilate(t, dil):
    L, C = t.shape
    return t.reshape(L // dil, dil, C).transpose(1, 0, 2)


def _undilate(t):
    dil, n, C = t.shape
    return t.transpose(1, 0, 2).reshape(dil * n, C)


def _pad_rows(t, halo):
    return jnp.pad(t, ((0, 0), (halo, halo), (0, 0)))


def _pcol(p, blk):
    return Rows(p, GROUP_WIDTH, blk)


def _pslice(p, blk):
    return p[:, blk * GROUP_WIDTH:(blk + 1) * GROUP_WIDTH]


def _conv_taps(p):
    return Rows(p, GROUP_WIDTH, P_XC, shifts=[LRU_CONV_LEFT - j for j in range(LRU_CONV)])


def _seg_matrix():
    h = np.arange(GROUP_WIDTH) // HEAD_DIM
    return jnp.asarray((h[:, None] == h[None, :]).astype(np.float32) / HEAD_DIM)


def _rope_tables(L):
    pos = jnp.arange(L, dtype=F32)
    inv_freq = ROPE_THETA ** (-jnp.arange(0, HEAD_DIM, 2, dtype=F32) / HEAD_DIM)
    ang = pos[:, None] * inv_freq[None, :]
    cos, sin = jnp.cos(ang), jnp.sin(ang)
    cs = jnp.tile(jnp.concatenate([cos, cos], axis=1), (1, GROUP_HEADS))
    sn = jnp.tile(jnp.concatenate([-sin, sin], axis=1), (1, GROUP_HEADS))
    return cs, sn


def _dil_branches(L):
    out = []
    for window, dil in DIL_PAIRS:
        radius = window // (2 * dil)
        n = L // dil
        out.append((dil, radius, min(256, n)))
    return out


def _dil_operands(qr, kr, p16, dil, radius):
    return _dilate(qr, dil), _pad_rows(_dilate(kr, dil), radius), _pad_rows(_dilate(_pslice(p16, P_VD), dil), radius)


def _layer_fwd(x, w, c):
    L, D = x.shape
    sv = {'x_in': x}
    h = _rowwise("mix_prenorm", _prenorm_fn, [x], [w['mix_norm_pre']], [(D, BF16)])[0]
    p, p16 = _matmul("mix_proj", h, w['w_in'], 'nn', F32, also=BF16)
    sv.update(p=p, p16=p16, h=h)

    la = _rowwise("gla_pre", _gla_pre_fn, [Rows(p, LANE, P_Z // LANE)], [w['gla_wg'], w['gla_bg']], [(2 * GROUP_WIDTH, F32)])[0]
    of, ob, sf, sb = _gla_scan_fwd(p, la)
    sv.update(la=la, sf=sf, sb=sb, of=of, ob=ob)

    yb = _na_fwd(p16, _rpb_expand(w['na_rpb'], c['rpb_tab']))

    a0, a1, u0, u1 = _rowwise("lru_pre", _lru_pre_fn, [_conv_taps(p)], [w[k] for k in LRU_PARAMS], [(GROUP_WIDTH, F32)] * 4)
    hf, hb = _lin_scans("lru_scan", [(a0, u0, False, 0), (a1, u1, True, 0)])
    sv.update(a0=a0, a1=a1, hf=hf, hb=hb)

    qr, kr = _rowwise("rope", _rope_fn, [_pcol(p, P_QD), _pcol(p, P_KD), c['cos'], c['sin']], [], [(GROUP_WIDTH, BF16)] * 2)
    os_, ls_, ops = [], [], []
    for dil, radius, tq in _dil_branches(L):
        ops.append(_dil_operands(qr, kr, p16, dil, radius))
        o, lse = _band_fwd(f"dil_attn{dil}", *ops[-1], tq, radius)
        os_.append(_undilate(o))
        ls_.append(_undilate(lse))
    sv.update(dil_ops=ops, dil_o=os_, dil_l=ls_)

    ycat = _rowwise("mix_post", _mix_post_fn, [of, ob, _pcol(p, P_GA), yb, hf, hb, _pcol(p, P_GC)] + os_ + ls_,
                    [w['gla_norm'], c['seg']], [(4 * GROUP_WIDTH, BF16)])[0]
    y = _matmul("mix_out", ycat, w['w_out'], 'nn', F32)
    xm = _rowwise("mix_postnorm", _postnorm_fn, [x, y], [w['mix_norm_post']], [(D, F32)])[0]
    sv.update(ycat=ycat, y=y, x_mid=xm)

    h2, gate, up, act = _ffn_up("ffn_up", xm, w['ffn_norm_pre'], w['ffn_wg'], w['ffn_wu'])
    f = _matmul("ffn_out", act, w['ffn_w_out'], 'nn', F32)
    xo = _rowwise("ffn_postnorm", _postnorm_fn, [xm, f], [w['ffn_norm_post']], [(D, F32)])[0]
    sv.update(gate=gate, up=up, act=act, f=f, h2=h2)
    return xo, sv


def _layer_bwd(dx, w, c, sv):
    L, D = dx.shape
    g = {}
    as_f32 = lambda t: (t.astype(F32),)
    df, g['ffn_norm_post'] = _rowwise_bwd("ffn_postnorm_b", lambda y, gn: _rms(y, gn), [sv['f']], [w['ffn_norm_post']],
                                          [dx], as_f32, [BF16], [True])
    dgate, dup = _ffn_down_bwd("ffn_out_bx", df, w['ffn_w_out'], sv['gate'], sv['up'])
    g['ffn_w_out'] = _matmul("ffn_out_bw", sv['act'], df, 'tn', F32)
    dh2 = _matmul("ffn_up_bx", dup, w['ffn_wu'], 'nt', F32, acc_in=_matmul("ffn_gate_bx", dgate, w['ffn_wg'], 'nt', F32))
    xm = sv['x_mid']
    g['ffn_wg'] = _matmul("ffn_gate_bw", sv['h2'], dgate, 'tn', F32)
    g['ffn_wu'] = _matmul("ffn_up_bw", sv['h2'], dup, 'tn', F32)
    dxm, g['ffn_norm_pre'] = _rowwise_bwd("ffn_prenorm_b", _prenorm_fn, [xm], [w['ffn_norm_pre']], [dh2], as_f32, [F32], [True],
                                          row_grad_add=[dx])

    dy, g['mix_norm_post'] = _rowwise_bwd("mix_postnorm_b", lambda y, gn: _rms(y, gn), [sv['y']], [w['mix_norm_post']],
                                          [dxm], as_f32, [BF16], [True])
    dycat = _matmul("mix_out_bx", dy, w['w_out'], 'nt', F32)
    g['w_out'] = _matmul("mix_out_bw", sv['ycat'], dy, 'tn', F32)
    p = sv['p']
    dya, dyb, dyc, dyd = (Rows(dycat, GROUP_WIDTH, k) for k in range(4))

    dof, dga, g['gla_norm'] = _rowwise_bwd("gla_post_b", _gla_post_fn, [sv['of'], sv['ob'], _pcol(p, P_GA)],
                                           [w['gla_norm'], c['seg']], [dya], as_f32, [F32, None, BF16], [True, False])
    la = sv['la']
    dqf, dkf, dvf, dlf, dqb_, dkb_, dvb_, dlb = _gla_scan_bwd(p, la, sv['sf'], sv['sb'], dof)
    dz, g['gla_wg'], g['gla_bg'] = _rowwise_bwd("gla_pre_b", _gla_pre_fn, [Rows(p, LANE, P_Z // LANE)], [w['gla_wg'], w['gla_bg']],
                                                [dlf, dlb], lambda a, b: (jnp.concatenate([a, b], axis=1),), [BF16], [True, True])

    btab = _rpb_expand(w['na_rpb'], c['rpb_tab'])
    dqn, dkn, dvn, dbt = _na_bwd(sv['p16'], btab, dycat)
    g['na_rpb'] = _rpb_contract(dbt, c['rpb_tab'])

    dh, dgc = _rowwise_bwd("lru_post_b", _lru_post_fn, [sv['hf'], sv['hb'], _pcol(p, P_GC)], [], [dyc], as_f32, [F32, None, BF16], [])
    lam0, lam1 = _lin_scans("lru_scan_b", [(sv['a0'], dh, True, -1), (sv['a1'], dh, False, 1)])
    res = _rowwise_bwd("lru_pre_b", _lru_pre_fn, [_conv_taps(p)], [w[k] for k in LRU_PARAMS],
                       [lam0, lam1, Rows(sv['hf'], shifts=[1]), Rows(sv['hb'], shifts=[-1])],
                       lambda l0, l1, hfp, hbn: (l0 * hfp, l1 * hbn, l0, l1), [F32] * 4, [True] * len(LRU_PARAMS))
    dxs = res[:4]
    for k, nm in enumerate(LRU_PARAMS):
        g[nm] = res[4 + k]
    dxc = [Rows(dxs[j], shifts=[j - LRU_CONV_LEFT]) for j in range(LRU_CONV)]

    comb = _rowwise_bwd("dil_comb_b", _dil_comb_fn, sv['dil_o'] + sv['dil_l'], [], [dyd], as_f32, [BF16] * 3 + [F32] * 3, [])
    dqs, dks, dvs = [], [], []
    for k, (dil, radius, tq) in enumerate(_dil_branches(L)):
        n = L // dil
        dq_, dk_, dv_ = _band_bwd(f"dil_attn{dil}_b", *sv['dil_ops'][k], _dilate(comb[k], dil), _dilate(comb[3 + k], dil), tq, radius)
        dqs.append(_undilate(dq_))
        dks.append(_undilate(dk_[:, radius:radius + n]))
        dvs.append(_undilate(dv_[:, radius:radius + n]))
    dqd, dkd = _rowwise_bwd("rope_b", _rope_fn, [_pcol(p, P_QD), _pcol(p, P_KD), c['cos'], c['sin']], [], dqs + dks,
                            lambda *t: (sum(v.astype(F32) for v in t[:3]), sum(v.astype(F32) for v in t[3:])), [BF16, BF16, None, None], [])

    dp = _assemble("mix_dp", [[dqf, dqb_], [dkf, dkb_], [dvf, dvb_], [dga], [dqn], [dkn], [dvn], dxc, [dgc], [dqd], [dkd], dvs, [dz]], BF16)
    dh1 = _matmul("mix_proj_bx", dp, w['w_in'], 'nt', F32)
    x_in = sv['x_in']
    g['w_in'] = _matmul("mix_proj_bw", sv['h'], dp, 'tn', F32)
    dxi, g['mix_norm_pre'] = _rowwise_bwd("mix_prenorm_b", _prenorm_fn, [x_in], [w['mix_norm_pre']], [dh1], as_f32, [F32], [True],
                                          row_grad_add=[dxm])
    return dxi, g


def _loss_fn(y, t):
    e = y - t
    return e * (1.0 / y.shape[1]), jnp.sum(e * e, axis=0, keepdims=True)


def _gather_cols(name, shard, axis):
    half = shard.shape[0] // 2
    mine = lax.dynamic_slice_in_dim(shard, lax.axis_index("c") * half, half, axis=0).astype(BF16)
    both = _exchange(name + "_c", _exchange(name + "_xy", mine, 'xy', True), 'c', True)
    shp = list(shard.shape)
    shp[axis] *= 4
    return jnp.moveaxis(both, 1, axis + 1).reshape(shp)


def _pack(arrs, mult=64 * LANE):
    flat = jnp.concatenate([a.reshape(-1) for a in arrs])
    pad = (-flat.shape[0]) % mult
    return jnp.pad(flat, (0, pad)).reshape(-1, LANE)


def _unpack(buf, shapes):
    flat, out, k = buf.reshape(-1), [], 0
    for s in shapes:
        sz = int(np.prod(s))
        out.append(flat[k:k + sz].reshape(s))
        k += sz
    return out


def _perm_in(w_in):
    pad = jnp.zeros(w_in.shape[:-1] + (D_INP - D_IN,), w_in.dtype)
    return jnp.concatenate([w_in[..., :P_QB * GROUP_WIDTH], w_in[..., P_QB * GROUP_WIDTH + 2 * GLA_RANK:],
                            w_in[..., P_QB * GROUP_WIDTH:P_QB * GROUP_WIDTH + 2 * GLA_RANK], pad], axis=-1)


def _unperm_in(g):
    return jnp.concatenate([g[..., :P_QB * GROUP_WIDTH], g[..., P_Z:P_Z + 2 * GLA_RANK], g[..., P_QB * GROUP_WIDTH:P_Z]], axis=-1)


def _block_diag(wb):
    l = wb.shape[0]
    eye = jnp.eye(GROUP_HEADS, dtype=wb.dtype)
    return jnp.einsum('lehij,hg->lehigj', wb, eye).reshape(l, 2, GROUP_WIDTH, GROUP_WIDTH)


def _block_diag_grad(gw):
    l = gw.shape[0]
    g6 = gw.reshape(l, 2, GROUP_HEADS, HEAD_DIM, GROUP_HEADS, HEAD_DIM)
    return jnp.stack([g6[:, :, h, :, h, :] for h in range(GROUP_HEADS)], axis=2)


def _gate_matrix(wg):
    l = wg.shape[0]
    m = jnp.zeros((l, LANE, 2 * GROUP_WIDTH), wg.dtype)
    for e in range(2):
        m = m.at[:, e * GLA_RANK:(e + 1) * GLA_RANK, e * GROUP_WIDTH:(e + 1) * GROUP_WIDTH].set(wg[:, e])
    return m


def _gate_matrix_grad(gm):
    return jnp.stack([gm[:, e * GLA_RANK:(e + 1) * GLA_RANK, e * GROUP_WIDTH:(e + 1) * GROUP_WIDTH] for e in range(2)], axis=1)


def _adam_fn(w, g, m, v):
    m = ADAM_B1 * m + (1.0 - ADAM_B1) * g
    v = ADAM_B2 * v + (1.0 - ADAM_B2) * (g * g)
    m_hat = m / (1.0 - ADAM_B1 ** ADAM_STEP)
    v_hat = v / (1.0 - ADAM_B2 ** ADAM_STEP)
    return -ADAM_LR * (m_hat / (jnp.sqrt(v_hat) + ADAM_EPS) + ADAM_WD * w), m, v


def _adam(name, w, g, m, v):
    shp = w.shape
    two = lambda t: t.reshape(-1, shp[-1])
    res = _rowwise(name, _adam_fn, [two(w), two(g), two(m), two(v)], [], [(shp[-1], F32)] * 3)
    return [r.reshape(shp) for r in res]


def _local_step(x, target, fw):
    L, D = x.shape
    depth = fw['w_in'].shape[0]
    cs, sn = _rope_tables(L)
    consts = {'seg': _seg_matrix(), 'rpb_tab': _rpb_tables(), 'cos': cs, 'sin': sn}
    layer = lambda l: {k: v[l] for k, v in fw.items()}
    saved = []
    for l in range(depth):
        x, sv = _layer_fwd(x, layer(l), consts)
        saved.append(sv)
    dx, sq = _rowwise("loss", _loss_fn, [x, target], [], [(D, F32)], acc_outs=[(1, D)])
    grads = [None] * depth
    for l in reversed(range(depth)):
        dx, grads[l] = _layer_bwd(dx, layer(l), consts, saved[l])
    return sq, dx, grads


def kernel(x, mix_norm_pre, mix_norm_post, w_in, gla_w_gate, gla_b_gate, gla_norm, na_rpb, lru_conv_w, lru_conv_b, lru_w_a, lru_b_a, lru_w_x, lru_b_x, lru_lambda, w_out, ffn_norm_pre, ffn_norm_post, ffn_w_in, ffn_w_out, loss_target, m_mix_norm_pre, m_mix_norm_post, m_w_in, m_gla_w_gate, m_gla_b_gate, m_gla_norm, m_na_rpb, m_lru_conv_w, m_lru_conv_b, m_lru_w_a, m_lru_b_a, m_lru_w_x, m_lru_b_x, m_lru_lambda, m_w_out, m_ffn_norm_pre, m_ffn_norm_post, m_ffn_w_in, m_ffn_w_out, v_mix_norm_pre, v_mix_norm_post, v_w_in, v_gla_w_gate, v_gla_b_gate, v_gla_norm, v_na_rpb, v_lru_conv_w, v_lru_conv_b, v_lru_w_a, v_lru_b_a, v_lru_w_x, v_lru_b_x, v_lru_lambda, v_w_out, v_ffn_norm_pre, v_ffn_norm_post, v_ffn_w_in, v_ffn_w_out):
    args = (mix_norm_pre, mix_norm_post, w_in, gla_w_gate, gla_b_gate, gla_norm, na_rpb, lru_conv_w, lru_conv_b, lru_w_a, lru_b_a, lru_w_x, lru_b_x, lru_lambda, w_out, ffn_norm_pre, ffn_norm_post, ffn_w_in, ffn_w_out,
            m_mix_norm_pre, m_mix_norm_post, m_w_in, m_gla_w_gate, m_gla_b_gate, m_gla_norm, m_na_rpb, m_lru_conv_w, m_lru_conv_b, m_lru_w_a, m_lru_b_a, m_lru_w_x, m_lru_b_x, m_lru_lambda, m_w_out, m_ffn_norm_pre, m_ffn_norm_post, m_ffn_w_in, m_ffn_w_out,
            v_mix_norm_pre, v_mix_norm_post, v_w_in, v_gla_w_gate, v_gla_b_gate, v_gla_norm, v_na_rpb, v_lru_conv_w, v_lru_conv_b, v_lru_w_a, v_lru_b_a, v_lru_w_x, v_lru_b_x, v_lru_lambda, v_w_out, v_ffn_norm_pre, v_ffn_norm_post, v_ffn_w_in, v_ffn_w_out)
    nw = len(WEIGHTS)
    W = dict(zip(WEIGHTS, args[:nw]))
    M = dict(zip(WEIGHTS, args[nw:2 * nw]))
    V = dict(zip(WEIGHTS, args[2 * nw:]))
    chip = 2 * lax.axis_index("x") + lax.axis_index("y")

    full = dict(W)
    full['w_in'] = _gather_cols("ag_w_in", w_in, 2)
    full['ffn_w_in'] = _gather_cols("ag_ffn_w_in", ffn_w_in, 2)
    full['w_out'] = _gather_cols("ag_w_out", w_out, 1)
    full['ffn_w_out'] = _gather_cols("ag_ffn_w_out", ffn_w_out, 1)
    small = list(SMALL_SHARDED)
    got = _exchange("ag_small", _pack([W[k] for k in small]), 'xy', True)
    for k, parts in zip(small, zip(*[_unpack(got[j], [W[k].shape for k in small]) for j in range(4)])):
        ax = SMALL_SHARDED[k]
        stacked = jnp.moveaxis(jnp.stack(parts), 0, ax)
        shp = list(W[k].shape)
        shp[ax] *= 4
        full[k] = stacked.reshape(shp)

    sq, dx0, g = _local_step(x[0], loss_target[0], _layer_weights(full))
    loss = lax.psum(0.5 * jnp.sum(sq) / x.shape[-1], ("x", "y", "c"))
    gfull = _stored_grads(g)

    grad = {}
    for k, cut in _big_cuts(g).items():
        grad[k] = _reduce_big("rs_" + k, cut).reshape(W[k].shape)
    rest = [k for k in WEIGHTS if k not in BIG]
    allg = _exchange("ar_small_c", _exchange("ar_small_xy", _pack([gfull[k] for k in rest]), 'xy', True), 'c', True)
    allg = allg.transpose(1, 0, 2, 3).reshape(8, -1, LANE)
    summed = _unpack(_ordered_sum("ar_small_sum", allg), [gfull[k].shape for k in rest])
    for k, s in zip(rest, summed):
        if k in SMALL_SHARDED:
            ax = SMALL_SHARDED[k]
            n = W[k].shape[ax]
            s = lax.dynamic_slice_in_dim(s, chip * n, n, axis=ax)
        grad[k] = s

    delta, new_m, new_v = {}, {}, {}
    for k in BIG:
        delta[k], new_m[k], new_v[k] = _adam("adam_" + k, W[k], grad[k], M[k], V[k])
    shapes = [W[k].shape for k in rest]
    res = _rowwise("adam_small", _adam_fn, [_pack([d[k] for k in rest]) for d in (W, grad, M, V)], [], [(LANE, F32)] * 3)
    for d, r in zip((delta, new_m, new_v), res):
        for k, t in zip(rest, _unpack(r, shapes)):
            d[k] = t

    return (loss, dx0[None], *[grad[k] for k in WEIGHTS], *[delta[k] for k in WEIGHTS],
            *[new_m[k] for k in WEIGHTS], *[new_v[k] for k in WEIGHTS])


def _layer_weights(full):
    depth = full['w_in'].shape[0]
    dff = full['ffn_w_in'].shape[-1] // 2
    row = lambda t: t[:, None, :]
    fw = {
        'mix_norm_pre': row(full['mix_norm_pre']), 'mix_norm_post': row(full['mix_norm_post']),
        'ffn_norm_pre': row(full['ffn_norm_pre']), 'ffn_norm_post': row(full['ffn_norm_post']),
        'w_in': _perm_in(full['w_in']), 'w_out': full['w_out'],
        'ffn_wg': full['ffn_w_in'][..., :dff], 'ffn_wu': full['ffn_w_in'][..., dff:], 'ffn_w_out': full['ffn_w_out'],
        'gla_wg': _gate_matrix(full['gla_w_gate']), 'gla_bg': full['gla_b_gate'].reshape(depth, 1, 2 * GROUP_WIDTH),
        'gla_norm': row(full['gla_norm']), 'na_rpb': full['na_rpb'],
        'lru_cb': row(full['lru_conv_b']),
    }
    wa_bd, wx_bd = _block_diag(full['lru_w_a']), _block_diag(full['lru_w_x'])
    for j in range(LRU_CONV):
        fw[f'lru_cw{j}'] = row(full['lru_conv_w'][:, j])
    for e in range(2):
        fw[f'lru_wa{e}'], fw[f'lru_wx{e}'] = wa_bd[:, e], wx_bd[:, e]
        fw[f'lru_ba{e}'], fw[f'lru_bx{e}'] = row(full['lru_b_a'][:, e]), row(full['lru_b_x'][:, e])
        fw[f'lru_lam{e}'] = row(full['lru_lambda'][:, e])
    return fw


def _orig_cols(gp, lo, hi):
    split, zend = P_QB * GROUP_WIDTH, P_QB * GROUP_WIDTH + 2 * GLA_RANK
    parts = []
    for a, b, at in ((0, split, 0), (split, zend, P_Z), (zend, D_IN, split)):
        s, e = max(lo, a), min(hi, b)
        if s < e:
            parts.append(gp[..., at + s - a:at + e - a])
    return parts[0] if len(parts) == 1 else jnp.concatenate(parts, axis=-1)


def _big_cuts(grads):
    depth = len(grads)
    halves = (range(0, depth // 2), range(depth // 2, depth))
    n_in, n_ff = D_IN // 4, grads[0]['ffn_wg'].shape[1] // 2

    def build(piece):
        return jnp.stack([jnp.stack([jnp.concatenate([piece(l, j) for l in hl], axis=0) for j in range(4)]) for hl in halves])

    def rows_of(key):
        return lambda l, j: jnp.split(grads[l][key], 4, axis=0)[j]

    return {
        'w_in': build(lambda l, j: _orig_cols(grads[l]['w_in'], j * n_in, (j + 1) * n_in)),
        'w_out': build(rows_of('w_out')),
        'ffn_w_in': build(lambda l, j: grads[l]['ffn_wg' if j < 2 else 'ffn_wu'][:, (j % 2) * n_ff:(j % 2 + 1) * n_ff]),
        'ffn_w_out': build(rows_of('ffn_w_out')),
    }


def _stored_grads(grads):
    depth = len(grads)
    g = {k: jnp.stack([gl[k] for gl in grads]) for k in grads[0]}
    return {
        'mix_norm_pre': g['mix_norm_pre'][:, 0], 'mix_norm_post': g['mix_norm_post'][:, 0],
        'ffn_norm_pre': g['ffn_norm_pre'][:, 0], 'ffn_norm_post': g['ffn_norm_post'][:, 0],
        'w_in': _unperm_in(g['w_in']), 'w_out': g['w_out'],
        'ffn_w_in': jnp.concatenate([g['ffn_wg'], g['ffn_wu']], axis=-1), 'ffn_w_out': g['ffn_w_out'],
        'gla_w_gate': _gate_matrix_grad(g['gla_wg']), 'gla_b_gate': g['gla_bg'].reshape(depth, 2, GROUP_WIDTH),
        'gla_norm': g['gla_norm'][:, 0], 'na_rpb': g['na_rpb'],
        'lru_conv_w': jnp.stack([g[f'lru_cw{j}'][:, 0] for j in range(LRU_CONV)], axis=1), 'lru_conv_b': g['lru_cb'][:, 0],
        'lru_w_a': _block_diag_grad(jnp.stack([g['lru_wa0'], g['lru_wa1']], axis=1)),
        'lru_w_x': _block_diag_grad(jnp.stack([g['lru_wx0'], g['lru_wx1']], axis=1)),
        'lru_b_a': jnp.stack([g['lru_ba0'][:, 0], g['lru_ba1'][:, 0]], axis=1),
        'lru_b_x': jnp.stack([g['lru_bx0'][:, 0], g['lru_bx1'][:, 0]], axis=1),
        'lru_lambda': jnp.stack([g['lru_lam0'][:, 0], g['lru_lam1'][:, 0]], axis=1),
    }
```
